```python
import math
import jax, jax.numpy as jnp
from jax import lax
import numpy as np

D_MODEL = 1024
BATCH = 8
SEQ = 8192
DEPTH = 2

N_MIXERS = 2
N_HEADS = 16
N_KV_HEADS = 4
HEAD_DIM = D_MODEL // N_HEADS
GROUP = N_HEADS // N_KV_HEADS
ROT_DIM = HEAD_DIM // 4
ROPE_THETA = 500000.0
WINDOW = 128
BLK = 128
QKV_DIM = (N_HEADS + 2 * N_KV_HEADS) * HEAD_DIM
CONV_CH = D_MODEL
CONV_WIDTH = 31
CONV_PAD = (CONV_WIDTH - 1) // 2
D_FF = ((8 * D_MODEL // 3 + 255) // 256) * 256
N_ATTN_LAYERS = (DEPTH + 1) // 2
N_CONV_LAYERS = DEPTH // 2
EPS = 1e-6
NEG = -1e30

kernel_name = "hybrid_window_gqa_conformer_conv_encoder"


def rmsnorm(x, g):
    xf = x.astype(jnp.float32)
    y = xf * lax.rsqrt(jnp.mean(xf * xf, axis=-1, keepdims=True) + EPS)
    return (y * g.astype(jnp.float32)).astype(x.dtype)


def layernorm(x, g, b):
    xf = x.astype(jnp.float32)
    mu = jnp.mean(xf, axis=-1, keepdims=True)
    var = jnp.mean(jnp.square(xf - mu), axis=-1, keepdims=True)
    y = (xf - mu) * lax.rsqrt(var + EPS)
    return (y * g.astype(jnp.float32) + b.astype(jnp.float32)).astype(x.dtype)


def partial_rope(x, cos, sin):
    half = ROT_DIM // 2
    x1 = x[..., :half].astype(jnp.float32)
    x2 = x[..., half:ROT_DIM].astype(jnp.float32)
    rot = jnp.concatenate([x1 * cos - x2 * sin, x2 * cos + x1 * sin], axis=-1)
    return jnp.concatenate([rot.astype(x.dtype), x[..., ROT_DIM:]], axis=-1)


def window_attention(h, w_qkv, w_o, sink):
    B, S, _ = h.shape
    nb = S // BLK
    qkv = h @ w_qkv
    q = qkv[..., :N_HEADS * HEAD_DIM].reshape(B, S, N_HEADS, HEAD_DIM)
    k = qkv[..., N_HEADS * HEAD_DIM:(N_HEADS + N_KV_HEADS) * HEAD_DIM].reshape(B, S, N_KV_HEADS, HEAD_DIM)
    v = qkv[..., (N_HEADS + N_KV_HEADS) * HEAD_DIM:].reshape(B, S, N_KV_HEADS, HEAD_DIM)

    pos = jnp.arange(S, dtype=jnp.float32)
    inv_freq = ROPE_THETA ** (-jnp.arange(0, ROT_DIM, 2, dtype=jnp.float32) / ROT_DIM)
    ang = pos[:, None] * inv_freq[None, :]
    cos = jnp.cos(ang)[:, None, :]
    sin = jnp.sin(ang)[:, None, :]
    q = partial_rope(q, cos, sin)
    k = partial_rope(k, cos, sin)

    qb = q.reshape(B, nb, BLK, N_KV_HEADS, GROUP, HEAD_DIM)
    pad = ((0, 0), (BLK, BLK), (0, 0), (0, 0))
    kp = jnp.pad(k, pad).reshape(B, nb + 2, BLK, N_KV_HEADS, HEAD_DIM)
    vp = jnp.pad(v, pad).reshape(B, nb + 2, BLK, N_KV_HEADS, HEAD_DIM)
    kb = jnp.concatenate([kp[:, :-2], kp[:, 1:-1], kp[:, 2:]], axis=2)
    vb = jnp.concatenate([vp[:, :-2], vp[:, 1:-1], vp[:, 2:]], axis=2)

    scale = 1.0 / math.sqrt(HEAD_DIM)
    s = jnp.einsum('bnqkgd,bnckd->bnkgqc', qb, kb).astype(jnp.float32) * scale

    qi = jnp.arange(BLK)
    ci = jnp.arange(3 * BLK)
    rel = ci[None, :] - BLK - qi[:, None]
    in_window = jnp.abs(rel) <= WINDOW
    key_pos = jnp.arange(nb)[:, None] * BLK - BLK + ci[None, :]
    in_range = (key_pos >= 0) & (key_pos < S)
    valid = in_window[None, :, :] & in_range[:, None, :]
    s = jnp.where(valid[None, :, None, None, :, :], s, NEG)

    sink_l = sink.astype(jnp.float32).reshape(1, 1, N_KV_HEADS, GROUP, 1, 1)
    m = jnp.maximum(jnp.max(s, axis=-1, keepdims=True), sink_l)
    e = jnp.exp(s - m)
    p = e / (jnp.sum(e, axis=-1, keepdims=True) + jnp.exp(sink_l - m))

    o = jnp.einsum('bnkgqc,bnckd->bnqkgd', p.astype(vb.dtype), vb)
    o = o.reshape(B, S, N_HEADS * HEAD_DIM)
    return o @ w_o


def conformer_conv(h, w_pw1, b_pw1, w_dw, b_dw, ln_g, ln_b, w_pw2, b_pw2):
    u = h @ w_pw1 + b_pw1
    a, gate = u[..., :CONV_CH], u[..., CONV_CH:]
    u = a * jax.nn.sigmoid(gate)
    u = lax.conv_general_dilated(
        u, w_dw[:, None, :].astype(u.dtype), window_strides=(1,),
        padding=[(CONV_PAD, CONV_PAD)],
        dimension_numbers=('NWC', 'WIO', 'NWC'),
        feature_group_count=CONV_CH) + b_dw
    u = layernorm(u, ln_g, ln_b)
    u = jax.nn.silu(u)
    return u @ w_pw2 + b_pw2


def swiglu_ffn(h, w_gu, w_down):
    gu = h @ w_gu
    return (jax.nn.silu(gu[..., :D_FF]) * gu[..., D_FF:]) @ w_down


def _fwd_setup_inputs(seed: int = 0) -> dict:
    key = jax.random.key(seed)
    ks = jax.random.split(key, 24)
    f32 = jnp.float32
    nrm = lambda k, shp, sc: jax.random.normal(k, shp, f32) * sc
    gain = lambda k, shp: 1.0 + 0.02 * jax.random.normal(k, shp, f32)
    na, nc = N_ATTN_LAYERS, N_CONV_LAYERS
    return {
        "x": nrm(ks[0], (BATCH, SEQ, D_MODEL), 1.0),
        "attn_norm": gain(ks[1], (na, D_MODEL)),
        "attn_w_qkv": nrm(ks[2], (na, D_MODEL, QKV_DIM), D_MODEL ** -0.5),
        "attn_w_o": nrm(ks[3], (na, N_HEADS * HEAD_DIM, D_MODEL), (N_HEADS * HEAD_DIM) ** -0.5),
        "attn_sink": nrm(ks[4], (na, N_HEADS), 0.5),
        "conv_norm": gain(ks[5], (nc, D_MODEL)),
        "conv_w_pw1": nrm(ks[6], (nc, D_MODEL, 2 * CONV_CH), D_MODEL ** -0.5),
        "conv_b_pw1": nrm(ks[7], (nc, 2 * CONV_CH), 0.02),
        "conv_w_dw": nrm(ks[8], (nc, CONV_WIDTH, CONV_CH), CONV_WIDTH ** -0.5),
        "conv_b_dw": nrm(ks[9], (nc, CONV_CH), 0.02),
        "conv_ln_g": gain(ks[10], (nc, CONV_CH)),
        "conv_ln_b": nrm(ks[11], (nc, CONV_CH), 0.02),
        "conv_w_pw2": nrm(ks[12], (nc, CONV_CH, D_MODEL), CONV_CH ** -0.5),
        "conv_b_pw2": nrm(ks[13], (nc, D_MODEL), 0.02),
        "ffn_norm": gain(ks[14], (DEPTH, D_MODEL)),
        "ffn_w_gu": nrm(ks[15], (DEPTH, D_MODEL, 2 * D_FF), D_MODEL ** -0.5),
        "ffn_w_down": nrm(ks[16], (DEPTH, D_FF, D_MODEL), D_FF ** -0.5),
        "final_norm": gain(ks[17], (D_MODEL,)),
    }


def _fwd_reference(x, attn_norm, attn_w_qkv, attn_w_o, attn_sink,
              conv_norm, conv_w_pw1, conv_b_pw1, conv_w_dw, conv_b_dw,
              conv_ln_g, conv_ln_b, conv_w_pw2, conv_b_pw2,
              ffn_norm, ffn_w_gu, ffn_w_down, final_norm):
    for i in range(DEPTH):
        j = i // N_MIXERS
        if i % N_MIXERS == 0:
            h = rmsnorm(x, attn_norm[j])
            x = x + window_attention(h, attn_w_qkv[j], attn_w_o[j], attn_sink[j])
        else:
            h = rmsnorm(x, conv_norm[j])
            x = x + conformer_conv(h, conv_w_pw1[j], conv_b_pw1[j], conv_w_dw[j],
                                   conv_b_dw[j], conv_ln_g[j], conv_ln_b[j],
                                   conv_w_pw2[j], conv_b_pw2[j])
        h = rmsnorm(x, ffn_norm[i])
        x = x + swiglu_ffn(h, ffn_w_gu[i], ffn_w_down[i])
    return rmsnorm(x, final_norm)


import jax as _jax
import jax.numpy as _jnp

TWIN_FORMAT = 'train_step'
FWD_PARAMS = ['x', 'attn_norm', 'attn_w_qkv', 'attn_w_o', 'attn_sink', 'conv_norm', 'conv_w_pw1', 'conv_b_pw1', 'conv_w_dw', 'conv_b_dw', 'conv_ln_g', 'conv_ln_b', 'conv_w_pw2', 'conv_b_pw2', 'ffn_norm', 'ffn_w_gu', 'ffn_w_down', 'final_norm']
TWIN_WEIGHTS = ['attn_norm', 'attn_w_qkv', 'attn_w_o', 'attn_sink', 'conv_norm', 'conv_w_pw1', 'conv_b_pw1', 'conv_w_dw', 'conv_b_dw', 'conv_ln_g', 'conv_ln_b', 'conv_w_pw2', 'conv_b_pw2', 'ffn_norm', 'ffn_w_gu', 'ffn_w_down', 'final_norm']
TWIN_DIFF_INPUT = 'x'
TWIN_INPUTS = ['x', 'attn_norm', 'attn_w_qkv', 'attn_w_o', 'attn_sink', 'conv_norm', 'conv_w_pw1', 'conv_b_pw1', 'conv_w_dw', 'conv_b_dw', 'conv_ln_g', 'conv_ln_b', 'conv_w_pw2', 'conv_b_pw2', 'ffn_norm', 'ffn_w_gu', 'ffn_w_down', 'final_norm', 'loss_target', 'm_attn_norm', 'm_attn_w_qkv', 'm_attn_w_o', 'm_attn_sink', 'm_conv_norm', 'm_conv_w_pw1', 'm_conv_b_pw1', 'm_conv_w_dw', 'm_conv_b_dw', 'm_conv_ln_g', 'm_conv_ln_b', 'm_conv_w_pw2', 'm_conv_b_pw2', 'm_ffn_norm', 'm_ffn_w_gu', 'm_ffn_w_down', 'm_final_norm', 'v_attn_norm', 'v_attn_w_qkv', 'v_attn_w_o', 'v_attn_sink', 'v_conv_norm', 'v_conv_w_pw1', 'v_conv_b_pw1', 'v_conv_w_dw', 'v_conv_b_dw', 'v_conv_ln_g', 'v_conv_ln_b', 'v_conv_w_pw2', 'v_conv_b_pw2', 'v_ffn_norm', 'v_ffn_w_gu', 'v_ffn_w_down', 'v_final_norm']
TWIN_OUTPUTS = ['loss', 'grad_x', 'grad_attn_norm', 'grad_attn_w_qkv', 'grad_attn_w_o', 'grad_attn_sink', 'grad_conv_norm', 'grad_conv_w_pw1', 'grad_conv_b_pw1', 'grad_conv_w_dw', 'grad_conv_b_dw', 'grad_conv_ln_g', 'grad_conv_ln_b', 'grad_conv_w_pw2', 'grad_conv_b_pw2', 'grad_ffn_norm', 'grad_ffn_w_gu', 'grad_ffn_w_down', 'grad_final_norm', 'delta_attn_norm', 'delta_attn_w_qkv', 'delta_attn_w_o', 'delta_attn_sink', 'delta_conv_norm', 'delta_conv_w_pw1', 'delta_conv_b_pw1', 'delta_conv_w_dw', 'delta_conv_b_dw', 'delta_conv_ln_g', 'delta_conv_ln_b', 'delta_conv_w_pw2', 'delta_conv_b_pw2', 'delta_ffn_norm', 'delta_ffn_w_gu', 'delta_ffn_w_down', 'delta_final_norm', 'new_m_attn_norm', 'new_m_attn_w_qkv', 'new_m_attn_w_o', 'new_m_attn_sink', 'new_m_conv_norm', 'new_m_conv_w_pw1', 'new_m_conv_b_pw1', 'new_m_conv_w_dw', 'new_m_conv_b_dw', 'new_m_conv_ln_g', 'new_m_conv_ln_b', 'new_m_conv_w_pw2', 'new_m_conv_b_pw2', 'new_m_ffn_norm', 'new_m_ffn_w_gu', 'new_m_ffn_w_down', 'new_m_final_norm', 'new_v_attn_norm', 'new_v_attn_w_qkv', 'new_v_attn_w_o', 'new_v_attn_sink', 'new_v_conv_norm', 'new_v_conv_w_pw1', 'new_v_conv_b_pw1', 'new_v_conv_w_dw', 'new_v_conv_b_dw', 'new_v_conv_ln_g', 'new_v_conv_ln_b', 'new_v_conv_w_pw2', 'new_v_conv_b_pw2', 'new_v_ffn_norm', 'new_v_ffn_w_gu', 'new_v_ffn_w_down', 'new_v_final_norm']
TWIN_LEAF_KINDS = {'loss': 'loss', 'grad_x': 'grad_x', 'grad_attn_norm': 'grad_w', 'grad_attn_w_qkv': 'grad_w', 'grad_attn_w_o': 'grad_w', 'grad_attn_sink': 'grad_w', 'grad_conv_norm': 'grad_w', 'grad_conv_w_pw1': 'grad_w', 'grad_conv_b_pw1': 'grad_w', 'grad_conv_w_dw': 'grad_w', 'grad_conv_b_dw': 'grad_w', 'grad_conv_ln_g': 'grad_w', 'grad_conv_ln_b': 'grad_w', 'grad_conv_w_pw2': 'grad_w', 'grad_conv_b_pw2': 'grad_w', 'grad_ffn_norm': 'grad_w', 'grad_ffn_w_gu': 'grad_w', 'grad_ffn_w_down': 'grad_w', 'grad_final_norm': 'grad_w', 'delta_attn_norm': 'delta_w', 'delta_attn_w_qkv': 'delta_w', 'delta_attn_w_o': 'delta_w', 'delta_attn_sink': 'delta_w', 'delta_conv_norm': 'delta_w', 'delta_conv_w_pw1': 'delta_w', 'delta_conv_b_pw1': 'delta_w', 'delta_conv_w_dw': 'delta_w', 'delta_conv_b_dw': 'delta_w', 'delta_conv_ln_g': 'delta_w', 'delta_conv_ln_b': 'delta_w', 'delta_conv_w_pw2': 'delta_w', 'delta_conv_b_pw2': 'delta_w', 'delta_ffn_norm': 'delta_w', 'delta_ffn_w_gu': 'delta_w', 'delta_ffn_w_down': 'delta_w', 'delta_final_norm': 'delta_w', 'new_m_attn_norm': 'new_m', 'new_m_attn_w_qkv': 'new_m', 'new_m_attn_w_o': 'new_m', 'new_m_attn_sink': 'new_m', 'new_m_conv_norm': 'new_m', 'new_m_conv_w_pw1': 'new_m', 'new_m_conv_b_pw1': 'new_m', 'new_m_conv_w_dw': 'new_m', 'new_m_conv_b_dw': 'new_m', 'new_m_conv_ln_g': 'new_m', 'new_m_conv_ln_b': 'new_m', 'new_m_conv_w_pw2': 'new_m', 'new_m_conv_b_pw2': 'new_m', 'new_m_ffn_norm': 'new_m', 'new_m_ffn_w_gu': 'new_m', 'new_m_ffn_w_down': 'new_m', 'new_m_final_norm': 'new_m', 'new_v_attn_norm': 'new_v', 'new_v_attn_w_qkv': 'new_v', 'new_v_attn_w_o': 'new_v', 'new_v_attn_sink': 'new_v', 'new_v_conv_norm': 'new_v', 'new_v_conv_w_pw1': 'new_v', 'new_v_conv_b_pw1': 'new_v', 'new_v_conv_w_dw': 'new_v', 'new_v_conv_b_dw': 'new_v', 'new_v_conv_ln_g': 'new_v', 'new_v_conv_ln_b': 'new_v', 'new_v_conv_w_pw2': 'new_v', 'new_v_conv_b_pw2': 'new_v', 'new_v_ffn_norm': 'new_v', 'new_v_ffn_w_gu': 'new_v', 'new_v_ffn_w_down': 'new_v', 'new_v_final_norm': 'new_v'}


def _forward(args):
    return _fwd_reference(*[args[k] for k in FWD_PARAMS])


def _output_shape():
    def fwd():
        inp = _fwd_setup_inputs(0)
        return _fwd_reference(*[inp[k] for k in FWD_PARAMS])
    out = _jax.eval_shape(fwd)
    return out.shape, out.dtype

N_MICROBATCH = 1
ADAM_LR = 0.001
ADAM_B1 = 0.9
ADAM_B2 = 0.999
ADAM_EPS = 1e-08
ADAM_WD = 0.01
ADAM_STEP = 10
PER_EXAMPLE_BATCH_AXIS = {'x': 0, 'loss_target': 0}
SHARED_INPUTS = []
_WEIGHT_DTYPES = {'attn_norm': _jnp.float32, 'attn_w_qkv': _jnp.float32, 'attn_w_o': _jnp.float32, 'attn_sink': _jnp.float32, 'conv_norm': _jnp.float32, 'conv_w_pw1': _jnp.float32, 'conv_b_pw1': _jnp.float32, 'conv_w_dw': _jnp.float32, 'conv_b_dw': _jnp.float32, 'conv_ln_g': _jnp.float32, 'conv_ln_b': _jnp.float32, 'conv_w_pw2': _jnp.float32, 'conv_b_pw2': _jnp.float32, 'ffn_norm': _jnp.float32, 'ffn_w_gu': _jnp.float32, 'ffn_w_down': _jnp.float32, 'final_norm': _jnp.float32}
MOMENT_SCALE = {'attn_norm': 6.790586e-02, 'attn_w_qkv': 5.316589e-02, 'attn_w_o': 3.797324e-02, 'attn_sink': 1.271114e-03, 'conv_norm': 1.358866e-01, 'conv_w_pw1': 9.736418e-02, 'conv_b_pw1': 1.000832e-01, 'conv_w_dw': 1.283003e-01, 'conv_b_dw': 2.710566e-01, 'conv_ln_g': 1.540409e-01, 'conv_ln_b': 1.484363e-01, 'conv_w_pw2': 1.293562e-01, 'conv_b_pw2': 2.748851e-01, 'ffn_norm': 1.959544e-01, 'ffn_w_gu': 7.634966e-02, 'ffn_w_down': 1.249979e-01, 'final_norm': 6.408514e+01}


def _to_microbatches(a, axis):
    t = _jnp.moveaxis(a, axis, 0)
    t = t.reshape((N_MICROBATCH, t.shape[0] // N_MICROBATCH) + t.shape[1:])
    return _jnp.moveaxis(t, 1, axis + 1)


def setup_inputs(seed: int = 0) -> dict:
    inp = _fwd_setup_inputs(seed)
    key = _jax.random.fold_in(_jax.random.key(seed), 7919)
    shape, _ = _output_shape()
    out = dict(inp)
    out["loss_target"] = _jax.random.normal(_jax.random.fold_in(key, 0), shape, _jnp.float32)
    for i, name in enumerate(TWIN_WEIGHTS):
        w = inp[name].astype(_jnp.float32)
        if MOMENT_SCALE is None:
            s = _jnp.sqrt(_jnp.mean(_jnp.square(w)) + 1e-30)
        else:
            s = MOMENT_SCALE[name]
        km, kv = _jax.random.split(_jax.random.fold_in(key, i + 1))
        out[name] = w
        out["m_" + name] = s * _jax.random.normal(km, w.shape, _jnp.float32)
        out["v_" + name] = (s * s) * _jax.random.uniform(kv, w.shape, _jnp.float32, 0.5, 1.5)
    if N_MICROBATCH > 1:
        for name, axis in PER_EXAMPLE_BATCH_AXIS.items():
            out[name] = _to_microbatches(out[name], axis)
    return {'x': out['x'], 'attn_norm': out['attn_norm'], 'attn_w_qkv': out['attn_w_qkv'], 'attn_w_o': out['attn_w_o'], 'attn_sink': out['attn_sink'], 'conv_norm': out['conv_norm'], 'conv_w_pw1': out['conv_w_pw1'], 'conv_b_pw1': out['conv_b_pw1'], 'conv_w_dw': out['conv_w_dw'], 'conv_b_dw': out['conv_b_dw'], 'conv_ln_g': out['conv_ln_g'], 'conv_ln_b': out['conv_ln_b'], 'conv_w_pw2': out['conv_w_pw2'], 'conv_b_pw2': out['conv_b_pw2'], 'ffn_norm': out['ffn_norm'], 'ffn_w_gu': out['ffn_w_gu'], 'ffn_w_down': out['ffn_w_down'], 'final_norm': out['final_norm'], 'loss_target': out['loss_target'], 'm_attn_norm': out['m_attn_norm'], 'm_attn_w_qkv': out['m_attn_w_qkv'], 'm_attn_w_o': out['m_attn_w_o'], 'm_attn_sink': out['m_attn_sink'], 'm_conv_norm': out['m_conv_norm'], 'm_conv_w_pw1': out['m_conv_w_pw1'], 'm_conv_b_pw1': out['m_conv_b_pw1'], 'm_conv_w_dw': out['m_conv_w_dw'], 'm_conv_b_dw': out['m_conv_b_dw'], 'm_conv_ln_g': out['m_conv_ln_g'], 'm_conv_ln_b': out['m_conv_ln_b'], 'm_conv_w_pw2': out['m_conv_w_pw2'], 'm_conv_b_pw2': out['m_conv_b_pw2'], 'm_ffn_norm': out['m_ffn_norm'], 'm_ffn_w_gu': out['m_ffn_w_gu'], 'm_ffn_w_down': out['m_ffn_w_down'], 'm_final_norm': out['m_final_norm'], 'v_attn_norm': out['v_attn_norm'], 'v_attn_w_qkv': out['v_attn_w_qkv'], 'v_attn_w_o': out['v_attn_w_o'], 'v_attn_sink': out['v_attn_sink'], 'v_conv_norm': out['v_conv_norm'], 'v_conv_w_pw1': out['v_conv_w_pw1'], 'v_conv_b_pw1': out['v_conv_b_pw1'], 'v_conv_w_dw': out['v_conv_w_dw'], 'v_conv_b_dw': out['v_conv_b_dw'], 'v_conv_ln_g': out['v_conv_ln_g'], 'v_conv_ln_b': out['v_conv_ln_b'], 'v_conv_w_pw2': out['v_conv_w_pw2'], 'v_conv_b_pw2': out['v_conv_b_pw2'], 'v_ffn_norm': out['v_ffn_norm'], 'v_ffn_w_gu': out['v_ffn_w_gu'], 'v_ffn_w_down': out['v_ffn_w_down'], 'v_final_norm': out['v_final_norm']}


def _loss(weights, diff, rest, loss_target):
    with _jax.named_scope("forward"):
        args = {**rest, TWIN_DIFF_INPUT: diff, **{k: w.astype(_WEIGHT_DTYPES[k]) for k, w in weights.items()}}
        y = _forward(args)
    with _jax.named_scope("loss_head"):
        err = _jnp.square(y.astype(_jnp.float32) - loss_target)
        return 0.5 * _jnp.sum(_jnp.mean(err, axis=-1)) if err.ndim else 0.5 * err


def _adamw(w, g, m, v):
    m = ADAM_B1 * m + (1.0 - ADAM_B1) * g
    v = ADAM_B2 * v + (1.0 - ADAM_B2) * _jnp.square(g)
    m_hat = m / (1.0 - ADAM_B1 ** ADAM_STEP)
    v_hat = v / (1.0 - ADAM_B2 ** ADAM_STEP)
    delta = -ADAM_LR * (m_hat / (_jnp.sqrt(v_hat) + ADAM_EPS) + ADAM_WD * w)
    return delta, m, v


def reference(x, attn_norm, attn_w_qkv, attn_w_o, attn_sink, conv_norm, conv_w_pw1, conv_b_pw1, conv_w_dw, conv_b_dw, conv_ln_g, conv_ln_b, conv_w_pw2, conv_b_pw2, ffn_norm, ffn_w_gu, ffn_w_down, final_norm, loss_target, m_attn_norm, m_attn_w_qkv, m_attn_w_o, m_attn_sink, m_conv_norm, m_conv_w_pw1, m_conv_b_pw1, m_conv_w_dw, m_conv_b_dw, m_conv_ln_g, m_conv_ln_b, m_conv_w_pw2, m_conv_b_pw2, m_ffn_norm, m_ffn_w_gu, m_ffn_w_down, m_final_norm, v_attn_norm, v_attn_w_qkv, v_attn_w_o, v_attn_sink, v_conv_norm, v_conv_w_pw1, v_conv_b_pw1, v_conv_w_dw, v_conv_b_dw, v_conv_ln_g, v_conv_ln_b, v_conv_w_pw2, v_conv_b_pw2, v_ffn_norm, v_ffn_w_gu, v_ffn_w_down, v_final_norm):
    given = dict(x=x, attn_norm=attn_norm, attn_w_qkv=attn_w_qkv, attn_w_o=attn_w_o, attn_sink=attn_sink, conv_norm=conv_norm, conv_w_pw1=conv_w_pw1, conv_b_pw1=conv_b_pw1, conv_w_dw=conv_w_dw, conv_b_dw=conv_b_dw, conv_ln_g=conv_ln_g, conv_ln_b=conv_ln_b, conv_w_pw2=conv_w_pw2, conv_b_pw2=conv_b_pw2, ffn_norm=ffn_norm, ffn_w_gu=ffn_w_gu, ffn_w_down=ffn_w_down, final_norm=final_norm, loss_target=loss_target, m_attn_norm=m_attn_norm, m_attn_w_qkv=m_attn_w_qkv, m_attn_w_o=m_attn_w_o, m_attn_sink=m_attn_sink, m_conv_norm=m_conv_norm, m_conv_w_pw1=m_conv_w_pw1, m_conv_b_pw1=m_conv_b_pw1, m_conv_w_dw=m_conv_w_dw, m_conv_b_dw=m_conv_b_dw, m_conv_ln_g=m_conv_ln_g, m_conv_ln_b=m_conv_ln_b, m_conv_w_pw2=m_conv_w_pw2, m_conv_b_pw2=m_conv_b_pw2, m_ffn_norm=m_ffn_norm, m_ffn_w_gu=m_ffn_w_gu, m_ffn_w_down=m_ffn_w_down, m_final_norm=m_final_norm, v_attn_norm=v_attn_norm, v_attn_w_qkv=v_attn_w_qkv, v_attn_w_o=v_attn_w_o, v_attn_sink=v_attn_sink, v_conv_norm=v_conv_norm, v_conv_w_pw1=v_conv_w_pw1, v_conv_b_pw1=v_conv_b_pw1, v_conv_w_dw=v_conv_w_dw, v_conv_b_dw=v_conv_b_dw, v_conv_ln_g=v_conv_ln_g, v_conv_ln_b=v_conv_ln_b, v_conv_w_pw2=v_conv_w_pw2, v_conv_b_pw2=v_conv_b_pw2, v_ffn_norm=v_ffn_norm, v_ffn_w_gu=v_ffn_w_gu, v_ffn_w_down=v_ffn_w_down, v_final_norm=v_final_norm)
    weights = {n: given[n] for n in TWIN_WEIGHTS}
    shared = {n: given[n] for n in SHARED_INPUTS}
    per_example = {n: given[n] for n in ['x']}
    grad_fn = _jax.value_and_grad(_loss, argnums=(0, 1))

    def one_microbatch(ex, loss_target):
        ex = dict(ex)
        diff = ex.pop(TWIN_DIFF_INPUT)
        return grad_fn(weights, diff, {**shared, **ex}, loss_target)

    if N_MICROBATCH == 1:
        loss, (grad_w, grad_x) = one_microbatch(per_example, given["loss_target"])
    else:
        def body(carry, xs):
            loss_sum, grad_sum = carry
            l_k, (gw_k, gx_k) = one_microbatch(xs[0], xs[1])
            with _jax.named_scope("update"):
                return (loss_sum + l_k, _jax.tree.map(_jnp.add, grad_sum, gw_k)), gx_k

        init = (_jnp.zeros((), _jnp.float32), _jax.tree.map(_jnp.zeros_like, weights))
        (loss, grad_w), grad_x = _jax.lax.scan(body, init, (per_example, given["loss_target"]))
    with _jax.named_scope("update"):
        delta_w, new_m, new_v = {}, {}, {}
        for n in TWIN_WEIGHTS:
            delta_w[n], new_m[n], new_v[n] = _adamw(weights[n], grad_w[n], given["m_" + n], given["v_" + n])
    return (loss, grad_x, *[grad_w[n] for n in TWIN_WEIGHTS], *[delta_w[n] for n in TWIN_WEIGHTS],
            *[new_m[n] for n in TWIN_WEIGHTS], *[new_v[n] for n in TWIN_WEIGHTS])
```

```python
import math

import jax
import jax.numpy as jnp
from jax import lax
from jax.experimental import pallas as pl
from jax.experimental.pallas import tpu as pltpu

F32 = jnp.float32
BF16 = jnp.bfloat16

D = 1024
NH = 16
NKV = 4
HD = 64
GROUP = NH // NKV
ROT = 16
THETA = 500000.0
BLK = 128
QKV = (NH + 2 * NKV) * HD
KOFF = NH * HD
VOFF = KOFF + NKV * HD
DFF = 2816
NDEV = 8
GUB = 2 * DFF // NDEV
PWB = 2 * D // NDEV
CW = 31
CPAD = 15
HALO = 16
EPS = 1e-6
NEG = -1e30
LR, B1, B2, AEPS, WD, STEP = 0.001, 0.9, 0.999, 1e-08, 0.01, 10

VMEM_LIMIT = 56 * 1024 * 1024
MESH = pl.DeviceIdType.MESH


def _pc(body, **kw):
    return pl.pallas_call(body, **kw)


def _cparams(**kw):
    return pltpu.CompilerParams(dimension_semantics=("arbitrary",), vmem_limit_bytes=VMEM_LIMIT, **kw)


WHOLE = pl.BlockSpec(memory_space=pltpu.VMEM)


def _rows(tm, width):
    return pl.BlockSpec((tm, width), lambda i: (i, 0))


def _blk3(nb, tm, width):
    return pl.BlockSpec((nb, tm, width), lambda i: (0, i, 0))


def _acc(rows, width):
    return pl.BlockSpec((rows, width), lambda i: (0, 0))


def _dot(a, b):
    return jnp.dot(a.astype(BF16), b.astype(BF16), preferred_element_type=F32)


def _dot_nt(a, b):
    return lax.dot_general(a.astype(BF16), b.astype(BF16), (((1,), (1,)), ((), ())), preferred_element_type=F32)


def _dot_tn(a, b):
    return lax.dot_general(a.astype(BF16), b.astype(BF16), (((0,), (0,)), ((), ())), preferred_element_type=F32)


def _sigmoid(x):
    return 1.0 / (1.0 + jnp.exp(-x))


def _rms(x, g):
    r = lax.rsqrt(jnp.mean(x * x, axis=-1, keepdims=True) + EPS)
    xh = x * r
    return xh, r, xh * g


def _rms_bwd(dh, xh, r, g):
    dxh = dh * g
    dg = jnp.sum(dh * xh, axis=0, keepdims=True)
    dx = r * (dxh - xh * jnp.mean(dxh * xh, axis=-1, keepdims=True))
    return dx, dg


def _lanes(t, width):
    return jnp.tile(t, (1, width // t.shape[1]))


def _rope(z, c, sa, sb):
    w = z.shape[1]
    return z * _lanes(c, w) + pltpu.roll(z, w - 8, 1) * _lanes(sa, w) + pltpu.roll(z, 8, 1) * _lanes(sb, w)


def _rope_t(dz, c, sa, sb):
    w = dz.shape[1]
    return dz * _lanes(c, w) + pltpu.roll(dz * _lanes(sa, w), 8, 1) + pltpu.roll(dz * _lanes(sb, w), w - 8, 1)


def _rope_tables(t):
    pos = jnp.arange(t, dtype=F32)
    inv_freq = THETA ** (-jnp.arange(0, ROT, 2, dtype=F32) / ROT)
    ang = pos[:, None] * inv_freq[None, :]
    cos, sin = jnp.cos(ang), jnp.sin(ang)
    one = jnp.ones((t, HD - ROT), F32)
    zero = jnp.zeros((t, HD - ROT), F32)
    z8 = jnp.zeros((t, 8), F32)
    c = jnp.concatenate([cos, cos, one], axis=1)
    sa = jnp.concatenate([-sin, z8, zero], axis=1)
    sb = jnp.concatenate([z8, sin, zero], axis=1)
    return tuple(jnp.tile(a, (1, 2)) for a in (c, sa, sb))


def _attn_pre(x, g, wqkv, tabs, tm):
    t = x.shape[0]

    def body(x_ref, g_ref, w_ref, c_ref, sa_ref, sb_ref, qkv_ref):
        _, _, h = _rms(x_ref[...], g_ref[...])
        z = _dot(h, w_ref[...])
        c, sa, sb = c_ref[...], sa_ref[...], sb_ref[...]
        qkv_ref[:, :KOFF] = (_rope(z[:, :KOFF], c, sa, sb) * 0.125).astype(BF16)
        qkv_ref[:, KOFF:VOFF] = _rope(z[:, KOFF:VOFF], c, sa, sb).astype(BF16)
        qkv_ref[:, VOFF:] = z[:, VOFF:].astype(BF16)

    return _pc(
        body, name="attn_pre", grid=(t // tm,),
        in_specs=[_rows(tm, D), _acc(1, D), WHOLE, _rows(tm, 128), _rows(tm, 128), _rows(tm, 128)],
        out_specs=_rows(tm, QKV), out_shape=jax.ShapeDtypeStruct((t, QKV), BF16),
        compiler_params=_cparams(),
    )(x, g, wqkv, *tabs)


def _attn_specs(nblk):
    kb, vb = KOFF // (NKV * HD), VOFF // (NKV * HD)
    prev = lambda i: jnp.maximum(i - 1, 0)
    nxt = lambda i: jnp.minimum(i + 1, nblk - 1)
    w = NKV * HD
    return [
        pl.BlockSpec((BLK, KOFF), lambda i: (i, 0)),
        pl.BlockSpec((BLK, w), lambda i: (prev(i), kb)),
        pl.BlockSpec((BLK, w), lambda i: (i, kb)),
        pl.BlockSpec((BLK, w), lambda i: (nxt(i), kb)),
        pl.BlockSpec((BLK, w), lambda i: (prev(i), vb)),
        pl.BlockSpec((BLK, w), lambda i: (i, vb)),
        pl.BlockSpec((BLK, w), lambda i: (nxt(i), vb)),
    ]


def _attn_mask(i, nblk):
    r = lax.broadcasted_iota(jnp.int32, (BLK, 3 * BLK), 0)
    c = lax.broadcasted_iota(jnp.int32, (BLK, 3 * BLK), 1)
    lo = jnp.where(i == 0, BLK, 0)
    hi = jnp.where(i == nblk - 1, 2 * BLK, 3 * BLK)
    return (c >= r) & (c - 2 * BLK <= r) & (c >= lo) & (c < hi)


def _attn_probs(q_h, k_h, valid, sink_h):
    s = jnp.where(valid, _dot_nt(q_h, k_h), NEG)
    m = jnp.maximum(jnp.max(s, axis=1, keepdims=True), sink_h)
    e = jnp.exp(s - m)
    es = jnp.exp(sink_h - m)
    inv = 1.0 / (jnp.sum(e, axis=1, keepdims=True) + es)
    return e * inv, es * inv


def _attn_fwd(qkv, sink):
    t = qkv.shape[0]
    nblk = t // BLK

    def body(sink_ref, q_ref, kp, kc, kn, vp, vc, vn, o_ref):
        i = pl.program_id(0)
        valid = _attn_mask(i, nblk)
        k = jnp.concatenate([kp[...], kc[...], kn[...]], axis=0)
        v = jnp.concatenate([vp[...], vc[...], vn[...]], axis=0)
        for h in range(NH):
            kv = h // GROUP
            p, _ = _attn_probs(q_ref[:, h * HD:(h + 1) * HD], k[:, kv * HD:(kv + 1) * HD], valid, sink_ref[h])
            o_ref[:, h * HD:(h + 1) * HD] = _dot(p, v[:, kv * HD:(kv + 1) * HD]).astype(BF16)

    return _pc(
        body, name="attn_fwd", grid=(nblk,),
        in_specs=[pl.BlockSpec(memory_space=pltpu.SMEM)] + _attn_specs(nblk),
        out_specs=pl.BlockSpec((BLK, D), lambda i: (i, 0)), out_shape=jax.ShapeDtypeStruct((t, D), BF16),
        compiler_params=_cparams(),
    )(sink, qkv, qkv, qkv, qkv, qkv, qkv, qkv)


def _attn_bwd(qkv, do, sink):
    t = qkv.shape[0]
    nblk = t // BLK
    w = NKV * HD

    def body(sink_ref, q_ref, kp, kc, kn, vp, vc, vn, do_ref, dq_ref, dk_ref, dv_ref, ds_ref):
        i = pl.program_id(0)

        @pl.when(i == 0)
        def _():
            dk_ref[...] = jnp.zeros_like(dk_ref)
            dv_ref[...] = jnp.zeros_like(dv_ref)
            ds_ref[...] = jnp.zeros_like(ds_ref)

        valid = _attn_mask(i, nblk)
        k = jnp.concatenate([kp[...], kc[...], kn[...]], axis=0)
        v = jnp.concatenate([vp[...], vc[...], vn[...]], axis=0)
        lane = lax.broadcasted_iota(jnp.int32, (1, 128), 1)
        dsink = jnp.zeros((1, 128), F32)
        rows = pl.ds(pl.multiple_of(i * BLK, BLK), 3 * BLK)
        for kv in range(NKV):
            k_h, v_h = k[:, kv * HD:(kv + 1) * HD], v[:, kv * HD:(kv + 1) * HD]
            dk_h = jnp.zeros((3 * BLK, HD), F32)
            dv_h = jnp.zeros((3 * BLK, HD), F32)
            for g in range(GROUP):
                h = kv * GROUP + g
                q_h = q_ref[:, h * HD:(h + 1) * HD]
                do_h = do_ref[:, h * HD:(h + 1) * HD]
                p, ps = _attn_probs(q_h, k_h, valid, sink_ref[h])
                dp = _dot_nt(do_h, v_h)
                delta = jnp.sum(p * dp, axis=1, keepdims=True)
                ds = p * (dp - delta)
                dq_ref[:, h * HD:(h + 1) * HD] = _dot(ds, k_h)
                dk_h = dk_h + _dot_tn(ds, q_h)
                dv_h = dv_h + _dot_tn(p, do_h)
                tot = jnp.sum(jnp.sum(ps * delta, axis=1, keepdims=True), axis=0, keepdims=True)
                dsink = dsink - jnp.where(lane == h, tot, 0.0)
            dk_ref[rows, kv * HD:(kv + 1) * HD] += dk_h
            dv_ref[rows, kv * HD:(kv + 1) * HD] += dv_h
        ds_ref[0:1, :] += dsink

    return _pc(
        body, name="attn_bwd", grid=(nblk,),
        in_specs=[pl.BlockSpec(memory_space=pltpu.SMEM)] + _attn_specs(nblk) + [pl.BlockSpec((BLK, D), lambda i: (i, 0))],
        out_specs=[pl.BlockSpec((BLK, D), lambda i: (i, 0)), _acc(t + 2 * BLK, w), _acc(t + 2 * BLK, w), _acc(8, 128)],
        out_shape=[jax.ShapeDtypeStruct((t, D), F32), jax.ShapeDtypeStruct((t + 2 * BLK, w), F32),
                   jax.ShapeDtypeStruct((t + 2 * BLK, w), F32), jax.ShapeDtypeStruct((8, 128), F32)],
        compiler_params=_cparams(),
    )(sink, qkv, qkv, qkv, qkv, qkv, qkv, qkv, do)


def _attn_post(o, x, wo, tm):
    t = x.shape[0]

    def body(o_ref, x_ref, w_ref, out_ref):
        out_ref[...] = x_ref[...] + _dot(o_ref[...], w_ref[...])

    return _pc(
        body, name="attn_post", grid=(t // tm,),
        in_specs=[_rows(tm, D), _rows(tm, D), WHOLE], out_specs=_rows(tm, D),
        out_shape=jax.ShapeDtypeStruct((t, D), F32), compiler_params=_cparams(),
    )(o, x, wo)


def _attn_post_bwd(dx, wo, tm):
    t = dx.shape[0]

    def body(dx_ref, w_ref, do_ref):
        do_ref[...] = _dot_nt(dx_ref[...], w_ref[...]).astype(BF16)

    return _pc(
        body, name="attn_post_bwd", grid=(t // tm,),
        in_specs=[_rows(tm, D), WHOLE], out_specs=_rows(tm, D),
        out_shape=jax.ShapeDtypeStruct((t, D), BF16), compiler_params=_cparams(),
    )(dx, wo)


def _attn_pre_bwd(dq, dk, dv, x, g, wqkv, tabs, dx_out, tm):
    t = x.shape[0]
    w = NKV * HD
    off = BLK // tm

    def body(dq_ref, dk_ref, dv_ref, x_ref, g_ref, w_ref, c_ref, sa_ref, sb_ref, dxo_ref, dqkv_ref, h_ref, dx_ref, dg_ref):
        @pl.when(pl.program_id(0) == 0)
        def _():
            dg_ref[...] = jnp.zeros_like(dg_ref)

        c, sa, sb = c_ref[...], sa_ref[...], sb_ref[...]
        dqkv_ref[:, :KOFF] = _rope_t(dq_ref[...] * 0.125, c, sa, sb).astype(BF16)
        dqkv_ref[:, KOFF:VOFF] = _rope_t(dk_ref[...], c, sa, sb).astype(BF16)
        dqkv_ref[:, VOFF:] = dv_ref[...].astype(BF16)
        g = g_ref[...]
        xh, r, h = _rms(x_ref[...], g)
        h_ref[...] = h.astype(BF16)
        dh = _dot_nt(dqkv_ref[...], w_ref[...])
        dx, dg = _rms_bwd(dh, xh, r, g)
        dx_ref[...] = dxo_ref[...] + dx
        dg_ref[...] += dg

    return _pc(
        body, name="attn_pre_bwd", grid=(t // tm,),
        in_specs=[_rows(tm, D), pl.BlockSpec((tm, w), lambda i: (i + off, 0)), pl.BlockSpec((tm, w), lambda i: (i + off, 0)),
                  _rows(tm, D), _acc(1, D), WHOLE, _rows(tm, 128), _rows(tm, 128), _rows(tm, 128), _rows(tm, D)],
        out_specs=[_rows(tm, QKV), _rows(tm, D), _rows(tm, D), _acc(1, D)],
        out_shape=[jax.ShapeDtypeStruct((t, QKV), BF16), jax.ShapeDtypeStruct((t, D), BF16),
                   jax.ShapeDtypeStruct((t, D), F32), jax.ShapeDtypeStruct((1, D), F32)],
        compiler_params=_cparams(),
    )(dq, dk, dv, x, g, wqkv, *tabs, dx_out)


def _ffn_fwd(x, g, wgu, wd, tm, name):
    t = x.shape[0]

    def body(x_ref, g_ref, wgu_ref, wd_ref, gu_ref, out_ref):
        x = x_ref[...]
        _, _, h = _rms(x, g_ref[...])
        hb = h.astype(BF16)
        y = x
        for j in range(4):
            gj = _dot(hb, wgu_ref[j])
            uj = _dot(hb, wgu_ref[j + 4])
            gu_ref[j] = gj.astype(BF16)
            gu_ref[j + 4] = uj.astype(BF16)
            y = y + _dot(gj * _sigmoid(gj) * uj, wd_ref[j])
        out_ref[...] = y

    return _pc(
        body, name=name, grid=(t // tm,),
        in_specs=[_rows(tm, D), _acc(1, D), WHOLE, WHOLE],
        out_specs=[_blk3(NDEV, tm, GUB), _rows(tm, D)],
        out_shape=[jax.ShapeDtypeStruct((NDEV, t, GUB), BF16), jax.ShapeDtypeStruct((t, D), F32)],
        compiler_params=_cparams(),
    )(x, g, wgu, wd)


def _ffn_bwd(dy, x, gu, g, wgu, wd, tm, name):
    t = x.shape[0]

    def body(dy_ref, x_ref, gu_ref, g_ref, wgu_ref, wd_ref, dgu_ref, act_ref, h_ref, dx_ref, dg_ref):
        @pl.when(pl.program_id(0) == 0)
        def _():
            dg_ref[...] = jnp.zeros_like(dg_ref)

        dy = dy_ref[...]
        dyb = dy.astype(BF16)
        gn = g_ref[...]
        xh, r, h = _rms(x_ref[...], gn)
        h_ref[...] = h.astype(BF16)
        dh = jnp.zeros_like(dy)
        for j in range(4):
            gj = gu_ref[j].astype(F32)
            uj = gu_ref[j + 4].astype(F32)
            sg = _sigmoid(gj)
            silu = gj * sg
            act_ref[j] = (silu * uj).astype(BF16)
            dact = _dot_nt(dyb, wd_ref[j])
            dgj = (dact * uj * (sg * (1.0 + gj * (1.0 - sg)))).astype(BF16)
            duj = (dact * silu).astype(BF16)
            dgu_ref[j] = dgj
            dgu_ref[j + 4] = duj
            dh = dh + _dot_nt(dgj, wgu_ref[j]) + _dot_nt(duj, wgu_ref[j + 4])
        dx, dg = _rms_bwd(dh, xh, r, gn)
        dx_ref[...] = dy + dx
        dg_ref[...] += dg

    return _pc(
        body, name=name, grid=(t // tm,),
        in_specs=[_rows(tm, D), _rows(tm, D), _blk3(NDEV, tm, GUB), _acc(1, D), WHOLE, WHOLE],
        out_specs=[_blk3(NDEV, tm, GUB), _blk3(4, tm, GUB), _rows(tm, D), _rows(tm, D), _acc(1, D)],
        out_shape=[jax.ShapeDtypeStruct((NDEV, t, GUB), BF16), jax.ShapeDtypeStruct((4, t, GUB), BF16),
                   jax.ShapeDtypeStruct((t, D), BF16), jax.ShapeDtypeStruct((t, D), F32), jax.ShapeDtypeStruct((1, D), F32)],
        compiler_params=_cparams(),
    )(dy, x, gu, g, wgu, wd)


def _wgrad(a, b, tk, name):
    na, t, ma = a.shape
    nb, _, mb = b.shape
    nk = t // tk

    def body(a_ref, b_ref, out_ref, acc):
        k = pl.program_id(2)

        @pl.when(k == 0)
        def _():
            acc[...] = jnp.zeros_like(acc)

        acc[...] += _dot_tn(a_ref[0], b_ref[0])

        @pl.when(k == nk - 1)
        def _():
            out_ref[0, 0] = acc[...].astype(BF16)

    return _pc(
        body, name=name, grid=(na, nb, nk),
        in_specs=[pl.BlockSpec((1, tk, ma), lambda i, j, k: (i, k, 0)), pl.BlockSpec((1, tk, mb), lambda i, j, k: (j, k, 0))],
        out_specs=pl.BlockSpec((1, 1, ma, mb), lambda i, j, k: (i, j, 0, 0)),
        out_shape=jax.ShapeDtypeStruct((na, nb, ma, mb), BF16),
        scratch_shapes=[pltpu.VMEM((ma, mb), F32)],
        compiler_params=pltpu.CompilerParams(dimension_semantics=("arbitrary", "arbitrary", "arbitrary"),
                                             vmem_limit_bytes=VMEM_LIMIT),
    )(a, b)


def _conv_pre(x, g, w, b, tm):
    t = x.shape[0]

    def body(x_ref, g_ref, w_ref, b_ref, u_ref, glu_ref):
        _, _, h = _rms(x_ref[...], g_ref[...])
        hb = h.astype(BF16)
        for j in range(4):
            a = _dot(hb, w_ref[j]) + b_ref[:, j * PWB:(j + 1) * PWB]
            gt = _dot(hb, w_ref[j + 4]) + b_ref[:, D + j * PWB:D + (j + 1) * PWB]
            u_ref[j] = a.astype(BF16)
            u_ref[j + 4] = gt.astype(BF16)
            glu_ref[:, j * PWB:(j + 1) * PWB] = a * _sigmoid(gt)

    return _pc(
        body, name="conv_pre", grid=(t // tm,),
        in_specs=[_rows(tm, D), _acc(1, D), WHOLE, _acc(1, 2 * D)],
        out_specs=[_blk3(NDEV, tm, PWB), _rows(tm, D)],
        out_shape=[jax.ShapeDtypeStruct((NDEV, t, PWB), BF16), jax.ShapeDtypeStruct((t, D), F32)],
        compiler_params=_cparams(),
    )(x, g, w, b)


def _halo_specs(t, tm):
    per = tm // HALO
    last = t // HALO - 1
    return [
        pl.BlockSpec((HALO, D), lambda i: (jnp.maximum(i * per - 1, 0), 0)),
        _rows(tm, D),
        pl.BlockSpec((HALO, D), lambda i: (jnp.minimum((i + 1) * per, last), 0)),
    ]


def _fill_halo(buf, prev, cur, nxt, tm):
    i = pl.program_id(0)
    buf[0:HALO, :] = jnp.where(i == 0, 0.0, prev[...])
    buf[HALO:HALO + tm, :] = cur[...]
    buf[HALO + tm:, :] = jnp.where(i == pl.num_programs(0) - 1, 0.0, nxt[...])


CCH = 32
CLN = 256


def _conv_mid(glu, wdw, bdw, tm):
    t = glu.shape[0]

    def body(prev, cur, nxt, w_ref, b_ref, out_ref, buf):
        _fill_halo(buf, prev, cur, nxt, tm)
        for c0 in range(0, D, CLN):
            for r0 in range(0, tm, CCH):
                acc = jnp.broadcast_to(b_ref[:, c0:c0 + CLN], (CCH, CLN))
                for k in range(CW):
                    s = r0 + k + HALO - CPAD
                    acc = acc + w_ref[k:k + 1, c0:c0 + CLN] * buf[s:s + CCH, c0:c0 + CLN]
                out_ref[r0:r0 + CCH, c0:c0 + CLN] = acc

    return _pc(
        body, name="conv_mid", grid=(t // tm,),
        in_specs=_halo_specs(t, tm) + [_acc(32, D), _acc(1, D)],
        out_specs=_rows(tm, D), out_shape=jax.ShapeDtypeStruct((t, D), F32),
        scratch_shapes=[pltpu.VMEM((tm + 2 * HALO, D), F32)],
        compiler_params=_cparams(),
    )(glu, glu, glu, wdw, bdw)


def _conv_mid_bwd(dcv, glu, wdw, tm):
    t = glu.shape[0]

    def body(dp, dc, dn, gp, gc, gn, w_ref, dglu_ref, dw_ref, dbuf, gbuf):
        @pl.when(pl.program_id(0) == 0)
        def _():
            dw_ref[...] = jnp.zeros_like(dw_ref)

        _fill_halo(dbuf, dp, dc, dn, tm)
        _fill_halo(gbuf, gp, gc, gn, tm)
        for c0 in range(0, D, CLN):
            dwk = [jnp.zeros((1, CLN), F32) for _ in range(CW)]
            for r0 in range(0, tm, CCH):
                d0 = dbuf[r0 + HALO:r0 + HALO + CCH, c0:c0 + CLN]
                acc = jnp.zeros((CCH, CLN), F32)
                for k in range(CW):
                    s = r0 - k + HALO + CPAD
                    acc = acc + w_ref[k:k + 1, c0:c0 + CLN] * dbuf[s:s + CCH, c0:c0 + CLN]
                    s = r0 + k + HALO - CPAD
                    dwk[k] = dwk[k] + jnp.sum(d0 * gbuf[s:s + CCH, c0:c0 + CLN], axis=0, keepdims=True)
                dglu_ref[r0:r0 + CCH, c0:c0 + CLN] = acc
            for k in range(CW):
                dw_ref[k:k + 1, c0:c0 + CLN] += dwk[k]

    return _pc(
        body, name="conv_mid_bwd", grid=(t // tm,),
        in_specs=_halo_specs(t, tm) + _halo_specs(t, tm) + [_acc(32, D)],
        out_specs=[_rows(tm, D), _acc(32, D)],
        out_shape=[jax.ShapeDtypeStruct((t, D), F32), jax.ShapeDtypeStruct((32, D), F32)],
        scratch_shapes=[pltpu.VMEM((tm + 2 * HALO, D), F32), pltpu.VMEM((tm + 2 * HALO, D), F32)],
        compiler_params=_cparams(),
    )(dcv, dcv, dcv, glu, glu, glu, wdw)


def _ln(cv, lg, lb):
    mu = jnp.mean(cv, axis=-1, keepdims=True)
    cc = cv - mu
    rs = lax.rsqrt(jnp.mean(cc * cc, axis=-1, keepdims=True) + EPS)
    lh = cc * rs
    return lh, rs, lh * lg + lb


def _conv_post(cv, x, lg, lb, w2, b2, tm):
    t = x.shape[0]

    def body(cv_ref, x_ref, lg_ref, lb_ref, w_ref, b_ref, s_ref, out_ref):
        _, _, ln = _ln(cv_ref[...], lg_ref[...], lb_ref[...])
        s = (ln * _sigmoid(ln)).astype(BF16)
        s_ref[...] = s
        out_ref[...] = x_ref[...] + _dot(s, w_ref[...]) + b_ref[...]

    return _pc(
        body, name="conv_post", grid=(t // tm,),
        in_specs=[_rows(tm, D), _rows(tm, D), _acc(1, D), _acc(1, D), WHOLE, _acc(1, D)],
        out_specs=[_rows(tm, D), _rows(tm, D)],
        out_shape=[jax.ShapeDtypeStruct((t, D), BF16), jax.ShapeDtypeStruct((t, D), F32)],
        compiler_params=_cparams(),
    )(cv, x, lg, lb, w2, b2)


def _conv_post_bwd(dx, cv, lg, lb, w2, tm):
    t = dx.shape[0]

    def body(dx_ref, cv_ref, lg_ref, lb_ref, w_ref, dcv_ref, part_ref):
        @pl.when(pl.program_id(0) == 0)
        def _():
            part_ref[...] = jnp.zeros_like(part_ref)

        dx = dx_ref[...]
        lg = lg_ref[...]
        lh, rs, ln = _ln(cv_ref[...], lg, lb_ref[...])
        sg = _sigmoid(ln)
        dln = _dot_nt(dx, w_ref[...]) * (sg * (1.0 + ln * (1.0 - sg)))
        dlh = dln * lg
        dcv = rs * (dlh - jnp.mean(dlh, axis=-1, keepdims=True) - lh * jnp.mean(dlh * lh, axis=-1, keepdims=True))
        dcv_ref[...] = dcv
        part_ref[0:1, :] += jnp.sum(dln * lh, axis=0, keepdims=True)
        part_ref[1:2, :] += jnp.sum(dln, axis=0, keepdims=True)
        part_ref[2:3, :] += jnp.sum(dcv, axis=0, keepdims=True)
        part_ref[3:4, :] += jnp.sum(dx, axis=0, keepdims=True)

    return _pc(
        body, name="conv_post_bwd", grid=(t // tm,),
        in_specs=[_rows(tm, D), _rows(tm, D), _acc(1, D), _acc(1, D), WHOLE],
        out_specs=[_rows(tm, D), _acc(8, D)],
        out_shape=[jax.ShapeDtypeStruct((t, D), F32), jax.ShapeDtypeStruct((8, D), F32)],
        compiler_params=_cparams(),
    )(dx, cv, lg, lb, w2)


def _conv_pre_bwd(dglu, u, x, g, w, dx_out, tm):
    t = x.shape[0]

    def body(dglu_ref, u_ref, x_ref, g_ref, w_ref, dxo_ref, du_ref, h_ref, dx_ref, dg_ref, db_ref):
        @pl.when(pl.program_id(0) == 0)
        def _():
            dg_ref[...] = jnp.zeros_like(dg_ref)
            db_ref[...] = jnp.zeros_like(db_ref)

        gn = g_ref[...]
        xh, r, h = _rms(x_ref[...], gn)
        h_ref[...] = h.astype(BF16)
        dh = jnp.zeros_like(xh)
        for j in range(4):
            a = u_ref[j].astype(F32)
            sg = _sigmoid(u_ref[j + 4].astype(F32))
            dgl = dglu_ref[:, j * PWB:(j + 1) * PWB]
            da = dgl * sg
            dgt = dgl * a * sg * (1.0 - sg)
            db_ref[:, j * PWB:(j + 1) * PWB] += jnp.sum(da, axis=0, keepdims=True)
            db_ref[:, D + j * PWB:D + (j + 1) * PWB] += jnp.sum(dgt, axis=0, keepdims=True)
            da, dgt = da.astype(BF16), dgt.astype(BF16)
            du_ref[j] = da
            du_ref[j + 4] = dgt
            dh = dh + _dot_nt(da, w_ref[j]) + _dot_nt(dgt, w_ref[j + 4])
        dx, dg = _rms_bwd(dh, xh, r, gn)
        dx_ref[...] = dxo_ref[...] + dx
        dg_ref[...] += dg

    return _pc(
        body, name="conv_pre_bwd", grid=(t // tm,),
        in_specs=[_rows(tm, D), _blk3(NDEV, tm, PWB), _rows(tm, D), _acc(1, D), WHOLE, _rows(tm, D)],
        out_specs=[_blk3(NDEV, tm, PWB), _rows(tm, D), _rows(tm, D), _acc(1, D), _acc(1, 2 * D)],
        out_shape=[jax.ShapeDtypeStruct((NDEV, t, PWB), BF16), jax.ShapeDtypeStruct((t, D), BF16),
                   jax.ShapeDtypeStruct((t, D), F32), jax.ShapeDtypeStruct((1, D), F32), jax.ShapeDtypeStruct((1, 2 * D), F32)],
        compiler_params=_cparams(),
    )(dglu, u, x, g, w, dx_out)


def _final(x, g, tgt, tm):
    t = x.shape[0]

    def body(x_ref, g_ref, t_ref, dx_ref, part_ref):
        @pl.when(pl.program_id(0) == 0)
        def _():
            part_ref[...] = jnp.zeros_like(part_ref)

        g = g_ref[...]
        xh, r, y = _rms(x_ref[...], g)
        err = y - t_ref[...]
        dx, dg = _rms_bwd(err * (1.0 / D), xh, r, g)
        dx_ref[...] = dx
        part_ref[0:1, :] += dg
        tok = jnp.sum(err * err, axis=-1, keepdims=True) * (1.0 / D)
        lane = lax.broadcasted_iota(jnp.int32, (1, D), 1)
        part_ref[1:2, :] += jnp.where(lane == 0, 0.5 * jnp.sum(tok, axis=0, keepdims=True), 0.0)

    return _pc(
        body, name="final", grid=(t // tm,),
        in_specs=[_rows(tm, D), _acc(1, D), _rows(tm, D)],
        out_specs=[_rows(tm, D), _acc(8, D)],
        out_shape=[jax.ShapeDtypeStruct((t, D), F32), jax.ShapeDtypeStruct((8, D), F32)],
        compiler_params=_cparams(),
    )(x, g, tgt)


TM = 256


def _local_step(x, tgt, p):
    t = x.shape[0]
    tabs = _rope_tables(t)
    qkv = _attn_pre(x, p["attn_norm"], p["wqkv"], tabs, TM)
    o = _attn_fwd(qkv, p["sink"])
    x1 = _attn_post(o, x, p["wo"], TM)
    gu0, x2 = _ffn_fwd(x1, p["ffn_norm0"], p["wgu0"], p["wd0"], TM, "ffn_fwd0")
    u, glu = _conv_pre(x2, p["conv_norm"], p["wpw1"], p["b_pw1"], TM)
    cv = _conv_mid(glu, p["w_dw"], p["b_dw"], TM)
    s, x3 = _conv_post(cv, x2, p["ln_g"], p["ln_b"], p["wpw2"], p["b_pw2"], TM)
    gu1, x4 = _ffn_fwd(x3, p["ffn_norm1"], p["wgu1"], p["wd1"], TM, "ffn_fwd1")
    dx4, fin = _final(x4, p["final_norm"], tgt, TM)
    g = {}
    dgu1, act1, h3, dx3, g["ffn_norm1"] = _ffn_bwd(dx4, x3, gu1, p["ffn_norm1"], p["wgu1"], p["wd1"], TM, "ffn_bwd1")
    dx4b = dx4.astype(BF16)[None]
    g["wgu1"] = _wgrad(h3[None], dgu1, 512, "dwgu1")[0]
    g["wd1"] = _wgrad(act1, dx4b, 512, "dwd1")[:, 0]
    dcv, cpart = _conv_post_bwd(dx3, cv, p["ln_g"], p["ln_b"], p["wpw2"], TM)
    dx3b = dx3.astype(BF16)[None]
    g["wpw2"] = _wgrad(s[None], dx3b, 512, "dwpw2")[0, 0]
    dglu, g["w_dw"] = _conv_mid_bwd(dcv, glu, p["w_dw"], TM)
    du, h2, dx2, g["conv_norm"], g["b_pw1"] = _conv_pre_bwd(dglu, u, x2, p["conv_norm"], p["wpw1"], dx3, TM)
    g["wpw1"] = _wgrad(h2[None], du, 512, "dwpw1")[0]
    dgu0, act0, h1, dx1, g["ffn_norm0"] = _ffn_bwd(dx2, x1, gu0, p["ffn_norm0"], p["wgu0"], p["wd0"], TM, "ffn_bwd0")
    dx2b = dx2.astype(BF16)[None]
    g["wgu0"] = _wgrad(h1[None], dgu0, 512, "dwgu0")[0]
    g["wd0"] = _wgrad(act0, dx2b, 512, "dwd0")[:, 0]
    do = _attn_post_bwd(dx1, p["wo"], TM)
    dx1b = dx1.astype(BF16)[None]
    g["wo"] = _wgrad(o[None], dx1b, 512, "dwo")[0, 0]
    dq, dk, dv, dsink = _attn_bwd(qkv, do, p["sink"])
    dqkv, h0, dx0, g["attn_norm"] = _attn_pre_bwd(dq, dk, dv, x, p["attn_norm"], p["wqkv"], tabs, dx1, BLK)
    g["wqkv"] = _wgrad(h0[None], dqkv[None], 512, "dwqkv")[0, 0]
    g["sink"] = dsink[0:1, :NH]
    g["ln_g"], g["ln_b"], g["b_dw"], g["b_pw2"] = cpart[0:1], cpart[1:2], cpart[2:3], cpart[3:4]
    g["final_norm"] = fin[0:1]
    return fin[1, 0], dx0, g


def _place():
    x, y, c = lax.axis_index("x"), lax.axis_index("y"), lax.axis_index("c")
    return x, y, c, 4 * x + 2 * y + c


def _peer(x, y, c, j):
    px = 1 - x if j & 4 else x
    py = 1 - y if j & 2 else y
    pc = 1 - c if j & 1 else c
    return (px, py, pc), 4 * px + 2 * py + pc


ANY = pl.BlockSpec(memory_space=pl.ANY)


def _exchange(arrs, scatter, name):
    n = len(arrs)

    def body(*refs):
        src, dst = refs[:n], refs[n:2 * n]
        send, recv, loc = refs[2 * n:]
        x, y, c, me = _place()

        def copies(k):
            if scatter[k]:
                local = pltpu.make_async_copy(src[k].at[me], dst[k].at[0], loc.at[k])
            else:
                local = pltpu.make_async_copy(src[k], dst[k].at[me], loc.at[k])
            out = []
            for j in range(1, NDEV):
                peer, pidx = _peer(x, y, c, j)
                if scatter[k]:
                    s, d_out, d_in = src[k].at[pidx], dst[k].at[j], dst[k].at[j]
                else:
                    s, d_out, d_in = src[k], dst[k].at[me], dst[k].at[pidx]
                out.append((
                    pltpu.make_async_remote_copy(src_ref=s, dst_ref=d_out, send_sem=send.at[k, j - 1], recv_sem=recv.at[k, j - 1],
                                                 device_id=peer, device_id_type=MESH),
                    pltpu.make_async_remote_copy(src_ref=s, dst_ref=d_in, send_sem=send.at[k, j - 1], recv_sem=recv.at[k, j - 1],
                                                 device_id=peer, device_id_type=MESH)))
            return local, out

        cps = [copies(k) for k in range(n)]
        for local, out in cps:
            local.start()
            for snd, _ in out:
                snd.start()
        for local, out in cps:
            for snd, rcv in out:
                rcv.wait_recv()
            for snd, _ in out:
                snd.wait_send()
            local.wait()

    def oshape(a, sc):
        return jax.ShapeDtypeStruct(a.shape if sc else (NDEV,) + a.shape, a.dtype)

    return _pc(
        body, name=name, in_specs=[ANY] * n, out_specs=[ANY] * n,
        out_shape=[oshape(a, sc) for a, sc in zip(arrs, scatter)],
        scratch_shapes=[pltpu.SemaphoreType.DMA((n, NDEV - 1)), pltpu.SemaphoreType.DMA((n, NDEV - 1)), pltpu.SemaphoreType.DMA((n,))],
        compiler_params=pltpu.CompilerParams(has_side_effects=True),
    )(*arrs)


def _adamw(w, g, m, v):
    m = B1 * m + (1.0 - B1) * g
    v = B2 * v + (1.0 - B2) * (g * g)
    m_hat = m / (1.0 - B1 ** STEP)
    v_hat = v / (1.0 - B2 ** STEP)
    return -LR * (m_hat / (jnp.sqrt(v_hat) + AEPS) + WD * w), m, v


def _reduce_adamw(land, w, m, v, tr, name):
    _, r, c = land.shape

    def body(l_ref, w_ref, m_ref, v_ref, g_ref, d_ref, nm_ref, nv_ref):
        g = l_ref[0].astype(F32)
        for j in range(1, NDEV):
            g = g + l_ref[j].astype(F32)
        g_ref[...] = g
        d_ref[...], nm_ref[...], nv_ref[...] = _adamw(w_ref[...], g, m_ref[...], v_ref[...])

    return _pc(
        body, name=name, grid=(r // tr,),
        in_specs=[_blk3(NDEV, tr, c)] + [_rows(tr, c)] * 3, out_specs=[_rows(tr, c)] * 4,
        out_shape=[jax.ShapeDtypeStruct((r, c), F32)] * 4, compiler_params=_cparams(),
    )(land, w, m, v)


def _sum_parts(parts):
    _, r, c = parts.shape

    def body(p_ref, out_ref):
        s = p_ref[0]
        for j in range(1, NDEV):
            s = s + p_ref[j]
        out_ref[...] = s

    return _pc(body, name="sum_parts", in_specs=[WHOLE], out_specs=WHOLE, out_shape=jax.ShapeDtypeStruct((r, c), F32))(parts)


def _adamw_small(w, g, m, v):
    def body(w_ref, g_ref, m_ref, v_ref, d_ref, nm_ref, nv_ref):
        d_ref[...], nm_ref[...], nv_ref[...] = _adamw(w_ref[...], g_ref[...], m_ref[...], v_ref[...])

    return _pc(body, name="adamw_small", in_specs=[WHOLE] * 4, out_specs=[WHOLE] * 3,
               out_shape=[jax.ShapeDtypeStruct(w.shape, F32)] * 3)(w, g, m, v)


def _rows128(a):
    a = a.astype(F32)
    if a.shape[-1] % 128:
        a = jnp.pad(a, [(0, 0)] * (a.ndim - 1) + [(0, 128 - a.shape[-1] % 128)])
    return a.reshape(-1, 128)


def _pack(arrs, rows):
    p = jnp.concatenate([_rows128(a) for a in arrs], axis=0)
    return jnp.pad(p, ((0, rows - p.shape[0]), (0, 0)))


SMALL = ("attn_norm", "ffn_norm", "final_norm", "attn_sink", "conv_norm", "conv_b_dw", "conv_ln_g", "conv_ln_b", "conv_b_pw2",
         "conv_b_pw1", "conv_w_dw")
SMALL_ROWS = 72


def _pack_small(d):
    return _pack([d[k] for k in SMALL], SMALL_ROWS)


def _unpack_small(p, like):
    out, r = {}, 0
    for k in SMALL:
        shp = like[k].shape
        n = -(-shp[-1] // 128) * (math.prod(shp[:-1]))
        blk = p[r:r + n]
        if shp[-1] % 128:
            blk = blk[:, :shp[-1]]
        out[k] = blk.reshape(shp)
        r += n
    return out


BIG = ("attn_w_qkv", "attn_w_o", "conv_w_pw1", "conv_w_pw2", "ffn_w_gu0", "ffn_w_gu1", "ffn_w_down0", "ffn_w_down1")
NAMES = ("attn_norm", "attn_w_qkv", "attn_w_o", "attn_sink", "conv_norm", "conv_w_pw1", "conv_b_pw1", "conv_w_dw", "conv_b_dw",
         "conv_ln_g", "conv_ln_b", "conv_w_pw2", "conv_b_pw2", "ffn_norm", "ffn_w_gu", "ffn_w_down", "final_norm")


def _split_layers(d):
    return {
        "attn_w_qkv": d["attn_w_qkv"][0], "attn_w_o": d["attn_w_o"][0], "conv_w_pw1": d["conv_w_pw1"][0], "conv_w_pw2": d["conv_w_pw2"][0],
        "ffn_w_gu0": d["ffn_w_gu"][0], "ffn_w_gu1": d["ffn_w_gu"][1], "ffn_w_down0": d["ffn_w_down"][0], "ffn_w_down1": d["ffn_w_down"][1],
    }


def _join_layers(d):
    return {
        "attn_w_qkv": d["attn_w_qkv"][None], "attn_w_o": d["attn_w_o"][None], "conv_w_pw1": d["conv_w_pw1"][None],
        "conv_w_pw2": d["conv_w_pw2"][None], "ffn_w_gu": jnp.stack([d["ffn_w_gu0"], d["ffn_w_gu1"]]),
        "ffn_w_down": jnp.stack([d["ffn_w_down0"], d["ffn_w_down1"]]),
    }


def kernel(x, attn_norm, attn_w_qkv, attn_w_o, attn_sink, conv_norm, conv_w_pw1, conv_b_pw1, conv_w_dw, conv_b_dw, conv_ln_g, conv_ln_b, conv_w_pw2, conv_b_pw2, ffn_norm, ffn_w_gu, ffn_w_down, final_norm, loss_target, m_attn_norm, m_attn_w_qkv, m_attn_w_o, m_attn_sink, m_conv_norm, m_conv_w_pw1, m_conv_b_pw1, m_conv_w_dw, m_conv_b_dw, m_conv_ln_g, m_conv_ln_b, m_conv_w_pw2, m_conv_b_pw2, m_ffn_norm, m_ffn_w_gu, m_ffn_w_down, m_final_norm, v_attn_norm, v_attn_w_qkv, v_attn_w_o, v_attn_sink, v_conv_norm, v_conv_w_pw1, v_conv_b_pw1, v_conv_w_dw, v_conv_b_dw, v_conv_ln_g, v_conv_ln_b, v_conv_w_pw2, v_conv_b_pw2, v_ffn_norm, v_ffn_w_gu, v_ffn_w_down, v_final_norm):
    w = dict(zip(NAMES, (attn_norm, attn_w_qkv, attn_w_o, attn_sink, conv_norm, conv_w_pw1, conv_b_pw1, conv_w_dw, conv_b_dw, conv_ln_g,
                         conv_ln_b, conv_w_pw2, conv_b_pw2, ffn_norm, ffn_w_gu, ffn_w_down, final_norm)))
    m = dict(zip(NAMES, (m_attn_norm, m_attn_w_qkv, m_attn_w_o, m_attn_sink, m_conv_norm, m_conv_w_pw1, m_conv_b_pw1, m_conv_w_dw,
                         m_conv_b_dw, m_conv_ln_g, m_conv_ln_b, m_conv_w_pw2, m_conv_b_pw2, m_ffn_norm, m_ffn_w_gu, m_ffn_w_down,
                         m_final_norm)))
    v = dict(zip(NAMES, (v_attn_norm, v_attn_w_qkv, v_attn_w_o, v_attn_sink, v_conv_norm, v_conv_w_pw1, v_conv_b_pw1, v_conv_w_dw,
                         v_conv_b_dw, v_conv_ln_g, v_conv_ln_b, v_conv_w_pw2, v_conv_b_pw2, v_ffn_norm, v_ffn_w_gu, v_ffn_w_down,
                         v_final_norm)))
    me = 4 * lax.axis_index("x") + 2 * lax.axis_index("y") + lax.axis_index("c")
    wb, mb, vb = _split_layers(w), _split_layers(m), _split_layers(v)

    shard_rows = _pack([w["conv_w_dw"][0], jnp.zeros((1, 128), F32), w["conv_norm"], w["conv_b_dw"], w["conv_ln_g"], w["conv_ln_b"],
                        w["conv_b_pw2"], w["conv_b_pw1"]], 40)
    gathered = _exchange([wb[k].astype(BF16) for k in BIG] + [shard_rows], [False] * 9, "all_gather")
    gw = dict(zip(BIG, gathered[:8]))
    sm = gathered[8]

    def full_vec(row, n=1):
        return sm[:, row:row + n, :].reshape(1, NDEV * n * 128)

    p = dict(
        attn_norm=w["attn_norm"], sink=w["attn_sink"][0], ffn_norm0=w["ffn_norm"][0:1], ffn_norm1=w["ffn_norm"][1:2],
        final_norm=w["final_norm"][None],
        wqkv=gw["attn_w_qkv"].transpose(1, 0, 2).reshape(D, QKV), wo=gw["attn_w_o"].reshape(D, D),
        wpw1=gw["conv_w_pw1"], wpw2=gw["conv_w_pw2"].reshape(D, D),
        wgu0=gw["ffn_w_gu0"], wgu1=gw["ffn_w_gu1"],
        wd0=gw["ffn_w_down0"].reshape(4, GUB, D), wd1=gw["ffn_w_down1"].reshape(4, GUB, D),
        w_dw=sm[:, 0:32, :].transpose(1, 0, 2).reshape(32, D), conv_norm=full_vec(32), b_dw=full_vec(33), ln_g=full_vec(34),
        ln_b=full_vec(35), b_pw2=full_vec(36), b_pw1=full_vec(37, 2))

    loss_part, grad_x, g = _local_step(x[0], loss_target[0], p)

    dbig = {
        "attn_w_qkv": g["wqkv"].reshape(D, NDEV, QKV // NDEV).transpose(1, 0, 2), "attn_w_o": g["wo"].reshape(NDEV, D // NDEV, D),
        "conv_w_pw1": g["wpw1"], "conv_w_pw2": g["wpw2"].reshape(NDEV, D // NDEV, D),
        "ffn_w_gu0": g["wgu0"], "ffn_w_gu1": g["wgu1"],
        "ffn_w_down0": g["wd0"].reshape(NDEV, DFF // NDEV, D), "ffn_w_down1": g["wd1"].reshape(NDEV, DFF // NDEV, D),
    }
    lane0 = (lax.broadcasted_iota(jnp.int32, (1, 128), 1) == 0).astype(F32)
    parts = _pack([g["attn_norm"], g["ffn_norm0"], g["ffn_norm1"], g["final_norm"], g["sink"], loss_part * lane0,
                   jnp.zeros((6, 128), F32), g["conv_norm"], g["b_dw"], g["ln_g"], g["ln_b"], g["b_pw2"], g["b_pw1"],
                   g["w_dw"].reshape(32, NDEV, 128)], 352)
    landed = _exchange([dbig[k] for k in BIG] + [parts], [True] * 8 + [False], "reduce_scatter")
    red = _sum_parts(landed[8])

    def shard_rows_of(row, n=1):
        return lax.dynamic_slice_in_dim(red, row + n * me, n, axis=0)

    gs = {
        "attn_norm": red[0:8].reshape(1, D), "ffn_norm": red[8:24].reshape(2, D), "final_norm": red[24:32].reshape(D),
        "attn_sink": red[32:33, :NH], "conv_norm": shard_rows_of(40), "conv_b_dw": shard_rows_of(48), "conv_ln_g": shard_rows_of(56),
        "conv_ln_b": shard_rows_of(64), "conv_b_pw2": shard_rows_of(72), "conv_b_pw1": shard_rows_of(80, 2).reshape(1, PWB),
        "conv_w_dw": lax.dynamic_index_in_dim(red[96:352].reshape(32, NDEV, 128), me, axis=1, keepdims=False)[None, :CW],
    }
    loss = red[33, 0]

    grads, deltas, new_m, new_v = dict(gs), {}, {}, {}
    ds, ms, vs = _adamw_small(_pack_small(w), _pack_small(gs), _pack_small(m), _pack_small(v))
    deltas.update(_unpack_small(ds, gs))
    new_m.update(_unpack_small(ms, gs))
    new_v.update(_unpack_small(vs, gs))
    gb, db, nmb, nvb = {}, {}, {}, {}
    for k, land in zip(BIG, landed[:8]):
        r = land.shape[1]
        tr = {1024: 256, 128: 128, 352: 176}[r]
        gb[k], db[k], nmb[k], nvb[k] = _reduce_adamw(land, wb[k], mb[k], vb[k], tr, "adamw_" + k)
    grads.update(_join_layers(gb))
    deltas.update(_join_layers(db))
    new_m.update(_join_layers(nmb))
    new_v.update(_join_layers(nvb))
    return (loss, grad_x[None], *[grads[k] for k in NAMES], *[deltas[k] for k in NAMES], *[new_m[k] for k in NAMES],
            *[new_v[k] for k in NAMES])
```

```python
import math

import jax
import jax.numpy as jnp
from jax import lax
from jax.experimental import pallas as pl
from jax.experimental.pallas import tpu as pltpu

F32 = jnp.float32
BF16 = jnp.bfloat16

D = 1024
NH = 16
NKV = 4
HD = 64
GROUP = NH // NKV
ROT = 16
THETA = 500000.0
BLK = 128
QKV = (NH + 2 * NKV) * HD
KOFF = NH * HD
VOFF = KOFF + NKV * HD
DFF = 2816
NDEV = 8
GUB = 2 * DFF // NDEV
PWB = 2 * D // NDEV
CW = 31
CPAD = 15
HALO = 16
EPS = 1e-6
NEG = -1e30
LR, B1, B2, AEPS, WD, STEP = 0.001, 0.9, 0.999, 1e-08, 0.01, 10

VMEM_LIMIT = 56 * 1024 * 1024
MESH = pl.DeviceIdType.MESH
WHOLE = pl.BlockSpec(memory_space=pltpu.VMEM)
ANY = pl.BlockSpec(memory_space=pl.ANY)


def _place():
    x, y, c = lax.axis_index("x"), lax.axis_index("y"), lax.axis_index("c")
    return x, y, c, 4 * x + 2 * y + c


def _peer(x, y, c, j):
    px = 1 - x if j & 4 else x
    py = 1 - y if j & 2 else y
    pc = 1 - c if j & 1 else c
    return (px, py, pc), 4 * px + 2 * py + pc


def _xchg_copies(src, dst, scatter, send, recv, loc):
    x, y, c, me = _place()
    out = []
    for k in range(len(src)):
        if scatter[k]:
            local = pltpu.make_async_copy(src[k].at[me], dst[k].at[0], loc.at[k])
        else:
            local = pltpu.make_async_copy(src[k], dst[k].at[me], loc.at[k])
        pairs = []
        for j in range(1, NDEV):
            peer, pidx = _peer(x, y, c, j)
            if scatter[k]:
                s, d_out, d_in = src[k].at[pidx], dst[k].at[j], dst[k].at[j]
            else:
                s, d_out, d_in = src[k], dst[k].at[me], dst[k].at[pidx]
            sems = dict(send_sem=send.at[k, j - 1], recv_sem=recv.at[k, j - 1], device_id=peer, device_id_type=MESH)
            pairs.append((pltpu.make_async_remote_copy(src_ref=s, dst_ref=d_out, **sems),
                          pltpu.make_async_remote_copy(src_ref=s, dst_ref=d_in, **sems)))
        out.append((local, pairs))
    return out


def _xchg_start(cps):
    for local, pairs in cps:
        local.start()
        for snd, _ in pairs:
            snd.start()


def _xchg_wait(cps):
    for local, pairs in cps:
        for _, rcv in pairs:
            rcv.wait_recv()
        for snd, _ in pairs:
            snd.wait_send()
        local.wait()


def _call(body, **kw):
    return pl.pallas_call(body, **kw)


def _pc(body, operands, *, name, in_specs, out_specs, out_shape, grid=None, scratch_shapes=(), xchg=None):
    kw = dict(name=name, compiler_params=pltpu.CompilerParams(
        dimension_semantics=None if grid is None else ("arbitrary",) * len(grid), vmem_limit_bytes=VMEM_LIMIT,
        has_side_effects=xchg is not None))
    if grid is not None:
        kw["grid"] = grid
    if xchg is None:
        outs = _call(body, in_specs=list(in_specs), out_specs=list(out_specs), out_shape=list(out_shape),
                     scratch_shapes=list(scratch_shapes), **kw)(*operands)
        return list(outs), []
    arrs, scatter = xchg
    nx, n_in, n_out, n_s = len(arrs), len(in_specs), len(out_specs), len(scratch_shapes)

    def wrapped(*refs):
        ins, refs = refs[:n_in], refs[n_in:]
        src, refs = refs[:nx], refs[nx:]
        outs, refs = refs[:n_out], refs[n_out:]
        dst, refs = refs[:nx], refs[nx:]
        scr, (send, recv, loc) = refs[:n_s], refs[n_s:]
        if grid is None:
            cps = _xchg_copies(src, dst, scatter, send, recv, loc)
            _xchg_start(cps)
            body(*ins, *outs, *scr)
            _xchg_wait(cps)
            return

        @pl.when(pl.program_id(0) == 0)
        def _():
            _xchg_start(_xchg_copies(src, dst, scatter, send, recv, loc))

        body(*ins, *outs, *scr)

        @pl.when(pl.program_id(0) == grid[0] - 1)
        def _():
            _xchg_wait(_xchg_copies(src, dst, scatter, send, recv, loc))

    xshape = [jax.ShapeDtypeStruct(a.shape if sc else (NDEV,) + a.shape, a.dtype) for a, sc in zip(arrs, scatter)]
    sems = [pltpu.SemaphoreType.DMA((nx, NDEV - 1)), pltpu.SemaphoreType.DMA((nx, NDEV - 1)), pltpu.SemaphoreType.DMA((nx,))]
    res = _call(wrapped, in_specs=list(in_specs) + [ANY] * nx, out_specs=list(out_specs) + [ANY] * nx,
                out_shape=list(out_shape) + xshape, scratch_shapes=list(scratch_shapes) + sems, **kw)(*operands, *arrs)
    return list(res[:n_out]), list(res[n_out:])


def _exchange(arrs, scatter, name):
    return _pc(lambda: None, [], name=name, in_specs=[], out_specs=[], out_shape=[], xchg=(arrs, scatter))[1]


def _rows(tm, width):
    return pl.BlockSpec((tm, width), lambda i: (i, 0))


def _cols(tm):
    return pl.BlockSpec((D, tm), lambda i: (0, i))


def _blk3(nb, tm, width):
    return pl.BlockSpec((nb, tm, width), lambda i: (0, i, 0))


def _acc(rows, width):
    return pl.BlockSpec((rows, width), lambda i: (0, 0))


def _sds(shape, dtype):
    return jax.ShapeDtypeStruct(shape, dtype)


def _dot(a, b):
    return jnp.dot(a.astype(BF16), b.astype(BF16), preferred_element_type=F32)


def _dot_nt(a, b):
    return lax.dot_general(a.astype(BF16), b.astype(BF16), (((1,), (1,)), ((), ())), preferred_element_type=F32)


def _dot_tn(a, b):
    return lax.dot_general(a.astype(BF16), b.astype(BF16), (((0,), (0,)), ((), ())), preferred_element_type=F32)


def _sigmoid(x):
    return 1.0 / (1.0 + jnp.exp(-x))


def _rms(x, g):
    r = lax.rsqrt(jnp.mean(x * x, axis=-1, keepdims=True) + EPS)
    xh = x * r
    return xh, r, xh * g


def _rms_bwd(dh, xh, r, g):
    dxh = dh * g
    dg = jnp.sum(dh * xh, axis=0, keepdims=True)
    dx = r * (dxh - xh * jnp.mean(dxh * xh, axis=-1, keepdims=True))
    return dx, dg


def _lanes(t, width):
    return jnp.tile(t, (1, width // t.shape[1]))


def _rope(z, c, sa, sb):
    w = z.shape[1]
    return z * _lanes(c, w) + pltpu.roll(z, w - 8, 1) * _lanes(sa, w) + pltpu.roll(z, 8, 1) * _lanes(sb, w)


def _rope_t(dz, c, sa, sb):
    w = dz.shape[1]
    return dz * _lanes(c, w) + pltpu.roll(dz * _lanes(sa, w), 8, 1) + pltpu.roll(dz * _lanes(sb, w), w - 8, 1)


def _rope_tables(t):
    pos = jnp.arange(t, dtype=F32)
    inv_freq = THETA ** (-jnp.arange(0, ROT, 2, dtype=F32) / ROT)
    ang = pos[:, None] * inv_freq[None, :]
    cos, sin = jnp.cos(ang), jnp.sin(ang)
    one = jnp.ones((t, HD - ROT), F32)
    zero = jnp.zeros((t, HD - ROT), F32)
    z8 = jnp.zeros((t, 8), F32)
    c = jnp.concatenate([cos, cos, one], axis=1)
    sa = jnp.concatenate([-sin, z8, zero], axis=1)
    sb = jnp.concatenate([z8, sin, zero], axis=1)
    return tuple(jnp.tile(a, (1, 2)) for a in (c, sa, sb))


def _attn_pre(x, g, wqkv, tabs, tm):
    t = x.shape[0]

    def body(x_ref, g_ref, w_ref, c_ref, sa_ref, sb_ref, qkv_ref):
        _, _, h = _rms(x_ref[...], g_ref[...])
        z = _dot(h, w_ref[...])
        c, sa, sb = c_ref[...], sa_ref[...], sb_ref[...]
        qkv_ref[:, :KOFF] = (_rope(z[:, :KOFF], c, sa, sb) * 0.125).astype(BF16)
        qkv_ref[:, KOFF:VOFF] = _rope(z[:, KOFF:VOFF], c, sa, sb).astype(BF16)
        qkv_ref[:, VOFF:] = z[:, VOFF:].astype(BF16)

    return _pc(body, [x, g, wqkv, *tabs], name="attn_pre", grid=(t // tm,),
               in_specs=[_rows(tm, D), _acc(1, D), WHOLE, _rows(tm, 128), _rows(tm, 128), _rows(tm, 128)],
               out_specs=[_rows(tm, QKV)], out_shape=[_sds((t, QKV), BF16)])[0][0]


def _attn_specs(nblk):
    kb, vb = KOFF // (NKV * HD), VOFF // (NKV * HD)
    prev = lambda i: jnp.maximum(i - 1, 0)
    nxt = lambda i: jnp.minimum(i + 1, nblk - 1)
    w = NKV * HD
    return [
        pl.BlockSpec((BLK, KOFF), lambda i: (i, 0)),
        pl.BlockSpec((BLK, w), lambda i: (prev(i), kb)),
        pl.BlockSpec((BLK, w), lambda i: (i, kb)),
        pl.BlockSpec((BLK, w), lambda i: (nxt(i), kb)),
        pl.BlockSpec((BLK, w), lambda i: (prev(i), vb)),
        pl.BlockSpec((BLK, w), lambda i: (i, vb)),
        pl.BlockSpec((BLK, w), lambda i: (nxt(i), vb)),
    ]


GR = GROUP * BLK


def _attn_mask(i, nblk):
    r = lax.broadcasted_iota(jnp.int32, (GR, 3 * BLK), 0) & (BLK - 1)
    c = lax.broadcasted_iota(jnp.int32, (GR, 3 * BLK), 1)
    lo = jnp.where(i == 0, BLK, 0)
    hi = jnp.where(i == nblk - 1, 2 * BLK, 3 * BLK)
    return (c >= r) & (c - 2 * BLK <= r) & (c >= lo) & (c < hi)


def _group(ref, kv):
    return jnp.concatenate([ref[:, (kv * GROUP + g) * HD:(kv * GROUP + g + 1) * HD] for g in range(GROUP)], axis=0)


def _group_sink(sink_ref, kv):
    return jnp.concatenate([jnp.full((BLK, 1), sink_ref[kv * GROUP + g], F32) for g in range(GROUP)], axis=0)


def _attn_probs(q_g, k_h, valid, sink_g):
    s = jnp.where(valid, _dot_nt(q_g, k_h), NEG)
    m = jnp.maximum(jnp.max(s, axis=1, keepdims=True), sink_g)
    e = jnp.exp(s - m)
    es = jnp.exp(sink_g - m)
    inv = 1.0 / (jnp.sum(e, axis=1, keepdims=True) + es)
    return e * inv, es * inv


def _attn_fwd(qkv, sink, xchg=None):
    t = qkv.shape[0]
    nblk = t // BLK

    def body(sink_ref, q_ref, kp, kc, kn, vp, vc, vn, o_ref):
        valid = _attn_mask(pl.program_id(0), nblk)
        k = jnp.concatenate([kp[...], kc[...], kn[...]], axis=0)
        v = jnp.concatenate([vp[...], vc[...], vn[...]], axis=0)
        for kv in range(NKV):
            p, _ = _attn_probs(_group(q_ref, kv), k[:, kv * HD:(kv + 1) * HD], valid, _group_sink(sink_ref, kv))
            o_g = _dot(p, v[:, kv * HD:(kv + 1) * HD]).astype(BF16)
            for g in range(GROUP):
                h = kv * GROUP + g
                o_ref[:, h * HD:(h + 1) * HD] = o_g[g * BLK:(g + 1) * BLK]

    outs, xo = _pc(body, [sink] + [qkv] * 7, name="attn_fwd", grid=(nblk,),
                   in_specs=[pl.BlockSpec(memory_space=pltpu.SMEM)] + _attn_specs(nblk),
                   out_specs=[pl.BlockSpec((BLK, D), lambda i: (i, 0))], out_shape=[_sds((t, D), BF16)], xchg=xchg)
    return outs[0], xo


def _attn_bwd(qkv, do, sink, xchg=None):
    t = qkv.shape[0]
    nblk = t // BLK
    w = NKV * HD

    def body(sink_ref, q_ref, kp, kc, kn, vp, vc, vn, do_ref, dq_ref, dk_ref, dv_ref, ds_ref):
        i = pl.program_id(0)

        @pl.when(i == 0)
        def _():
            dk_ref[...] = jnp.zeros_like(dk_ref)
            dv_ref[...] = jnp.zeros_like(dv_ref)
            ds_ref[...] = jnp.zeros_like(ds_ref)

        valid = _attn_mask(i, nblk)
        k = jnp.concatenate([kp[...], kc[...], kn[...]], axis=0)
        v = jnp.concatenate([vp[...], vc[...], vn[...]], axis=0)
        lane = lax.broadcasted_iota(jnp.int32, (1, 128), 1)
        dsink = jnp.zeros((1, 128), F32)
        rows = pl.ds(pl.multiple_of(i * BLK, BLK), 3 * BLK)
        for kv in range(NKV):
            k_h, v_h = k[:, kv * HD:(kv + 1) * HD], v[:, kv * HD:(kv + 1) * HD]
            q_g, do_g = _group(q_ref, kv), _group(do_ref, kv)
            p, ps = _attn_probs(q_g, k_h, valid, _group_sink(sink_ref, kv))
            dp = _dot_nt(do_g, v_h)
            delta = jnp.sum(p * dp, axis=1, keepdims=True)
            ds = (p * (dp - delta)).astype(BF16)
            dq_g = _dot(ds, k_h)
            dk_ref[rows, kv * HD:(kv + 1) * HD] += _dot_tn(ds, q_g)
            dv_ref[rows, kv * HD:(kv + 1) * HD] += _dot_tn(p, do_g)
            psd = ps * delta
            for g in range(GROUP):
                h = kv * GROUP + g
                dq_ref[:, h * HD:(h + 1) * HD] = dq_g[g * BLK:(g + 1) * BLK]
                dsink = dsink - jnp.where(lane == h, jnp.sum(psd[g * BLK:(g + 1) * BLK], axis=0, keepdims=True), 0.0)
        ds_ref[0:1, :] += dsink

    outs, xo = _pc(body, [sink] + [qkv] * 7 + [do], name="attn_bwd", grid=(nblk,),
                   in_specs=[pl.BlockSpec(memory_space=pltpu.SMEM)] + _attn_specs(nblk) + [pl.BlockSpec((BLK, D), lambda i: (i, 0))],
                   out_specs=[pl.BlockSpec((BLK, D), lambda i: (i, 0)), _acc(t + 2 * BLK, w), _acc(t + 2 * BLK, w), _acc(8, 128)],
                   out_shape=[_sds((t, D), F32), _sds((t + 2 * BLK, w), F32), _sds((t + 2 * BLK, w), F32), _sds((8, 128), F32)],
                   xchg=xchg)
    return outs, xo


def _attn_post(o, x, wo, tm):
    t = x.shape[0]

    def body(o_ref, x_ref, w_ref, out_ref, ot_ref):
        o = o_ref[...]
        out_ref[...] = x_ref[...] + _dot(o, w_ref[...])
        ot_ref[...] = o.T

    return _pc(body, [o, x, wo], name="attn_post", grid=(t // tm,),
               in_specs=[_rows(tm, D), _rows(tm, D), WHOLE], out_specs=[_rows(tm, D), _cols(tm)],
               out_shape=[_sds((t, D), F32), _sds((D, t), BF16)])[0]


def _attn_post_bwd(dx, wo, tm):
    t = dx.shape[0]

    def body(dx_ref, w_ref, do_ref):
        do_ref[...] = _dot_nt(dx_ref[...], w_ref[...]).astype(BF16)

    return _pc(body, [dx, wo], name="attn_post_bwd", grid=(t // tm,),
               in_specs=[_rows(tm, D), WHOLE], out_specs=[_rows(tm, D)], out_shape=[_sds((t, D), BF16)])[0][0]


def _attn_pre_bwd(dq, dk, dv, x, g, wqkv, tabs, dx_out, tm, xchg=None):
    t = x.shape[0]
    w = NKV * HD
    off = BLK // tm

    def body(dq_ref, dk_ref, dv_ref, x_ref, g_ref, w_ref, c_ref, sa_ref, sb_ref, dxo_ref, dqkv_ref, ht_ref, dx_ref, dg_ref):
        @pl.when(pl.program_id(0) == 0)
        def _():
            dg_ref[...] = jnp.zeros_like(dg_ref)

        c, sa, sb = c_ref[...], sa_ref[...], sb_ref[...]
        dqkv_ref[:, :KOFF] = _rope_t(dq_ref[...] * 0.125, c, sa, sb).astype(BF16)
        dqkv_ref[:, KOFF:VOFF] = _rope_t(dk_ref[...], c, sa, sb).astype(BF16)
        dqkv_ref[:, VOFF:] = dv_ref[...].astype(BF16)
        g = g_ref[...]
        xh, r, h = _rms(x_ref[...], g)
        ht_ref[...] = h.astype(BF16).T
        dh = _dot_nt(dqkv_ref[...], w_ref[...])
        dx, dg = _rms_bwd(dh, xh, r, g)
        dx_ref[...] = dxo_ref[...] + dx
        dg_ref[...] += dg

    return _pc(body, [dq, dk, dv, x, g, wqkv, *tabs, dx_out], name="attn_pre_bwd", grid=(t // tm,),
               in_specs=[_rows(tm, D), pl.BlockSpec((tm, w), lambda i: (i + off, 0)), pl.BlockSpec((tm, w), lambda i: (i + off, 0)),
                         _rows(tm, D), _acc(1, D), WHOLE, _rows(tm, 128), _rows(tm, 128), _rows(tm, 128), _rows(tm, D)],
               out_specs=[_rows(tm, QKV), _cols(tm), _rows(tm, D), _acc(1, D)],
               out_shape=[_sds((t, QKV), BF16), _sds((D, t), BF16), _sds((t, D), F32), _sds((1, D), F32)], xchg=xchg)


def _ffn_fwd(x, g, wgu, wd, tm, name, xchg=None):
    t = x.shape[0]

    def body(x_ref, g_ref, wgu_ref, wd_ref, gu_ref, out_ref):
        x = x_ref[...]
        _, _, h = _rms(x, g_ref[...])
        hb = h.astype(BF16)
        y = x
        for j in range(4):
            gj = _dot(hb, wgu_ref[j])
            uj = _dot(hb, wgu_ref[j + 4])
            gu_ref[j] = gj.astype(BF16)
            gu_ref[j + 4] = uj.astype(BF16)
            y = y + _dot(gj * _sigmoid(gj) * uj, wd_ref[j])
        out_ref[...] = y

    return _pc(body, [x, g, wgu, wd], name=name, grid=(t // tm,),
               in_specs=[_rows(tm, D), _acc(1, D), WHOLE, WHOLE], out_specs=[_blk3(NDEV, tm, GUB), _rows(tm, D)],
               out_shape=[_sds((NDEV, t, GUB), BF16), _sds((t, D), F32)], xchg=xchg)


def _ffn_bwd(dy, x, gu, g, wgu, wd, tm, name, xchg=None):
    t = x.shape[0]

    def body(dy_ref, x_ref, gu_ref, g_ref, wgu_ref, wd_ref, dgu_ref, act_ref, ht_ref, dx_ref, dg_ref):
        @pl.when(pl.program_id(0) == 0)
        def _():
            dg_ref[...] = jnp.zeros_like(dg_ref)

        dy = dy_ref[...]
        dyb = dy.astype(BF16)
        gn = g_ref[...]
        xh, r, h = _rms(x_ref[...], gn)
        ht_ref[...] = h.astype(BF16).T
        dh = jnp.zeros_like(dy)
        for j in range(4):
            gj = gu_ref[j].astype(F32)
            uj = gu_ref[j + 4].astype(F32)
            sg = _sigmoid(gj)
            silu = gj * sg
            act_ref[j] = (silu * uj).astype(BF16)
            dact = _dot_nt(dyb, wd_ref[j])
            dgj = (dact * uj * (sg * (1.0 + gj * (1.0 - sg)))).astype(BF16)
            duj = (dact * silu).astype(BF16)
            dgu_ref[j] = dgj
            dgu_ref[j + 4] = duj
            dh = dh + _dot_nt(dgj, wgu_ref[j]) + _dot_nt(duj, wgu_ref[j + 4])
        dx, dg = _rms_bwd(dh, xh, r, gn)
        dx_ref[...] = dy + dx
        dg_ref[...] += dg

    return _pc(body, [dy, x, gu, g, wgu, wd], name=name, grid=(t // tm,),
               in_specs=[_rows(tm, D), _rows(tm, D), _blk3(NDEV, tm, GUB), _acc(1, D), WHOLE, WHOLE],
               out_specs=[_blk3(NDEV, tm, GUB), _blk3(4, tm, GUB), _cols(tm), _rows(tm, D), _acc(1, D)],
               out_shape=[_sds((NDEV, t, GUB), BF16), _sds((4, t, GUB), BF16), _sds((D, t), BF16), _sds((t, D), F32),
                          _sds((1, D), F32)], xchg=xchg)


def _wgrad(at, b, tk, name):
    ma, t = at.shape
    nb, _, mb = b.shape
    nk = t // tk

    def body(a_ref, b_ref, out_ref, acc):
        k = pl.program_id(1)

        @pl.when(k == 0)
        def _():
            acc[...] = jnp.zeros_like(acc)

        acc[...] += _dot(a_ref[...], b_ref[0])

        @pl.when(k == nk - 1)
        def _():
            out_ref[0] = acc[...].astype(BF16)

    return _pc(body, [at, b], name=name, grid=(nb, nk),
               in_specs=[pl.BlockSpec((ma, tk), lambda j, k: (0, k)), pl.BlockSpec((1, tk, mb), lambda j, k: (j, k, 0))],
               out_specs=[pl.BlockSpec((1, ma, mb), lambda j, k: (j, 0, 0))], out_shape=[_sds((nb, ma, mb), BF16)],
               scratch_shapes=[pltpu.VMEM((ma, mb), F32)])[0][0]


def _conv_pre(x, g, w, b, tm):
    t = x.shape[0]

    def body(x_ref, g_ref, w_ref, b_ref, u_ref, glu_ref):
        _, _, h = _rms(x_ref[...], g_ref[...])
        hb = h.astype(BF16)
        for j in range(4):
            a = _dot(hb, w_ref[j]) + b_ref[:, j * PWB:(j + 1) * PWB]
            gt = _dot(hb, w_ref[j + 4]) + b_ref[:, D + j * PWB:D + (j + 1) * PWB]
            u_ref[j] = a.astype(BF16)
            u_ref[j + 4] = gt.astype(BF16)
            glu_ref[:, j * PWB:(j + 1) * PWB] = a * _sigmoid(gt)

    return _pc(body, [x, g, w, b], name="conv_pre", grid=(t // tm,),
               in_specs=[_rows(tm, D), _acc(1, D), WHOLE, _acc(1, 2 * D)], out_specs=[_blk3(NDEV, tm, PWB), _rows(tm, D)],
               out_shape=[_sds((NDEV, t, PWB), BF16), _sds((t, D), F32)])[0]


def _halo_specs(t, tm):
    per = tm // HALO
    last = t // HALO - 1
    return [
        pl.BlockSpec((HALO, D), lambda i: (jnp.maximum(i * per - 1, 0), 0)),
        _rows(tm, D),
        pl.BlockSpec((HALO, D), lambda i: (jnp.minimum((i + 1) * per, last), 0)),
    ]


def _fill_halo(buf, prev, cur, nxt, tm):
    i = pl.program_id(0)
    buf[0:HALO, :] = jnp.where(i == 0, 0.0, prev[...])
    buf[HALO:HALO + tm, :] = cur[...]
    buf[HALO + tm:, :] = jnp.where(i == pl.num_programs(0) - 1, 0.0, nxt[...])


CCH = 32
CLN = 256


def _conv_mid(glu, wdw, bdw, tm, xchg=None):
    t = glu.shape[0]

    def body(prev, cur, nxt, w_ref, b_ref, out_ref, buf):
        _fill_halo(buf, prev, cur, nxt, tm)
        for c0 in range(0, D, CLN):
            for r0 in range(0, tm, CCH):
                acc = jnp.broadcast_to(b_ref[:, c0:c0 + CLN], (CCH, CLN))
                for k in range(CW):
                    s = r0 + k + HALO - CPAD
                    acc = acc + w_ref[k:k + 1, c0:c0 + CLN] * buf[s:s + CCH, c0:c0 + CLN]
                out_ref[r0:r0 + CCH, c0:c0 + CLN] = acc

    outs, xo = _pc(body, [glu, glu, glu, wdw, bdw], name="conv_mid", grid=(t // tm,),
                   in_specs=_halo_specs(t, tm) + [_acc(32, D), _acc(1, D)], out_specs=[_rows(tm, D)],
                   out_shape=[_sds((t, D), F32)], scratch_shapes=[pltpu.VMEM((tm + 2 * HALO, D), F32)], xchg=xchg)
    return outs[0], xo


def _conv_mid_bwd(dcv, glu, wdw, tm, xchg=None):
    t = glu.shape[0]

    def body(dp, dc, dn, gp, gc, gn, w_ref, dglu_ref, dw_ref, dbuf, gbuf):
        @pl.when(pl.program_id(0) == 0)
        def _():
            dw_ref[...] = jnp.zeros_like(dw_ref)

        _fill_halo(dbuf, dp, dc, dn, tm)
        _fill_halo(gbuf, gp, gc, gn, tm)
        for c0 in range(0, D, CLN):
            dwk = [jnp.zeros((1, CLN), F32) for _ in range(CW)]
            for r0 in range(0, tm, CCH):
                d0 = dbuf[r0 + HALO:r0 + HALO + CCH, c0:c0 + CLN]
                acc = jnp.zeros((CCH, CLN), F32)
                for k in range(CW):
                    s = r0 - k + HALO + CPAD
                    acc = acc + w_ref[k:k + 1, c0:c0 + CLN] * dbuf[s:s + CCH, c0:c0 + CLN]
                    s = r0 + k + HALO - CPAD
                    dwk[k] = dwk[k] + jnp.sum(d0 * gbuf[s:s + CCH, c0:c0 + CLN], axis=0, keepdims=True)
                dglu_ref[r0:r0 + CCH, c0:c0 + CLN] = acc
            for k in range(CW):
                dw_ref[k:k + 1, c0:c0 + CLN] += dwk[k]

    return _pc(body, [dcv, dcv, dcv, glu, glu, glu, wdw], name="conv_mid_bwd", grid=(t // tm,),
               in_specs=_halo_specs(t, tm) + _halo_specs(t, tm) + [_acc(32, D)], out_specs=[_rows(tm, D), _acc(32, D)],
               out_shape=[_sds((t, D), F32), _sds((32, D), F32)],
               scratch_shapes=[pltpu.VMEM((tm + 2 * HALO, D), F32), pltpu.VMEM((tm + 2 * HALO, D), F32)], xchg=xchg)


def _ln(cv, lg, lb):
    mu = jnp.mean(cv, axis=-1, keepdims=True)
    cc = cv - mu
    rs = lax.rsqrt(jnp.mean(cc * cc, axis=-1, keepdims=True) + EPS)
    lh = cc * rs
    return lh, rs, lh * lg + lb


def _conv_post(cv, x, lg, lb, w2, b2, tm):
    t = x.shape[0]

    def body(cv_ref, x_ref, lg_ref, lb_ref, w_ref, b_ref, st_ref, out_ref):
        _, _, ln = _ln(cv_ref[...], lg_ref[...], lb_ref[...])
        s = (ln * _sigmoid(ln)).astype(BF16)
        st_ref[...] = s.T
        out_ref[...] = x_ref[...] + _dot(s, w_ref[...]) + b_ref[...]

    return _pc(body, [cv, x, lg, lb, w2, b2], name="conv_post", grid=(t // tm,),
               in_specs=[_rows(tm, D), _rows(tm, D), _acc(1, D), _acc(1, D), WHOLE, _acc(1, D)],
               out_specs=[_cols(tm), _rows(tm, D)], out_shape=[_sds((D, t), BF16), _sds((t, D), F32)])[0]


def _conv_post_bwd(dx, cv, lg, lb, w2, tm):
    t = dx.shape[0]

    def body(dx_ref, cv_ref, lg_ref, lb_ref, w_ref, dcv_ref, part_ref):
        @pl.when(pl.program_id(0) == 0)
        def _():
            part_ref[...] = jnp.zeros_like(part_ref)

        dx = dx_ref[...]
        lg = lg_ref[...]
        lh, rs, ln = _ln(cv_ref[...], lg, lb_ref[...])
        sg = _sigmoid(ln)
        dln = _dot_nt(dx, w_ref[...]) * (sg * (1.0 + ln * (1.0 - sg)))
        dlh = dln * lg
        dcv = rs * (dlh - jnp.mean(dlh, axis=-1, keepdims=True) - lh * jnp.mean(dlh * lh, axis=-1, keepdims=True))
        dcv_ref[...] = dcv
        part_ref[0:1, :] += jnp.sum(dln * lh, axis=0, keepdims=True)
        part_ref[1:2, :] += jnp.sum(dln, axis=0, keepdims=True)
        part_ref[2:3, :] += jnp.sum(dcv, axis=0, keepdims=True)
        part_ref[3:4, :] += jnp.sum(dx, axis=0, keepdims=True)

    return _pc(body, [dx, cv, lg, lb, w2], name="conv_post_bwd", grid=(t // tm,),
               in_specs=[_rows(tm, D), _rows(tm, D), _acc(1, D), _acc(1, D), WHOLE], out_specs=[_rows(tm, D), _acc(8, D)],
               out_shape=[_sds((t, D), F32), _sds((8, D), F32)])[0]


def _conv_pre_bwd(dglu, u, x, g, w, dx_out, tm):
    t = x.shape[0]

    def body(dglu_ref, u_ref, x_ref, g_ref, w_ref, dxo_ref, du_ref, ht_ref, dx_ref, dxt_ref, dg_ref, db_ref):
        @pl.when(pl.program_id(0) == 0)
        def _():
            dg_ref[...] = jnp.zeros_like(dg_ref)
            db_ref[...] = jnp.zeros_like(db_ref)

        gn = g_ref[...]
        xh, r, h = _rms(x_ref[...], gn)
        ht_ref[...] = h.astype(BF16).T
        dh = jnp.zeros_like(xh)
        for j in range(4):
            a = u_ref[j].astype(F32)
            sg = _sigmoid(u_ref[j + 4].astype(F32))
            dgl = dglu_ref[:, j * PWB:(j + 1) * PWB]
            da = dgl * sg
            dgt = dgl * a * sg * (1.0 - sg)
            db_ref[:, j * PWB:(j + 1) * PWB] += jnp.sum(da, axis=0, keepdims=True)
            db_ref[:, D + j * PWB:D + (j + 1) * PWB] += jnp.sum(dgt, axis=0, keepdims=True)
            da, dgt = da.astype(BF16), dgt.astype(BF16)
            du_ref[j] = da
            du_ref[j + 4] = dgt
            dh = dh + _dot_nt(da, w_ref[j]) + _dot_nt(dgt, w_ref[j + 4])
        dx, dg = _rms_bwd(dh, xh, r, gn)
        dx = dxo_ref[...] + dx
        dx_ref[...] = dx
        dxt_ref[...] = dx.astype(BF16).T
        dg_ref[...] += dg

    return _pc(body, [dglu, u, x, g, w, dx_out], name="conv_pre_bwd", grid=(t // tm,),
               in_specs=[_rows(tm, D), _blk3(NDEV, tm, PWB), _rows(tm, D), _acc(1, D), WHOLE, _rows(tm, D)],
               out_specs=[_blk3(NDEV, tm, PWB), _cols(tm), _rows(tm, D), _cols(tm), _acc(1, D), _acc(1, 2 * D)],
               out_shape=[_sds((NDEV, t, PWB), BF16), _sds((D, t), BF16), _sds((t, D), F32), _sds((D, t), BF16),
                          _sds((1, D), F32), _sds((1, 2 * D), F32)])[0]


def _final(x, g, tgt, tm):
    t = x.shape[0]

    def body(x_ref, g_ref, t_ref, dx_ref, dxt_ref, part_ref):
        @pl.when(pl.program_id(0) == 0)
        def _():
            part_ref[...] = jnp.zeros_like(part_ref)

        g = g_ref[...]
        xh, r, y = _rms(x_ref[...], g)
        err = y - t_ref[...]
        dx, dg = _rms_bwd(err * (1.0 / D), xh, r, g)
        dx_ref[...] = dx
        dxt_ref[...] = dx.astype(BF16).T
        part_ref[0:1, :] += dg
        tok = jnp.sum(err * err, axis=-1, keepdims=True) * (1.0 / D)
        lane = lax.broadcasted_iota(jnp.int32, (1, D), 1)
        part_ref[1:2, :] += jnp.where(lane == 0, 0.5 * jnp.sum(tok, axis=0, keepdims=True), 0.0)

    return _pc(body, [x, g, tgt], name="final", grid=(t // tm,),
               in_specs=[_rows(tm, D), _acc(1, D), _rows(tm, D)], out_specs=[_rows(tm, D), _cols(tm), _acc(8, D)],
               out_shape=[_sds((t, D), F32), _sds((D, t), BF16), _sds((8, D), F32)])[0]


def _adamw(w, g, m, v):
    m = B1 * m + (1.0 - B1) * g
    v = B2 * v + (1.0 - B2) * (g * g)
    m_hat = m / (1.0 - B1 ** STEP)
    v_hat = v / (1.0 - B2 ** STEP)
    return -LR * (m_hat / (jnp.sqrt(v_hat) + AEPS) + WD * w), m, v


def _reduce_adamw(land, w, m, v, tr, name):
    _, r, c = land.shape

    def body(l_ref, w_ref, m_ref, v_ref, g_ref, d_ref, nm_ref, nv_ref):
        g = l_ref[0].astype(F32)
        for j in range(1, NDEV):
            g = g + l_ref[j].astype(F32)
        g_ref[...] = g
        d_ref[...], nm_ref[...], nv_ref[...] = _adamw(w_ref[...], g, m_ref[...], v_ref[...])

    return _pc(body, [land, w, m, v], name=name, grid=(r // tr,),
               in_specs=[_blk3(NDEV, tr, c)] + [_rows(tr, c)] * 3, out_specs=[_rows(tr, c)] * 4,
               out_shape=[_sds((r, c), F32)] * 4)[0]


def _sum_parts(parts):
    _, r, c = parts.shape

    def body(p_ref, out_ref):
        s = p_ref[0]
        for j in range(1, NDEV):
            s = s + p_ref[j]
        out_ref[...] = s

    return _pc(body, [parts], name="sum_parts", in_specs=[WHOLE], out_specs=[WHOLE], out_shape=[_sds((r, c), F32)])[0][0]


def _adamw_small(w, g, m, v):
    def body(w_ref, g_ref, m_ref, v_ref, d_ref, nm_ref, nv_ref):
        d_ref[...], nm_ref[...], nv_ref[...] = _adamw(w_ref[...], g_ref[...], m_ref[...], v_ref[...])

    return _pc(body, [w, g, m, v], name="adamw_small", in_specs=[WHOLE] * 4, out_specs=[WHOLE] * 3,
               out_shape=[_sds(w.shape, F32)] * 3)[0]


def _rows128(a):
    a = a.astype(F32)
    if a.shape[-1] % 128:
        a = jnp.pad(a, [(0, 0)] * (a.ndim - 1) + [(0, 128 - a.shape[-1] % 128)])
    return a.reshape(-1, 128)


def _pack(arrs, rows):
    p = jnp.concatenate([_rows128(a) for a in arrs], axis=0)
    return jnp.pad(p, ((0, rows - p.shape[0]), (0, 0)))


SMALL = ("attn_norm", "ffn_norm", "final_norm", "attn_sink", "conv_norm", "conv_b_dw", "conv_ln_g", "conv_ln_b", "conv_b_pw2",
         "conv_b_pw1", "conv_w_dw")
SMALL_ROWS = 72


def _pack_small(d):
    return _pack([d[k] for k in SMALL], SMALL_ROWS)


def _unpack_small(p, like):
    out, r = {}, 0
    for k in SMALL:
        shp = like[k].shape
        n = -(-shp[-1] // 128) * (math.prod(shp[:-1]))
        blk = p[r:r + n]
        if shp[-1] % 128:
            blk = blk[:, :shp[-1]]
        out[k] = blk.reshape(shp)
        r += n
    return out


BIG = ("attn_w_qkv", "attn_w_o", "conv_w_pw1", "conv_w_pw2", "ffn_w_gu0", "ffn_w_gu1", "ffn_w_down0", "ffn_w_down1")
NAMES = ("attn_norm", "attn_w_qkv", "attn_w_o", "attn_sink", "conv_norm", "conv_w_pw1", "conv_b_pw1", "conv_w_dw", "conv_b_dw",
         "conv_ln_g", "conv_ln_b", "conv_w_pw2", "conv_b_pw2", "ffn_norm", "ffn_w_gu", "ffn_w_down", "final_norm")
TM = 256
TK = 2048


def _split_layers(d):
    return {
        "attn_w_qkv": d["attn_w_qkv"][0], "attn_w_o": d["attn_w_o"][0], "conv_w_pw1": d["conv_w_pw1"][0], "conv_w_pw2": d["conv_w_pw2"][0],
        "ffn_w_gu0": d["ffn_w_gu"][0], "ffn_w_gu1": d["ffn_w_gu"][1], "ffn_w_down0": d["ffn_w_down"][0], "ffn_w_down1": d["ffn_w_down"][1],
    }


def _join_layers(d):
    return {
        "attn_w_qkv": d["attn_w_qkv"][None], "attn_w_o": d["attn_w_o"][None], "conv_w_pw1": d["conv_w_pw1"][None],
        "conv_w_pw2": d["conv_w_pw2"][None], "ffn_w_gu": jnp.stack([d["ffn_w_gu0"], d["ffn_w_gu1"]]),
        "ffn_w_down": jnp.stack([d["ffn_w_down0"], d["ffn_w_down1"]]),
    }


def _down_blocks(dwt):
    return dwt.reshape(4, D, 2, GUB // 2).transpose(0, 2, 3, 1).reshape(NDEV, DFF // NDEV, D)


def kernel(x, attn_norm, attn_w_qkv, attn_w_o, attn_sink, conv_norm, conv_w_pw1, conv_b_pw1, conv_w_dw, conv_b_dw, conv_ln_g, conv_ln_b, conv_w_pw2, conv_b_pw2, ffn_norm, ffn_w_gu, ffn_w_down, final_norm, loss_target, m_attn_norm, m_attn_w_qkv, m_attn_w_o, m_attn_sink, m_conv_norm, m_conv_w_pw1, m_conv_b_pw1, m_conv_w_dw, m_conv_b_dw, m_conv_ln_g, m_conv_ln_b, m_conv_w_pw2, m_conv_b_pw2, m_ffn_norm, m_ffn_w_gu, m_ffn_w_down, m_final_norm, v_attn_norm, v_attn_w_qkv, v_attn_w_o, v_attn_sink, v_conv_norm, v_conv_w_pw1, v_conv_b_pw1, v_conv_w_dw, v_conv_b_dw, v_conv_ln_g, v_conv_ln_b, v_conv_w_pw2, v_conv_b_pw2, v_ffn_norm, v_ffn_w_gu, v_ffn_w_down, v_final_norm):
    w = dict(zip(NAMES, (attn_norm, attn_w_qkv, attn_w_o, attn_sink, conv_norm, conv_w_pw1, conv_b_pw1, conv_w_dw, conv_b_dw, conv_ln_g,
                         conv_ln_b, conv_w_pw2, conv_b_pw2, ffn_norm, ffn_w_gu, ffn_w_down, final_norm)))
    m = dict(zip(NAMES, (m_attn_norm, m_attn_w_qkv, m_attn_w_o, m_attn_sink, m_conv_norm, m_conv_w_pw1, m_conv_b_pw1, m_conv_w_dw,
                         m_conv_b_dw, m_conv_ln_g, m_conv_ln_b, m_conv_w_pw2, m_conv_b_pw2, m_ffn_norm, m_ffn_w_gu, m_ffn_w_down,
                         m_final_norm)))
    v = dict(zip(NAMES, (v_attn_norm, v_attn_w_qkv, v_attn_w_o, v_attn_sink, v_conv_norm, v_conv_w_pw1, v_conv_b_pw1, v_conv_w_dw,
                         v_conv_b_dw, v_conv_ln_g, v_conv_ln_b, v_conv_w_pw2, v_conv_b_pw2, v_ffn_norm, v_ffn_w_gu, v_ffn_w_down,
                         v_final_norm)))
    me = 4 * lax.axis_index("x") + 2 * lax.axis_index("y") + lax.axis_index("c")
    wb, mb, vb = _split_layers(w), _split_layers(m), _split_layers(v)
    sh = {k: wb[k].astype(BF16) for k in BIG}
    x0, tgt = x[0], loss_target[0]
    t = x0.shape[0]
    tabs = _rope_tables(t)
    g_a, sink, g_f0, g_f1, g_fin = w["attn_norm"], w["attn_sink"][0], w["ffn_norm"][0:1], w["ffn_norm"][1:2], w["final_norm"][None]
    gather, scatter = False, True

    shard_rows = _pack([w["conv_w_dw"][0], jnp.zeros((1, 128), F32), w["conv_norm"], w["conv_b_dw"], w["conv_ln_g"], w["conv_ln_b"],
                        w["conv_b_pw2"], w["conv_b_pw1"]], 40)
    wqkv_g, wo_g, sm = _exchange([sh["attn_w_qkv"], sh["attn_w_o"], shard_rows], [gather] * 3, "gather_attn")
    wqkv, wo = wqkv_g.transpose(1, 0, 2).reshape(D, QKV), wo_g.reshape(D, D)

    def full_vec(row, n=1):
        return sm[:, row:row + n, :].reshape(1, NDEV * n * 128)

    w_dw, g_c, b_dw, ln_g, ln_b = sm[:, 0:32, :].transpose(1, 0, 2).reshape(32, D), full_vec(32), full_vec(33), full_vec(34), full_vec(35)
    b_pw2, b_pw1 = full_vec(36), full_vec(37, 2)

    qkv = _attn_pre(x0, g_a, wqkv, tabs, TM)
    o, (wgu0, wd0_g) = _attn_fwd(qkv, sink, xchg=([sh["ffn_w_gu0"], sh["ffn_w_down0"]], [gather] * 2))
    wd0 = wd0_g.reshape(4, GUB, D)
    x1, o_t = _attn_post(o, x0, wo, TM)
    (gu0, x2), (wpw1, wpw2_g, wgu1) = _ffn_fwd(x1, g_f0, wgu0, wd0, TM, "ffn_fwd0",
                                               xchg=([sh["conv_w_pw1"], sh["conv_w_pw2"], sh["ffn_w_gu1"]], [gather] * 3))
    wpw2 = wpw2_g.reshape(D, D)
    u, glu = _conv_pre(x2, g_c, wpw1, b_pw1, TM)
    cv, (wd1_g,) = _conv_mid(glu, w_dw, b_dw, TM, xchg=([sh["ffn_w_down1"]], [gather]))
    wd1 = wd1_g.reshape(4, GUB, D)
    s_t, x3 = _conv_post(cv, x2, ln_g, ln_b, wpw2, b_pw2, TM)
    (gu1, x4), _ = _ffn_fwd(x3, g_f1, wgu1, wd1, TM, "ffn_fwd1")
    dx4, dx4_t, fin = _final(x4, g_fin, tgt, TM)

    land = {}
    (dgu1, act1, h3_t, dx3, dg_f1), _ = _ffn_bwd(dx4, x3, gu1, g_f1, wgu1, wd1, TM, "ffn_bwd1")
    dwgu1 = _wgrad(h3_t, dgu1, TK, "dwgu1")
    dwd1 = _down_blocks(_wgrad(dx4_t, act1, TK, "dwd1"))
    dcv, cpart = _conv_post_bwd(dx3, cv, ln_g, ln_b, wpw2, TM)
    dwpw2 = _wgrad(s_t, dx3[None], TK, "dwpw2").reshape(NDEV, D // NDEV, D)
    (dglu, dw_dw), (land["ffn_w_gu1"], land["ffn_w_down1"]) = _conv_mid_bwd(dcv, glu, w_dw, TM, xchg=([dwgu1, dwd1], [scatter] * 2))
    du, h2_t, dx2, dx2_t, dg_c, db_pw1 = _conv_pre_bwd(dglu, u, x2, g_c, wpw1, dx3, TM)
    dwpw1 = _wgrad(h2_t, du, TK, "dwpw1")
    (dgu0, act0, h1_t, dx1, dg_f0), (land["conv_w_pw1"], land["conv_w_pw2"]) = _ffn_bwd(
        dx2, x1, gu0, g_f0, wgu0, wd0, TM, "ffn_bwd0", xchg=([dwpw1, dwpw2], [scatter] * 2))
    dwgu0 = _wgrad(h1_t, dgu0, TK, "dwgu0")
    dwd0 = _down_blocks(_wgrad(dx2_t, act0, TK, "dwd0"))
    do = _attn_post_bwd(dx1, wo, TM)
    dwo = _wgrad(o_t, dx1[None], TK, "dwo").reshape(NDEV, D // NDEV, D)
    (dq, dk, dv, dsink), (land["ffn_w_gu0"], land["ffn_w_down0"]) = _attn_bwd(qkv, do, sink, xchg=([dwgu0, dwd0], [scatter] * 2))
    (dqkv, h0_t, dx0, dg_a), (land["attn_w_o"],) = _attn_pre_bwd(dq, dk, dv, x0, g_a, wqkv, tabs, dx1, BLK, xchg=([dwo], [scatter]))
    dwqkv = _wgrad(h0_t, dqkv[None], TK, "dwqkv")[0].reshape(D, NDEV, QKV // NDEV).transpose(1, 0, 2)

    lane0 = (lax.broadcasted_iota(jnp.int32, (1, 128), 1) == 0).astype(F32)
    parts = _pack([dg_a, dg_f0, dg_f1, fin[0:1], dsink[0:1, :NH], fin[1, 0] * lane0, jnp.zeros((6, 128), F32), dg_c, cpart[2:3],
                   cpart[0:1], cpart[1:2], cpart[3:4], db_pw1, dw_dw.reshape(32, NDEV, 128)], 352)
    land["attn_w_qkv"], parts_g = _exchange([dwqkv, parts], [scatter, gather], "scatter_attn")
    red = _sum_parts(parts_g)

    def shard_rows_of(row, n=1):
        return lax.dynamic_slice_in_dim(red, row + n * me, n, axis=0)

    gs = {
        "attn_norm": red[0:8].reshape(1, D), "ffn_norm": red[8:24].reshape(2, D), "final_norm": red[24:32].reshape(D),
        "attn_sink": red[32:33, :NH], "conv_norm": shard_rows_of(40), "conv_b_dw": shard_rows_of(48), "conv_ln_g": shard_rows_of(56),
        "conv_ln_b": shard_rows_of(64), "conv_b_pw2": shard_rows_of(72), "conv_b_pw1": shard_rows_of(80, 2).reshape(1, PWB),
        "conv_w_dw": lax.dynamic_index_in_dim(red[96:352].reshape(32, NDEV, 128), me, axis=1, keepdims=False)[None, :CW],
    }
    loss = red[33, 0]

    grads, deltas, new_m, new_v = dict(gs), {}, {}, {}
    ds, ms, vs = _adamw_small(_pack_small(w), _pack_small(gs), _pack_small(m), _pack_small(v))
    deltas.update(_unpack_small(ds, gs))
    new_m.update(_unpack_small(ms, gs))
    new_v.update(_unpack_small(vs, gs))
    gb, db, nmb, nvb = {}, {}, {}, {}
    for k in BIG:
        tr = {1024: 256, 128: 128, 352: 176}[land[k].shape[1]]
        gb[k], db[k], nmb[k], nvb[k] = _reduce_adamw(land[k], wb[k], mb[k], vb[k], tr, "adamw_" + k)
    grads.update(_join_layers(gb))
    deltas.update(_join_layers(db))
    new_m.update(_join_layers(nmb))
    new_v.update(_join_layers(nvb))
    return (loss, dx0[None], *[grads[k] for k in NAMES], *[deltas[k] for k in NAMES], *[new_m[k] for k in NAMES],
            *[new_v[k] for k in NAMES])
```

```python
import math

import jax
import jax.numpy as jnp
from jax import lax
from jax.experimental import pallas as pl
from jax.experimental.pallas import tpu as pltpu

F32 = jnp.float32
BF16 = jnp.bfloat16

D = 1024
NH = 16
NKV = 4
HD = 64
GROUP = NH // NKV
ROT = 16
THETA = 500000.0
BLK = 128
QKV = (NH + 2 * NKV) * HD
KOFF = NH * HD
VOFF = KOFF + NKV * HD
DFF = 2816
NDEV = 8
GUB = 2 * DFF // NDEV
PWB = 2 * D // NDEV
CW = 31
CPAD = 15
HALO = 16
EPS = 1e-6
NEG = -1e30
LR, B1, B2, AEPS, WD, STEP = 0.001, 0.9, 0.999, 1e-08, 0.01, 10

VMEM_LIMIT = 56 * 1024 * 1024
MESH = pl.DeviceIdType.MESH
WHOLE = pl.BlockSpec(memory_space=pltpu.VMEM)
ANY = pl.BlockSpec(memory_space=pl.ANY)


def _place():
    x, y, c = lax.axis_index("x"), lax.axis_index("y"), lax.axis_index("c")
    return x, y, c, 4 * x + 2 * y + c


def _peer(x, y, c, j):
    px = 1 - x if j & 4 else x
    py = 1 - y if j & 2 else y
    pc = 1 - c if j & 1 else c
    return (px, py, pc), 4 * px + 2 * py + pc


def _xchg_copies(src, dst, scatter, send, recv, loc):
    x, y, c, me = _place()
    out = []
    for k in range(len(src)):
        if scatter[k]:
            local = pltpu.make_async_copy(src[k].at[me], dst[k].at[0], loc.at[k])
        else:
            local = pltpu.make_async_copy(src[k], dst[k].at[me], loc.at[k])
        pairs = []
        for j in range(1, NDEV):
            peer, pidx = _peer(x, y, c, j)
            if scatter[k]:
                s, d_out, d_in = src[k].at[pidx], dst[k].at[j], dst[k].at[j]
            else:
                s, d_out, d_in = src[k], dst[k].at[me], dst[k].at[pidx]
            sems = dict(send_sem=send.at[k, j - 1], recv_sem=recv.at[k, j - 1], device_id=peer, device_id_type=MESH)
            pairs.append((pltpu.make_async_remote_copy(src_ref=s, dst_ref=d_out, **sems),
                          pltpu.make_async_remote_copy(src_ref=s, dst_ref=d_in, **sems)))
        out.append((local, pairs))
    return out


def _xchg_start(cps):
    for local, pairs in cps:
        local.start()
        for snd, _ in pairs:
            snd.start()


def _xchg_wait(cps):
    for local, pairs in cps:
        for _, rcv in pairs:
            rcv.wait_recv()
        for snd, _ in pairs:
            snd.wait_send()
        local.wait()


def _call(body, **kw):
    return pl.pallas_call(body, **kw)


def _pc(body, operands, *, name, in_specs, out_specs, out_shape, grid=None, scratch_shapes=(), xchg=None):
    kw = dict(name=name, compiler_params=pltpu.CompilerParams(
        dimension_semantics=None if grid is None else ("arbitrary",) * len(grid), vmem_limit_bytes=VMEM_LIMIT,
        has_side_effects=xchg is not None))
    if grid is not None:
        kw["grid"] = grid
    if xchg is None:
        outs = _call(body, in_specs=list(in_specs), out_specs=list(out_specs), out_shape=list(out_shape),
                     scratch_shapes=list(scratch_shapes), **kw)(*operands)
        return list(outs), []
    arrs, scatter = xchg
    nx, n_in, n_out, n_s = len(arrs), len(in_specs), len(out_specs), len(scratch_shapes)

    def wrapped(*refs):
        ins, refs = refs[:n_in], refs[n_in:]
        src, refs = refs[:nx], refs[nx:]
        outs, refs = refs[:n_out], refs[n_out:]
        dst, refs = refs[:nx], refs[nx:]
        scr, (send, recv, loc) = refs[:n_s], refs[n_s:]
        if grid is None:
            cps = _xchg_copies(src, dst, scatter, send, recv, loc)
            _xchg_start(cps)
            body(*ins, *outs, *scr)
            _xchg_wait(cps)
            return

        @pl.when(pl.program_id(0) == 0)
        def _():
            _xchg_start(_xchg_copies(src, dst, scatter, send, recv, loc))

        body(*ins, *outs, *scr)

        @pl.when(pl.program_id(0) == grid[0] - 1)
        def _():
            _xchg_wait(_xchg_copies(src, dst, scatter, send, recv, loc))

    xshape = [jax.ShapeDtypeStruct(a.shape if sc else (NDEV,) + a.shape, a.dtype) for a, sc in zip(arrs, scatter)]
    sems = [pltpu.SemaphoreType.DMA((nx, NDEV - 1)), pltpu.SemaphoreType.DMA((nx, NDEV - 1)), pltpu.SemaphoreType.DMA((nx,))]
    res = _call(wrapped, in_specs=list(in_specs) + [ANY] * nx, out_specs=list(out_specs) + [ANY] * nx,
                out_shape=list(out_shape) + xshape, scratch_shapes=list(scratch_shapes) + sems, **kw)(*operands, *arrs)
    return list(res[:n_out]), list(res[n_out:])


def _exchange(arrs, scatter, name):
    return _pc(lambda: None, [], name=name, in_specs=[], out_specs=[], out_shape=[], xchg=(arrs, scatter))[1]


def _rows(tm, width):
    return pl.BlockSpec((tm, width), lambda i: (i, 0))


def _cols(tm):
    return pl.BlockSpec((D, tm), lambda i: (0, i))


def _blk3(nb, tm, width):
    return pl.BlockSpec((nb, tm, width), lambda i: (0, i, 0))


def _acc(rows, width):
    return pl.BlockSpec((rows, width), lambda i: (0, 0))


def _sds(shape, dtype):
    return jax.ShapeDtypeStruct(shape, dtype)


def _dot(a, b):
    return jnp.dot(a.astype(BF16), b.astype(BF16), preferred_element_type=F32)


def _dot_nt(a, b):
    return lax.dot_general(a.astype(BF16), b.astype(BF16), (((1,), (1,)), ((), ())), preferred_element_type=F32)


def _dot_tn(a, b):
    return lax.dot_general(a.astype(BF16), b.astype(BF16), (((0,), (0,)), ((), ())), preferred_element_type=F32)


def _sigmoid(x):
    return 1.0 / (1.0 + jnp.exp(-x))


def _rms(x, g):
    r = lax.rsqrt(jnp.mean(x * x, axis=-1, keepdims=True) + EPS)
    xh = x * r
    return xh, r, xh * g


def _rms_bwd(dh, xh, r, g):
    dxh = dh * g
    dg = jnp.sum(dh * xh, axis=0, keepdims=True)
    dx = r * (dxh - xh * jnp.mean(dxh * xh, axis=-1, keepdims=True))
    return dx, dg


def _lanes(t, width):
    return jnp.tile(t, (1, width // t.shape[1]))


def _rope(z, c, sa, sb):
    w = z.shape[1]
    return z * _lanes(c, w) + pltpu.roll(z, w - 8, 1) * _lanes(sa, w) + pltpu.roll(z, 8, 1) * _lanes(sb, w)


def _rope_t(dz, c, sa, sb):
    w = dz.shape[1]
    return dz * _lanes(c, w) + pltpu.roll(dz * _lanes(sa, w), 8, 1) + pltpu.roll(dz * _lanes(sb, w), w - 8, 1)


def _rope_tables(t):
    pos = jnp.arange(t, dtype=F32)
    inv_freq = THETA ** (-jnp.arange(0, ROT, 2, dtype=F32) / ROT)
    ang = pos[:, None] * inv_freq[None, :]
    cos, sin = jnp.cos(ang), jnp.sin(ang)
    one = jnp.ones((t, HD - ROT), F32)
    zero = jnp.zeros((t, HD - ROT), F32)
    z8 = jnp.zeros((t, 8), F32)
    c = jnp.concatenate([cos, cos, one], axis=1)
    sa = jnp.concatenate([-sin, z8, zero], axis=1)
    sb = jnp.concatenate([z8, sin, zero], axis=1)
    return tuple(jnp.tile(a, (1, 2)) for a in (c, sa, sb))


KVW = NKV * HD
GW = GROUP * BLK


def _attn_pre(x, g, wqkv, tabs, tm):
    t = x.shape[0]

    def body(x_ref, g_ref, w_ref, c_ref, sa_ref, sb_ref, qt_ref, kv_ref):
        _, _, h = _rms(x_ref[...], g_ref[...])
        z = _dot(h, w_ref[...])
        c, sa, sb = c_ref[...], sa_ref[...], sb_ref[...]
        qt_ref[...] = (_rope(z[:, :KOFF], c, sa, sb) * 0.125).T.astype(BF16)
        kv_ref[:, :KVW] = _rope(z[:, KOFF:VOFF], c, sa, sb).astype(BF16)
        kv_ref[:, KVW:] = z[:, VOFF:].astype(BF16)

    return _pc(body, [x, g, wqkv, *tabs], name="attn_pre", grid=(t // tm,),
               in_specs=[_rows(tm, D), _acc(1, D), WHOLE, _rows(tm, 128), _rows(tm, 128), _rows(tm, 128)],
               out_specs=[_cols(tm), _rows(tm, 2 * KVW)], out_shape=[_sds((D, t), BF16), _sds((t, 2 * KVW), BF16)])[0]


def _attn_specs(nblk):
    prev = lambda i: jnp.maximum(i - 1, 0)
    nxt = lambda i: jnp.minimum(i + 1, nblk - 1)
    return [
        pl.BlockSpec((D, BLK), lambda i: (0, i)),
        pl.BlockSpec((BLK, KVW), lambda i: (prev(i), 0)),
        pl.BlockSpec((BLK, KVW), lambda i: (i, 0)),
        pl.BlockSpec((BLK, KVW), lambda i: (nxt(i), 0)),
        pl.BlockSpec((BLK, KVW), lambda i: (prev(i), 1)),
        pl.BlockSpec((BLK, KVW), lambda i: (i, 1)),
        pl.BlockSpec((BLK, KVW), lambda i: (nxt(i), 1)),
    ]


def _attn_mask(i, nblk):
    c = lax.broadcasted_iota(jnp.int32, (3 * BLK, GW), 0)
    r = lax.broadcasted_iota(jnp.int32, (3 * BLK, GW), 1) & (BLK - 1)
    lo = jnp.where(i == 0, BLK, 0)
    hi = jnp.where(i == nblk - 1, 2 * BLK, 3 * BLK)
    return (c >= r) & (c - 2 * BLK <= r) & (c >= lo) & (c < hi)


def _group(ref, kv):
    return jnp.concatenate([ref[(kv * GROUP + g) * HD:(kv * GROUP + g + 1) * HD, :] for g in range(GROUP)], axis=1)


def _group_sink(sink_ref, kv):
    return jnp.concatenate([jnp.full((1, BLK), sink_ref[kv * GROUP + g], F32) for g in range(GROUP)], axis=1)


def _attn_probs(k_h, qt_g, valid, sink_g):
    s = jnp.where(valid, _dot(k_h, qt_g), NEG)
    m = jnp.maximum(jnp.max(s, axis=0, keepdims=True), sink_g)
    e = jnp.exp(s - m)
    es = jnp.exp(sink_g - m)
    inv = 1.0 / (jnp.sum(e, axis=0, keepdims=True) + es)
    return e * inv, es * inv


def _attn_fwd(qt, kv, sink, xchg=None):
    t = kv.shape[0]
    nblk = t // BLK

    def body(sink_ref, qt_ref, kp, kc, kn, vp, vc, vn, ot_ref):
        valid = _attn_mask(pl.program_id(0), nblk)
        k = jnp.concatenate([kp[...], kc[...], kn[...]], axis=0)
        v = jnp.concatenate([vp[...], vc[...], vn[...]], axis=0)
        for h in range(NKV):
            p, _ = _attn_probs(k[:, h * HD:(h + 1) * HD], _group(qt_ref, h), valid, _group_sink(sink_ref, h))
            ot_g = _dot_tn(v[:, h * HD:(h + 1) * HD], p).astype(BF16)
            for g in range(GROUP):
                ot_ref[(h * GROUP + g) * HD:(h * GROUP + g + 1) * HD, :] = ot_g[:, g * BLK:(g + 1) * BLK]

    outs, xo = _pc(body, [sink, qt] + [kv] * 6, name="attn_fwd", grid=(nblk,),
                   in_specs=[pl.BlockSpec(memory_space=pltpu.SMEM)] + _attn_specs(nblk),
                   out_specs=[pl.BlockSpec((D, BLK), lambda i: (0, i))], out_shape=[_sds((D, t), BF16)], xchg=xchg)
    return outs[0], xo


def _attn_bwd(qt, kv, dot, sink, pad, xchg=None):
    t = kv.shape[0]
    nblk = t // BLK

    def body(sink_ref, qt_ref, kp, kc, kn, vp, vc, vn, dot_ref, dqt_ref, dk_ref, dv_ref, ds_ref):
        i = pl.program_id(0)

        @pl.when(i == 0)
        def _():
            dk_ref[...] = jnp.zeros_like(dk_ref)
            dv_ref[...] = jnp.zeros_like(dv_ref)
            ds_ref[...] = jnp.zeros_like(ds_ref)

        valid = _attn_mask(i, nblk)
        k = jnp.concatenate([kp[...], kc[...], kn[...]], axis=0)
        v = jnp.concatenate([vp[...], vc[...], vn[...]], axis=0)
        lane = lax.broadcasted_iota(jnp.int32, (1, 128), 1)
        dsink = jnp.zeros((1, 128), F32)
        rows = pl.ds(pl.multiple_of(i * BLK + (pad - BLK), BLK), 3 * BLK)
        for h in range(NKV):
            k_h, v_h = k[:, h * HD:(h + 1) * HD], v[:, h * HD:(h + 1) * HD]
            qt_g, dot_g = _group(qt_ref, h), _group(dot_ref, h)
            p, ps = _attn_probs(k_h, qt_g, valid, _group_sink(sink_ref, h))
            dp = _dot(v_h, dot_g)
            delta = jnp.sum(p * dp, axis=0, keepdims=True)
            ds = (p * (dp - delta)).astype(BF16)
            dqt_g = _dot_tn(k_h, ds)
            dk_ref[rows, h * HD:(h + 1) * HD] += _dot_nt(ds, qt_g)
            dv_ref[rows, h * HD:(h + 1) * HD] += _dot_nt(p, dot_g)
            psd = ps * delta
            for g in range(GROUP):
                n = h * GROUP + g
                dqt_ref[n * HD:(n + 1) * HD, :] = dqt_g[:, g * BLK:(g + 1) * BLK]
                dsink = dsink - jnp.where(lane == n, jnp.sum(psd[:, g * BLK:(g + 1) * BLK], axis=1, keepdims=True), 0.0)
        ds_ref[0:1, :] += dsink

    outs, xo = _pc(body, [sink, qt] + [kv] * 6 + [dot], name="attn_bwd", grid=(nblk,),
                   in_specs=[pl.BlockSpec(memory_space=pltpu.SMEM)] + _attn_specs(nblk) + [pl.BlockSpec((D, BLK), lambda i: (0, i))],
                   out_specs=[pl.BlockSpec((D, BLK), lambda i: (0, i)), _acc(t + 2 * pad, KVW), _acc(t + 2 * pad, KVW), _acc(8, 128)],
                   out_shape=[_sds((D, t), F32), _sds((t + 2 * pad, KVW), F32), _sds((t + 2 * pad, KVW), F32), _sds((8, 128), F32)],
                   xchg=xchg)
    return outs, xo


def _attn_post(ot, x, wo, tm):
    t = x.shape[0]

    def body(ot_ref, x_ref, w_ref, out_ref):
        out_ref[...] = x_ref[...] + _dot_tn(ot_ref[...], w_ref[...])

    return _pc(body, [ot, x, wo], name="attn_post", grid=(t // tm,),
               in_specs=[_cols(tm), _rows(tm, D), WHOLE], out_specs=[_rows(tm, D)], out_shape=[_sds((t, D), F32)])[0][0]


def _attn_post_bwd(dxt, wo, tm):
    t = dxt.shape[1]

    def body(dxt_ref, w_ref, dot_ref):
        dot_ref[...] = _dot(w_ref[...], dxt_ref[...]).astype(BF16)

    return _pc(body, [dxt, wo], name="attn_post_bwd", grid=(t // tm,),
               in_specs=[_cols(tm), WHOLE], out_specs=[_cols(tm)], out_shape=[_sds((D, t), BF16)])[0][0]


def _attn_pre_bwd(dqt, dk, dv, x, g, wqkv, tabs, dx_out, tm):
    t = x.shape[0]

    def body(dqt_ref, dk_ref, dv_ref, x_ref, g_ref, w_ref, c_ref, sa_ref, sb_ref, dxo_ref, dqkv_ref, ht_ref, dx_ref, dg_ref):
        @pl.when(pl.program_id(0) == 0)
        def _():
            dg_ref[...] = jnp.zeros_like(dg_ref)

        c, sa, sb = c_ref[...], sa_ref[...], sb_ref[...]
        dqkv_ref[:, :KOFF] = _rope_t(dqt_ref[...].T * 0.125, c, sa, sb).astype(BF16)
        dqkv_ref[:, KOFF:VOFF] = _rope_t(dk_ref[...], c, sa, sb).astype(BF16)
        dqkv_ref[:, VOFF:] = dv_ref[...].astype(BF16)
        g = g_ref[...]
        xh, r, h = _rms(x_ref[...], g)
        ht_ref[...] = h.astype(BF16).T
        dh = _dot_nt(dqkv_ref[...], w_ref[...])
        dx, dg = _rms_bwd(dh, xh, r, g)
        dx_ref[...] = dxo_ref[...] + dx
        dg_ref[...] += dg

    return _pc(body, [dqt, dk, dv, x, g, wqkv, *tabs, dx_out], name="attn_pre_bwd", grid=(t // tm,),
               in_specs=[_cols(tm), pl.BlockSpec((tm, KVW), lambda i: (i + 1, 0)), pl.BlockSpec((tm, KVW), lambda i: (i + 1, 0)),
                         _rows(tm, D), _acc(1, D), WHOLE, _rows(tm, 128), _rows(tm, 128), _rows(tm, 128), _rows(tm, D)],
               out_specs=[_rows(tm, QKV), _cols(tm), _rows(tm, D), _acc(1, D)],
               out_shape=[_sds((t, QKV), BF16), _sds((D, t), BF16), _sds((t, D), F32), _sds((1, D), F32)])[0]


def _ffn_fwd(x, g, wgu, wd, tm, name, xchg=None):
    t = x.shape[0]

    def body(x_ref, g_ref, wgu_ref, wd_ref, gu_ref, out_ref):
        x = x_ref[...]
        _, _, h = _rms(x, g_ref[...])
        hb = h.astype(BF16)
        y = x
        for j in range(4):
            gj = _dot(hb, wgu_ref[j])
            uj = _dot(hb, wgu_ref[j + 4])
            gu_ref[j] = gj.astype(BF16)
            gu_ref[j + 4] = uj.astype(BF16)
            y = y + _dot(gj * _sigmoid(gj) * uj, wd_ref[j])
        out_ref[...] = y

    return _pc(body, [x, g, wgu, wd], name=name, grid=(t // tm,),
               in_specs=[_rows(tm, D), _acc(1, D), WHOLE, WHOLE], out_specs=[_blk3(NDEV, tm, GUB), _rows(tm, D)],
               out_shape=[_sds((NDEV, t, GUB), BF16), _sds((t, D), F32)], xchg=xchg)


def _ffn_bwd(dy, x, gu, g, wgu, wd, tm, name, xchg=None):
    t = x.shape[0]

    def body(dy_ref, x_ref, gu_ref, g_ref, wgu_ref, wd_ref, dgu_ref, act_ref, ht_ref, dx_ref, dxt_ref, dg_ref):
        @pl.when(pl.program_id(0) == 0)
        def _():
            dg_ref[...] = jnp.zeros_like(dg_ref)

        dy = dy_ref[...]
        dyb = dy.astype(BF16)
        gn = g_ref[...]
        xh, r, h = _rms(x_ref[...], gn)
        ht_ref[...] = h.astype(BF16).T
        dh = jnp.zeros_like(dy)
        for j in range(4):
            gj = gu_ref[j].astype(F32)
            uj = gu_ref[j + 4].astype(F32)
            sg = _sigmoid(gj)
            silu = gj * sg
            act_ref[j] = (silu * uj).astype(BF16)
            dact = _dot_nt(dyb, wd_ref[j])
            dgj = (dact * uj * (sg * (1.0 + gj * (1.0 - sg)))).astype(BF16)
            duj = (dact * silu).astype(BF16)
            dgu_ref[j] = dgj
            dgu_ref[j + 4] = duj
            dh = dh + _dot_nt(dgj, wgu_ref[j]) + _dot_nt(duj, wgu_ref[j + 4])
        dx, dg = _rms_bwd(dh, xh, r, gn)
        dx = dy + dx
        dx_ref[...] = dx
        dxt_ref[...] = dx.astype(BF16).T
        dg_ref[...] += dg

    return _pc(body, [dy, x, gu, g, wgu, wd], name=name, grid=(t // tm,),
               in_specs=[_rows(tm, D), _rows(tm, D), _blk3(NDEV, tm, GUB), _acc(1, D), WHOLE, WHOLE],
               out_specs=[_blk3(NDEV, tm, GUB), _blk3(4, tm, GUB), _cols(tm), _rows(tm, D), _cols(tm), _acc(1, D)],
               out_shape=[_sds((NDEV, t, GUB), BF16), _sds((4, t, GUB), BF16), _sds((D, t), BF16), _sds((t, D), F32),
                          _sds((D, t), BF16), _sds((1, D), F32)], xchg=xchg)


def _wgrad(at, b, tk, name):
    ma, t = at.shape
    nb, _, mb = b.shape
    nk = t // tk

    def body(a_ref, b_ref, out_ref, acc):
        k = pl.program_id(1)

        @pl.when(k == 0)
        def _():
            acc[...] = jnp.zeros_like(acc)

        acc[...] += _dot(a_ref[...], b_ref[0])

        @pl.when(k == nk - 1)
        def _():
            out_ref[0] = acc[...].astype(BF16)

    return _pc(body, [at, b], name=name, grid=(nb, nk),
               in_specs=[pl.BlockSpec((ma, tk), lambda j, k: (0, k)), pl.BlockSpec((1, tk, mb), lambda j, k: (j, k, 0))],
               out_specs=[pl.BlockSpec((1, ma, mb), lambda j, k: (j, 0, 0))], out_shape=[_sds((nb, ma, mb), BF16)],
               scratch_shapes=[pltpu.VMEM((ma, mb), F32)])[0][0]


def _conv_pre(x, g, w, b, tm):
    t = x.shape[0]

    def body(x_ref, g_ref, w_ref, b_ref, u_ref, glu_ref):
        _, _, h = _rms(x_ref[...], g_ref[...])
        hb = h.astype(BF16)
        for j in range(4):
            a = _dot(hb, w_ref[j]) + b_ref[:, j * PWB:(j + 1) * PWB]
            gt = _dot(hb, w_ref[j + 4]) + b_ref[:, D + j * PWB:D + (j + 1) * PWB]
            u_ref[j] = a.astype(BF16)
            u_ref[j + 4] = gt.astype(BF16)
            glu_ref[:, j * PWB:(j + 1) * PWB] = a * _sigmoid(gt)

    return _pc(body, [x, g, w, b], name="conv_pre", grid=(t // tm,),
               in_specs=[_rows(tm, D), _acc(1, D), WHOLE, _acc(1, 2 * D)], out_specs=[_blk3(NDEV, tm, PWB), _rows(tm, D)],
               out_shape=[_sds((NDEV, t, PWB), BF16), _sds((t, D), F32)])[0]


def _halo_specs(t, tm):
    per = tm // HALO
    last = t // HALO - 1
    return [
        pl.BlockSpec((HALO, D), lambda i: (jnp.maximum(i * per - 1, 0), 0)),
        _rows(tm, D),
        pl.BlockSpec((HALO, D), lambda i: (jnp.minimum((i + 1) * per, last), 0)),
    ]


SUB = 8
CCH = 32
CLN = 256


def _fill_shifted(sh, prev, cur, nxt, tm):
    i = pl.program_id(0)
    rows = jnp.concatenate([jnp.where(i == 0, 0.0, prev[...]), cur[...], jnp.where(i == pl.num_programs(0) - 1, 0.0, nxt[...])], axis=0)
    n = tm + 2 * HALO - SUB
    for b in range(SUB):
        sh[b] = rows[b:b + n]


def _shifted(sh, off, r0, c0):
    return sh[off % SUB, r0 + off - off % SUB:r0 + off - off % SUB + CCH, c0:c0 + CLN]


def _conv_mid(glu, wdw, bdw, tm, xchg=None):
    t = glu.shape[0]

    def body(prev, cur, nxt, w_ref, b_ref, out_ref, sh):
        _fill_shifted(sh, prev, cur, nxt, tm)
        for c0 in range(0, D, CLN):
            for r0 in range(0, tm, CCH):
                acc = jnp.broadcast_to(b_ref[:, c0:c0 + CLN], (CCH, CLN))
                for k in range(CW):
                    acc = acc + w_ref[k:k + 1, c0:c0 + CLN] * _shifted(sh, k + HALO - CPAD, r0, c0)
                out_ref[r0:r0 + CCH, c0:c0 + CLN] = acc

    outs, xo = _pc(body, [glu, glu, glu, wdw, bdw], name="conv_mid", grid=(t // tm,),
                   in_specs=_halo_specs(t, tm) + [_acc(32, D), _acc(1, D)], out_specs=[_rows(tm, D)],
                   out_shape=[_sds((t, D), F32)], scratch_shapes=[pltpu.VMEM((SUB, tm + 2 * HALO - SUB, D), F32)], xchg=xchg)
    return outs[0], xo


def _conv_mid_bwd(dcv, glu, wdw, tm, xchg=None):
    t = glu.shape[0]

    def body(dp, dc, dn, gp, gc, gn, w_ref, dglu_ref, dw_ref, dsh, gsh):
        @pl.when(pl.program_id(0) == 0)
        def _():
            dw_ref[...] = jnp.zeros_like(dw_ref)

        _fill_shifted(dsh, dp, dc, dn, tm)
        _fill_shifted(gsh, gp, gc, gn, tm)
        for c0 in range(0, D, CLN):
            dwk = [jnp.zeros((SUB, CLN), F32) for _ in range(CW)]
            for r0 in range(0, tm, CCH):
                d0 = _shifted(dsh, HALO, r0, c0)
                acc = jnp.zeros((CCH, CLN), F32)
                for k in range(CW):
                    acc = acc + w_ref[k:k + 1, c0:c0 + CLN] * _shifted(dsh, HALO + CPAD - k, r0, c0)
                    prod = d0 * _shifted(gsh, k + HALO - CPAD, r0, c0)
                    for r in range(0, CCH, SUB):
                        dwk[k] = dwk[k] + prod[r:r + SUB]
                dglu_ref[r0:r0 + CCH, c0:c0 + CLN] = acc
            for k in range(CW):
                dw_ref[k:k + 1, c0:c0 + CLN] += jnp.sum(dwk[k], axis=0, keepdims=True)

    n = tm + 2 * HALO - SUB
    return _pc(body, [dcv, dcv, dcv, glu, glu, glu, wdw], name="conv_mid_bwd", grid=(t // tm,),
               in_specs=_halo_specs(t, tm) + _halo_specs(t, tm) + [_acc(32, D)], out_specs=[_rows(tm, D), _acc(32, D)],
               out_shape=[_sds((t, D), F32), _sds((32, D), F32)],
               scratch_shapes=[pltpu.VMEM((SUB, n, D), F32), pltpu.VMEM((SUB, n, D), F32)], xchg=xchg)


def _ln(cv, lg, lb):
    mu = jnp.mean(cv, axis=-1, keepdims=True)
    cc = cv - mu
    rs = lax.rsqrt(jnp.mean(cc * cc, axis=-1, keepdims=True) + EPS)
    lh = cc * rs
    return lh, rs, lh * lg + lb


def _conv_post(cv, x, lg, lb, w2, b2, tm):
    t = x.shape[0]

    def body(cv_ref, x_ref, lg_ref, lb_ref, w_ref, b_ref, st_ref, out_ref):
        _, _, ln = _ln(cv_ref[...], lg_ref[...], lb_ref[...])
        s = (ln * _sigmoid(ln)).astype(BF16)
        st_ref[...] = s.T
        out_ref[...] = x_ref[...] + _dot(s, w_ref[...]) + b_ref[...]

    return _pc(body, [cv, x, lg, lb, w2, b2], name="conv_post", grid=(t // tm,),
               in_specs=[_rows(tm, D), _rows(tm, D), _acc(1, D), _acc(1, D), WHOLE, _acc(1, D)],
               out_specs=[_cols(tm), _rows(tm, D)], out_shape=[_sds((D, t), BF16), _sds((t, D), F32)])[0]


def _conv_post_bwd(dx, cv, lg, lb, w2, tm):
    t = dx.shape[0]

    def body(dx_ref, cv_ref, lg_ref, lb_ref, w_ref, dcv_ref, part_ref):
        @pl.when(pl.program_id(0) == 0)
        def _():
            part_ref[...] = jnp.zeros_like(part_ref)

        dx = dx_ref[...]
        lg = lg_ref[...]
        lh, rs, ln = _ln(cv_ref[...], lg, lb_ref[...])
        sg = _sigmoid(ln)
        dln = _dot_nt(dx, w_ref[...]) * (sg * (1.0 + ln * (1.0 - sg)))
        dlh = dln * lg
        dcv = rs * (dlh - jnp.mean(dlh, axis=-1, keepdims=True) - lh * jnp.mean(dlh * lh, axis=-1, keepdims=True))
        dcv_ref[...] = dcv
        part_ref[0:1, :] += jnp.sum(dln * lh, axis=0, keepdims=True)
        part_ref[1:2, :] += jnp.sum(dln, axis=0, keepdims=True)
        part_ref[2:3, :] += jnp.sum(dcv, axis=0, keepdims=True)
        part_ref[3:4, :] += jnp.sum(dx, axis=0, keepdims=True)

    return _pc(body, [dx, cv, lg, lb, w2], name="conv_post_bwd", grid=(t // tm,),
               in_specs=[_rows(tm, D), _rows(tm, D), _acc(1, D), _acc(1, D), WHOLE], out_specs=[_rows(tm, D), _acc(8, D)],
               out_shape=[_sds((t, D), F32), _sds((8, D), F32)])[0]


def _conv_pre_bwd(dglu, u, x, g, w, dx_out, tm):
    t = x.shape[0]

    def body(dglu_ref, u_ref, x_ref, g_ref, w_ref, dxo_ref, du_ref, ht_ref, dx_ref, dxt_ref, dg_ref, db_ref):
        @pl.when(pl.program_id(0) == 0)
        def _():
            dg_ref[...] = jnp.zeros_like(dg_ref)
            db_ref[...] = jnp.zeros_like(db_ref)

        gn = g_ref[...]
        xh, r, h = _rms(x_ref[...], gn)
        ht_ref[...] = h.astype(BF16).T
        dh = jnp.zeros_like(xh)
        for j in range(4):
            a = u_ref[j].astype(F32)
            sg = _sigmoid(u_ref[j + 4].astype(F32))
            dgl = dglu_ref[:, j * PWB:(j + 1) * PWB]
            da = dgl * sg
            dgt = dgl * a * sg * (1.0 - sg)
            db_ref[:, j * PWB:(j + 1) * PWB] += jnp.sum(da, axis=0, keepdims=True)
            db_ref[:, D + j * PWB:D + (j + 1) * PWB] += jnp.sum(dgt, axis=0, keepdims=True)
            da, dgt = da.astype(BF16), dgt.astype(BF16)
            du_ref[j] = da
            du_ref[j + 4] = dgt
            dh = dh + _dot_nt(da, w_ref[j]) + _dot_nt(dgt, w_ref[j + 4])
        dx, dg = _rms_bwd(dh, xh, r, gn)
        dx = dxo_ref[...] + dx
        dx_ref[...] = dx
        dxt_ref[...] = dx.astype(BF16).T
        dg_ref[...] += dg

    return _pc(body, [dglu, u, x, g, w, dx_out], name="conv_pre_bwd", grid=(t // tm,),
               in_specs=[_rows(tm, D), _blk3(NDEV, tm, PWB), _rows(tm, D), _acc(1, D), WHOLE, _rows(tm, D)],
               out_specs=[_blk3(NDEV, tm, PWB), _cols(tm), _rows(tm, D), _cols(tm), _acc(1, D), _acc(1, 2 * D)],
               out_shape=[_sds((NDEV, t, PWB), BF16), _sds((D, t), BF16), _sds((t, D), F32), _sds((D, t), BF16),
                          _sds((1, D), F32), _sds((1, 2 * D), F32)])[0]


def _final(x, g, tgt, tm):
    t = x.shape[0]

    def body(x_ref, g_ref, t_ref, dx_ref, dxt_ref, part_ref):
        @pl.when(pl.program_id(0) == 0)
        def _():
            part_ref[...] = jnp.zeros_like(part_ref)

        g = g_ref[...]
        xh, r, y = _rms(x_ref[...], g)
        err = y - t_ref[...]
        dx, dg = _rms_bwd(err * (1.0 / D), xh, r, g)
        dx_ref[...] = dx
        dxt_ref[...] = dx.astype(BF16).T
        part_ref[0:1, :] += dg
        tok = jnp.sum(err * err, axis=-1, keepdims=True) * (1.0 / D)
        lane = lax.broadcasted_iota(jnp.int32, (1, D), 1)
        part_ref[1:2, :] += jnp.where(lane == 0, 0.5 * jnp.sum(tok, axis=0, keepdims=True), 0.0)

    return _pc(body, [x, g, tgt], name="final", grid=(t // tm,),
               in_specs=[_rows(tm, D), _acc(1, D), _rows(tm, D)], out_specs=[_rows(tm, D), _cols(tm), _acc(8, D)],
               out_shape=[_sds((t, D), F32), _sds((D, t), BF16), _sds((8, D), F32)])[0]


def _adamw(w, g, m, v):
    m = B1 * m + (1.0 - B1) * g
    v = B2 * v + (1.0 - B2) * (g * g)
    m_hat = m / (1.0 - B1 ** STEP)
    v_hat = v / (1.0 - B2 ** STEP)
    return -LR * (m_hat / (jnp.sqrt(v_hat) + AEPS) + WD * w), m, v


def _reduce_adamw(land, w, m, v, tr, name):
    _, r, c = land.shape

    def body(l_ref, w_ref, m_ref, v_ref, g_ref, d_ref, nm_ref, nv_ref):
        g = l_ref[0].astype(F32)
        for j in range(1, NDEV):
            g = g + l_ref[j].astype(F32)
        g_ref[...] = g
        d_ref[...], nm_ref[...], nv_ref[...] = _adamw(w_ref[...], g, m_ref[...], v_ref[...])

    return _pc(body, [land, w, m, v], name=name, grid=(r // tr,),
               in_specs=[_blk3(NDEV, tr, c)] + [_rows(tr, c)] * 3, out_specs=[_rows(tr, c)] * 4,
               out_shape=[_sds((r, c), F32)] * 4)[0]


def _sum_parts(parts):
    _, r, c = parts.shape

    def body(p_ref, out_ref):
        s = p_ref[0]
        for j in range(1, NDEV):
            s = s + p_ref[j]
        out_ref[...] = s

    return _pc(body, [parts], name="sum_parts", in_specs=[WHOLE], out_specs=[WHOLE], out_shape=[_sds((r, c), F32)])[0][0]


def _adamw_small(w, g, m, v):
    def body(w_ref, g_ref, m_ref, v_ref, d_ref, nm_ref, nv_ref):
        d_ref[...], nm_ref[...], nv_ref[...] = _adamw(w_ref[...], g_ref[...], m_ref[...], v_ref[...])

    return _pc(body, [w, g, m, v], name="adamw_small", in_specs=[WHOLE] * 4, out_specs=[WHOLE] * 3,
               out_shape=[_sds(w.shape, F32)] * 3)[0]


def _rows128(a):
    a = a.astype(F32)
    if a.shape[-1] % 128:
        a = jnp.pad(a, [(0, 0)] * (a.ndim - 1) + [(0, 128 - a.shape[-1] % 128)])
    return a.reshape(-1, 128)


def _pack(arrs, rows):
    p = jnp.concatenate([_rows128(a) for a in arrs], axis=0)
    return jnp.pad(p, ((0, rows - p.shape[0]), (0, 0)))


SMALL = ("attn_norm", "ffn_norm", "final_norm", "attn_sink", "conv_norm", "conv_b_dw", "conv_ln_g", "conv_ln_b", "conv_b_pw2",
         "conv_b_pw1", "conv_w_dw")
SMALL_ROWS = 72


def _pack_small(d):
    return _pack([d[k] for k in SMALL], SMALL_ROWS)


def _unpack_small(p, like):
    out, r = {}, 0
    for k in SMALL:
        shp = like[k].shape
        n = -(-shp[-1] // 128) * (math.prod(shp[:-1]))
        blk = p[r:r + n]
        if shp[-1] % 128:
            blk = blk[:, :shp[-1]]
        out[k] = blk.reshape(shp)
        r += n
    return out


BIG = ("attn_w_qkv", "attn_w_o", "conv_w_pw1", "conv_w_pw2", "ffn_w_gu0", "ffn_w_gu1", "ffn_w_down0", "ffn_w_down1")
NAMES = ("attn_norm", "attn_w_qkv", "attn_w_o", "attn_sink", "conv_norm", "conv_w_pw1", "conv_b_pw1", "conv_w_dw", "conv_b_dw",
         "conv_ln_g", "conv_ln_b", "conv_w_pw2", "conv_b_pw2", "ffn_norm", "ffn_w_gu", "ffn_w_down", "final_norm")
TM = 256
TK = 2048


def _split_layers(d):
    return {
        "attn_w_qkv": d["attn_w_qkv"][0], "attn_w_o": d["attn_w_o"][0], "conv_w_pw1": d["conv_w_pw1"][0], "conv_w_pw2": d["conv_w_pw2"][0],
        "ffn_w_gu0": d["ffn_w_gu"][0], "ffn_w_gu1": d["ffn_w_gu"][1], "ffn_w_down0": d["ffn_w_down"][0], "ffn_w_down1": d["ffn_w_down"][1],
    }


def _join_layers(d):
    return {
        "attn_w_qkv": d["attn_w_qkv"][None], "attn_w_o": d["attn_w_o"][None], "conv_w_pw1": d["conv_w_pw1"][None],
        "conv_w_pw2": d["conv_w_pw2"][None], "ffn_w_gu": jnp.stack([d["ffn_w_gu0"], d["ffn_w_gu1"]]),
        "ffn_w_down": jnp.stack([d["ffn_w_down0"], d["ffn_w_down1"]]),
    }


def _down_blocks(dwt):
    return dwt.reshape(4, D, 2, GUB // 2).transpose(0, 2, 3, 1).reshape(NDEV, DFF // NDEV, D)


def kernel(x, attn_norm, attn_w_qkv, attn_w_o, attn_sink, conv_norm, conv_w_pw1, conv_b_pw1, conv_w_dw, conv_b_dw, conv_ln_g, conv_ln_b, conv_w_pw2, conv_b_pw2, ffn_norm, ffn_w_gu, ffn_w_down, final_norm, loss_target, m_attn_norm, m_attn_w_qkv, m_attn_w_o, m_attn_sink, m_conv_norm, m_conv_w_pw1, m_conv_b_pw1, m_conv_w_dw, m_conv_b_dw, m_conv_ln_g, m_conv_ln_b, m_conv_w_pw2, m_conv_b_pw2, m_ffn_norm, m_ffn_w_gu, m_ffn_w_down, m_final_norm, v_attn_norm, v_attn_w_qkv, v_attn_w_o, v_attn_sink, v_conv_norm, v_conv_w_pw1, v_conv_b_pw1, v_conv_w_dw, v_conv_b_dw, v_conv_ln_g, v_conv_ln_b, v_conv_w_pw2, v_conv_b_pw2, v_ffn_norm, v_ffn_w_gu, v_ffn_w_down, v_final_norm):
    w = dict(zip(NAMES, (attn_norm, attn_w_qkv, attn_w_o, attn_sink, conv_norm, conv_w_pw1, conv_b_pw1, conv_w_dw, conv_b_dw, conv_ln_g,
                         conv_ln_b, conv_w_pw2, conv_b_pw2, ffn_norm, ffn_w_gu, ffn_w_down, final_norm)))
    m = dict(zip(NAMES, (m_attn_norm, m_attn_w_qkv, m_attn_w_o, m_attn_sink, m_conv_norm, m_conv_w_pw1, m_conv_b_pw1, m_conv_w_dw,
                         m_conv_b_dw, m_conv_ln_g, m_conv_ln_b, m_conv_w_pw2, m_conv_b_pw2, m_ffn_norm, m_ffn_w_gu, m_ffn_w_down,
                         m_final_norm)))
    v = dict(zip(NAMES, (v_attn_norm, v_attn_w_qkv, v_attn_w_o, v_attn_sink, v_conv_norm, v_conv_w_pw1, v_conv_b_pw1, v_conv_w_dw,
                         v_conv_b_dw, v_conv_ln_g, v_conv_ln_b, v_conv_w_pw2, v_conv_b_pw2, v_ffn_norm, v_ffn_w_gu, v_ffn_w_down,
                         v_final_norm)))
    me = 4 * lax.axis_index("x") + 2 * lax.axis_index("y") + lax.axis_index("c")
    wb, mb, vb = _split_layers(w), _split_layers(m), _split_layers(v)
    sh = {k: wb[k].astype(BF16) for k in BIG}
    x0, tgt = x[0], loss_target[0]
    t = x0.shape[0]
    tabs = _rope_tables(t)
    g_a, sink, g_f0, g_f1, g_fin = w["attn_norm"], w["attn_sink"][0], w["ffn_norm"][0:1], w["ffn_norm"][1:2], w["final_norm"][None]
    gather, scatter = False, True

    shard_rows = _pack([w["conv_w_dw"][0], jnp.zeros((1, 128), F32), w["conv_norm"], w["conv_b_dw"], w["conv_ln_g"], w["conv_ln_b"],
                        w["conv_b_pw2"], w["conv_b_pw1"]], 40)
    wqkv_g, wo_g, sm = _exchange([sh["attn_w_qkv"], sh["attn_w_o"], shard_rows], [gather] * 3, "gather_attn")
    wqkv, wo = wqkv_g.transpose(1, 0, 2).reshape(D, QKV), wo_g.reshape(D, D)

    def full_vec(row, n=1):
        return sm[:, row:row + n, :].reshape(1, NDEV * n * 128)

    w_dw, g_c, b_dw, ln_g, ln_b = sm[:, 0:32, :].transpose(1, 0, 2).reshape(32, D), full_vec(32), full_vec(33), full_vec(34), full_vec(35)
    b_pw2, b_pw1 = full_vec(36), full_vec(37, 2)

    q_t, kv = _attn_pre(x0, g_a, wqkv, tabs, TM)
    o_t, (wgu0, wd0_g) = _attn_fwd(q_t, kv, sink, xchg=([sh["ffn_w_gu0"], sh["ffn_w_down0"]], [gather] * 2))
    wd0 = wd0_g.reshape(4, GUB, D)
    x1 = _attn_post(o_t, x0, wo, TM)
    (gu0, x2), (wpw1, wpw2_g, wgu1) = _ffn_fwd(x1, g_f0, wgu0, wd0, TM, "ffn_fwd0",
                                               xchg=([sh["conv_w_pw1"], sh["conv_w_pw2"], sh["ffn_w_gu1"]], [gather] * 3))
    wpw2 = wpw2_g.reshape(D, D)
    u, glu = _conv_pre(x2, g_c, wpw1, b_pw1, TM)
    cv, (wd1_g,) = _conv_mid(glu, w_dw, b_dw, TM, xchg=([sh["ffn_w_down1"]], [gather]))
    wd1 = wd1_g.reshape(4, GUB, D)
    s_t, x3 = _conv_post(cv, x2, ln_g, ln_b, wpw2, b_pw2, TM)
    (gu1, x4), _ = _ffn_fwd(x3, g_f1, wgu1, wd1, TM, "ffn_fwd1")
    dx4, dx4_t, fin = _final(x4, g_fin, tgt, TM)

    land = {}
    tk = min(TK, t)
    (dgu1, act1, h3_t, dx3, _, dg_f1), _ = _ffn_bwd(dx4, x3, gu1, g_f1, wgu1, wd1, TM, "ffn_bwd1")
    dwgu1 = _wgrad(h3_t, dgu1, tk, "dwgu1")
    dwd1 = _down_blocks(_wgrad(dx4_t, act1, tk, "dwd1"))
    dcv, cpart = _conv_post_bwd(dx3, cv, ln_g, ln_b, wpw2, TM)
    dwpw2 = _wgrad(s_t, dx3[None], tk, "dwpw2").reshape(NDEV, D // NDEV, D)
    (dglu, dw_dw), (land["ffn_w_gu1"], land["ffn_w_down1"]) = _conv_mid_bwd(dcv, glu, w_dw, TM, xchg=([dwgu1, dwd1], [scatter] * 2))
    du, h2_t, dx2, dx2_t, dg_c, db_pw1 = _conv_pre_bwd(dglu, u, x2, g_c, wpw1, dx3, TM)
    dwpw1 = _wgrad(h2_t, du, tk, "dwpw1")
    (dgu0, act0, h1_t, dx1, dx1_t, dg_f0), (land["conv_w_pw1"], land["conv_w_pw2"]) = _ffn_bwd(
        dx2, x1, gu0, g_f0, wgu0, wd0, TM, "ffn_bwd0", xchg=([dwpw1, dwpw2], [scatter] * 2))
    dwgu0 = _wgrad(h1_t, dgu0, tk, "dwgu0")
    dwd0 = _down_blocks(_wgrad(dx2_t, act0, tk, "dwd0"))
    do_t = _attn_post_bwd(dx1_t, wo, TM)
    dwo = _wgrad(o_t, dx1[None], tk, "dwo").reshape(NDEV, D // NDEV, D)
    (dq_t, dk, dv, dsink), (land["ffn_w_gu0"], land["ffn_w_down0"], land["attn_w_o"]) = _attn_bwd(
        q_t, kv, do_t, sink, TM, xchg=([dwgu0, dwd0, dwo], [scatter] * 3))
    dqkv, h0_t, dx0, dg_a = _attn_pre_bwd(dq_t, dk, dv, x0, g_a, wqkv, tabs, dx1, TM)
    dwqkv = _wgrad(h0_t, dqkv[None], tk, "dwqkv")[0].reshape(D, NDEV, QKV // NDEV).transpose(1, 0, 2)

    lane0 = (lax.broadcasted_iota(jnp.int32, (1, 128), 1) == 0).astype(F32)
    parts = _pack([dg_a, dg_f0, dg_f1, fin[0:1], dsink[0:1, :NH], fin[1, 0] * lane0, jnp.zeros((6, 128), F32), dg_c, cpart[2:3],
                   cpart[0:1], cpart[1:2], cpart[3:4], db_pw1, dw_dw.reshape(32, NDEV, 128)], 352)
    land["attn_w_qkv"], parts_g = _exchange([dwqkv, parts], [scatter, gather], "scatter_attn")
    red = _sum_parts(parts_g)

    def shard_rows_of(row, n=1):
        return lax.dynamic_slice_in_dim(red, row + n * me, n, axis=0)

    gs = {
        "attn_norm": red[0:8].reshape(1, D), "ffn_norm": red[8:24].reshape(2, D), "final_norm": red[24:32].reshape(D),
        "attn_sink": red[32:33, :NH], "conv_norm": shard_rows_of(40), "conv_b_dw": shard_rows_of(48), "conv_ln_g": shard_rows_of(56),
        "conv_ln_b": shard_rows_of(64), "conv_b_pw2": shard_rows_of(72), "conv_b_pw1": shard_rows_of(80, 2).reshape(1, PWB),
        "conv_w_dw": lax.dynamic_index_in_dim(red[96:352].reshape(32, NDEV, 128), me, axis=1, keepdims=False)[None, :CW],
    }
    loss = red[33, 0]

    grads, deltas, new_m, new_v = dict(gs), {}, {}, {}
    ds, ms, vs = _adamw_small(_pack_small(w), _pack_small(gs), _pack_small(m), _pack_small(v))
    deltas.update(_unpack_small(ds, gs))
    new_m.update(_unpack_small(ms, gs))
    new_v.update(_unpack_small(vs, gs))
    gb, db, nmb, nvb = {}, {}, {}, {}
    for k in BIG:
        tr = {1024: 256, 128: 128, 352: 176}[land[k].shape[1]]
        gb[k], db[k], nmb[k], nvb[k] = _reduce_adamw(land[k], wb[k], mb[k], vb[k], tr, "adamw_" + k)
    grads.update(_join_layers(gb))
    deltas.update(_join_layers(db))
    new_m.update(_join_layers(nmb))
    new_v.update(_join_layers(nvb))
    return (loss, dx0[None], *[grads[k] for k in NAMES], *[deltas[k] for k in NAMES], *[new_m[k] for k in NAMES],
            *[new_v[k] for k in NAMES])
```

```python
import math

import jax
import jax.numpy as jnp
from jax import lax
from jax.experimental import pallas as pl
from jax.experimental.pallas import tpu as pltpu

F32 = jnp.float32
BF16 = jnp.bfloat16

D = 1024
NH = 16
NKV = 4
HD = 64
GROUP = NH // NKV
ROT = 16
THETA = 500000.0
BLK = 128
QKV = (NH + 2 * NKV) * HD
KOFF = NH * HD
VOFF = KOFF + NKV * HD
DFF = 2816
NDEV = 8
GUB = 2 * DFF // NDEV
PWB = 2 * D // NDEV
CW = 31
CPAD = 15
HALO = 16
EPS = 1e-6
NEG = -1e30
LR, B1, B2, AEPS, WD, STEP = 0.001, 0.9, 0.999, 1e-08, 0.01, 10

VMEM_LIMIT = 56 * 1024 * 1024
MESH = pl.DeviceIdType.MESH
WHOLE = pl.BlockSpec(memory_space=pltpu.VMEM)
ANY = pl.BlockSpec(memory_space=pl.ANY)


def _place():
    x, y, c = lax.axis_index("x"), lax.axis_index("y"), lax.axis_index("c")
    return x, y, c, 4 * x + 2 * y + c


def _peer(x, y, c, j):
    px = 1 - x if j & 4 else x
    py = 1 - y if j & 2 else y
    pc = 1 - c if j & 1 else c
    return (px, py, pc), 4 * px + 2 * py + pc


def _xchg_copies(src, dst, scatter, send, recv, loc):
    x, y, c, me = _place()
    out = []
    for k in range(len(src)):
        if scatter[k]:
            local = pltpu.make_async_copy(src[k].at[me], dst[k].at[0], loc.at[k])
        else:
            local = pltpu.make_async_copy(src[k], dst[k].at[me], loc.at[k])
        pairs = []
        for j in range(1, NDEV):
            peer, pidx = _peer(x, y, c, j)
            if scatter[k]:
                s, d_out, d_in = src[k].at[pidx], dst[k].at[j], dst[k].at[j]
            else:
                s, d_out, d_in = src[k], dst[k].at[me], dst[k].at[pidx]
            sems = dict(send_sem=send.at[k, j - 1], recv_sem=recv.at[k, j - 1], device_id=peer, device_id_type=MESH)
            pairs.append((pltpu.make_async_remote_copy(src_ref=s, dst_ref=d_out, **sems),
                          pltpu.make_async_remote_copy(src_ref=s, dst_ref=d_in, **sems)))
        out.append((local, pairs))
    return out


def _xchg_start(cps):
    for local, pairs in cps:
        local.start()
        for snd, _ in pairs:
            snd.start()


def _xchg_wait(cps):
    for local, pairs in cps:
        for _, rcv in pairs:
            rcv.wait_recv()
        for snd, _ in pairs:
            snd.wait_send()
        local.wait()


def _call(body, **kw):
    return pl.pallas_call(body, **kw)


def _pc(body, operands, *, name, in_specs, out_specs, out_shape, grid=None, scratch_shapes=(), xchg=None):
    kw = dict(name=name, compiler_params=pltpu.CompilerParams(
        dimension_semantics=None if grid is None else ("arbitrary",) * len(grid), vmem_limit_bytes=VMEM_LIMIT,
        has_side_effects=xchg is not None))
    if grid is not None:
        kw["grid"] = grid
    if xchg is None:
        outs = _call(body, in_specs=list(in_specs), out_specs=list(out_specs), out_shape=list(out_shape),
                     scratch_shapes=list(scratch_shapes), **kw)(*operands)
        return list(outs), []
    arrs, scatter = xchg
    nx, n_in, n_out, n_s = len(arrs), len(in_specs), len(out_specs), len(scratch_shapes)

    def wrapped(*refs):
        ins, refs = refs[:n_in], refs[n_in:]
        src, refs = refs[:nx], refs[nx:]
        outs, refs = refs[:n_out], refs[n_out:]
        dst, refs = refs[:nx], refs[nx:]
        scr, (send, recv, loc) = refs[:n_s], refs[n_s:]
        if grid is None:
            cps = _xchg_copies(src, dst, scatter, send, recv, loc)
            _xchg_start(cps)
            body(*ins, *outs, *scr)
            _xchg_wait(cps)
            return

        @pl.when(pl.program_id(0) == 0)
        def _():
            _xchg_start(_xchg_copies(src, dst, scatter, send, recv, loc))

        body(*ins, *outs, *scr)

        @pl.when(pl.program_id(0) == grid[0] - 1)
        def _():
            _xchg_wait(_xchg_copies(src, dst, scatter, send, recv, loc))

    xshape = [jax.ShapeDtypeStruct(a.shape if sc else (NDEV,) + a.shape, a.dtype) for a, sc in zip(arrs, scatter)]
    sems = [pltpu.SemaphoreType.DMA((nx, NDEV - 1)), pltpu.SemaphoreType.DMA((nx, NDEV - 1)), pltpu.SemaphoreType.DMA((nx,))]
    res = _call(wrapped, in_specs=list(in_specs) + [ANY] * nx, out_specs=list(out_specs) + [ANY] * nx,
                out_shape=list(out_shape) + xshape, scratch_shapes=list(scratch_shapes) + sems, **kw)(*operands, *arrs)
    return list(res[:n_out]), list(res[n_out:])


def _exchange(arrs, scatter, name):
    return _pc(lambda: None, [], name=name, in_specs=[], out_specs=[], out_shape=[], xchg=(arrs, scatter))[1]


def _rows(tm, width):
    return pl.BlockSpec((tm, width), lambda i: (i, 0))


def _cols(tm):
    return pl.BlockSpec((D, tm), lambda i: (0, i))


def _blk3(nb, tm, width):
    return pl.BlockSpec((nb, tm, width), lambda i: (0, i, 0))


def _acc(rows, width):
    return pl.BlockSpec((rows, width), lambda i: (0, 0))


def _sds(shape, dtype):
    return jax.ShapeDtypeStruct(shape, dtype)


def _dot(a, b):
    return jnp.dot(a.astype(BF16), b.astype(BF16), preferred_element_type=F32)


def _dot_nt(a, b):
    return lax.dot_general(a.astype(BF16), b.astype(BF16), (((1,), (1,)), ((), ())), preferred_element_type=F32)


def _dot_tn(a, b):
    return lax.dot_general(a.astype(BF16), b.astype(BF16), (((0,), (0,)), ((), ())), preferred_element_type=F32)


def _sigmoid(x):
    return 1.0 / (1.0 + jnp.exp(-x))


def _rms(x, g):
    r = lax.rsqrt(jnp.mean(x * x, axis=-1, keepdims=True) + EPS)
    xh = x * r
    return xh, r, xh * g


def _rms_bwd(dh, xh, r, g):
    dxh = dh * g
    dg = jnp.sum(dh * xh, axis=0, keepdims=True)
    dx = r * (dxh - xh * jnp.mean(dxh * xh, axis=-1, keepdims=True))
    return dx, dg


def _lanes(t, width):
    return jnp.tile(t, (1, width // t.shape[1]))


def _rope(z, c, sa, sb):
    w = z.shape[1]
    return z * _lanes(c, w) + pltpu.roll(z, w - 8, 1) * _lanes(sa, w) + pltpu.roll(z, 8, 1) * _lanes(sb, w)


def _rope_t(dz, c, sa, sb):
    w = dz.shape[1]
    return dz * _lanes(c, w) + pltpu.roll(dz * _lanes(sa, w), 8, 1) + pltpu.roll(dz * _lanes(sb, w), w - 8, 1)


def _rope_tables(t):
    pos = jnp.arange(t, dtype=F32)
    inv_freq = THETA ** (-jnp.arange(0, ROT, 2, dtype=F32) / ROT)
    ang = pos[:, None] * inv_freq[None, :]
    cos, sin = jnp.cos(ang), jnp.sin(ang)
    one = jnp.ones((t, HD - ROT), F32)
    zero = jnp.zeros((t, HD - ROT), F32)
    z8 = jnp.zeros((t, 8), F32)
    c = jnp.concatenate([cos, cos, one], axis=1)
    sa = jnp.concatenate([-sin, z8, zero], axis=1)
    sb = jnp.concatenate([z8, sin, zero], axis=1)
    return tuple(jnp.tile(a, (1, 2)) for a in (c, sa, sb))


KVW = NKV * HD
GW = GROUP * BLK


def _attn_pre(x, g, wqkv, tabs, tm, xchg=None):
    t = x.shape[0]

    def body(x_ref, g_ref, w_ref, c_ref, sa_ref, sb_ref, qt_ref, kv_ref):
        _, _, h = _rms(x_ref[...], g_ref[...])
        z = _dot(h, w_ref[...])
        c, sa, sb = c_ref[...], sa_ref[...], sb_ref[...]
        qt_ref[...] = (_rope(z[:, :KOFF], c, sa, sb) * 0.125).T.astype(BF16)
        kv_ref[:, :KVW] = _rope(z[:, KOFF:VOFF], c, sa, sb).astype(BF16)
        kv_ref[:, KVW:] = z[:, VOFF:].astype(BF16)

    return _pc(body, [x, g, wqkv, *tabs], name="attn_pre", grid=(t // tm,),
               in_specs=[_rows(tm, D), _acc(1, D), WHOLE, _rows(tm, 128), _rows(tm, 128), _rows(tm, 128)],
               out_specs=[_cols(tm), _rows(tm, 2 * KVW)], out_shape=[_sds((D, t), BF16), _sds((t, 2 * KVW), BF16)], xchg=xchg)


def _attn_specs(nblk):
    prev = lambda i: jnp.maximum(i - 1, 0)
    nxt = lambda i: jnp.minimum(i + 1, nblk - 1)
    return [
        pl.BlockSpec((D, BLK), lambda i: (0, i)),
        pl.BlockSpec((BLK, KVW), lambda i: (prev(i), 0)),
        pl.BlockSpec((BLK, KVW), lambda i: (i, 0)),
        pl.BlockSpec((BLK, KVW), lambda i: (nxt(i), 0)),
        pl.BlockSpec((BLK, KVW), lambda i: (prev(i), 1)),
        pl.BlockSpec((BLK, KVW), lambda i: (i, 1)),
        pl.BlockSpec((BLK, KVW), lambda i: (nxt(i), 1)),
    ]


def _attn_mask(i, nblk):
    c = lax.broadcasted_iota(jnp.int32, (3 * BLK, GW), 0)
    r = lax.broadcasted_iota(jnp.int32, (3 * BLK, GW), 1) & (BLK - 1)
    lo = jnp.where(i == 0, BLK, 0)
    hi = jnp.where(i == nblk - 1, 2 * BLK, 3 * BLK)
    return (c >= r) & (c - 2 * BLK <= r) & (c >= lo) & (c < hi)


def _group(ref, kv):
    return jnp.concatenate([ref[(kv * GROUP + g) * HD:(kv * GROUP + g + 1) * HD, :] for g in range(GROUP)], axis=1)


def _group_sink(sink_ref, kv):
    return jnp.concatenate([jnp.full((1, BLK), sink_ref[kv * GROUP + g], F32) for g in range(GROUP)], axis=1)


def _attn_probs(k_h, qt_g, valid, sink_g):
    s = jnp.where(valid, _dot(k_h, qt_g), NEG)
    m = jnp.maximum(jnp.max(s, axis=0, keepdims=True), sink_g)
    e = jnp.exp(s - m)
    es = jnp.exp(sink_g - m)
    inv = 1.0 / (jnp.sum(e, axis=0, keepdims=True) + es)
    return e * inv, es * inv


def _attn_fwd(qt, kv, sink, xchg=None):
    t = kv.shape[0]
    nblk = t // BLK

    def body(sink_ref, qt_ref, kp, kc, kn, vp, vc, vn, ot_ref):
        valid = _attn_mask(pl.program_id(0), nblk)
        k = jnp.concatenate([kp[...], kc[...], kn[...]], axis=0)
        v = jnp.concatenate([vp[...], vc[...], vn[...]], axis=0)
        for h in range(NKV):
            p, _ = _attn_probs(k[:, h * HD:(h + 1) * HD], _group(qt_ref, h), valid, _group_sink(sink_ref, h))
            ot_g = _dot_tn(v[:, h * HD:(h + 1) * HD], p).astype(BF16)
            for g in range(GROUP):
                ot_ref[(h * GROUP + g) * HD:(h * GROUP + g + 1) * HD, :] = ot_g[:, g * BLK:(g + 1) * BLK]

    outs, xo = _pc(body, [sink, qt] + [kv] * 6, name="attn_fwd", grid=(nblk,),
                   in_specs=[pl.BlockSpec(memory_space=pltpu.SMEM)] + _attn_specs(nblk),
                   out_specs=[pl.BlockSpec((D, BLK), lambda i: (0, i))], out_shape=[_sds((D, t), BF16)], xchg=xchg)
    return outs[0], xo


def _attn_bwd(qt, kv, dot, sink, pad, xchg=None):
    t = kv.shape[0]
    nblk = t // BLK

    def body(sink_ref, qt_ref, kp, kc, kn, vp, vc, vn, dot_ref, dqt_ref, dk_ref, dv_ref, ds_ref):
        i = pl.program_id(0)

        @pl.when(i == 0)
        def _():
            dk_ref[...] = jnp.zeros_like(dk_ref)
            dv_ref[...] = jnp.zeros_like(dv_ref)
            ds_ref[...] = jnp.zeros_like(ds_ref)

        valid = _attn_mask(i, nblk)
        k = jnp.concatenate([kp[...], kc[...], kn[...]], axis=0)
        v = jnp.concatenate([vp[...], vc[...], vn[...]], axis=0)
        lane = lax.broadcasted_iota(jnp.int32, (1, 128), 1)
        dsink = jnp.zeros((1, 128), F32)
        rows = pl.ds(pl.multiple_of(i * BLK + (pad - BLK), BLK), 3 * BLK)
        for h in range(NKV):
            k_h, v_h = k[:, h * HD:(h + 1) * HD], v[:, h * HD:(h + 1) * HD]
            qt_g, dot_g = _group(qt_ref, h), _group(dot_ref, h)
            p, ps = _attn_probs(k_h, qt_g, valid, _group_sink(sink_ref, h))
            dp = _dot(v_h, dot_g)
            delta = jnp.sum(p * dp, axis=0, keepdims=True)
            ds = (p * (dp - delta)).astype(BF16)
            dqt_g = _dot_tn(k_h, ds)
            dk_ref[rows, h * HD:(h + 1) * HD] += _dot_nt(ds, qt_g)
            dv_ref[rows, h * HD:(h + 1) * HD] += _dot_nt(p, dot_g)
            psd = ps * delta
            for g in range(GROUP):
                n = h * GROUP + g
                dqt_ref[n * HD:(n + 1) * HD, :] = dqt_g[:, g * BLK:(g + 1) * BLK]
                dsink = dsink - jnp.where(lane == n, jnp.sum(psd[:, g * BLK:(g + 1) * BLK], axis=1, keepdims=True), 0.0)
        ds_ref[0:1, :] += dsink

    outs, xo = _pc(body, [sink, qt] + [kv] * 6 + [dot], name="attn_bwd", grid=(nblk,),
                   in_specs=[pl.BlockSpec(memory_space=pltpu.SMEM)] + _attn_specs(nblk) + [pl.BlockSpec((D, BLK), lambda i: (0, i))],
                   out_specs=[pl.BlockSpec((D, BLK), lambda i: (0, i)), _acc(t + 2 * pad, KVW), _acc(t + 2 * pad, KVW), _acc(8, 128)],
                   out_shape=[_sds((D, t), F32), _sds((t + 2 * pad, KVW), F32), _sds((t + 2 * pad, KVW), F32), _sds((8, 128), F32)],
                   xchg=xchg)
    return outs, xo


def _attn_pre_bwd(dqt, dk, dv, x, g, wqkv, tabs, dx_out, tm):
    t = x.shape[0]

    def body(dqt_ref, dk_ref, dv_ref, x_ref, g_ref, w_ref, c_ref, sa_ref, sb_ref, dxo_ref, dqkv_ref, ht_ref, dx_ref, dg_ref):
        @pl.when(pl.program_id(0) == 0)
        def _():
            dg_ref[...] = jnp.zeros_like(dg_ref)

        c, sa, sb = c_ref[...], sa_ref[...], sb_ref[...]
        dqkv_ref[:, :KOFF] = _rope_t(dqt_ref[...].T * 0.125, c, sa, sb).astype(BF16)
        dqkv_ref[:, KOFF:VOFF] = _rope_t(dk_ref[...], c, sa, sb).astype(BF16)
        dqkv_ref[:, VOFF:] = dv_ref[...].astype(BF16)
        g = g_ref[...]
        xh, r, h = _rms(x_ref[...], g)
        ht_ref[...] = h.astype(BF16).T
        dh = _dot_nt(dqkv_ref[...], w_ref[...])
        dx, dg = _rms_bwd(dh, xh, r, g)
        dx_ref[...] = dxo_ref[...] + dx
        dg_ref[...] += dg

    return _pc(body, [dqt, dk, dv, x, g, wqkv, *tabs, dx_out], name="attn_pre_bwd", grid=(t // tm,),
               in_specs=[_cols(tm), pl.BlockSpec((tm, KVW), lambda i: (i + 1, 0)), pl.BlockSpec((tm, KVW), lambda i: (i + 1, 0)),
                         _rows(tm, D), _acc(1, D), WHOLE, _rows(tm, 128), _rows(tm, 128), _rows(tm, 128), _rows(tm, D)],
               out_specs=[_rows(tm, QKV), _cols(tm), _rows(tm, D), _acc(1, D)],
               out_shape=[_sds((t, QKV), BF16), _sds((D, t), BF16), _sds((t, D), F32), _sds((1, D), F32)])[0]


def _ffn(x, g, wgu_ref, wd_ref, gu_ref):
    _, _, h = _rms(x, g)
    hb = h.astype(BF16)
    y = x
    for j in range(4):
        gj = _dot(hb, wgu_ref[j])
        uj = _dot(hb, wgu_ref[j + 4])
        gu_ref[j] = gj.astype(BF16)
        gu_ref[j + 4] = uj.astype(BF16)
        y = y + _dot(gj * _sigmoid(gj) * uj, wd_ref[j])
    return y


def _ffn_fwd_attn(ot, x, wo, g, wgu, wd, tm, xchg=None):
    t = x.shape[0]

    def body(ot_ref, x_ref, wo_ref, g_ref, wgu_ref, wd_ref, x1_ref, gu_ref, out_ref):
        x1 = x_ref[...] + _dot_tn(ot_ref[...], wo_ref[...])
        x1_ref[...] = x1
        out_ref[...] = _ffn(x1, g_ref[...], wgu_ref, wd_ref, gu_ref)

    return _pc(body, [ot, x, wo, g, wgu, wd], name="ffn_fwd0", grid=(t // tm,),
               in_specs=[_cols(tm), _rows(tm, D), WHOLE, _acc(1, D), WHOLE, WHOLE],
               out_specs=[_rows(tm, D), _blk3(NDEV, tm, GUB), _rows(tm, D)],
               out_shape=[_sds((t, D), F32), _sds((NDEV, t, GUB), BF16), _sds((t, D), F32)], xchg=xchg)


def _ffn_fwd_final(x, g, wgu, wd, g_fin, tgt, tm, xchg=None):
    t = x.shape[0]

    def body(x_ref, g_ref, wgu_ref, wd_ref, gf_ref, t_ref, gu_ref, dx_ref, dxt_ref, part_ref):
        @pl.when(pl.program_id(0) == 0)
        def _():
            part_ref[...] = jnp.zeros_like(part_ref)

        gf = gf_ref[...]
        xh, r, y = _rms(_ffn(x_ref[...], g_ref[...], wgu_ref, wd_ref, gu_ref), gf)
        err = y - t_ref[...]
        dx, dg = _rms_bwd(err * (1.0 / D), xh, r, gf)
        dx_ref[...] = dx
        dxt_ref[...] = dx.astype(BF16).T
        part_ref[0:1, :] += dg
        tok = jnp.sum(err * err, axis=-1, keepdims=True) * (1.0 / D)
        lane = lax.broadcasted_iota(jnp.int32, (1, D), 1)
        part_ref[1:2, :] += jnp.where(lane == 0, 0.5 * jnp.sum(tok, axis=0, keepdims=True), 0.0)

    return _pc(body, [x, g, wgu, wd, g_fin, tgt], name="ffn_fwd1", grid=(t // tm,),
               in_specs=[_rows(tm, D), _acc(1, D), WHOLE, WHOLE, _acc(1, D), _rows(tm, D)],
               out_specs=[_blk3(NDEV, tm, GUB), _rows(tm, D), _cols(tm), _acc(8, D)],
               out_shape=[_sds((NDEV, t, GUB), BF16), _sds((t, D), F32), _sds((D, t), BF16), _sds((8, D), F32)], xchg=xchg)


def _ffn_bwd(dy, x, gu, g, wgu, wd, tm, name, wo=None, xchg=None):
    t = x.shape[0]

    def body(dy_ref, x_ref, gu_ref, g_ref, wgu_ref, wd_ref, *rest):
        wo_ref = rest[0] if wo is not None else None
        dgu_ref, act_ref, ht_ref, dx_ref, dxt_ref, dg_ref = rest[wo is not None:][:6]

        @pl.when(pl.program_id(0) == 0)
        def _():
            dg_ref[...] = jnp.zeros_like(dg_ref)

        dy = dy_ref[...]
        dyb = dy.astype(BF16)
        gn = g_ref[...]
        xh, r, h = _rms(x_ref[...], gn)
        ht_ref[...] = h.astype(BF16).T
        dh = jnp.zeros_like(dy)
        for j in range(4):
            gj = gu_ref[j].astype(F32)
            uj = gu_ref[j + 4].astype(F32)
            sg = _sigmoid(gj)
            silu = gj * sg
            act_ref[j] = (silu * uj).astype(BF16)
            dact = _dot_nt(dyb, wd_ref[j])
            dgj = (dact * uj * (sg * (1.0 + gj * (1.0 - sg)))).astype(BF16)
            duj = (dact * silu).astype(BF16)
            dgu_ref[j] = dgj
            dgu_ref[j + 4] = duj
            dh = dh + _dot_nt(dgj, wgu_ref[j]) + _dot_nt(duj, wgu_ref[j + 4])
        dx, dg = _rms_bwd(dh, xh, r, gn)
        dx = dy + dx
        dx_ref[...] = dx
        dxt = dx.astype(BF16).T
        dxt_ref[...] = dxt
        dg_ref[...] += dg
        if wo is not None:
            rest[7][...] = _dot(wo_ref[...], dxt).astype(BF16)

    extra = wo is not None
    return _pc(body, [dy, x, gu, g, wgu, wd] + [wo] * extra, name=name, grid=(t // tm,),
               in_specs=[_rows(tm, D), _rows(tm, D), _blk3(NDEV, tm, GUB), _acc(1, D), WHOLE, WHOLE] + [WHOLE] * extra,
               out_specs=[_blk3(NDEV, tm, GUB), _blk3(4, tm, GUB), _cols(tm), _rows(tm, D), _cols(tm), _acc(1, D)] + [_cols(tm)] * extra,
               out_shape=[_sds((NDEV, t, GUB), BF16), _sds((4, t, GUB), BF16), _sds((D, t), BF16), _sds((t, D), F32),
                          _sds((D, t), BF16), _sds((1, D), F32)] + [_sds((D, t), BF16)] * extra, xchg=xchg)


def _wgrad(at, b, tk, name):
    ma, t = at.shape
    nb, _, mb = b.shape
    nk = t // tk

    def body(a_ref, b_ref, out_ref, acc):
        k = pl.program_id(1)

        @pl.when(k == 0)
        def _():
            acc[...] = jnp.zeros_like(acc)

        acc[...] += _dot(a_ref[...], b_ref[0])

        @pl.when(k == nk - 1)
        def _():
            out_ref[0] = acc[...].astype(BF16)

    return _pc(body, [at, b], name=name, grid=(nb, nk),
               in_specs=[pl.BlockSpec((ma, tk), lambda j, k: (0, k)), pl.BlockSpec((1, tk, mb), lambda j, k: (j, k, 0))],
               out_specs=[pl.BlockSpec((1, ma, mb), lambda j, k: (j, 0, 0))], out_shape=[_sds((nb, ma, mb), BF16)],
               scratch_shapes=[pltpu.VMEM((ma, mb), F32)])[0][0]


def _conv_pre(x, g, w, b, tm):
    t = x.shape[0]

    def body(x_ref, g_ref, w_ref, b_ref, u_ref, glu_ref):
        _, _, h = _rms(x_ref[...], g_ref[...])
        hb = h.astype(BF16)
        for j in range(4):
            a = _dot(hb, w_ref[j]) + b_ref[:, j * PWB:(j + 1) * PWB]
            gt = _dot(hb, w_ref[j + 4]) + b_ref[:, D + j * PWB:D + (j + 1) * PWB]
            u_ref[j] = a.astype(BF16)
            u_ref[j + 4] = gt.astype(BF16)
            glu_ref[:, j * PWB:(j + 1) * PWB] = a * _sigmoid(gt)

    return _pc(body, [x, g, w, b], name="conv_pre", grid=(t // tm,),
               in_specs=[_rows(tm, D), _acc(1, D), WHOLE, _acc(1, 2 * D)], out_specs=[_blk3(NDEV, tm, PWB), _rows(tm, D)],
               out_shape=[_sds((NDEV, t, PWB), BF16), _sds((t, D), F32)])[0]


def _halo_specs(t, tm):
    per = tm // HALO
    last = t // HALO - 1
    return [
        pl.BlockSpec((HALO, D), lambda i: (jnp.maximum(i * per - 1, 0), 0)),
        _rows(tm, D),
        pl.BlockSpec((HALO, D), lambda i: (jnp.minimum((i + 1) * per, last), 0)),
    ]


SUB = 8
CCH = 32
CLN = 256


def _fill_shifted(sh, prev, cur, nxt, tm):
    i = pl.program_id(0)
    rows = jnp.concatenate([jnp.where(i == 0, 0.0, prev[...]), cur[...], jnp.where(i == pl.num_programs(0) - 1, 0.0, nxt[...])], axis=0)
    n = tm + 2 * HALO - SUB
    for b in range(SUB):
        sh[b] = rows[b:b + n]


def _shifted(sh, off, r0, c0):
    return sh[off % SUB, r0 + off - off % SUB:r0 + off - off % SUB + CCH, c0:c0 + CLN]


def _conv_mid(glu, wdw, bdw, tm, xchg=None):
    t = glu.shape[0]

    def body(prev, cur, nxt, w_ref, b_ref, out_ref, sh):
        _fill_shifted(sh, prev, cur, nxt, tm)
        for c0 in range(0, D, CLN):
            for r0 in range(0, tm, CCH):
                acc = jnp.broadcast_to(b_ref[:, c0:c0 + CLN], (CCH, CLN))
                for k in range(CW):
                    acc = acc + w_ref[k:k + 1, c0:c0 + CLN] * _shifted(sh, k + HALO - CPAD, r0, c0)
                out_ref[r0:r0 + CCH, c0:c0 + CLN] = acc

    outs, xo = _pc(body, [glu, glu, glu, wdw, bdw], name="conv_mid", grid=(t // tm,),
                   in_specs=_halo_specs(t, tm) + [_acc(32, D), _acc(1, D)], out_specs=[_rows(tm, D)],
                   out_shape=[_sds((t, D), F32)], scratch_shapes=[pltpu.VMEM((SUB, tm + 2 * HALO - SUB, D), F32)], xchg=xchg)
    return outs[0], xo


def _conv_mid_bwd(dcv, glu, wdw, tm, xchg=None):
    t = glu.shape[0]

    def body(dp, dc, dn, gp, gc, gn, w_ref, dglu_ref, dw_ref, dsh, gsh):
        @pl.when(pl.program_id(0) == 0)
        def _():
            dw_ref[...] = jnp.zeros_like(dw_ref)

        _fill_shifted(dsh, dp, dc, dn, tm)
        _fill_shifted(gsh, gp, gc, gn, tm)
        for c0 in range(0, D, CLN):
            dwk = [jnp.zeros((SUB, CLN), F32) for _ in range(CW)]
            for r0 in range(0, tm, CCH):
                d0 = _shifted(dsh, HALO, r0, c0)
                acc = jnp.zeros((CCH, CLN), F32)
                for k in range(CW):
                    acc = acc + w_ref[k:k + 1, c0:c0 + CLN] * _shifted(dsh, HALO + CPAD - k, r0, c0)
                    prod = d0 * _shifted(gsh, k + HALO - CPAD, r0, c0)
                    for r in range(0, CCH, SUB):
                        dwk[k] = dwk[k] + prod[r:r + SUB]
                dglu_ref[r0:r0 + CCH, c0:c0 + CLN] = acc
            for k in range(CW):
                dw_ref[k:k + 1, c0:c0 + CLN] += jnp.sum(dwk[k], axis=0, keepdims=True)

    n = tm + 2 * HALO - SUB
    return _pc(body, [dcv, dcv, dcv, glu, glu, glu, wdw], name="conv_mid_bwd", grid=(t // tm,),
               in_specs=_halo_specs(t, tm) + _halo_specs(t, tm) + [_acc(32, D)], out_specs=[_rows(tm, D), _acc(32, D)],
               out_shape=[_sds((t, D), F32), _sds((32, D), F32)],
               scratch_shapes=[pltpu.VMEM((SUB, n, D), F32), pltpu.VMEM((SUB, n, D), F32)], xchg=xchg)


def _ln(cv, lg, lb):
    mu = jnp.mean(cv, axis=-1, keepdims=True)
    cc = cv - mu
    rs = lax.rsqrt(jnp.mean(cc * cc, axis=-1, keepdims=True) + EPS)
    lh = cc * rs
    return lh, rs, lh * lg + lb


def _conv_post(cv, x, lg, lb, w2, b2, tm):
    t = x.shape[0]

    def body(cv_ref, x_ref, lg_ref, lb_ref, w_ref, b_ref, st_ref, out_ref):
        _, _, ln = _ln(cv_ref[...], lg_ref[...], lb_ref[...])
        s = (ln * _sigmoid(ln)).astype(BF16)
        st_ref[...] = s.T
        out_ref[...] = x_ref[...] + _dot(s, w_ref[...]) + b_ref[...]

    return _pc(body, [cv, x, lg, lb, w2, b2], name="conv_post", grid=(t // tm,),
               in_specs=[_rows(tm, D), _rows(tm, D), _acc(1, D), _acc(1, D), WHOLE, _acc(1, D)],
               out_specs=[_cols(tm), _rows(tm, D)], out_shape=[_sds((D, t), BF16), _sds((t, D), F32)])[0]


def _conv_post_bwd(dx, cv, lg, lb, w2, tm, xchg=None):
    t = dx.shape[0]

    def body(dx_ref, cv_ref, lg_ref, lb_ref, w_ref, dcv_ref, part_ref):
        @pl.when(pl.program_id(0) == 0)
        def _():
            part_ref[...] = jnp.zeros_like(part_ref)

        dx = dx_ref[...]
        lg = lg_ref[...]
        lh, rs, ln = _ln(cv_ref[...], lg, lb_ref[...])
        sg = _sigmoid(ln)
        dln = _dot_nt(dx, w_ref[...]) * (sg * (1.0 + ln * (1.0 - sg)))
        dlh = dln * lg
        dcv = rs * (dlh - jnp.mean(dlh, axis=-1, keepdims=True) - lh * jnp.mean(dlh * lh, axis=-1, keepdims=True))
        dcv_ref[...] = dcv
        part_ref[0:1, :] += jnp.sum(dln * lh, axis=0, keepdims=True)
        part_ref[1:2, :] += jnp.sum(dln, axis=0, keepdims=True)
        part_ref[2:3, :] += jnp.sum(dcv, axis=0, keepdims=True)
        part_ref[3:4, :] += jnp.sum(dx, axis=0, keepdims=True)

    return _pc(body, [dx, cv, lg, lb, w2], name="conv_post_bwd", grid=(t // tm,),
               in_specs=[_rows(tm, D), _rows(tm, D), _acc(1, D), _acc(1, D), WHOLE], out_specs=[_rows(tm, D), _acc(8, D)],
               out_shape=[_sds((t, D), F32), _sds((8, D), F32)], xchg=xchg)


def _conv_pre_bwd(dglu, u, x, g, w, dx_out, tm):
    t = x.shape[0]

    def body(dglu_ref, u_ref, x_ref, g_ref, w_ref, dxo_ref, du_ref, ht_ref, dx_ref, dxt_ref, dg_ref, db_ref):
        @pl.when(pl.program_id(0) == 0)
        def _():
            dg_ref[...] = jnp.zeros_like(dg_ref)
            db_ref[...] = jnp.zeros_like(db_ref)

        gn = g_ref[...]
        xh, r, h = _rms(x_ref[...], gn)
        ht_ref[...] = h.astype(BF16).T
        dh = jnp.zeros_like(xh)
        for j in range(4):
            a = u_ref[j].astype(F32)
            sg = _sigmoid(u_ref[j + 4].astype(F32))
            dgl = dglu_ref[:, j * PWB:(j + 1) * PWB]
            da = dgl * sg
            dgt = dgl * a * sg * (1.0 - sg)
            db_ref[:, j * PWB:(j + 1) * PWB] += jnp.sum(da, axis=0, keepdims=True)
            db_ref[:, D + j * PWB:D + (j + 1) * PWB] += jnp.sum(dgt, axis=0, keepdims=True)
            da, dgt = da.astype(BF16), dgt.astype(BF16)
            du_ref[j] = da
            du_ref[j + 4] = dgt
            dh = dh + _dot_nt(da, w_ref[j]) + _dot_nt(dgt, w_ref[j + 4])
        dx, dg = _rms_bwd(dh, xh, r, gn)
        dx = dxo_ref[...] + dx
        dx_ref[...] = dx
        dxt_ref[...] = dx.astype(BF16).T
        dg_ref[...] += dg

    return _pc(body, [dglu, u, x, g, w, dx_out], name="conv_pre_bwd", grid=(t // tm,),
               in_specs=[_rows(tm, D), _blk3(NDEV, tm, PWB), _rows(tm, D), _acc(1, D), WHOLE, _rows(tm, D)],
               out_specs=[_blk3(NDEV, tm, PWB), _cols(tm), _rows(tm, D), _cols(tm), _acc(1, D), _acc(1, 2 * D)],
               out_shape=[_sds((NDEV, t, PWB), BF16), _sds((D, t), BF16), _sds((t, D), F32), _sds((D, t), BF16),
                          _sds((1, D), F32), _sds((1, 2 * D), F32)])[0]


def _adamw(w, g, m, v):
    m = B1 * m + (1.0 - B1) * g
    v = B2 * v + (1.0 - B2) * (g * g)
    m_hat = m / (1.0 - B1 ** STEP)
    v_hat = v / (1.0 - B2 ** STEP)
    return -LR * (m_hat / (jnp.sqrt(v_hat) + AEPS) + WD * w), m, v


def _reduce_adamw(land, w, m, v, tr, name):
    _, r, c = land.shape

    def body(l_ref, w_ref, m_ref, v_ref, g_ref, d_ref, nm_ref, nv_ref):
        g = l_ref[0].astype(F32)
        for j in range(1, NDEV):
            g = g + l_ref[j].astype(F32)
        g_ref[...] = g
        d_ref[...], nm_ref[...], nv_ref[...] = _adamw(w_ref[...], g, m_ref[...], v_ref[...])

    return _pc(body, [land, w, m, v], name=name, grid=(r // tr,),
               in_specs=[_blk3(NDEV, tr, c)] + [_rows(tr, c)] * 3, out_specs=[_rows(tr, c)] * 4,
               out_shape=[_sds((r, c), F32)] * 4)[0]


def _sum_parts(parts):
    _, r, c = parts.shape

    def body(p_ref, out_ref):
        s = p_ref[0]
        for j in range(1, NDEV):
            s = s + p_ref[j]
        out_ref[...] = s

    return _pc(body, [parts], name="sum_parts", in_specs=[WHOLE], out_specs=[WHOLE], out_shape=[_sds((r, c), F32)])[0][0]


def _adamw_small(w, g, m, v):
    def body(w_ref, g_ref, m_ref, v_ref, d_ref, nm_ref, nv_ref):
        d_ref[...], nm_ref[...], nv_ref[...] = _adamw(w_ref[...], g_ref[...], m_ref[...], v_ref[...])

    return _pc(body, [w, g, m, v], name="adamw_small", in_specs=[WHOLE] * 4, out_specs=[WHOLE] * 3,
               out_shape=[_sds(w.shape, F32)] * 3)[0]


def _rows128(a):
    a = a.astype(F32)
    if a.shape[-1] % 128:
        a = jnp.pad(a, [(0, 0)] * (a.ndim - 1) + [(0, 128 - a.shape[-1] % 128)])
    return a.reshape(-1, 128)


def _pack(arrs, rows):
    p = jnp.concatenate([_rows128(a) for a in arrs], axis=0)
    return jnp.pad(p, ((0, rows - p.shape[0]), (0, 0)))


SMALL = ("attn_norm", "ffn_norm", "final_norm", "attn_sink", "conv_norm", "conv_b_dw", "conv_ln_g", "conv_ln_b", "conv_b_pw2",
         "conv_b_pw1", "conv_w_dw")
SMALL_ROWS = 72


def _pack_small(d):
    return _pack([d[k] for k in SMALL], SMALL_ROWS)


def _unpack_small(p, like):
    out, r = {}, 0
    for k in SMALL:
        shp = like[k].shape
        n = -(-shp[-1] // 128) * (math.prod(shp[:-1]))
        blk = p[r:r + n]
        if shp[-1] % 128:
            blk = blk[:, :shp[-1]]
        out[k] = blk.reshape(shp)
        r += n
    return out


BIG = ("attn_w_qkv", "attn_w_o", "conv_w_pw1", "conv_w_pw2", "ffn_w_gu0", "ffn_w_gu1", "ffn_w_down0", "ffn_w_down1")
NAMES = ("attn_norm", "attn_w_qkv", "attn_w_o", "attn_sink", "conv_norm", "conv_w_pw1", "conv_b_pw1", "conv_w_dw", "conv_b_dw",
         "conv_ln_g", "conv_ln_b", "conv_w_pw2", "conv_b_pw2", "ffn_norm", "ffn_w_gu", "ffn_w_down", "final_norm")
TM = 256
TL = 512
TK = 2048


def _split_layers(d):
    return {
        "attn_w_qkv": d["attn_w_qkv"][0], "attn_w_o": d["attn_w_o"][0], "conv_w_pw1": d["conv_w_pw1"][0], "conv_w_pw2": d["conv_w_pw2"][0],
        "ffn_w_gu0": d["ffn_w_gu"][0], "ffn_w_gu1": d["ffn_w_gu"][1], "ffn_w_down0": d["ffn_w_down"][0], "ffn_w_down1": d["ffn_w_down"][1],
    }


def _join_layers(d):
    return {
        "attn_w_qkv": d["attn_w_qkv"][None], "attn_w_o": d["attn_w_o"][None], "conv_w_pw1": d["conv_w_pw1"][None],
        "conv_w_pw2": d["conv_w_pw2"][None], "ffn_w_gu": jnp.stack([d["ffn_w_gu0"], d["ffn_w_gu1"]]),
        "ffn_w_down": jnp.stack([d["ffn_w_down0"], d["ffn_w_down1"]]),
    }


def _down_blocks(dwt):
    return dwt.reshape(4, D, 2, GUB // 2).transpose(0, 2, 3, 1).reshape(NDEV, DFF // NDEV, D)


def kernel(x, attn_norm, attn_w_qkv, attn_w_o, attn_sink, conv_norm, conv_w_pw1, conv_b_pw1, conv_w_dw, conv_b_dw, conv_ln_g, conv_ln_b, conv_w_pw2, conv_b_pw2, ffn_norm, ffn_w_gu, ffn_w_down, final_norm, loss_target, m_attn_norm, m_attn_w_qkv, m_attn_w_o, m_attn_sink, m_conv_norm, m_conv_w_pw1, m_conv_b_pw1, m_conv_w_dw, m_conv_b_dw, m_conv_ln_g, m_conv_ln_b, m_conv_w_pw2, m_conv_b_pw2, m_ffn_norm, m_ffn_w_gu, m_ffn_w_down, m_final_norm, v_attn_norm, v_attn_w_qkv, v_attn_w_o, v_attn_sink, v_conv_norm, v_conv_w_pw1, v_conv_b_pw1, v_conv_w_dw, v_conv_b_dw, v_conv_ln_g, v_conv_ln_b, v_conv_w_pw2, v_conv_b_pw2, v_ffn_norm, v_ffn_w_gu, v_ffn_w_down, v_final_norm):
    w = dict(zip(NAMES, (attn_norm, attn_w_qkv, attn_w_o, attn_sink, conv_norm, conv_w_pw1, conv_b_pw1, conv_w_dw, conv_b_dw, conv_ln_g,
                         conv_ln_b, conv_w_pw2, conv_b_pw2, ffn_norm, ffn_w_gu, ffn_w_down, final_norm)))
    m = dict(zip(NAMES, (m_attn_norm, m_attn_w_qkv, m_attn_w_o, m_attn_sink, m_conv_norm, m_conv_w_pw1, m_conv_b_pw1, m_conv_w_dw,
                         m_conv_b_dw, m_conv_ln_g, m_conv_ln_b, m_conv_w_pw2, m_conv_b_pw2, m_ffn_norm, m_ffn_w_gu, m_ffn_w_down,
                         m_final_norm)))
    v = dict(zip(NAMES, (v_attn_norm, v_attn_w_qkv, v_attn_w_o, v_attn_sink, v_conv_norm, v_conv_w_pw1, v_conv_b_pw1, v_conv_w_dw,
                         v_conv_b_dw, v_conv_ln_g, v_conv_ln_b, v_conv_w_pw2, v_conv_b_pw2, v_ffn_norm, v_ffn_w_gu, v_ffn_w_down,
                         v_final_norm)))
    me = 4 * lax.axis_index("x") + 2 * lax.axis_index("y") + lax.axis_index("c")
    wb, mb, vb = _split_layers(w), _split_layers(m), _split_layers(v)
    sh = {k: wb[k].astype(BF16) for k in BIG}
    x0, tgt = x[0], loss_target[0]
    t = x0.shape[0]
    tabs = _rope_tables(t)
    g_a, sink, g_f0, g_f1, g_fin = w["attn_norm"], w["attn_sink"][0], w["ffn_norm"][0:1], w["ffn_norm"][1:2], w["final_norm"][None]
    gather, scatter = False, True

    shard_rows = _pack([w["conv_w_dw"][0], jnp.zeros((1, 128), F32), w["conv_norm"], w["conv_b_dw"], w["conv_ln_g"], w["conv_ln_b"],
                        w["conv_b_pw2"], w["conv_b_pw1"]], 40)
    wqkv_g, sm = _exchange([sh["attn_w_qkv"], shard_rows], [gather] * 2, "gather_attn")
    wqkv = wqkv_g.transpose(1, 0, 2).reshape(D, QKV)

    def full_vec(row, n=1):
        return sm[:, row:row + n, :].reshape(1, NDEV * n * 128)

    w_dw, g_c, b_dw, ln_g, ln_b = sm[:, 0:32, :].transpose(1, 0, 2).reshape(32, D), full_vec(32), full_vec(33), full_vec(34), full_vec(35)
    b_pw2, b_pw1 = full_vec(36), full_vec(37, 2)

    (q_t, kv), (wd0_g,) = _attn_pre(x0, g_a, wqkv, tabs, TM, xchg=([sh["ffn_w_down0"]], [gather]))
    o_t, (wo_g, wgu0) = _attn_fwd(q_t, kv, sink, xchg=([sh["attn_w_o"], sh["ffn_w_gu0"]], [gather] * 2))
    wo, wd0 = wo_g.reshape(D, D), wd0_g.reshape(4, GUB, D)
    (x1, gu0, x2), (wpw1, wpw2_g, wgu1) = _ffn_fwd_attn(
        o_t, x0, wo, g_f0, wgu0, wd0, TM, xchg=([sh["conv_w_pw1"], sh["conv_w_pw2"], sh["ffn_w_gu1"]], [gather] * 3))
    wpw2 = wpw2_g.reshape(D, D)
    tl = min(TL, t)
    u, glu = _conv_pre(x2, g_c, wpw1, b_pw1, tl)
    cv, (wd1_g,) = _conv_mid(glu, w_dw, b_dw, TM, xchg=([sh["ffn_w_down1"]], [gather]))
    wd1 = wd1_g.reshape(4, GUB, D)
    s_t, x3 = _conv_post(cv, x2, ln_g, ln_b, wpw2, b_pw2, tl)
    (gu1, dx4, dx4_t, fin), _ = _ffn_fwd_final(x3, g_f1, wgu1, wd1, g_fin, tgt, TM)

    land = {}
    tk = min(TK, t)
    (dgu1, act1, h3_t, dx3, _, dg_f1), _ = _ffn_bwd(dx4, x3, gu1, g_f1, wgu1, wd1, TM, "ffn_bwd1")
    dwgu1 = _wgrad(h3_t, dgu1, tk, "dwgu1")
    dwd1 = _down_blocks(_wgrad(dx4_t, act1, tk, "dwd1"))
    (dcv, cpart), (land["ffn_w_down1"],) = _conv_post_bwd(dx3, cv, ln_g, ln_b, wpw2, tl, xchg=([dwd1], [scatter]))
    dwpw2 = _wgrad(s_t, dx3[None], tk, "dwpw2").reshape(NDEV, D // NDEV, D)
    (dglu, dw_dw), (land["ffn_w_gu1"], land["conv_w_pw2"]) = _conv_mid_bwd(dcv, glu, w_dw, TM, xchg=([dwgu1, dwpw2], [scatter] * 2))
    du, h2_t, dx2, dx2_t, dg_c, db_pw1 = _conv_pre_bwd(dglu, u, x2, g_c, wpw1, dx3, tl)
    dwpw1 = _wgrad(h2_t, du, tk, "dwpw1")
    (dgu0, act0, h1_t, dx1, _, dg_f0, do_t), (land["conv_w_pw1"],) = _ffn_bwd(
        dx2, x1, gu0, g_f0, wgu0, wd0, TM, "ffn_bwd0", wo=wo, xchg=([dwpw1], [scatter]))
    dwgu0 = _wgrad(h1_t, dgu0, tk, "dwgu0")
    dwd0 = _down_blocks(_wgrad(dx2_t, act0, tk, "dwd0"))
    dwo = _wgrad(o_t, dx1[None], tk, "dwo").reshape(NDEV, D // NDEV, D)
    (dq_t, dk, dv, dsink), (land["ffn_w_gu0"], land["ffn_w_down0"], land["attn_w_o"]) = _attn_bwd(
        q_t, kv, do_t, sink, TM, xchg=([dwgu0, dwd0, dwo], [scatter] * 3))
    dqkv, h0_t, dx0, dg_a = _attn_pre_bwd(dq_t, dk, dv, x0, g_a, wqkv, tabs, dx1, TM)
    dwqkv = _wgrad(h0_t, dqkv[None], tk, "dwqkv")[0].reshape(D, NDEV, QKV // NDEV).transpose(1, 0, 2)

    lane0 = (lax.broadcasted_iota(jnp.int32, (1, 128), 1) == 0).astype(F32)
    parts = _pack([dg_a, dg_f0, dg_f1, fin[0:1], dsink[0:1, :NH], fin[1, 0] * lane0, jnp.zeros((6, 128), F32), dg_c, cpart[2:3],
                   cpart[0:1], cpart[1:2], cpart[3:4], db_pw1, dw_dw.reshape(32, NDEV, 128)], 352)
    land["attn_w_qkv"], parts_g = _exchange([dwqkv, parts], [scatter, gather], "scatter_attn")
    red = _sum_parts(parts_g)

    def shard_rows_of(row, n=1):
        return lax.dynamic_slice_in_dim(red, row + n * me, n, axis=0)

    gs = {
        "attn_norm": red[0:8].reshape(1, D), "ffn_norm": red[8:24].reshape(2, D), "final_norm": red[24:32].reshape(D),
        "attn_sink": red[32:33, :NH], "conv_norm": shard_rows_of(40), "conv_b_dw": shard_rows_of(48), "conv_ln_g": shard_rows_of(56),
        "conv_ln_b": shard_rows_of(64), "conv_b_pw2": shard_rows_of(72), "conv_b_pw1": shard_rows_of(80, 2).reshape(1, PWB),
        "conv_w_dw": lax.dynamic_index_in_dim(red[96:352].reshape(32, NDEV, 128), me, axis=1, keepdims=False)[None, :CW],
    }
    loss = red[33, 0]

    grads, deltas, new_m, new_v = dict(gs), {}, {}, {}
    ds, ms, vs = _adamw_small(_pack_small(w), _pack_small(gs), _pack_small(m), _pack_small(v))
    deltas.update(_unpack_small(ds, gs))
    new_m.update(_unpack_small(ms, gs))
    new_v.update(_unpack_small(vs, gs))
    gb, db, nmb, nvb = {}, {}, {}, {}
    for k in BIG:
        tr = {1024: 256, 128: 128, 352: 176}[land[k].shape[1]]
        gb[k], db[k], nmb[k], nvb[k] = _reduce_adamw(land[k], wb[k], mb[k], vb[k], tr, "adamw_" + k)
    grads.update(_join_layers(gb))
    deltas.update(_join_layers(db))
    new_m.update(_join_layers(nmb))
    new_v.update(_join_layers(nvb))
    return (loss, dx0[None], *[grads[k] for k in NAMES], *[deltas[k] for k in NAMES], *[new_m[k] for k in NAMES],
            *[new_v[k] for k in NAMES])
```

```python
import math

import jax
import jax.numpy as jnp
from jax import lax
from jax.experimental import pallas as pl
from jax.experimental.pallas import tpu as pltpu

F32 = jnp.float32
BF16 = jnp.bfloat16

D = 1024
NH = 16
NKV = 4
HD = 64
GROUP = NH // NKV
ROT = 16
THETA = 500000.0
BLK = 128
QKV = (NH + 2 * NKV) * HD
KOFF = NH * HD
VOFF = KOFF + NKV * HD
DFF = 2816
NDEV = 8
GUB = 2 * DFF // NDEV
PWB = 2 * D // NDEV
CW = 31
CPAD = 15
HALO = 16
EPS = 1e-6
NEG = -1e30
LR, B1, B2, AEPS, WD, STEP = 0.001, 0.9, 0.999, 1e-08, 0.01, 10

VMEM_LIMIT = 56 * 1024 * 1024
MESH = pl.DeviceIdType.MESH
WHOLE = pl.BlockSpec(memory_space=pltpu.VMEM)
ANY = pl.BlockSpec(memory_space=pl.ANY)


def _place():
    x, y, c = lax.axis_index("x"), lax.axis_index("y"), lax.axis_index("c")
    return x, y, c, 4 * x + 2 * y + c


def _peer(x, y, c, j):
    px = 1 - x if j & 4 else x
    py = 1 - y if j & 2 else y
    pc = 1 - c if j & 1 else c
    return (px, py, pc), 4 * px + 2 * py + pc


def _xchg_copies(src, dst, scatter, send, recv, loc):
    x, y, c, me = _place()
    out = []
    for k in range(len(src)):
        if scatter[k]:
            local = pltpu.make_async_copy(src[k].at[me], dst[k].at[0], loc.at[k])
        else:
            local = pltpu.make_async_copy(src[k], dst[k].at[me], loc.at[k])
        pairs = []
        for j in range(1, NDEV):
            peer, pidx = _peer(x, y, c, j)
            if scatter[k]:
                s, d_out, d_in = src[k].at[pidx], dst[k].at[j], dst[k].at[j]
            else:
                s, d_out, d_in = src[k], dst[k].at[me], dst[k].at[pidx]
            sems = dict(send_sem=send.at[k, j - 1], recv_sem=recv.at[k, j - 1], device_id=peer, device_id_type=MESH)
            pairs.append((pltpu.make_async_remote_copy(src_ref=s, dst_ref=d_out, **sems),
                          pltpu.make_async_remote_copy(src_ref=s, dst_ref=d_in, **sems)))
        out.append((local, pairs))
    return out


def _xchg_start(cps):
    for local, pairs in cps:
        local.start()
        for snd, _ in pairs:
            snd.start()


def _xchg_wait(cps):
    for local, pairs in cps:
        for _, rcv in pairs:
            rcv.wait_recv()
        for snd, _ in pairs:
            snd.wait_send()
        local.wait()


def _call(body, **kw):
    return pl.pallas_call(body, **kw)


def _pc(body, operands, *, name, in_specs, out_specs, out_shape, grid=None, scratch_shapes=(), xchg=None):
    kw = dict(name=name, compiler_params=pltpu.CompilerParams(
        dimension_semantics=None if grid is None else ("arbitrary",) * len(grid), vmem_limit_bytes=VMEM_LIMIT,
        has_side_effects=xchg is not None))
    if grid is not None:
        kw["grid"] = grid
    if xchg is None:
        outs = _call(body, in_specs=list(in_specs), out_specs=list(out_specs), out_shape=list(out_shape),
                     scratch_shapes=list(scratch_shapes), **kw)(*operands)
        return list(outs), []
    arrs, scatter = xchg
    nx, n_in, n_out, n_s = len(arrs), len(in_specs), len(out_specs), len(scratch_shapes)

    def wrapped(*refs):
        ins, refs = refs[:n_in], refs[n_in:]
        src, refs = refs[:nx], refs[nx:]
        outs, refs = refs[:n_out], refs[n_out:]
        dst, refs = refs[:nx], refs[nx:]
        scr, (send, recv, loc) = refs[:n_s], refs[n_s:]
        if grid is None:
            cps = _xchg_copies(src, dst, scatter, send, recv, loc)
            _xchg_start(cps)
            body(*ins, *outs, *scr)
            _xchg_wait(cps)
            return

        @pl.when(pl.program_id(0) == 0)
        def _():
            _xchg_start(_xchg_copies(src, dst, scatter, send, recv, loc))

        body(*ins, *outs, *scr)

        @pl.when(pl.program_id(0) == grid[0] - 1)
        def _():
            _xchg_wait(_xchg_copies(src, dst, scatter, send, recv, loc))

    xshape = [jax.ShapeDtypeStruct(a.shape if sc else (NDEV,) + a.shape, a.dtype) for a, sc in zip(arrs, scatter)]
    sems = [pltpu.SemaphoreType.DMA((nx, NDEV - 1)), pltpu.SemaphoreType.DMA((nx, NDEV - 1)), pltpu.SemaphoreType.DMA((nx,))]
    res = _call(wrapped, in_specs=list(in_specs) + [ANY] * nx, out_specs=list(out_specs) + [ANY] * nx,
                out_shape=list(out_shape) + xshape, scratch_shapes=list(scratch_shapes) + sems, **kw)(*operands, *arrs)
    return list(res[:n_out]), list(res[n_out:])


def _exchange(arrs, scatter, name):
    return _pc(lambda: None, [], name=name, in_specs=[], out_specs=[], out_shape=[], xchg=(arrs, scatter))[1]


def _rows(tm, width):
    return pl.BlockSpec((tm, width), lambda i: (i, 0))


def _cols(tm):
    return pl.BlockSpec((D, tm), lambda i: (0, i))


def _blk3(nb, tm, width):
    return pl.BlockSpec((nb, tm, width), lambda i: (0, i, 0))


def _acc(rows, width):
    return pl.BlockSpec((rows, width), lambda i: (0, 0))


def _sds(shape, dtype):
    return jax.ShapeDtypeStruct(shape, dtype)


def _dot(a, b):
    return jnp.dot(a.astype(BF16), b.astype(BF16), preferred_element_type=F32)


def _dot_nt(a, b):
    return lax.dot_general(a.astype(BF16), b.astype(BF16), (((1,), (1,)), ((), ())), preferred_element_type=F32)


def _dot_tn(a, b):
    return lax.dot_general(a.astype(BF16), b.astype(BF16), (((0,), (0,)), ((), ())), preferred_element_type=F32)


def _sigmoid(x):
    return 1.0 / (1.0 + jnp.exp(-x))


def _rms(x, g):
    r = lax.rsqrt(jnp.mean(x * x, axis=-1, keepdims=True) + EPS)
    xh = x * r
    return xh, r, xh * g


def _rms_bwd(dh, xh, r, g):
    dxh = dh * g
    dg = jnp.sum(dh * xh, axis=0, keepdims=True)
    dx = r * (dxh - xh * jnp.mean(dxh * xh, axis=-1, keepdims=True))
    return dx, dg


def _lanes(t, width):
    return jnp.tile(t, (1, width // t.shape[1]))


def _rope(z, c, sa, sb):
    w = z.shape[1]
    return z * _lanes(c, w) + pltpu.roll(z, w - 8, 1) * _lanes(sa, w) + pltpu.roll(z, 8, 1) * _lanes(sb, w)


def _rope_t(dz, c, sa, sb):
    w = dz.shape[1]
    return dz * _lanes(c, w) + pltpu.roll(dz * _lanes(sa, w), 8, 1) + pltpu.roll(dz * _lanes(sb, w), w - 8, 1)


def _rope_tables(t):
    pos = jnp.arange(t, dtype=F32)
    inv_freq = THETA ** (-jnp.arange(0, ROT, 2, dtype=F32) / ROT)
    ang = pos[:, None] * inv_freq[None, :]
    cos, sin = jnp.cos(ang), jnp.sin(ang)
    one = jnp.ones((t, HD - ROT), F32)
    zero = jnp.zeros((t, HD - ROT), F32)
    z8 = jnp.zeros((t, 8), F32)
    c = jnp.concatenate([cos, cos, one], axis=1)
    sa = jnp.concatenate([-sin, z8, zero], axis=1)
    sb = jnp.concatenate([z8, sin, zero], axis=1)
    return tuple(jnp.tile(a, (1, 2)) for a in (c, sa, sb))


KVW = NKV * HD
GW = GROUP * BLK


def _attn_pre(x, g, wqkv, tabs, tm, xchg=None):
    t = x.shape[0]

    def body(x_ref, g_ref, w_ref, c_ref, sa_ref, sb_ref, qt_ref, kv_ref):
        _, _, h = _rms(x_ref[...], g_ref[...])
        z = _dot(h, w_ref[...])
        c, sa, sb = c_ref[...], sa_ref[...], sb_ref[...]
        qt_ref[...] = (_rope(z[:, :KOFF], c, sa, sb) * 0.125).T.astype(BF16)
        kv_ref[:, :KVW] = _rope(z[:, KOFF:VOFF], c, sa, sb).astype(BF16)
        kv_ref[:, KVW:] = z[:, VOFF:].astype(BF16)

    return _pc(body, [x, g, wqkv, *tabs], name="attn_pre", grid=(t // tm,),
               in_specs=[_rows(tm, D), _acc(1, D), WHOLE, _rows(tm, 128), _rows(tm, 128), _rows(tm, 128)],
               out_specs=[_cols(tm), _rows(tm, 2 * KVW)], out_shape=[_sds((D, t), BF16), _sds((t, 2 * KVW), BF16)], xchg=xchg)


def _attn_specs(nblk):
    prev = lambda i: jnp.maximum(i - 1, 0)
    nxt = lambda i: jnp.minimum(i + 1, nblk - 1)
    return [
        pl.BlockSpec((D, BLK), lambda i: (0, i)),
        pl.BlockSpec((BLK, KVW), lambda i: (prev(i), 0)),
        pl.BlockSpec((BLK, KVW), lambda i: (i, 0)),
        pl.BlockSpec((BLK, KVW), lambda i: (nxt(i), 0)),
        pl.BlockSpec((BLK, KVW), lambda i: (prev(i), 1)),
        pl.BlockSpec((BLK, KVW), lambda i: (i, 1)),
        pl.BlockSpec((BLK, KVW), lambda i: (nxt(i), 1)),
    ]


def _attn_mask(i, nblk):
    c = lax.broadcasted_iota(jnp.int32, (3 * BLK, GW), 0)
    r = lax.broadcasted_iota(jnp.int32, (3 * BLK, GW), 1) & (BLK - 1)
    lo = jnp.where(i == 0, BLK, 0)
    hi = jnp.where(i == nblk - 1, 2 * BLK, 3 * BLK)
    return (c >= r) & (c - 2 * BLK <= r) & (c >= lo) & (c < hi)


def _group(ref, kv):
    return jnp.concatenate([ref[(kv * GROUP + g) * HD:(kv * GROUP + g + 1) * HD, :] for g in range(GROUP)], axis=1)


def _group_sink(sink_ref, kv):
    return jnp.concatenate([jnp.full((1, BLK), sink_ref[kv * GROUP + g], F32) for g in range(GROUP)], axis=1)


def _attn_probs(k_h, qt_g, valid, sink_g):
    s = jnp.where(valid, _dot(k_h, qt_g), NEG)
    m = jnp.maximum(jnp.max(s, axis=0, keepdims=True), sink_g)
    e = jnp.exp(s - m)
    es = jnp.exp(sink_g - m)
    inv = 1.0 / (jnp.sum(e, axis=0, keepdims=True) + es)
    return e * inv, es * inv


def _attn_fwd(qt, kv, sink, xchg=None):
    t = kv.shape[0]
    nblk = t // BLK

    def body(sink_ref, qt_ref, kp, kc, kn, vp, vc, vn, ot_ref):
        valid = _attn_mask(pl.program_id(0), nblk)
        k = jnp.concatenate([kp[...], kc[...], kn[...]], axis=0)
        v = jnp.concatenate([vp[...], vc[...], vn[...]], axis=0)
        for h in range(NKV):
            p, _ = _attn_probs(k[:, h * HD:(h + 1) * HD], _group(qt_ref, h), valid, _group_sink(sink_ref, h))
            ot_g = _dot_tn(v[:, h * HD:(h + 1) * HD], p).astype(BF16)
            for g in range(GROUP):
                ot_ref[(h * GROUP + g) * HD:(h * GROUP + g + 1) * HD, :] = ot_g[:, g * BLK:(g + 1) * BLK]

    outs, xo = _pc(body, [sink, qt] + [kv] * 6, name="attn_fwd", grid=(nblk,),
                   in_specs=[pl.BlockSpec(memory_space=pltpu.SMEM)] + _attn_specs(nblk),
                   out_specs=[pl.BlockSpec((D, BLK), lambda i: (0, i))], out_shape=[_sds((D, t), BF16)], xchg=xchg)
    return outs[0], xo


def _attn_bwd(qt, kv, dot, sink, pad, xchg=None):
    t = kv.shape[0]
    nblk = t // BLK

    def body(sink_ref, qt_ref, kp, kc, kn, vp, vc, vn, dot_ref, dqt_ref, dk_ref, dv_ref, ds_ref):
        i = pl.program_id(0)

        @pl.when(i == 0)
        def _():
            dk_ref[...] = jnp.zeros_like(dk_ref)
            dv_ref[...] = jnp.zeros_like(dv_ref)
            ds_ref[...] = jnp.zeros_like(ds_ref)

        valid = _attn_mask(i, nblk)
        k = jnp.concatenate([kp[...], kc[...], kn[...]], axis=0)
        v = jnp.concatenate([vp[...], vc[...], vn[...]], axis=0)
        lane = lax.broadcasted_iota(jnp.int32, (1, 128), 1)
        dsink = jnp.zeros((1, 128), F32)
        rows = pl.ds(pl.multiple_of(i * BLK + (pad - BLK), BLK), 3 * BLK)
        for h in range(NKV):
            k_h, v_h = k[:, h * HD:(h + 1) * HD], v[:, h * HD:(h + 1) * HD]
            qt_g, dot_g = _group(qt_ref, h), _group(dot_ref, h)
            p, ps = _attn_probs(k_h, qt_g, valid, _group_sink(sink_ref, h))
            dp = _dot(v_h, dot_g)
            delta = jnp.sum(p * dp, axis=0, keepdims=True)
            ds = (p * (dp - delta)).astype(BF16)
            dqt_g = _dot_tn(k_h, ds)
            dk_ref[rows, h * HD:(h + 1) * HD] += _dot_nt(ds, qt_g)
            dv_ref[rows, h * HD:(h + 1) * HD] += _dot_nt(p, dot_g)
            psd = ps * delta
            for g in range(GROUP):
                n = h * GROUP + g
                dqt_ref[n * HD:(n + 1) * HD, :] = dqt_g[:, g * BLK:(g + 1) * BLK]
                dsink = dsink - jnp.where(lane == n, jnp.sum(psd[:, g * BLK:(g + 1) * BLK], axis=1, keepdims=True), 0.0)
        ds_ref[0:1, :] += dsink

    outs, xo = _pc(body, [sink, qt] + [kv] * 6 + [dot], name="attn_bwd", grid=(nblk,),
                   in_specs=[pl.BlockSpec(memory_space=pltpu.SMEM)] + _attn_specs(nblk) + [pl.BlockSpec((D, BLK), lambda i: (0, i))],
                   out_specs=[pl.BlockSpec((D, BLK), lambda i: (0, i)), _acc(t + 2 * pad, KVW), _acc(t + 2 * pad, KVW), _acc(8, 128)],
                   out_shape=[_sds((D, t), F32), _sds((t + 2 * pad, KVW), F32), _sds((t + 2 * pad, KVW), F32), _sds((8, 128), F32)],
                   xchg=xchg)
    return outs, xo


def _attn_pre_bwd(dqt, dk, dv, x, g, wqkv, tabs, dx_out, tm):
    t = x.shape[0]

    def body(dqt_ref, dk_ref, dv_ref, x_ref, g_ref, w_ref, c_ref, sa_ref, sb_ref, dxo_ref, dqkv_ref, ht_ref, dx_ref, dg_ref):
        @pl.when(pl.program_id(0) == 0)
        def _():
            dg_ref[...] = jnp.zeros_like(dg_ref)

        c, sa, sb = c_ref[...], sa_ref[...], sb_ref[...]
        dqkv_ref[:, :KOFF] = _rope_t(dqt_ref[...].T * 0.125, c, sa, sb).astype(BF16)
        dqkv_ref[:, KOFF:VOFF] = _rope_t(dk_ref[...], c, sa, sb).astype(BF16)
        dqkv_ref[:, VOFF:] = dv_ref[...].astype(BF16)
        g = g_ref[...]
        xh, r, h = _rms(x_ref[...], g)
        ht_ref[...] = h.astype(BF16).T
        dh = _dot_nt(dqkv_ref[...], w_ref[...])
        dx, dg = _rms_bwd(dh, xh, r, g)
        dx_ref[...] = dxo_ref[...] + dx
        dg_ref[...] += dg

    return _pc(body, [dqt, dk, dv, x, g, wqkv, *tabs, dx_out], name="attn_pre_bwd", grid=(t // tm,),
               in_specs=[_cols(tm), pl.BlockSpec((tm, KVW), lambda i: (i + 1, 0)), pl.BlockSpec((tm, KVW), lambda i: (i + 1, 0)),
                         _rows(tm, D), _acc(1, D), WHOLE, _rows(tm, 128), _rows(tm, 128), _rows(tm, 128), _rows(tm, D)],
               out_specs=[_rows(tm, QKV), _cols(tm), _rows(tm, D), _acc(1, D)],
               out_shape=[_sds((t, QKV), BF16), _sds((D, t), BF16), _sds((t, D), F32), _sds((1, D), F32)])[0]


def _ffn(x, g, wgu_ref, wd_ref, gu_ref):
    _, _, h = _rms(x, g)
    hb = h.astype(BF16)
    y = x
    for j in range(4):
        gj = _dot_nt(hb, wgu_ref[j])
        uj = _dot_nt(hb, wgu_ref[j + 4])
        gu_ref[j] = gj.astype(BF16)
        gu_ref[j + 4] = uj.astype(BF16)
        y = y + _dot(gj * _sigmoid(gj) * uj, wd_ref[j])
    return y


def _ffn_fwd_attn(ot, x, wo, g, wgu, wd, tm, xchg=None):
    t = x.shape[0]

    def body(ot_ref, x_ref, wo_ref, g_ref, wgu_ref, wd_ref, x1_ref, gu_ref, out_ref):
        x1 = x_ref[...] + _dot_tn(ot_ref[...], wo_ref[...])
        x1_ref[...] = x1
        out_ref[...] = _ffn(x1, g_ref[...], wgu_ref, wd_ref, gu_ref)

    return _pc(body, [ot, x, wo, g, wgu, wd], name="ffn_fwd0", grid=(t // tm,),
               in_specs=[_cols(tm), _rows(tm, D), WHOLE, _acc(1, D), WHOLE, WHOLE],
               out_specs=[_rows(tm, D), _blk3(NDEV, tm, GUB), _rows(tm, D)],
               out_shape=[_sds((t, D), F32), _sds((NDEV, t, GUB), BF16), _sds((t, D), F32)], xchg=xchg)


def _ffn_fwd_final(x, g, wgu, wd, g_fin, tgt, tm, xchg=None):
    t = x.shape[0]

    def body(x_ref, g_ref, wgu_ref, wd_ref, gf_ref, t_ref, gu_ref, dx_ref, part_ref):
        @pl.when(pl.program_id(0) == 0)
        def _():
            part_ref[...] = jnp.zeros_like(part_ref)

        gf = gf_ref[...]
        xh, r, y = _rms(_ffn(x_ref[...], g_ref[...], wgu_ref, wd_ref, gu_ref), gf)
        err = y - t_ref[...]
        dx, dg = _rms_bwd(err * (1.0 / D), xh, r, gf)
        dx_ref[...] = dx
        part_ref[0:1, :] += dg
        tok = jnp.sum(err * err, axis=-1, keepdims=True) * (1.0 / D)
        lane = lax.broadcasted_iota(jnp.int32, (1, D), 1)
        part_ref[1:2, :] += jnp.where(lane == 0, 0.5 * jnp.sum(tok, axis=0, keepdims=True), 0.0)

    return _pc(body, [x, g, wgu, wd, g_fin, tgt], name="ffn_fwd1", grid=(t // tm,),
               in_specs=[_rows(tm, D), _acc(1, D), WHOLE, WHOLE, _acc(1, D), _rows(tm, D)],
               out_specs=[_blk3(NDEV, tm, GUB), _rows(tm, D), _acc(8, D)],
               out_shape=[_sds((NDEV, t, GUB), BF16), _sds((t, D), F32), _sds((8, D), F32)], xchg=xchg)


def _ffn_bwd(dy, x, gu, g, wgu, wd, tm, name, wo=None, xchg=None):
    t = x.shape[0]

    def body(dy_ref, x_ref, gu_ref, g_ref, wgu_ref, wd_ref, *rest):
        wo_ref = rest[0] if wo is not None else None
        dgut_ref, actt_ref, h_ref, dx_ref, dg_ref = rest[wo is not None:][:5]

        @pl.when(pl.program_id(0) == 0)
        def _():
            dg_ref[...] = jnp.zeros_like(dg_ref)

        dy = dy_ref[...]
        dyb = dy.astype(BF16)
        gn = g_ref[...]
        xh, r, h = _rms(x_ref[...], gn)
        h_ref[...] = h.astype(BF16)
        dh = jnp.zeros_like(dy)
        for j in range(4):
            gj = gu_ref[j].astype(F32)
            uj = gu_ref[j + 4].astype(F32)
            sg = _sigmoid(gj)
            silu = gj * sg
            actt_ref[j] = (silu * uj).astype(BF16).T
            dact = _dot_nt(dyb, wd_ref[j])
            dgj = (dact * uj * (sg * (1.0 + gj * (1.0 - sg)))).astype(BF16)
            duj = (dact * silu).astype(BF16)
            dgut_ref[j] = dgj.T
            dgut_ref[j + 4] = duj.T
            dh = dh + _dot(dgj, wgu_ref[j]) + _dot(duj, wgu_ref[j + 4])
        dx, dg = _rms_bwd(dh, xh, r, gn)
        dx = dy + dx
        dx_ref[...] = dx
        dg_ref[...] += dg
        if wo is not None:
            rest[6][...] = _dot(wo_ref[...], dx.astype(BF16).T).astype(BF16)

    extra = wo is not None
    tcols = lambda nb: pl.BlockSpec((nb, GUB, tm), lambda i: (0, 0, i))
    return _pc(body, [dy, x, gu, g, wgu, wd] + [wo] * extra, name=name, grid=(t // tm,),
               in_specs=[_rows(tm, D), _rows(tm, D), _blk3(NDEV, tm, GUB), _acc(1, D), WHOLE, WHOLE] + [WHOLE] * extra,
               out_specs=[tcols(NDEV), tcols(4), _rows(tm, D), _rows(tm, D), _acc(1, D)] + [_cols(tm)] * extra,
               out_shape=[_sds((NDEV, GUB, t), BF16), _sds((4, GUB, t), BF16), _sds((t, D), BF16), _sds((t, D), F32),
                          _sds((1, D), F32)] + [_sds((D, t), BF16)] * extra, xchg=xchg)


def _wgrad(at, b, tk, name):
    na, ma, t = at.shape
    nb, _, mb = b.shape
    nk = t // tk

    def body(a_ref, b_ref, out_ref, acc):
        k = pl.program_id(1)

        @pl.when(k == 0)
        def _():
            acc[...] = jnp.zeros_like(acc)

        acc[...] += _dot(a_ref[0], b_ref[0])

        @pl.when(k == nk - 1)
        def _():
            out_ref[0] = acc[...].astype(BF16)

    return _pc(body, [at, b], name=name, grid=(max(na, nb), nk),
               in_specs=[pl.BlockSpec((1, ma, tk), lambda j, k: (j if na > 1 else 0, 0, k)),
                         pl.BlockSpec((1, tk, mb), lambda j, k: (j if nb > 1 else 0, k, 0))],
               out_specs=[pl.BlockSpec((1, ma, mb), lambda j, k: (j, 0, 0))], out_shape=[_sds((max(na, nb), ma, mb), BF16)],
               scratch_shapes=[pltpu.VMEM((ma, mb), F32)])[0][0]


def _conv_pre(x, g, w, b, tm):
    t = x.shape[0]

    def body(x_ref, g_ref, w_ref, b_ref, u_ref, glu_ref):
        _, _, h = _rms(x_ref[...], g_ref[...])
        hb = h.astype(BF16)
        for j in range(4):
            a = _dot(hb, w_ref[j]) + b_ref[:, j * PWB:(j + 1) * PWB]
            gt = _dot(hb, w_ref[j + 4]) + b_ref[:, D + j * PWB:D + (j + 1) * PWB]
            u_ref[j] = a.astype(BF16)
            u_ref[j + 4] = gt.astype(BF16)
            glu_ref[:, j * PWB:(j + 1) * PWB] = a * _sigmoid(gt)

    return _pc(body, [x, g, w, b], name="conv_pre", grid=(t // tm,),
               in_specs=[_rows(tm, D), _acc(1, D), WHOLE, _acc(1, 2 * D)], out_specs=[_blk3(NDEV, tm, PWB), _rows(tm, D)],
               out_shape=[_sds((NDEV, t, PWB), BF16), _sds((t, D), F32)])[0]


def _halo_specs(t, tm):
    per = tm // HALO
    last = t // HALO - 1
    return [
        pl.BlockSpec((HALO, D), lambda i: (jnp.maximum(i * per - 1, 0), 0)),
        _rows(tm, D),
        pl.BlockSpec((HALO, D), lambda i: (jnp.minimum((i + 1) * per, last), 0)),
    ]


SUB = 8
CCH = 32
CLN = 256


def _fill_shifted(sh, prev, cur, nxt, tm):
    i = pl.program_id(0)
    rows = jnp.concatenate([jnp.where(i == 0, 0.0, prev[...]), cur[...], jnp.where(i == pl.num_programs(0) - 1, 0.0, nxt[...])], axis=0)
    n = tm + 2 * HALO - SUB
    for b in range(SUB):
        sh[b] = rows[b:b + n]


def _shifted(sh, off, r0, c0):
    return sh[off % SUB, r0 + off - off % SUB:r0 + off - off % SUB + CCH, c0:c0 + CLN]


def _conv_mid(glu, wdw, bdw, tm, xchg=None):
    t = glu.shape[0]

    def body(prev, cur, nxt, w_ref, b_ref, out_ref, sh):
        _fill_shifted(sh, prev, cur, nxt, tm)
        for c0 in range(0, D, CLN):
            for r0 in range(0, tm, CCH):
                acc = jnp.broadcast_to(b_ref[:, c0:c0 + CLN], (CCH, CLN))
                for k in range(CW):
                    acc = acc + w_ref[k:k + 1, c0:c0 + CLN] * _shifted(sh, k + HALO - CPAD, r0, c0)
                out_ref[r0:r0 + CCH, c0:c0 + CLN] = acc

    outs, xo = _pc(body, [glu, glu, glu, wdw, bdw], name="conv_mid", grid=(t // tm,),
                   in_specs=_halo_specs(t, tm) + [_acc(32, D), _acc(1, D)], out_specs=[_rows(tm, D)],
                   out_shape=[_sds((t, D), F32)], scratch_shapes=[pltpu.VMEM((SUB, tm + 2 * HALO - SUB, D), F32)], xchg=xchg)
    return outs[0], xo


def _conv_mid_bwd(dcv, glu, wdw, tm, xchg=None):
    t = glu.shape[0]

    def body(dp, dc, dn, gp, gc, gn, w_ref, dglu_ref, dw_ref, dsh, gsh):
        @pl.when(pl.program_id(0) == 0)
        def _():
            dw_ref[...] = jnp.zeros_like(dw_ref)

        _fill_shifted(dsh, dp, dc, dn, tm)
        _fill_shifted(gsh, gp, gc, gn, tm)
        for c0 in range(0, D, CLN):
            dwk = [jnp.zeros((SUB, CLN), F32) for _ in range(CW)]
            for r0 in range(0, tm, CCH):
                d0 = _shifted(dsh, HALO, r0, c0)
                acc = jnp.zeros((CCH, CLN), F32)
                for k in range(CW):
                    acc = acc + w_ref[k:k + 1, c0:c0 + CLN] * _shifted(dsh, HALO + CPAD - k, r0, c0)
                    prod = d0 * _shifted(gsh, k + HALO - CPAD, r0, c0)
                    for r in range(0, CCH, SUB):
                        dwk[k] = dwk[k] + prod[r:r + SUB]
                dglu_ref[r0:r0 + CCH, c0:c0 + CLN] = acc
            for k in range(CW):
                dw_ref[k:k + 1, c0:c0 + CLN] += jnp.sum(dwk[k], axis=0, keepdims=True)

    n = tm + 2 * HALO - SUB
    return _pc(body, [dcv, dcv, dcv, glu, glu, glu, wdw], name="conv_mid_bwd", grid=(t // tm,),
               in_specs=_halo_specs(t, tm) + _halo_specs(t, tm) + [_acc(32, D)], out_specs=[_rows(tm, D), _acc(32, D)],
               out_shape=[_sds((t, D), F32), _sds((32, D), F32)],
               scratch_shapes=[pltpu.VMEM((SUB, n, D), F32), pltpu.VMEM((SUB, n, D), F32)], xchg=xchg)


def _ln(cv, lg, lb):
    mu = jnp.mean(cv, axis=-1, keepdims=True)
    cc = cv - mu
    rs = lax.rsqrt(jnp.mean(cc * cc, axis=-1, keepdims=True) + EPS)
    lh = cc * rs
    return lh, rs, lh * lg + lb


def _conv_post(cv, x, lg, lb, w2, b2, tm):
    t = x.shape[0]

    def body(cv_ref, x_ref, lg_ref, lb_ref, w_ref, b_ref, st_ref, out_ref):
        _, _, ln = _ln(cv_ref[...], lg_ref[...], lb_ref[...])
        s = (ln * _sigmoid(ln)).astype(BF16)
        st_ref[...] = s.T
        out_ref[...] = x_ref[...] + _dot(s, w_ref[...]) + b_ref[...]

    return _pc(body, [cv, x, lg, lb, w2, b2], name="conv_post", grid=(t // tm,),
               in_specs=[_rows(tm, D), _rows(tm, D), _acc(1, D), _acc(1, D), WHOLE, _acc(1, D)],
               out_specs=[_cols(tm), _rows(tm, D)], out_shape=[_sds((D, t), BF16), _sds((t, D), F32)])[0]


def _conv_post_bwd(dx, cv, lg, lb, w2, tm, xchg=None):
    t = dx.shape[0]

    def body(dx_ref, cv_ref, lg_ref, lb_ref, w_ref, dcv_ref, part_ref):
        @pl.when(pl.program_id(0) == 0)
        def _():
            part_ref[...] = jnp.zeros_like(part_ref)

        dx = dx_ref[...]
        lg = lg_ref[...]
        lh, rs, ln = _ln(cv_ref[...], lg, lb_ref[...])
        sg = _sigmoid(ln)
        dln = _dot_nt(dx, w_ref[...]) * (sg * (1.0 + ln * (1.0 - sg)))
        dlh = dln * lg
        dcv = rs * (dlh - jnp.mean(dlh, axis=-1, keepdims=True) - lh * jnp.mean(dlh * lh, axis=-1, keepdims=True))
        dcv_ref[...] = dcv
        part_ref[0:1, :] += jnp.sum(dln * lh, axis=0, keepdims=True)
        part_ref[1:2, :] += jnp.sum(dln, axis=0, keepdims=True)
        part_ref[2:3, :] += jnp.sum(dcv, axis=0, keepdims=True)
        part_ref[3:4, :] += jnp.sum(dx, axis=0, keepdims=True)

    return _pc(body, [dx, cv, lg, lb, w2], name="conv_post_bwd", grid=(t // tm,),
               in_specs=[_rows(tm, D), _rows(tm, D), _acc(1, D), _acc(1, D), WHOLE], out_specs=[_rows(tm, D), _acc(8, D)],
               out_shape=[_sds((t, D), F32), _sds((8, D), F32)], xchg=xchg)


def _conv_pre_bwd(dglu, u, x, g, w, dx_out, tm):
    t = x.shape[0]

    def body(dglu_ref, u_ref, x_ref, g_ref, w_ref, dxo_ref, du_ref, ht_ref, dx_ref, dg_ref, db_ref):
        @pl.when(pl.program_id(0) == 0)
        def _():
            dg_ref[...] = jnp.zeros_like(dg_ref)
            db_ref[...] = jnp.zeros_like(db_ref)

        gn = g_ref[...]
        xh, r, h = _rms(x_ref[...], gn)
        ht_ref[...] = h.astype(BF16).T
        dh = jnp.zeros_like(xh)
        for j in range(4):
            a = u_ref[j].astype(F32)
            sg = _sigmoid(u_ref[j + 4].astype(F32))
            dgl = dglu_ref[:, j * PWB:(j + 1) * PWB]
            da = dgl * sg
            dgt = dgl * a * sg * (1.0 - sg)
            db_ref[:, j * PWB:(j + 1) * PWB] += jnp.sum(da, axis=0, keepdims=True)
            db_ref[:, D + j * PWB:D + (j + 1) * PWB] += jnp.sum(dgt, axis=0, keepdims=True)
            da, dgt = da.astype(BF16), dgt.astype(BF16)
            du_ref[j] = da
            du_ref[j + 4] = dgt
            dh = dh + _dot_nt(da, w_ref[j]) + _dot_nt(dgt, w_ref[j + 4])
        dx, dg = _rms_bwd(dh, xh, r, gn)
        dx_ref[...] = dxo_ref[...] + dx
        dg_ref[...] += dg

    return _pc(body, [dglu, u, x, g, w, dx_out], name="conv_pre_bwd", grid=(t // tm,),
               in_specs=[_rows(tm, D), _blk3(NDEV, tm, PWB), _rows(tm, D), _acc(1, D), WHOLE, _rows(tm, D)],
               out_specs=[_blk3(NDEV, tm, PWB), _cols(tm), _rows(tm, D), _acc(1, D), _acc(1, 2 * D)],
               out_shape=[_sds((NDEV, t, PWB), BF16), _sds((D, t), BF16), _sds((t, D), F32), _sds((1, D), F32),
                          _sds((1, 2 * D), F32)])[0]


def _adamw(w, g, m, v):
    m = B1 * m + (1.0 - B1) * g
    v = B2 * v + (1.0 - B2) * (g * g)
    m_hat = m / (1.0 - B1 ** STEP)
    v_hat = v / (1.0 - B2 ** STEP)
    return -LR * (m_hat / (jnp.sqrt(v_hat) + AEPS) + WD * w), m, v


def _reduce_adamw(lands, w, m, v, tr, name):
    nl, r, c = w.shape

    def body(*refs):
        l_refs, (w_ref, m_ref, v_ref, g_ref, d_ref, nm_ref, nv_ref) = refs[:nl], refs[nl:]

        def total(ref):
            g = ref[0].astype(F32)
            for j in range(1, NDEV):
                g = g + ref[j].astype(F32)
            return g

        g = total(l_refs[0])
        for n in range(1, nl):
            g = jnp.where(pl.program_id(0) == n, total(l_refs[n]), g)
        g_ref[0] = g
        d_ref[0], nm_ref[0], nv_ref[0] = _adamw(w_ref[0], g, m_ref[0], v_ref[0])

    layer = pl.BlockSpec((1, tr, c), lambda l, i: (l, i, 0))
    return _pc(body, [*lands, w, m, v], name=name, grid=(nl, r // tr),
               in_specs=[pl.BlockSpec((NDEV, tr, c), lambda l, i: (0, i, 0))] * nl + [layer] * 3, out_specs=[layer] * 4,
               out_shape=[_sds((nl, r, c), F32)] * 4)[0]


def _sum_parts(parts):
    _, r, c = parts.shape

    def body(p_ref, out_ref):
        s = p_ref[0]
        for j in range(1, NDEV):
            s = s + p_ref[j]
        out_ref[...] = s

    return _pc(body, [parts], name="sum_parts", in_specs=[WHOLE], out_specs=[WHOLE], out_shape=[_sds((r, c), F32)])[0][0]


def _adamw_small(w, g, m, v):
    def body(w_ref, g_ref, m_ref, v_ref, d_ref, nm_ref, nv_ref):
        d_ref[...], nm_ref[...], nv_ref[...] = _adamw(w_ref[...], g_ref[...], m_ref[...], v_ref[...])

    return _pc(body, [w, g, m, v], name="adamw_small", in_specs=[WHOLE] * 4, out_specs=[WHOLE] * 3,
               out_shape=[_sds(w.shape, F32)] * 3)[0]


def _rows128(a):
    a = a.astype(F32)
    if a.shape[-1] % 128:
        a = jnp.pad(a, [(0, 0)] * (a.ndim - 1) + [(0, 128 - a.shape[-1] % 128)])
    return a.reshape(-1, 128)


def _pack(arrs, rows):
    p = jnp.concatenate([_rows128(a) for a in arrs], axis=0)
    return jnp.pad(p, ((0, rows - p.shape[0]), (0, 0)))


SMALL = ("attn_norm", "ffn_norm", "final_norm", "attn_sink", "conv_norm", "conv_b_dw", "conv_ln_g", "conv_ln_b", "conv_b_pw2",
         "conv_b_pw1", "conv_w_dw")
SMALL_ROWS = 72


def _pack_small(d):
    return _pack([d[k] for k in SMALL], SMALL_ROWS)


def _unpack_small(p, like):
    out, r = {}, 0
    for k in SMALL:
        shp = like[k].shape
        n = -(-shp[-1] // 128) * (math.prod(shp[:-1]))
        blk = p[r:r + n]
        if shp[-1] % 128:
            blk = blk[:, :shp[-1]]
        out[k] = blk.reshape(shp)
        r += n
    return out


NAMES = ("attn_norm", "attn_w_qkv", "attn_w_o", "attn_sink", "conv_norm", "conv_w_pw1", "conv_b_pw1", "conv_w_dw", "conv_b_dw",
         "conv_ln_g", "conv_ln_b", "conv_w_pw2", "conv_b_pw2", "ffn_norm", "ffn_w_gu", "ffn_w_down", "final_norm")
TM = 256
TL = 512
TK = 2048


def _gu_t(a):
    return jnp.swapaxes(a, 1, 2)


def kernel(x, attn_norm, attn_w_qkv, attn_w_o, attn_sink, conv_norm, conv_w_pw1, conv_b_pw1, conv_w_dw, conv_b_dw, conv_ln_g, conv_ln_b, conv_w_pw2, conv_b_pw2, ffn_norm, ffn_w_gu, ffn_w_down, final_norm, loss_target, m_attn_norm, m_attn_w_qkv, m_attn_w_o, m_attn_sink, m_conv_norm, m_conv_w_pw1, m_conv_b_pw1, m_conv_w_dw, m_conv_b_dw, m_conv_ln_g, m_conv_ln_b, m_conv_w_pw2, m_conv_b_pw2, m_ffn_norm, m_ffn_w_gu, m_ffn_w_down, m_final_norm, v_attn_norm, v_attn_w_qkv, v_attn_w_o, v_attn_sink, v_conv_norm, v_conv_w_pw1, v_conv_b_pw1, v_conv_w_dw, v_conv_b_dw, v_conv_ln_g, v_conv_ln_b, v_conv_w_pw2, v_conv_b_pw2, v_ffn_norm, v_ffn_w_gu, v_ffn_w_down, v_final_norm):
    w = dict(zip(NAMES, (attn_norm, attn_w_qkv, attn_w_o, attn_sink, conv_norm, conv_w_pw1, conv_b_pw1, conv_w_dw, conv_b_dw, conv_ln_g,
                         conv_ln_b, conv_w_pw2, conv_b_pw2, ffn_norm, ffn_w_gu, ffn_w_down, final_norm)))
    m = dict(zip(NAMES, (m_attn_norm, m_attn_w_qkv, m_attn_w_o, m_attn_sink, m_conv_norm, m_conv_w_pw1, m_conv_b_pw1, m_conv_w_dw,
                         m_conv_b_dw, m_conv_ln_g, m_conv_ln_b, m_conv_w_pw2, m_conv_b_pw2, m_ffn_norm, m_ffn_w_gu, m_ffn_w_down,
                         m_final_norm)))
    v = dict(zip(NAMES, (v_attn_norm, v_attn_w_qkv, v_attn_w_o, v_attn_sink, v_conv_norm, v_conv_w_pw1, v_conv_b_pw1, v_conv_w_dw,
                         v_conv_b_dw, v_conv_ln_g, v_conv_ln_b, v_conv_w_pw2, v_conv_b_pw2, v_ffn_norm, v_ffn_w_gu, v_ffn_w_down,
                         v_final_norm)))
    me = 4 * lax.axis_index("x") + 2 * lax.axis_index("y") + lax.axis_index("c")
    for d in (w, m, v):
        d["ffn_w_gu"] = _gu_t(d["ffn_w_gu"])
    sh = {k: w[k][0].astype(BF16) for k in ("attn_w_qkv", "attn_w_o", "conv_w_pw1", "conv_w_pw2")}
    gu_b, down_b = w["ffn_w_gu"].astype(BF16), w["ffn_w_down"].astype(BF16)
    sh.update(ffn_w_gu0=gu_b[0], ffn_w_gu1=gu_b[1], ffn_w_down0=down_b[0], ffn_w_down1=down_b[1])
    x0, tgt = x[0], loss_target[0]
    t = x0.shape[0]
    tabs = _rope_tables(t)
    g_a, sink, g_f0, g_f1, g_fin = w["attn_norm"], w["attn_sink"][0], w["ffn_norm"][0:1], w["ffn_norm"][1:2], w["final_norm"][None]
    gather, scatter = False, True

    shard_rows = _pack([w["conv_w_dw"][0], jnp.zeros((1, 128), F32), w["conv_norm"], w["conv_b_dw"], w["conv_ln_g"], w["conv_ln_b"],
                        w["conv_b_pw2"], w["conv_b_pw1"]], 40)
    wqkv_g, sm = _exchange([sh["attn_w_qkv"], shard_rows], [gather] * 2, "gather_attn")
    wqkv = wqkv_g.transpose(1, 0, 2).reshape(D, QKV)

    def full_vec(row, n=1):
        return sm[:, row:row + n, :].reshape(1, NDEV * n * 128)

    w_dw, g_c, b_dw, ln_g, ln_b = sm[:, 0:32, :].transpose(1, 0, 2).reshape(32, D), full_vec(32), full_vec(33), full_vec(34), full_vec(35)
    b_pw2, b_pw1 = full_vec(36), full_vec(37, 2)

    (q_t, kv), (wd0_g,) = _attn_pre(x0, g_a, wqkv, tabs, TM, xchg=([sh["ffn_w_down0"]], [gather]))
    o_t, (wo_g, wgu0) = _attn_fwd(q_t, kv, sink, xchg=([sh["attn_w_o"], sh["ffn_w_gu0"]], [gather] * 2))
    wo, wd0 = wo_g.reshape(D, D), wd0_g.reshape(4, GUB, D)
    (x1, gu0, x2), (wpw1, wpw2_g, wgu1) = _ffn_fwd_attn(
        o_t, x0, wo, g_f0, wgu0, wd0, TM, xchg=([sh["conv_w_pw1"], sh["conv_w_pw2"], sh["ffn_w_gu1"]], [gather] * 3))
    wpw2 = wpw2_g.reshape(D, D)
    tl = min(TL, t)
    u, glu = _conv_pre(x2, g_c, wpw1, b_pw1, tl)
    cv, (wd1_g,) = _conv_mid(glu, w_dw, b_dw, TM, xchg=([sh["ffn_w_down1"]], [gather]))
    wd1 = wd1_g.reshape(4, GUB, D)
    s_t, x3 = _conv_post(cv, x2, ln_g, ln_b, wpw2, b_pw2, tl)
    (gu1, dx4, fin), _ = _ffn_fwd_final(x3, g_f1, wgu1, wd1, g_fin, tgt, TM)

    land = {}
    tk = min(TK, t)
    (dgu1_t, act1_t, h3, dx3, dg_f1), _ = _ffn_bwd(dx4, x3, gu1, g_f1, wgu1, wd1, TM, "ffn_bwd1")
    dwgu1 = _wgrad(dgu1_t, h3[None], tk, "dwgu1")
    dwd1 = _wgrad(act1_t, dx4[None], tk, "dwd1").reshape(NDEV, DFF // NDEV, D)
    (dcv, cpart), (land["ffn_w_down1"],) = _conv_post_bwd(dx3, cv, ln_g, ln_b, wpw2, tl, xchg=([dwd1], [scatter]))
    dwpw2 = _wgrad(s_t[None], dx3[None], tk, "dwpw2").reshape(NDEV, D // NDEV, D)
    (dglu, dw_dw), (land["ffn_w_gu1"], land["conv_w_pw2"]) = _conv_mid_bwd(dcv, glu, w_dw, TM, xchg=([dwgu1, dwpw2], [scatter] * 2))
    du, h2_t, dx2, dg_c, db_pw1 = _conv_pre_bwd(dglu, u, x2, g_c, wpw1, dx3, tl)
    dwpw1 = _wgrad(h2_t[None], du, tk, "dwpw1")
    (dgu0_t, act0_t, h1, dx1, dg_f0, do_t), (land["conv_w_pw1"],) = _ffn_bwd(
        dx2, x1, gu0, g_f0, wgu0, wd0, TM, "ffn_bwd0", wo=wo, xchg=([dwpw1], [scatter]))
    dwgu0 = _wgrad(dgu0_t, h1[None], tk, "dwgu0")
    dwd0 = _wgrad(act0_t, dx2[None], tk, "dwd0").reshape(NDEV, DFF // NDEV, D)
    dwo = _wgrad(o_t[None], dx1[None], tk, "dwo").reshape(NDEV, D // NDEV, D)
    (dq_t, dk, dv, dsink), (land["ffn_w_gu0"], land["ffn_w_down0"], land["attn_w_o"]) = _attn_bwd(
        q_t, kv, do_t, sink, TM, xchg=([dwgu0, dwd0, dwo], [scatter] * 3))
    dqkv, h0_t, dx0, dg_a = _attn_pre_bwd(dq_t, dk, dv, x0, g_a, wqkv, tabs, dx1, TM)
    dwqkv = _wgrad(h0_t[None], dqkv[None], tk, "dwqkv")[0].reshape(D, NDEV, QKV // NDEV).transpose(1, 0, 2)

    lane0 = (lax.broadcasted_iota(jnp.int32, (1, 128), 1) == 0).astype(F32)
    parts = _pack([dg_a, dg_f0, dg_f1, fin[0:1], dsink[0:1, :NH], fin[1, 0] * lane0, jnp.zeros((6, 128), F32), dg_c, cpart[2:3],
                   cpart[0:1], cpart[1:2], cpart[3:4], db_pw1, dw_dw.reshape(32, NDEV, 128)], 352)
    land["attn_w_qkv"], parts_g = _exchange([dwqkv, parts], [scatter, gather], "scatter_attn")
    red = _sum_parts(parts_g)

    def shard_rows_of(row, n=1):
        return lax.dynamic_slice_in_dim(red, row + n * me, n, axis=0)

    gs = {
        "attn_norm": red[0:8].reshape(1, D), "ffn_norm": red[8:24].reshape(2, D), "final_norm": red[24:32].reshape(D),
        "attn_sink": red[32:33, :NH], "conv_norm": shard_rows_of(40), "conv_b_dw": shard_rows_of(48), "conv_ln_g": shard_rows_of(56),
        "conv_ln_b": shard_rows_of(64), "conv_b_pw2": shard_rows_of(72), "conv_b_pw1": shard_rows_of(80, 2).reshape(1, PWB),
        "conv_w_dw": lax.dynamic_index_in_dim(red[96:352].reshape(32, NDEV, 128), me, axis=1, keepdims=False)[None, :CW],
    }
    loss = red[33, 0]

    grads, deltas, new_m, new_v = dict(gs), {}, {}, {}
    ds, ms, vs = _adamw_small(_pack_small(w), _pack_small(gs), _pack_small(m), _pack_small(v))
    deltas.update(_unpack_small(ds, gs))
    new_m.update(_unpack_small(ms, gs))
    new_v.update(_unpack_small(vs, gs))
    for k in ("attn_w_qkv", "attn_w_o", "conv_w_pw1", "conv_w_pw2", "ffn_w_gu", "ffn_w_down"):
        lands = [land[k + "0"], land[k + "1"]] if k.startswith("ffn") else [land[k]]
        tr = {1024: 256, 128: 128, 352: 176, GUB: 176}[w[k].shape[1]]
        grads[k], deltas[k], new_m[k], new_v[k] = _reduce_adamw(lands, w[k], m[k], v[k], tr, "adamw_" + k)
    for d in (grads, deltas, new_m, new_v):
        d["ffn_w_gu"] = _gu_t(d["ffn_w_gu"])
    return (loss, dx0[None], *[grads[k] for k in NAMES], *[deltas[k] for k in NAMES], *[new_m[k] for k in NAMES],
            *[new_v[k] for k in NAMES])
```

```python
import math

import jax
import jax.numpy as jnp
from jax import lax
from jax.experimental import pallas as pl
from jax.experimental.pallas import tpu as pltpu

F32 = jnp.float32
BF16 = jnp.bfloat16

D = 1024
NH = 16
NKV = 4
HD = 64
GROUP = NH // NKV
ROT = 16
THETA = 500000.0
BLK = 128
QKV = (NH + 2 * NKV) * HD
KOFF = NH * HD
VOFF = KOFF + NKV * HD
DFF = 2816
NDEV = 8
GUB = 2 * DFF // NDEV
PWB = 2 * D // NDEV
CW = 31
CPAD = 15
HALO = 16
EPS = 1e-6
NEG = -1e30
LR, B1, B2, AEPS, WD, STEP = 0.001, 0.9, 0.999, 1e-08, 0.01, 10

VMEM_LIMIT = 56 * 1024 * 1024
MESH = pl.DeviceIdType.MESH
WHOLE = pl.BlockSpec(memory_space=pltpu.VMEM)
ANY = pl.BlockSpec(memory_space=pl.ANY)


def _place():
    x, y, c = lax.axis_index("x"), lax.axis_index("y"), lax.axis_index("c")
    return x, y, c, 4 * x + 2 * y + c


def _peer(x, y, c, j):
    px = 1 - x if j & 4 else x
    py = 1 - y if j & 2 else y
    pc = 1 - c if j & 1 else c
    return (px, py, pc), 4 * px + 2 * py + pc


def _xchg_copies(src, dst, scatter, send, recv, loc):
    x, y, c, me = _place()
    out = []
    for k in range(len(src)):
        if scatter[k]:
            local = pltpu.make_async_copy(src[k].at[me], dst[k].at[0], loc.at[k])
        else:
            local = pltpu.make_async_copy(src[k], dst[k].at[me], loc.at[k])
        pairs = []
        for j in range(1, NDEV):
            peer, pidx = _peer(x, y, c, j)
            if scatter[k]:
                s, d_out, d_in = src[k].at[pidx], dst[k].at[j], dst[k].at[j]
            else:
                s, d_out, d_in = src[k], dst[k].at[me], dst[k].at[pidx]
            sems = dict(send_sem=send.at[k, j - 1], recv_sem=recv.at[k, j - 1], device_id=peer, device_id_type=MESH)
            pairs.append((pltpu.make_async_remote_copy(src_ref=s, dst_ref=d_out, **sems),
                          pltpu.make_async_remote_copy(src_ref=s, dst_ref=d_in, **sems)))
        out.append((local, pairs))
    return out


def _xchg_start(cps):
    for local, pairs in cps:
        local.start()
        for snd, _ in pairs:
            snd.start()


def _xchg_wait(cps):
    for local, pairs in cps:
        for _, rcv in pairs:
            rcv.wait_recv()
        for snd, _ in pairs:
            snd.wait_send()
        local.wait()


def _call(body, **kw):
    return pl.pallas_call(body, **kw)


def _pc(body, operands, *, name, in_specs, out_specs, out_shape, grid=None, scratch_shapes=(), xchg=None):
    kw = dict(name=name, compiler_params=pltpu.CompilerParams(
        dimension_semantics=None if grid is None else ("arbitrary",) * len(grid), vmem_limit_bytes=VMEM_LIMIT,
        has_side_effects=xchg is not None))
    if grid is not None:
        kw["grid"] = grid
    if xchg is None:
        outs = _call(body, in_specs=list(in_specs), out_specs=list(out_specs), out_shape=list(out_shape),
                     scratch_shapes=list(scratch_shapes), **kw)(*operands)
        return list(outs), []
    arrs, scatter = xchg
    nx, n_in, n_out, n_s = len(arrs), len(in_specs), len(out_specs), len(scratch_shapes)

    def wrapped(*refs):
        ins, refs = refs[:n_in], refs[n_in:]
        src, refs = refs[:nx], refs[nx:]
        outs, refs = refs[:n_out], refs[n_out:]
        dst, refs = refs[:nx], refs[nx:]
        scr, (send, recv, loc) = refs[:n_s], refs[n_s:]
        if grid is None:
            cps = _xchg_copies(src, dst, scatter, send, recv, loc)
            _xchg_start(cps)
            body(*ins, *outs, *scr)
            _xchg_wait(cps)
            return

        first = last = True
        for axis, n in enumerate(grid):
            first &= pl.program_id(axis) == 0
            last &= pl.program_id(axis) == n - 1

        @pl.when(first)
        def _():
            _xchg_start(_xchg_copies(src, dst, scatter, send, recv, loc))

        body(*ins, *outs, *scr)

        @pl.when(last)
        def _():
            _xchg_wait(_xchg_copies(src, dst, scatter, send, recv, loc))

    xshape = [jax.ShapeDtypeStruct(a.shape if sc else (NDEV,) + a.shape, a.dtype) for a, sc in zip(arrs, scatter)]
    sems = [pltpu.SemaphoreType.DMA((nx, NDEV - 1)), pltpu.SemaphoreType.DMA((nx, NDEV - 1)), pltpu.SemaphoreType.DMA((nx,))]
    res = _call(wrapped, in_specs=list(in_specs) + [ANY] * nx, out_specs=list(out_specs) + [ANY] * nx,
                out_shape=list(out_shape) + xshape, scratch_shapes=list(scratch_shapes) + sems, **kw)(*operands, *arrs)
    return list(res[:n_out]), list(res[n_out:])


def _exchange(arrs, scatter, name):
    return _pc(lambda: None, [], name=name, in_specs=[], out_specs=[], out_shape=[], xchg=(arrs, scatter))[1]


def _rows(tm, width):
    return pl.BlockSpec((tm, width), lambda i: (i, 0))


def _cols(tm):
    return pl.BlockSpec((D, tm), lambda i: (0, i))


def _blk3(nb, tm, width):
    return pl.BlockSpec((nb, tm, width), lambda i: (0, i, 0))


def _acc(rows, width):
    return pl.BlockSpec((rows, width), lambda i: (0, 0))


def _sds(shape, dtype):
    return jax.ShapeDtypeStruct(shape, dtype)


def _dot(a, b):
    return jnp.dot(a.astype(BF16), b.astype(BF16), preferred_element_type=F32)


def _dot_nt(a, b):
    return lax.dot_general(a.astype(BF16), b.astype(BF16), (((1,), (1,)), ((), ())), preferred_element_type=F32)


def _dot_tn(a, b):
    return lax.dot_general(a.astype(BF16), b.astype(BF16), (((0,), (0,)), ((), ())), preferred_element_type=F32)


def _sigmoid(x):
    return 1.0 / (1.0 + jnp.exp(-x))


def _rms(x, g):
    r = lax.rsqrt(jnp.mean(x * x, axis=-1, keepdims=True) + EPS)
    xh = x * r
    return xh, r, xh * g


def _rms_bwd(dh, xh, r, g):
    dxh = dh * g
    dg = jnp.sum(dh * xh, axis=0, keepdims=True)
    dx = r * (dxh - xh * jnp.mean(dxh * xh, axis=-1, keepdims=True))
    return dx, dg


def _lanes(t, width):
    return jnp.tile(t, (1, width // t.shape[1]))


def _rope(z, c, sa, sb):
    w = z.shape[1]
    return z * _lanes(c, w) + pltpu.roll(z, w - 8, 1) * _lanes(sa, w) + pltpu.roll(z, 8, 1) * _lanes(sb, w)


def _rope_t(dz, c, sa, sb):
    w = dz.shape[1]
    return dz * _lanes(c, w) + pltpu.roll(dz * _lanes(sa, w), 8, 1) + pltpu.roll(dz * _lanes(sb, w), w - 8, 1)


def _rope_tables(t):
    pos = jnp.arange(t, dtype=F32)
    inv_freq = THETA ** (-jnp.arange(0, ROT, 2, dtype=F32) / ROT)
    ang = pos[:, None] * inv_freq[None, :]
    cos, sin = jnp.cos(ang), jnp.sin(ang)
    one = jnp.ones((t, HD - ROT), F32)
    zero = jnp.zeros((t, HD - ROT), F32)
    z8 = jnp.zeros((t, 8), F32)
    c = jnp.concatenate([cos, cos, one], axis=1)
    sa = jnp.concatenate([-sin, z8, zero], axis=1)
    sb = jnp.concatenate([z8, sin, zero], axis=1)
    return tuple(jnp.tile(a, (1, 2)) for a in (c, sa, sb))


KVW = NKV * HD
GW = GROUP * BLK


def _attn_pre(x, g, wqkv, tabs, tm, xchg=None):
    t = x.shape[0]

    def body(x_ref, g_ref, w_ref, c_ref, sa_ref, sb_ref, qt_ref, kv_ref):
        _, _, h = _rms(x_ref[...], g_ref[...])
        z = _dot(h, w_ref[...])
        c, sa, sb = c_ref[...], sa_ref[...], sb_ref[...]
        qt_ref[...] = (_rope(z[:, :KOFF], c, sa, sb) * 0.125).T.astype(BF16)
        kv_ref[:, :KVW] = _rope(z[:, KOFF:VOFF], c, sa, sb).astype(BF16)
        kv_ref[:, KVW:] = z[:, VOFF:].astype(BF16)

    return _pc(body, [x, g, wqkv, *tabs], name="attn_pre", grid=(t // tm,),
               in_specs=[_rows(tm, D), _acc(1, D), WHOLE, _rows(tm, 128), _rows(tm, 128), _rows(tm, 128)],
               out_specs=[_cols(tm), _rows(tm, 2 * KVW)], out_shape=[_sds((D, t), BF16), _sds((t, 2 * KVW), BF16)], xchg=xchg)


def _attn_specs(nblk):
    prev = lambda i: jnp.maximum(i - 1, 0)
    nxt = lambda i: jnp.minimum(i + 1, nblk - 1)
    return [
        pl.BlockSpec((D, BLK), lambda i: (0, i)),
        pl.BlockSpec((BLK, KVW), lambda i: (prev(i), 0)),
        pl.BlockSpec((BLK, KVW), lambda i: (i, 0)),
        pl.BlockSpec((BLK, KVW), lambda i: (nxt(i), 0)),
        pl.BlockSpec((BLK, KVW), lambda i: (prev(i), 1)),
        pl.BlockSpec((BLK, KVW), lambda i: (i, 1)),
        pl.BlockSpec((BLK, KVW), lambda i: (nxt(i), 1)),
    ]


def _attn_bias():
    c = lax.broadcasted_iota(jnp.int32, (3, 3 * BLK, GW), 1)
    r = lax.broadcasted_iota(jnp.int32, (3, 3 * BLK, GW), 2) & (BLK - 1)
    slab = lax.broadcasted_iota(jnp.int32, (3, 3 * BLK, GW), 0)
    valid = (c >= r) & (c - 2 * BLK <= r) & ((slab != 0) | (c >= BLK)) & ((slab != 2) | (c < 2 * BLK))
    return jnp.where(valid, 0.0, NEG).astype(F32)


def _bias_spec(nblk):
    assert nblk >= 2
    return pl.BlockSpec((1, 3 * BLK, GW), lambda i: (jnp.where(i == 0, 0, jnp.where(i == nblk - 1, 2, 1)), 0, 0))


def _group(ref, kv):
    return jnp.concatenate([ref[(kv * GROUP + g) * HD:(kv * GROUP + g + 1) * HD, :] for g in range(GROUP)], axis=1)


def _group_sink(sink_ref, kv):
    return jnp.concatenate([jnp.full((1, BLK), sink_ref[kv * GROUP + g], F32) for g in range(GROUP)], axis=1)


def _attn_exp(k_h, qt_g, bias, sink_g):
    s = _dot(k_h, qt_g) + bias
    m = jnp.maximum(jnp.max(s, axis=0, keepdims=True), sink_g)
    e = jnp.exp(s - m)
    es = jnp.exp(sink_g - m)
    return e, 1.0 / (jnp.sum(e, axis=0, keepdims=True) + es), es


def _attn_fwd(qt, kv, sink, xchg=None):
    t = kv.shape[0]
    nblk = t // BLK

    def body(sink_ref, bias_ref, qt_ref, kp, kc, kn, vp, vc, vn, ot_ref):
        bias = bias_ref[0]
        k = jnp.concatenate([kp[...], kc[...], kn[...]], axis=0)
        v = jnp.concatenate([vp[...], vc[...], vn[...]], axis=0)
        for h in range(NKV):
            e, inv, _ = _attn_exp(k[:, h * HD:(h + 1) * HD], _group(qt_ref, h), bias, _group_sink(sink_ref, h))
            ot_g = (_dot_tn(v[:, h * HD:(h + 1) * HD], e) * inv).astype(BF16)
            for g in range(GROUP):
                ot_ref[(h * GROUP + g) * HD:(h * GROUP + g + 1) * HD, :] = ot_g[:, g * BLK:(g + 1) * BLK]

    outs, xo = _pc(body, [sink, _attn_bias(), qt] + [kv] * 6, name="attn_fwd", grid=(nblk,),
                   in_specs=[pl.BlockSpec(memory_space=pltpu.SMEM), _bias_spec(nblk)] + _attn_specs(nblk),
                   out_specs=[pl.BlockSpec((D, BLK), lambda i: (0, i))], out_shape=[_sds((D, t), BF16)], xchg=xchg)
    return outs[0], xo


def _attn_bwd(qt, kv, dot, sink, pad, xchg=None):
    t = kv.shape[0]
    nblk = t // BLK

    def body(sink_ref, bias_ref, qt_ref, kp, kc, kn, vp, vc, vn, dot_ref, dqt_ref, dk_ref, dv_ref, ds_ref):
        i = pl.program_id(0)

        @pl.when(i == 0)
        def _():
            dk_ref[...] = jnp.zeros_like(dk_ref)
            dv_ref[...] = jnp.zeros_like(dv_ref)
            ds_ref[...] = jnp.zeros_like(ds_ref)

        bias = bias_ref[0]
        k = jnp.concatenate([kp[...], kc[...], kn[...]], axis=0)
        v = jnp.concatenate([vp[...], vc[...], vn[...]], axis=0)
        lane = lax.broadcasted_iota(jnp.int32, (1, 128), 1)
        dsink = jnp.zeros((1, 128), F32)
        rows = pl.ds(pl.multiple_of(i * BLK + (pad - BLK), BLK), 3 * BLK)
        for h in range(NKV):
            k_h, v_h = k[:, h * HD:(h + 1) * HD], v[:, h * HD:(h + 1) * HD]
            qt_g, dot_g = _group(qt_ref, h), _group(dot_ref, h)
            e, inv, es = _attn_exp(k_h, qt_g, bias, _group_sink(sink_ref, h))
            p, ps = e * inv, es * inv
            dp = _dot(v_h, dot_g)
            delta = jnp.sum(p * dp, axis=0, keepdims=True)
            ds = (p * (dp - delta)).astype(BF16)
            dqt_g = _dot_tn(k_h, ds)
            dk_ref[rows, h * HD:(h + 1) * HD] += _dot_nt(ds, qt_g)
            dv_ref[rows, h * HD:(h + 1) * HD] += _dot_nt(p, dot_g)
            psd = ps * delta
            for g in range(GROUP):
                n = h * GROUP + g
                dqt_ref[n * HD:(n + 1) * HD, :] = dqt_g[:, g * BLK:(g + 1) * BLK]
                dsink = dsink - jnp.where(lane == n, jnp.sum(psd[:, g * BLK:(g + 1) * BLK], axis=1, keepdims=True), 0.0)
        ds_ref[0:1, :] += dsink

    outs, xo = _pc(body, [sink, _attn_bias(), qt] + [kv] * 6 + [dot], name="attn_bwd", grid=(nblk,),
                   in_specs=[pl.BlockSpec(memory_space=pltpu.SMEM), _bias_spec(nblk)] + _attn_specs(nblk) + [pl.BlockSpec((D, BLK), lambda i: (0, i))],
                   out_specs=[pl.BlockSpec((D, BLK), lambda i: (0, i)), _acc(t + 2 * pad, KVW), _acc(t + 2 * pad, KVW), _acc(8, 128)],
                   out_shape=[_sds((D, t), F32), _sds((t + 2 * pad, KVW), F32), _sds((t + 2 * pad, KVW), F32), _sds((8, 128), F32)],
                   xchg=xchg)
    return outs, xo


def _attn_pre_bwd(dqt, dk, dv, x, g, wqkv, tabs, dx_out, tm):
    t = x.shape[0]

    def body(dqt_ref, dk_ref, dv_ref, x_ref, g_ref, w_ref, c_ref, sa_ref, sb_ref, dxo_ref, dqkv_ref, ht_ref, dx_ref, dg_ref):
        @pl.when(pl.program_id(0) == 0)
        def _():
            dg_ref[...] = jnp.zeros_like(dg_ref)

        c, sa, sb = c_ref[...], sa_ref[...], sb_ref[...]
        dqkv_ref[:, :KOFF] = _rope_t(dqt_ref[...].T * 0.125, c, sa, sb).astype(BF16)
        dqkv_ref[:, KOFF:VOFF] = _rope_t(dk_ref[...], c, sa, sb).astype(BF16)
        dqkv_ref[:, VOFF:] = dv_ref[...].astype(BF16)
        g = g_ref[...]
        xh, r, h = _rms(x_ref[...], g)
        ht_ref[...] = h.astype(BF16).T
        dh = _dot_nt(dqkv_ref[...], w_ref[...])
        dx, dg = _rms_bwd(dh, xh, r, g)
        dx_ref[...] = dxo_ref[...] + dx
        dg_ref[...] += dg

    return _pc(body, [dqt, dk, dv, x, g, wqkv, *tabs, dx_out], name="attn_pre_bwd", grid=(t // tm,),
               in_specs=[_cols(tm), pl.BlockSpec((tm, KVW), lambda i: (i + 1, 0)), pl.BlockSpec((tm, KVW), lambda i: (i + 1, 0)),
                         _rows(tm, D), _acc(1, D), WHOLE, _rows(tm, 128), _rows(tm, 128), _rows(tm, 128), _rows(tm, D)],
               out_specs=[_rows(tm, QKV), _cols(tm), _rows(tm, D), _acc(1, D)],
               out_shape=[_sds((t, QKV), BF16), _sds((D, t), BF16), _sds((t, D), F32), _sds((1, D), F32)])[0]


def _ffn(x, g, wgu_ref, wd_ref, gu_ref):
    _, _, h = _rms(x, g)
    hb = h.astype(BF16)
    y = x
    for j in range(4):
        gj = _dot_nt(hb, wgu_ref[j])
        uj = _dot_nt(hb, wgu_ref[j + 4])
        gu_ref[j] = gj.astype(BF16)
        gu_ref[j + 4] = uj.astype(BF16)
        y = y + _dot(gj * _sigmoid(gj) * uj, wd_ref[j])
    return y


def _ffn_fwd_attn(ot, x, wo, g, wgu, wd, tm, xchg=None):
    t = x.shape[0]

    def body(ot_ref, x_ref, wo_ref, g_ref, wgu_ref, wd_ref, x1_ref, gu_ref, out_ref):
        x1 = x_ref[...] + _dot_tn(ot_ref[...], wo_ref[...])
        x1_ref[...] = x1
        out_ref[...] = _ffn(x1, g_ref[...], wgu_ref, wd_ref, gu_ref)

    return _pc(body, [ot, x, wo, g, wgu, wd], name="ffn_fwd0", grid=(t // tm,),
               in_specs=[_cols(tm), _rows(tm, D), WHOLE, _acc(1, D), WHOLE, WHOLE],
               out_specs=[_rows(tm, D), _blk3(NDEV, tm, GUB), _rows(tm, D)],
               out_shape=[_sds((t, D), F32), _sds((NDEV, t, GUB), BF16), _sds((t, D), F32)], xchg=xchg)


def _ffn_fwd_final(x, g, wgu, wd, g_fin, tgt, tm, xchg=None):
    t = x.shape[0]

    def body(x_ref, g_ref, wgu_ref, wd_ref, gf_ref, t_ref, gu_ref, dx_ref, part_ref):
        @pl.when(pl.program_id(0) == 0)
        def _():
            part_ref[...] = jnp.zeros_like(part_ref)

        gf = gf_ref[...]
        xh, r, y = _rms(_ffn(x_ref[...], g_ref[...], wgu_ref, wd_ref, gu_ref), gf)
        err = y - t_ref[...]
        dx, dg = _rms_bwd(err * (1.0 / D), xh, r, gf)
        dx_ref[...] = dx
        part_ref[0:1, :] += dg
        tok = jnp.sum(err * err, axis=-1, keepdims=True) * (1.0 / D)
        lane = lax.broadcasted_iota(jnp.int32, (1, D), 1)
        part_ref[1:2, :] += jnp.where(lane == 0, 0.5 * jnp.sum(tok, axis=0, keepdims=True), 0.0)

    return _pc(body, [x, g, wgu, wd, g_fin, tgt], name="ffn_fwd1", grid=(t // tm,),
               in_specs=[_rows(tm, D), _acc(1, D), WHOLE, WHOLE, _acc(1, D), _rows(tm, D)],
               out_specs=[_blk3(NDEV, tm, GUB), _rows(tm, D), _acc(8, D)],
               out_shape=[_sds((NDEV, t, GUB), BF16), _sds((t, D), F32), _sds((8, D), F32)], xchg=xchg)


def _ffn_bwd(dy, x, gu, g, wgu, wd, tm, name, wo=None, xchg=None):
    t = x.shape[0]

    def body(dy_ref, x_ref, gu_ref, g_ref, wgu_ref, wd_ref, *rest):
        wo_ref = rest[0] if wo is not None else None
        dgut_ref, actt_ref, h_ref, dx_ref, dg_ref = rest[wo is not None:][:5]

        @pl.when(pl.program_id(0) == 0)
        def _():
            dg_ref[...] = jnp.zeros_like(dg_ref)

        dy = dy_ref[...]
        dyb = dy.astype(BF16)
        gn = g_ref[...]
        xh, r, h = _rms(x_ref[...], gn)
        h_ref[...] = h.astype(BF16)
        dh = jnp.zeros_like(dy)
        for j in range(4):
            gj = gu_ref[j].astype(F32)
            uj = gu_ref[j + 4].astype(F32)
            sg = _sigmoid(gj)
            silu = gj * sg
            actt_ref[j] = (silu * uj).astype(BF16).T
            dact = _dot_nt(dyb, wd_ref[j])
            dgj = (dact * uj * (sg * (1.0 + gj * (1.0 - sg)))).astype(BF16)
            duj = (dact * silu).astype(BF16)
            dgut_ref[j] = dgj.T
            dgut_ref[j + 4] = duj.T
            dh = dh + _dot(dgj, wgu_ref[j]) + _dot(duj, wgu_ref[j + 4])
        dx, dg = _rms_bwd(dh, xh, r, gn)
        dx = dy + dx
        dx_ref[...] = dx
        dg_ref[...] += dg
        if wo is not None:
            rest[6][...] = _dot(wo_ref[...], dx.astype(BF16).T).astype(BF16)

    extra = wo is not None
    tcols = lambda nb: pl.BlockSpec((nb, GUB, tm), lambda i: (0, 0, i))
    return _pc(body, [dy, x, gu, g, wgu, wd] + [wo] * extra, name=name, grid=(t // tm,),
               in_specs=[_rows(tm, D), _rows(tm, D), _blk3(NDEV, tm, GUB), _acc(1, D), WHOLE, WHOLE] + [WHOLE] * extra,
               out_specs=[tcols(NDEV), tcols(4), _rows(tm, D), _rows(tm, D), _acc(1, D)] + [_cols(tm)] * extra,
               out_shape=[_sds((NDEV, GUB, t), BF16), _sds((4, GUB, t), BF16), _sds((t, D), BF16), _sds((t, D), F32),
                          _sds((1, D), F32)] + [_sds((D, t), BF16)] * extra, xchg=xchg)


def _wgrad(at, b, tk, name, xchg=None):
    na, ma, t = at.shape
    nb, _, mb = b.shape
    nk = t // tk

    def body(a_ref, b_ref, out_ref, acc):
        k = pl.program_id(1)

        @pl.when(k == 0)
        def _():
            acc[...] = jnp.zeros_like(acc)

        acc[...] += _dot(a_ref[0], b_ref[0])

        @pl.when(k == nk - 1)
        def _():
            out_ref[0] = acc[...].astype(BF16)

    outs, xo = _pc(body, [at, b], name=name, grid=(max(na, nb), nk),
                   in_specs=[pl.BlockSpec((1, ma, tk), lambda j, k: (j if na > 1 else 0, 0, k)),
                             pl.BlockSpec((1, tk, mb), lambda j, k: (j if nb > 1 else 0, k, 0))],
                   out_specs=[pl.BlockSpec((1, ma, mb), lambda j, k: (j, 0, 0))], out_shape=[_sds((max(na, nb), ma, mb), BF16)],
                   scratch_shapes=[pltpu.VMEM((ma, mb), F32)], xchg=xchg)
    return outs[0] if xchg is None else (outs[0], xo)


def _conv_pre(x, g, w, b, tm, xchg=None):
    t = x.shape[0]

    def body(x_ref, g_ref, w_ref, b_ref, u_ref, glu_ref):
        _, _, h = _rms(x_ref[...], g_ref[...])
        hb = h.astype(BF16)
        for j in range(4):
            a = _dot(hb, w_ref[j]) + b_ref[:, j * PWB:(j + 1) * PWB]
            gt = _dot(hb, w_ref[j + 4]) + b_ref[:, D + j * PWB:D + (j + 1) * PWB]
            u_ref[j] = a.astype(BF16)
            u_ref[j + 4] = gt.astype(BF16)
            glu_ref[:, j * PWB:(j + 1) * PWB] = a * _sigmoid(gt)

    return _pc(body, [x, g, w, b], name="conv_pre", grid=(t // tm,),
               in_specs=[_rows(tm, D), _acc(1, D), WHOLE, _acc(1, 2 * D)], out_specs=[_blk3(NDEV, tm, PWB), _rows(tm, D)],
               out_shape=[_sds((NDEV, t, PWB), BF16), _sds((t, D), F32)], xchg=xchg)


def _halo_specs(t, tm):
    per = tm // HALO
    last = t // HALO - 1
    return [
        pl.BlockSpec((HALO, D), lambda i: (jnp.maximum(i * per - 1, 0), 0)),
        _rows(tm, D),
        pl.BlockSpec((HALO, D), lambda i: (jnp.minimum((i + 1) * per, last), 0)),
    ]


SUB = 8
CCH = 32
CLN = 256


def _fill_shifted(sh, prev, cur, nxt, tm):
    i = pl.program_id(0)
    rows = jnp.concatenate([jnp.where(i == 0, 0.0, prev[...]), cur[...], jnp.where(i == pl.num_programs(0) - 1, 0.0, nxt[...])], axis=0)
    n = tm + 2 * HALO - SUB
    for b in range(SUB):
        sh[b] = rows[b:b + n]


def _shifted(sh, off, r0, c0):
    return sh[off % SUB, r0 + off - off % SUB:r0 + off - off % SUB + CCH, c0:c0 + CLN]


def _conv_mid(glu, wdw, bdw, tm, xchg=None):
    t = glu.shape[0]

    def body(prev, cur, nxt, w_ref, b_ref, out_ref, sh):
        _fill_shifted(sh, prev, cur, nxt, tm)
        for c0 in range(0, D, CLN):
            for r0 in range(0, tm, CCH):
                acc = jnp.broadcast_to(b_ref[:, c0:c0 + CLN], (CCH, CLN))
                for k in range(CW):
                    acc = acc + w_ref[k:k + 1, c0:c0 + CLN] * _shifted(sh, k + HALO - CPAD, r0, c0)
                out_ref[r0:r0 + CCH, c0:c0 + CLN] = acc

    outs, xo = _pc(body, [glu, glu, glu, wdw, bdw], name="conv_mid", grid=(t // tm,),
                   in_specs=_halo_specs(t, tm) + [_acc(32, D), _acc(1, D)], out_specs=[_rows(tm, D)],
                   out_shape=[_sds((t, D), F32)], scratch_shapes=[pltpu.VMEM((SUB, tm + 2 * HALO - SUB, D), F32)], xchg=xchg)
    return outs[0], xo


def _conv_mid_bwd(dcv, glu, wdw, tm, xchg=None):
    t = glu.shape[0]

    def body(dp, dc, dn, gp, gc, gn, w_ref, dglu_ref, dw_ref, dsh, gsh):
        @pl.when(pl.program_id(0) == 0)
        def _():
            dw_ref[...] = jnp.zeros_like(dw_ref)

        _fill_shifted(dsh, dp, dc, dn, tm)
        _fill_shifted(gsh, gp, gc, gn, tm)
        for c0 in range(0, D, CLN):
            for r0 in range(0, tm, CCH):
                acc = jnp.zeros((CCH, CLN), F32)
                for k in range(CW):
                    acc = acc + w_ref[k:k + 1, c0:c0 + CLN] * _shifted(dsh, HALO + CPAD - k, r0, c0)
                dglu_ref[r0:r0 + CCH, c0:c0 + CLN] = acc
            for k in range(CW):
                dwk = jnp.zeros((SUB, CLN), F32)
                for r0 in range(0, tm, CCH):
                    prod = _shifted(dsh, HALO, r0, c0) * _shifted(gsh, k + HALO - CPAD, r0, c0)
                    for r in range(0, CCH, SUB):
                        dwk = dwk + prod[r:r + SUB]
                dw_ref[k:k + 1, c0:c0 + CLN] += jnp.sum(dwk, axis=0, keepdims=True)

    n = tm + 2 * HALO - SUB
    return _pc(body, [dcv, dcv, dcv, glu, glu, glu, wdw], name="conv_mid_bwd", grid=(t // tm,),
               in_specs=_halo_specs(t, tm) + _halo_specs(t, tm) + [_acc(32, D)], out_specs=[_rows(tm, D), _acc(32, D)],
               out_shape=[_sds((t, D), F32), _sds((32, D), F32)],
               scratch_shapes=[pltpu.VMEM((SUB, n, D), F32), pltpu.VMEM((SUB, n, D), F32)], xchg=xchg)


def _ln(cv, lg, lb):
    mu = jnp.mean(cv, axis=-1, keepdims=True)
    cc = cv - mu
    rs = lax.rsqrt(jnp.mean(cc * cc, axis=-1, keepdims=True) + EPS)
    lh = cc * rs
    return lh, rs, lh * lg + lb


def _conv_post(cv, x, lg, lb, w2, b2, tm):
    t = x.shape[0]

    def body(cv_ref, x_ref, lg_ref, lb_ref, w_ref, b_ref, st_ref, out_ref):
        _, _, ln = _ln(cv_ref[...], lg_ref[...], lb_ref[...])
        s = (ln * _sigmoid(ln)).astype(BF16)
        st_ref[...] = s.T
        out_ref[...] = x_ref[...] + _dot(s, w_ref[...]) + b_ref[...]

    return _pc(body, [cv, x, lg, lb, w2, b2], name="conv_post", grid=(t // tm,),
               in_specs=[_rows(tm, D), _rows(tm, D), _acc(1, D), _acc(1, D), WHOLE, _acc(1, D)],
               out_specs=[_cols(tm), _rows(tm, D)], out_shape=[_sds((D, t), BF16), _sds((t, D), F32)])[0]


def _conv_post_bwd(dx, cv, lg, lb, w2, tm, xchg=None):
    t = dx.shape[0]

    def body(dx_ref, cv_ref, lg_ref, lb_ref, w_ref, dcv_ref, part_ref):
        @pl.when(pl.program_id(0) == 0)
        def _():
            part_ref[...] = jnp.zeros_like(part_ref)

        dx = dx_ref[...]
        lg = lg_ref[...]
        lh, rs, ln = _ln(cv_ref[...], lg, lb_ref[...])
        sg = _sigmoid(ln)
        dln = _dot_nt(dx, w_ref[...]) * (sg * (1.0 + ln * (1.0 - sg)))
        dlh = dln * lg
        dcv = rs * (dlh - jnp.mean(dlh, axis=-1, keepdims=True) - lh * jnp.mean(dlh * lh, axis=-1, keepdims=True))
        dcv_ref[...] = dcv
        part_ref[0:1, :] += jnp.sum(dln * lh, axis=0, keepdims=True)
        part_ref[1:2, :] += jnp.sum(dln, axis=0, keepdims=True)
        part_ref[2:3, :] += jnp.sum(dcv, axis=0, keepdims=True)
        part_ref[3:4, :] += jnp.sum(dx, axis=0, keepdims=True)

    return _pc(body, [dx, cv, lg, lb, w2], name="conv_post_bwd", grid=(t // tm,),
               in_specs=[_rows(tm, D), _rows(tm, D), _acc(1, D), _acc(1, D), WHOLE], out_specs=[_rows(tm, D), _acc(8, D)],
               out_shape=[_sds((t, D), F32), _sds((8, D), F32)], xchg=xchg)


def _conv_pre_bwd(dglu, u, x, g, w, dx_out, tm):
    t = x.shape[0]

    def body(dglu_ref, u_ref, x_ref, g_ref, w_ref, dxo_ref, du_ref, ht_ref, dx_ref, dg_ref, db_ref):
        @pl.when(pl.program_id(0) == 0)
        def _():
            dg_ref[...] = jnp.zeros_like(dg_ref)
            db_ref[...] = jnp.zeros_like(db_ref)

        gn = g_ref[...]
        xh, r, h = _rms(x_ref[...], gn)
        ht_ref[...] = h.astype(BF16).T
        dh = jnp.zeros_like(xh)
        for j in range(4):
            a = u_ref[j].astype(F32)
            sg = _sigmoid(u_ref[j + 4].astype(F32))
            dgl = dglu_ref[:, j * PWB:(j + 1) * PWB]
            da = dgl * sg
            dgt = dgl * a * sg * (1.0 - sg)
            db_ref[:, j * PWB:(j + 1) * PWB] += jnp.sum(da, axis=0, keepdims=True)
            db_ref[:, D + j * PWB:D + (j + 1) * PWB] += jnp.sum(dgt, axis=0, keepdims=True)
            da, dgt = da.astype(BF16), dgt.astype(BF16)
            du_ref[j] = da
            du_ref[j + 4] = dgt
            dh = dh + _dot_nt(da, w_ref[j]) + _dot_nt(dgt, w_ref[j + 4])
        dx, dg = _rms_bwd(dh, xh, r, gn)
        dx_ref[...] = dxo_ref[...] + dx
        dg_ref[...] += dg

    return _pc(body, [dglu, u, x, g, w, dx_out], name="conv_pre_bwd", grid=(t // tm,),
               in_specs=[_rows(tm, D), _blk3(NDEV, tm, PWB), _rows(tm, D), _acc(1, D), WHOLE, _rows(tm, D)],
               out_specs=[_blk3(NDEV, tm, PWB), _cols(tm), _rows(tm, D), _acc(1, D), _acc(1, 2 * D)],
               out_shape=[_sds((NDEV, t, PWB), BF16), _sds((D, t), BF16), _sds((t, D), F32), _sds((1, D), F32),
                          _sds((1, 2 * D), F32)])[0]


def _adamw(w, g, m, v):
    m = B1 * m + (1.0 - B1) * g
    v = B2 * v + (1.0 - B2) * (g * g)
    m_hat = m / (1.0 - B1 ** STEP)
    v_hat = v / (1.0 - B2 ** STEP)
    return -LR * (m_hat / (jnp.sqrt(v_hat) + AEPS) + WD * w), m, v


def _reduce_adamw(lands, w, m, v, tr, name, xchg=None):
    nl, r, c = w.shape

    def body(*refs):
        l_refs, (w_ref, m_ref, v_ref, g_ref, d_ref, nm_ref, nv_ref) = refs[:nl], refs[nl:]

        def total(ref):
            g = ref[0].astype(F32)
            for j in range(1, NDEV):
                g = g + ref[j].astype(F32)
            return g

        g = total(l_refs[0])
        for n in range(1, nl):
            g = jnp.where(pl.program_id(0) == n, total(l_refs[n]), g)
        g_ref[0] = g
        d_ref[0], nm_ref[0], nv_ref[0] = _adamw(w_ref[0], g, m_ref[0], v_ref[0])

    layer = pl.BlockSpec((1, tr, c), lambda l, i: (l, i, 0))
    outs, xo = _pc(body, [*lands, w, m, v], name=name, grid=(nl, r // tr),
                   in_specs=[pl.BlockSpec((NDEV, tr, c), lambda l, i: (0, i, 0))] * nl + [layer] * 3, out_specs=[layer] * 4,
                   out_shape=[_sds((nl, r, c), F32)] * 4, xchg=xchg)
    return outs if xchg is None else (outs, xo)


def _sum_parts(parts):
    _, r, c = parts.shape

    def body(p_ref, out_ref):
        s = p_ref[0]
        for j in range(1, NDEV):
            s = s + p_ref[j]
        out_ref[...] = s

    return _pc(body, [parts], name="sum_parts", in_specs=[WHOLE], out_specs=[WHOLE], out_shape=[_sds((r, c), F32)])[0][0]


def _adamw_small(w, g, m, v):
    def body(w_ref, g_ref, m_ref, v_ref, d_ref, nm_ref, nv_ref):
        d_ref[...], nm_ref[...], nv_ref[...] = _adamw(w_ref[...], g_ref[...], m_ref[...], v_ref[...])

    return _pc(body, [w, g, m, v], name="adamw_small", in_specs=[WHOLE] * 4, out_specs=[WHOLE] * 3,
               out_shape=[_sds(w.shape, F32)] * 3)[0]


def _rows128(a):
    a = a.astype(F32)
    if a.shape[-1] % 128:
        a = jnp.pad(a, [(0, 0)] * (a.ndim - 1) + [(0, 128 - a.shape[-1] % 128)])
    return a.reshape(-1, 128)


def _pack(arrs, rows):
    p = jnp.concatenate([_rows128(a) for a in arrs], axis=0)
    return jnp.pad(p, ((0, rows - p.shape[0]), (0, 0)))


SMALL = ("attn_norm", "ffn_norm", "final_norm", "attn_sink", "conv_norm", "conv_b_dw", "conv_ln_g", "conv_ln_b", "conv_b_pw2",
         "conv_b_pw1", "conv_w_dw")
SMALL_ROWS = 72


def _pack_small(d):
    return _pack([d[k] for k in SMALL], SMALL_ROWS)


def _unpack_small(p, like):
    out, r = {}, 0
    for k in SMALL:
        shp = like[k].shape
        n = -(-shp[-1] // 128) * (math.prod(shp[:-1]))
        blk = p[r:r + n]
        if shp[-1] % 128:
            blk = blk[:, :shp[-1]]
        out[k] = blk.reshape(shp)
        r += n
    return out


NAMES = ("attn_norm", "attn_w_qkv", "attn_w_o", "attn_sink", "conv_norm", "conv_w_pw1", "conv_b_pw1", "conv_w_dw", "conv_b_dw",
         "conv_ln_g", "conv_ln_b", "conv_w_pw2", "conv_b_pw2", "ffn_norm", "ffn_w_gu", "ffn_w_down", "final_norm")
TM = 256
TL = 512
TK = 2048


def _gu_t(a):
    return jnp.swapaxes(a, 1, 2)


def kernel(x, attn_norm, attn_w_qkv, attn_w_o, attn_sink, conv_norm, conv_w_pw1, conv_b_pw1, conv_w_dw, conv_b_dw, conv_ln_g, conv_ln_b, conv_w_pw2, conv_b_pw2, ffn_norm, ffn_w_gu, ffn_w_down, final_norm, loss_target, m_attn_norm, m_attn_w_qkv, m_attn_w_o, m_attn_sink, m_conv_norm, m_conv_w_pw1, m_conv_b_pw1, m_conv_w_dw, m_conv_b_dw, m_conv_ln_g, m_conv_ln_b, m_conv_w_pw2, m_conv_b_pw2, m_ffn_norm, m_ffn_w_gu, m_ffn_w_down, m_final_norm, v_attn_norm, v_attn_w_qkv, v_attn_w_o, v_attn_sink, v_conv_norm, v_conv_w_pw1, v_conv_b_pw1, v_conv_w_dw, v_conv_b_dw, v_conv_ln_g, v_conv_ln_b, v_conv_w_pw2, v_conv_b_pw2, v_ffn_norm, v_ffn_w_gu, v_ffn_w_down, v_final_norm):
    w = dict(zip(NAMES, (attn_norm, attn_w_qkv, attn_w_o, attn_sink, conv_norm, conv_w_pw1, conv_b_pw1, conv_w_dw, conv_b_dw, conv_ln_g,
                         conv_ln_b, conv_w_pw2, conv_b_pw2, ffn_norm, ffn_w_gu, ffn_w_down, final_norm)))
    m = dict(zip(NAMES, (m_attn_norm, m_attn_w_qkv, m_attn_w_o, m_attn_sink, m_conv_norm, m_conv_w_pw1, m_conv_b_pw1, m_conv_w_dw,
                         m_conv_b_dw, m_conv_ln_g, m_conv_ln_b, m_conv_w_pw2, m_conv_b_pw2, m_ffn_norm, m_ffn_w_gu, m_ffn_w_down,
                         m_final_norm)))
    v = dict(zip(NAMES, (v_attn_norm, v_attn_w_qkv, v_attn_w_o, v_attn_sink, v_conv_norm, v_conv_w_pw1, v_conv_b_pw1, v_conv_w_dw,
                         v_conv_b_dw, v_conv_ln_g, v_conv_ln_b, v_conv_w_pw2, v_conv_b_pw2, v_ffn_norm, v_ffn_w_gu, v_ffn_w_down,
                         v_final_norm)))
    me = 4 * lax.axis_index("x") + 2 * lax.axis_index("y") + lax.axis_index("c")
    for d in (w, m, v):
        d["ffn_w_gu"] = _gu_t(d["ffn_w_gu"])
    sh = {k: w[k][0].astype(BF16) for k in ("attn_w_qkv", "attn_w_o", "conv_w_pw1", "conv_w_pw2")}
    gu_b, down_b = w["ffn_w_gu"].astype(BF16), w["ffn_w_down"].astype(BF16)
    sh.update(ffn_w_gu0=gu_b[0], ffn_w_gu1=gu_b[1], ffn_w_down0=down_b[0], ffn_w_down1=down_b[1])
    x0, tgt = x[0], loss_target[0]
    t = x0.shape[0]
    tabs = _rope_tables(t)
    g_a, sink, g_f0, g_f1, g_fin = w["attn_norm"], w["attn_sink"][0], w["ffn_norm"][0:1], w["ffn_norm"][1:2], w["final_norm"][None]
    gather, scatter = False, True

    shard_rows = _pack([w["conv_w_dw"][0], jnp.zeros((1, 128), F32), w["conv_norm"], w["conv_b_dw"], w["conv_ln_g"], w["conv_ln_b"],
                        w["conv_b_pw2"], w["conv_b_pw1"]], 40)
    wqkv_g, sm = _exchange([sh["attn_w_qkv"], shard_rows], [gather] * 2, "gather_attn")
    wqkv = wqkv_g.transpose(1, 0, 2).reshape(D, QKV)

    def full_vec(row, n=1):
        return sm[:, row:row + n, :].reshape(1, NDEV * n * 128)

    w_dw, g_c, b_dw, ln_g, ln_b = sm[:, 0:32, :].transpose(1, 0, 2).reshape(32, D), full_vec(32), full_vec(33), full_vec(34), full_vec(35)
    b_pw2, b_pw1 = full_vec(36), full_vec(37, 2)

    (q_t, kv), (wd0_g,) = _attn_pre(x0, g_a, wqkv, tabs, TM, xchg=([sh["ffn_w_down0"]], [gather]))
    o_t, (wo_g, wgu0) = _attn_fwd(q_t, kv, sink, xchg=([sh["attn_w_o"], sh["ffn_w_gu0"]], [gather] * 2))
    wo, wd0 = wo_g.reshape(D, D), wd0_g.reshape(4, GUB, D)
    (x1, gu0, x2), (wpw1, wgu1) = _ffn_fwd_attn(
        o_t, x0, wo, g_f0, wgu0, wd0, TM, xchg=([sh["conv_w_pw1"], sh["ffn_w_gu1"]], [gather] * 2))
    tl = min(TL, t)
    (u, glu), (wpw2_g,) = _conv_pre(x2, g_c, wpw1, b_pw1, tl, xchg=([sh["conv_w_pw2"]], [gather]))
    wpw2 = wpw2_g.reshape(D, D)
    cv, (wd1_g,) = _conv_mid(glu, w_dw, b_dw, TM, xchg=([sh["ffn_w_down1"]], [gather]))
    wd1 = wd1_g.reshape(4, GUB, D)
    s_t, x3 = _conv_post(cv, x2, ln_g, ln_b, wpw2, b_pw2, tl)
    (gu1, dx4, fin), _ = _ffn_fwd_final(x3, g_f1, wgu1, wd1, g_fin, tgt, TM)

    land = {}
    tk = min(TK, t)
    (dgu1_t, act1_t, h3, dx3, dg_f1), _ = _ffn_bwd(dx4, x3, gu1, g_f1, wgu1, wd1, TM, "ffn_bwd1")
    dwgu1 = _wgrad(dgu1_t, h3[None], tk, "dwgu1")
    dwd1 = _wgrad(act1_t, dx4[None], tk, "dwd1").reshape(NDEV, DFF // NDEV, D)
    (dcv, cpart), (land["ffn_w_down1"],) = _conv_post_bwd(dx3, cv, ln_g, ln_b, wpw2, tl, xchg=([dwd1], [scatter]))
    dwpw2 = _wgrad(s_t[None], dx3[None], tk, "dwpw2").reshape(NDEV, D // NDEV, D)
    (dglu, dw_dw), (land["ffn_w_gu1"], land["conv_w_pw2"]) = _conv_mid_bwd(dcv, glu, w_dw, TM, xchg=([dwgu1, dwpw2], [scatter] * 2))
    du, h2_t, dx2, dg_c, db_pw1 = _conv_pre_bwd(dglu, u, x2, g_c, wpw1, dx3, tl)
    dwpw1 = _wgrad(h2_t[None], du, tk, "dwpw1")
    (dgu0_t, act0_t, h1, dx1, dg_f0, do_t), (land["conv_w_pw1"],) = _ffn_bwd(
        dx2, x1, gu0, g_f0, wgu0, wd0, TM, "ffn_bwd0", wo=wo, xchg=([dwpw1], [scatter]))
    dwgu0 = _wgrad(dgu0_t, h1[None], tk, "dwgu0")
    dwd0 = _wgrad(act0_t, dx2[None], tk, "dwd0").reshape(NDEV, DFF // NDEV, D)
    dwo = _wgrad(o_t[None], dx1[None], tk, "dwo").reshape(NDEV, D // NDEV, D)
    (dq_t, dk, dv, dsink), (land["ffn_w_gu0"], land["ffn_w_down0"]) = _attn_bwd(
        q_t, kv, do_t, sink, TM, xchg=([dwgu0, dwd0], [scatter] * 2))
    dqkv, h0_t, dx0, dg_a = _attn_pre_bwd(dq_t, dk, dv, x0, g_a, wqkv, tabs, dx1, TM)
    dwqkv, (land["attn_w_o"],) = _wgrad(h0_t[None], dqkv[None], tk, "dwqkv", xchg=([dwo], [scatter]))
    dwqkv = dwqkv[0].reshape(D, NDEV, QKV // NDEV).transpose(1, 0, 2)

    lane0 = (lax.broadcasted_iota(jnp.int32, (1, 128), 1) == 0).astype(F32)
    parts = _pack([dg_a, dg_f0, dg_f1, fin[0:1], dsink[0:1, :NH], fin[1, 0] * lane0, jnp.zeros((6, 128), F32), dg_c, cpart[2:3],
                   cpart[0:1], cpart[1:2], cpart[3:4], db_pw1, dw_dw.reshape(32, NDEV, 128)], 352)
    land["attn_w_qkv"], parts_g = _exchange([dwqkv, parts], [scatter, gather], "scatter_attn")
    red = _sum_parts(parts_g)

    def shard_rows_of(row, n=1):
        return lax.dynamic_slice_in_dim(red, row + n * me, n, axis=0)

    gs = {
        "attn_norm": red[0:8].reshape(1, D), "ffn_norm": red[8:24].reshape(2, D), "final_norm": red[24:32].reshape(D),
        "attn_sink": red[32:33, :NH], "conv_norm": shard_rows_of(40), "conv_b_dw": shard_rows_of(48), "conv_ln_g": shard_rows_of(56),
        "conv_ln_b": shard_rows_of(64), "conv_b_pw2": shard_rows_of(72), "conv_b_pw1": shard_rows_of(80, 2).reshape(1, PWB),
        "conv_w_dw": lax.dynamic_index_in_dim(red[96:352].reshape(32, NDEV, 128), me, axis=1, keepdims=False)[None, :CW],
    }
    loss = red[33, 0]

    grads, deltas, new_m, new_v = dict(gs), {}, {}, {}
    ds, ms, vs = _adamw_small(_pack_small(w), _pack_small(gs), _pack_small(m), _pack_small(v))
    deltas.update(_unpack_small(ds, gs))
    new_m.update(_unpack_small(ms, gs))
    new_v.update(_unpack_small(vs, gs))
    for k in ("attn_w_qkv", "attn_w_o", "conv_w_pw1", "conv_w_pw2", "ffn_w_gu", "ffn_w_down"):
        lands = [land[k + "0"], land[k + "1"]] if k.startswith("ffn") else [land[k]]
        tr = {1024: 256, 128: 128, 352: 176, GUB: 176}[w[k].shape[1]]
        grads[k], deltas[k], new_m[k], new_v[k] = _reduce_adamw(lands, w[k], m[k], v[k], tr, "adamw_" + k)
    for d in (grads, deltas, new_m, new_v):
        d["ffn_w_gu"] = _gu_t(d["ffn_w_gu"])
    return (loss, dx0[None], *[grads[k] for k in NAMES], *[deltas[k] for k in NAMES], *[new_m[k] for k in NAMES],
            *[new_v[k] for k in NAMES])
```

```python
import math

import jax
import jax.numpy as jnp
from jax import lax
from jax.experimental import pallas as pl
from jax.experimental.pallas import tpu as pltpu

F32 = jnp.float32
BF16 = jnp.bfloat16

D = 1024
NH = 16
NKV = 4
HD = 64
GROUP = NH // NKV
ROT = 16
THETA = 500000.0
BLK = 128
QKV = (NH + 2 * NKV) * HD
KOFF = NH * HD
VOFF = KOFF + NKV * HD
DFF = 2816
NDEV = 8
GUB = 2 * DFF // NDEV
PWB = 2 * D // NDEV
CW = 31
CPAD = 15
HALO = 16
EPS = 1e-6
NEG = -1e30
LR, B1, B2, AEPS, WD, STEP = 0.001, 0.9, 0.999, 1e-08, 0.01, 10

VMEM_LIMIT = 56 * 1024 * 1024
MESH = pl.DeviceIdType.MESH
WHOLE = pl.BlockSpec(memory_space=pltpu.VMEM)
ANY = pl.BlockSpec(memory_space=pl.ANY)


def _place():
    x, y, c = lax.axis_index("x"), lax.axis_index("y"), lax.axis_index("c")
    return x, y, c, 4 * x + 2 * y + c


def _peer(x, y, c, j):
    px = 1 - x if j & 4 else x
    py = 1 - y if j & 2 else y
    pc = 1 - c if j & 1 else c
    return (px, py, pc), 4 * px + 2 * py + pc


SIBLING = 1
OTHER_CHIPS = (2, 4, 6)


class _Exchange:
    def __init__(self, src, dst, scatter, send, recv, loc):
        self.src, self.dst, self.scatter, self.send, self.recv, self.loc = src, dst, scatter, send, recv, loc
        self.x, self.y, self.c, self.me = _place()

    def _remote(self, k, j, s, d, to):
        peer, _ = _peer(self.x, self.y, self.c, to)
        return pltpu.make_async_remote_copy(src_ref=s, dst_ref=d, send_sem=self.send.at[k, j - 1], recv_sem=self.recv.at[k, j - 1],
                                            device_id=peer, device_id_type=MESH)

    def _slot(self, j):
        return _peer(self.x, self.y, self.c, j)[1]

    def local(self, k):
        if self.scatter[k]:
            return pltpu.make_async_copy(self.src[k].at[self.me], self.dst[k].at[0], self.loc.at[k])
        return pltpu.make_async_copy(self.src[k], self.dst[k].at[self.me], self.loc.at[k])

    def direct(self, k, j):
        if self.scatter[k]:
            return self._remote(k, j, self.src[k].at[self._slot(j)], self.dst[k].at[j], j)
        return self._remote(k, j, self.src[k], self.dst[k].at[self.me], j)

    def passed_on(self, k, j):
        rows = self.dst[k].at[self._slot(j)]
        return self._remote(k, j + 1, rows, rows, SIBLING)

    def arrival(self, k, j):
        rows = self.dst[k].at[j if self.scatter[k] else self._slot(j)]
        return self._remote(k, j, rows, rows, j)

    def sent(self, k):
        return tuple(range(1, NDEV)) if self.scatter[k] else (SIBLING,) + OTHER_CHIPS

    def start(self):
        for k in range(len(self.src)):
            self.local(k).start()
            for j in self.sent(k):
                self.direct(k, j).start()

    def finish(self):
        gathers = [k for k in range(len(self.src)) if not self.scatter[k]]
        for k in gathers:
            for j in OTHER_CHIPS:
                self.arrival(k, j).wait_recv()
                self.passed_on(k, j).start()
        for k in range(len(self.src)):
            for j in range(1, NDEV):
                if self.scatter[k] or j not in OTHER_CHIPS:
                    self.arrival(k, j).wait_recv()
            for j in self.sent(k):
                self.direct(k, j).wait_send()
            if not self.scatter[k]:
                for j in OTHER_CHIPS:
                    self.passed_on(k, j).wait_send()
            self.local(k).wait()


def _call(body, **kw):
    return pl.pallas_call(body, **kw)


def _pc(body, operands, *, name, in_specs, out_specs, out_shape, grid=None, scratch_shapes=(), xchg=None):
    kw = dict(name=name, compiler_params=pltpu.CompilerParams(
        dimension_semantics=None if grid is None else ("arbitrary",) * len(grid), vmem_limit_bytes=VMEM_LIMIT,
        has_side_effects=xchg is not None))
    if grid is not None:
        kw["grid"] = grid
    if xchg is None:
        outs = _call(body, in_specs=list(in_specs), out_specs=list(out_specs), out_shape=list(out_shape),
                     scratch_shapes=list(scratch_shapes), **kw)(*operands)
        return list(outs), []
    arrs, scatter = xchg
    nx, n_in, n_out, n_s = len(arrs), len(in_specs), len(out_specs), len(scratch_shapes)

    def wrapped(*refs):
        ins, refs = refs[:n_in], refs[n_in:]
        src, refs = refs[:nx], refs[nx:]
        outs, refs = refs[:n_out], refs[n_out:]
        dst, refs = refs[:nx], refs[nx:]
        scr, (send, recv, loc) = refs[:n_s], refs[n_s:]
        if grid is None:
            _Exchange(src, dst, scatter, send, recv, loc).start()
            body(*ins, *outs, *scr)
            _Exchange(src, dst, scatter, send, recv, loc).finish()
            return

        first = last = True
        for axis, n in enumerate(grid):
            first &= pl.program_id(axis) == 0
            last &= pl.program_id(axis) == n - 1

        @pl.when(first)
        def _():
            _Exchange(src, dst, scatter, send, recv, loc).start()

        body(*ins, *outs, *scr)

        @pl.when(last)
        def _():
            _Exchange(src, dst, scatter, send, recv, loc).finish()

    xshape = [jax.ShapeDtypeStruct(a.shape if sc else (NDEV,) + a.shape, a.dtype) for a, sc in zip(arrs, scatter)]
    sems = [pltpu.SemaphoreType.DMA((nx, NDEV - 1)), pltpu.SemaphoreType.DMA((nx, NDEV - 1)), pltpu.SemaphoreType.DMA((nx,))]
    res = _call(wrapped, in_specs=list(in_specs) + [ANY] * nx, out_specs=list(out_specs) + [ANY] * nx,
                out_shape=list(out_shape) + xshape, scratch_shapes=list(scratch_shapes) + sems, **kw)(*operands, *arrs)
    return list(res[:n_out]), list(res[n_out:])


def _exchange(arrs, scatter, name):
    return _pc(lambda: None, [], name=name, in_specs=[], out_specs=[], out_shape=[], xchg=(arrs, scatter))[1]


def _rows(tm, width):
    return pl.BlockSpec((tm, width), lambda i: (i, 0))


def _cols(tm):
    return pl.BlockSpec((D, tm), lambda i: (0, i))


def _blk3(nb, tm, width):
    return pl.BlockSpec((nb, tm, width), lambda i: (0, i, 0))


def _acc(rows, width):
    return pl.BlockSpec((rows, width), lambda i: (0, 0))


def _sds(shape, dtype):
    return jax.ShapeDtypeStruct(shape, dtype)


def _dot(a, b):
    return jnp.dot(a.astype(BF16), b.astype(BF16), preferred_element_type=F32)


def _dot_nt(a, b):
    return lax.dot_general(a.astype(BF16), b.astype(BF16), (((1,), (1,)), ((), ())), preferred_element_type=F32)


def _dot_tn(a, b):
    return lax.dot_general(a.astype(BF16), b.astype(BF16), (((0,), (0,)), ((), ())), preferred_element_type=F32)


def _sigmoid(x):
    return 1.0 / (1.0 + jnp.exp(-x))


def _rms(x, g):
    r = lax.rsqrt(jnp.mean(x * x, axis=-1, keepdims=True) + EPS)
    xh = x * r
    return xh, r, xh * g


def _rms_bwd(dh, xh, r, g):
    dxh = dh * g
    dg = jnp.sum(dh * xh, axis=0, keepdims=True)
    dx = r * (dxh - xh * jnp.mean(dxh * xh, axis=-1, keepdims=True))
    return dx, dg


def _lanes(t, width):
    return jnp.tile(t, (1, width // t.shape[1]))


def _rope(z, c, sa, sb):
    w = z.shape[1]
    return z * _lanes(c, w) + pltpu.roll(z, w - 8, 1) * _lanes(sa, w) + pltpu.roll(z, 8, 1) * _lanes(sb, w)


def _rope_t(dz, c, sa, sb):
    w = dz.shape[1]
    return dz * _lanes(c, w) + pltpu.roll(dz * _lanes(sa, w), 8, 1) + pltpu.roll(dz * _lanes(sb, w), w - 8, 1)


def _rope_tables(t):
    pos = jnp.arange(t, dtype=F32)
    inv_freq = THETA ** (-jnp.arange(0, ROT, 2, dtype=F32) / ROT)
    ang = pos[:, None] * inv_freq[None, :]
    cos, sin = jnp.cos(ang), jnp.sin(ang)
    one = jnp.ones((t, HD - ROT), F32)
    zero = jnp.zeros((t, HD - ROT), F32)
    z8 = jnp.zeros((t, 8), F32)
    c = jnp.concatenate([cos, cos, one], axis=1)
    sa = jnp.concatenate([-sin, z8, zero], axis=1)
    sb = jnp.concatenate([z8, sin, zero], axis=1)
    return tuple(jnp.tile(a, (1, 2)) for a in (c, sa, sb))


KVW = NKV * HD
GW = GROUP * BLK


def _attn_pre(x, g, wqkv, tabs, tm, xchg=None):
    t = x.shape[0]

    def body(x_ref, g_ref, w_ref, c_ref, sa_ref, sb_ref, qt_ref, kv_ref):
        _, _, h = _rms(x_ref[...], g_ref[...])
        z = _dot(h, w_ref[...])
        c, sa, sb = c_ref[...], sa_ref[...], sb_ref[...]
        qt_ref[...] = (_rope(z[:, :KOFF], c, sa, sb) * 0.125).T.astype(BF16)
        kv_ref[:, :KVW] = _rope(z[:, KOFF:VOFF], c, sa, sb).astype(BF16)
        kv_ref[:, KVW:] = z[:, VOFF:].astype(BF16)

    return _pc(body, [x, g, wqkv, *tabs], name="attn_pre", grid=(t // tm,),
               in_specs=[_rows(tm, D), _acc(1, D), WHOLE, _rows(tm, 128), _rows(tm, 128), _rows(tm, 128)],
               out_specs=[_cols(tm), _rows(tm, 2 * KVW)], out_shape=[_sds((D, t), BF16), _sds((t, 2 * KVW), BF16)], xchg=xchg)


def _attn_specs(nblk):
    prev = lambda i: jnp.maximum(i - 1, 0)
    nxt = lambda i: jnp.minimum(i + 1, nblk - 1)
    return [
        pl.BlockSpec((D, BLK), lambda i: (0, i)),
        pl.BlockSpec((BLK, KVW), lambda i: (prev(i), 0)),
        pl.BlockSpec((BLK, KVW), lambda i: (i, 0)),
        pl.BlockSpec((BLK, KVW), lambda i: (nxt(i), 0)),
        pl.BlockSpec((BLK, KVW), lambda i: (prev(i), 1)),
        pl.BlockSpec((BLK, KVW), lambda i: (i, 1)),
        pl.BlockSpec((BLK, KVW), lambda i: (nxt(i), 1)),
    ]


def _attn_bias():
    c = lax.broadcasted_iota(jnp.int32, (3, 3 * BLK, GW), 1)
    r = lax.broadcasted_iota(jnp.int32, (3, 3 * BLK, GW), 2) & (BLK - 1)
    slab = lax.broadcasted_iota(jnp.int32, (3, 3 * BLK, GW), 0)
    valid = (c >= r) & (c - 2 * BLK <= r) & ((slab != 0) | (c >= BLK)) & ((slab != 2) | (c < 2 * BLK))
    return jnp.where(valid, 0.0, NEG).astype(F32)


def _bias_spec(nblk):
    assert nblk >= 2
    return pl.BlockSpec((1, 3 * BLK, GW), lambda i: (jnp.where(i == 0, 0, jnp.where(i == nblk - 1, 2, 1)), 0, 0))


def _group(ref, kv):
    return jnp.concatenate([ref[(kv * GROUP + g) * HD:(kv * GROUP + g + 1) * HD, :] for g in range(GROUP)], axis=1)


def _group_sink(sink_ref, kv):
    return jnp.concatenate([jnp.full((1, BLK), sink_ref[kv * GROUP + g], F32) for g in range(GROUP)], axis=1)


def _attn_exp(k_h, qt_g, bias, sink_g):
    s = _dot(k_h, qt_g) + bias
    m = jnp.maximum(jnp.max(s, axis=0, keepdims=True), sink_g)
    e = jnp.exp(s - m)
    es = jnp.exp(sink_g - m)
    return e, 1.0 / (jnp.sum(e, axis=0, keepdims=True) + es), es


def _attn_fwd(qt, kv, sink, xchg=None):
    t = kv.shape[0]
    nblk = t // BLK

    def body(sink_ref, bias_ref, qt_ref, kp, kc, kn, vp, vc, vn, ot_ref):
        bias = bias_ref[0]
        k = jnp.concatenate([kp[...], kc[...], kn[...]], axis=0)
        v = jnp.concatenate([vp[...], vc[...], vn[...]], axis=0)
        for h in range(NKV):
            e, inv, _ = _attn_exp(k[:, h * HD:(h + 1) * HD], _group(qt_ref, h), bias, _group_sink(sink_ref, h))
            ot_g = (_dot_tn(v[:, h * HD:(h + 1) * HD], e) * inv).astype(BF16)
            for g in range(GROUP):
                ot_ref[(h * GROUP + g) * HD:(h * GROUP + g + 1) * HD, :] = ot_g[:, g * BLK:(g + 1) * BLK]

    outs, xo = _pc(body, [sink, _attn_bias(), qt] + [kv] * 6, name="attn_fwd", grid=(nblk,),
                   in_specs=[pl.BlockSpec(memory_space=pltpu.SMEM), _bias_spec(nblk)] + _attn_specs(nblk),
                   out_specs=[pl.BlockSpec((D, BLK), lambda i: (0, i))], out_shape=[_sds((D, t), BF16)], xchg=xchg)
    return outs[0], xo


def _attn_bwd(qt, kv, dot, sink, pad, xchg=None):
    t = kv.shape[0]
    nblk = t // BLK

    def body(sink_ref, bias_ref, qt_ref, kp, kc, kn, vp, vc, vn, dot_ref, dqt_ref, dk_ref, dv_ref, ds_ref):
        i = pl.program_id(0)

        @pl.when(i == 0)
        def _():
            dk_ref[...] = jnp.zeros_like(dk_ref)
            dv_ref[...] = jnp.zeros_like(dv_ref)
            ds_ref[...] = jnp.zeros_like(ds_ref)

        bias = bias_ref[0]
        k = jnp.concatenate([kp[...], kc[...], kn[...]], axis=0)
        v = jnp.concatenate([vp[...], vc[...], vn[...]], axis=0)
        lane = lax.broadcasted_iota(jnp.int32, (1, 128), 1)
        dsink = jnp.zeros((1, 128), F32)
        rows = pl.ds(pl.multiple_of(i * BLK + (pad - BLK), BLK), 3 * BLK)
        for h in range(NKV):
            k_h, v_h = k[:, h * HD:(h + 1) * HD], v[:, h * HD:(h + 1) * HD]
            qt_g, dot_g = _group(qt_ref, h), _group(dot_ref, h)
            e, inv, es = _attn_exp(k_h, qt_g, bias, _group_sink(sink_ref, h))
            p, ps = e * inv, es * inv
            dp = _dot(v_h, dot_g)
            delta = jnp.sum(p * dp, axis=0, keepdims=True)
            ds = (p * (dp - delta)).astype(BF16)
            dqt_g = _dot_tn(k_h, ds)
            dk_ref[rows, h * HD:(h + 1) * HD] += _dot_nt(ds, qt_g)
            dv_ref[rows, h * HD:(h + 1) * HD] += _dot_nt(p, dot_g)
            psd = ps * delta
            for g in range(GROUP):
                n = h * GROUP + g
                dqt_ref[n * HD:(n + 1) * HD, :] = dqt_g[:, g * BLK:(g + 1) * BLK]
                dsink = dsink - jnp.where(lane == n, jnp.sum(psd[:, g * BLK:(g + 1) * BLK], axis=1, keepdims=True), 0.0)
        ds_ref[0:1, :] += dsink

    outs, xo = _pc(body, [sink, _attn_bias(), qt] + [kv] * 6 + [dot], name="attn_bwd", grid=(nblk,),
                   in_specs=[pl.BlockSpec(memory_space=pltpu.SMEM), _bias_spec(nblk)] + _attn_specs(nblk) + [pl.BlockSpec((D, BLK), lambda i: (0, i))],
                   out_specs=[pl.BlockSpec((D, BLK), lambda i: (0, i)), _acc(t + 2 * pad, KVW), _acc(t + 2 * pad, KVW), _acc(8, 128)],
                   out_shape=[_sds((D, t), F32), _sds((t + 2 * pad, KVW), F32), _sds((t + 2 * pad, KVW), F32), _sds((8, 128), F32)],
                   xchg=xchg)
    return outs, xo


def _attn_pre_bwd(dqt, dk, dv, x, g, wqkv, tabs, dx_out, tm):
    t = x.shape[0]

    def body(dqt_ref, dk_ref, dv_ref, x_ref, g_ref, w_ref, c_ref, sa_ref, sb_ref, dxo_ref, dqkv_ref, ht_ref, dx_ref, dg_ref):
        @pl.when(pl.program_id(0) == 0)
        def _():
            dg_ref[...] = jnp.zeros_like(dg_ref)

        c, sa, sb = c_ref[...], sa_ref[...], sb_ref[...]
        dqkv_ref[:, :KOFF] = _rope_t(dqt_ref[...].T * 0.125, c, sa, sb).astype(BF16)
        dqkv_ref[:, KOFF:VOFF] = _rope_t(dk_ref[...], c, sa, sb).astype(BF16)
        dqkv_ref[:, VOFF:] = dv_ref[...].astype(BF16)
        g = g_ref[...]
        xh, r, h = _rms(x_ref[...], g)
        ht_ref[...] = h.astype(BF16).T
        dh = _dot_nt(dqkv_ref[...], w_ref[...])
        dx, dg = _rms_bwd(dh, xh, r, g)
        dx_ref[...] = dxo_ref[...] + dx
        dg_ref[...] += dg

    return _pc(body, [dqt, dk, dv, x, g, wqkv, *tabs, dx_out], name="attn_pre_bwd", grid=(t // tm,),
               in_specs=[_cols(tm), pl.BlockSpec((tm, KVW), lambda i: (i + 1, 0)), pl.BlockSpec((tm, KVW), lambda i: (i + 1, 0)),
                         _rows(tm, D), _acc(1, D), WHOLE, _rows(tm, 128), _rows(tm, 128), _rows(tm, 128), _rows(tm, D)],
               out_specs=[_rows(tm, QKV), _cols(tm), _rows(tm, D), _acc(1, D)],
               out_shape=[_sds((t, QKV), BF16), _sds((D, t), BF16), _sds((t, D), F32), _sds((1, D), F32)])[0]


def _ffn(x, g, wgu_ref, wd_ref, gu_ref):
    _, _, h = _rms(x, g)
    hb = h.astype(BF16)
    y = x
    for j in range(4):
        gj = _dot_nt(hb, wgu_ref[j])
        uj = _dot_nt(hb, wgu_ref[j + 4])
        gu_ref[j] = gj.astype(BF16)
        gu_ref[j + 4] = uj.astype(BF16)
        y = y + _dot(gj * _sigmoid(gj) * uj, wd_ref[j])
    return y


def _ffn_fwd_attn(ot, x, wo, g, wgu, wd, tm, xchg=None):
    t = x.shape[0]

    def body(ot_ref, x_ref, wo_ref, g_ref, wgu_ref, wd_ref, x1_ref, gu_ref, out_ref):
        x1 = x_ref[...] + _dot_tn(ot_ref[...], wo_ref[...])
        x1_ref[...] = x1
        out_ref[...] = _ffn(x1, g_ref[...], wgu_ref, wd_ref, gu_ref)

    return _pc(body, [ot, x, wo, g, wgu, wd], name="ffn_fwd0", grid=(t // tm,),
               in_specs=[_cols(tm), _rows(tm, D), WHOLE, _acc(1, D), WHOLE, WHOLE],
               out_specs=[_rows(tm, D), _blk3(NDEV, tm, GUB), _rows(tm, D)],
               out_shape=[_sds((t, D), F32), _sds((NDEV, t, GUB), BF16), _sds((t, D), F32)], xchg=xchg)


def _ffn_fwd_final(x, g, wgu, wd, g_fin, tgt, tm, xchg=None):
    t = x.shape[0]

    def body(x_ref, g_ref, wgu_ref, wd_ref, gf_ref, t_ref, gu_ref, dx_ref, part_ref):
        @pl.when(pl.program_id(0) == 0)
        def _():
            part_ref[...] = jnp.zeros_like(part_ref)

        gf = gf_ref[...]
        xh, r, y = _rms(_ffn(x_ref[...], g_ref[...], wgu_ref, wd_ref, gu_ref), gf)
        err = y - t_ref[...]
        dx, dg = _rms_bwd(err * (1.0 / D), xh, r, gf)
        dx_ref[...] = dx
        part_ref[0:1, :] += dg
        tok = jnp.sum(err * err, axis=-1, keepdims=True) * (1.0 / D)
        lane = lax.broadcasted_iota(jnp.int32, (1, D), 1)
        part_ref[1:2, :] += jnp.where(lane == 0, 0.5 * jnp.sum(tok, axis=0, keepdims=True), 0.0)

    return _pc(body, [x, g, wgu, wd, g_fin, tgt], name="ffn_fwd1", grid=(t // tm,),
               in_specs=[_rows(tm, D), _acc(1, D), WHOLE, WHOLE, _acc(1, D), _rows(tm, D)],
               out_specs=[_blk3(NDEV, tm, GUB), _rows(tm, D), _acc(8, D)],
               out_shape=[_sds((NDEV, t, GUB), BF16), _sds((t, D), F32), _sds((8, D), F32)], xchg=xchg)


def _ffn_bwd(dy, x, gu, g, wgu, wd, tm, name, wo=None, xchg=None):
    t = x.shape[0]

    def body(dy_ref, x_ref, gu_ref, g_ref, wgu_ref, wd_ref, *rest):
        wo_ref = rest[0] if wo is not None else None
        dgut_ref, actt_ref, h_ref, dx_ref, dg_ref = rest[wo is not None:][:5]

        @pl.when(pl.program_id(0) == 0)
        def _():
            dg_ref[...] = jnp.zeros_like(dg_ref)

        dy = dy_ref[...]
        dyb = dy.astype(BF16)
        gn = g_ref[...]
        xh, r, h = _rms(x_ref[...], gn)
        h_ref[...] = h.astype(BF16)
        dh = jnp.zeros_like(dy)
        for j in range(4):
            gj = gu_ref[j].astype(F32)
            uj = gu_ref[j + 4].astype(F32)
            sg = _sigmoid(gj)
            silu = gj * sg
            actt_ref[j] = (silu * uj).astype(BF16).T
            dact = _dot_nt(dyb, wd_ref[j])
            dgj = (dact * uj * (sg * (1.0 + gj * (1.0 - sg)))).astype(BF16)
            duj = (dact * silu).astype(BF16)
            dgut_ref[j] = dgj.T
            dgut_ref[j + 4] = duj.T
            dh = dh + _dot(dgj, wgu_ref[j]) + _dot(duj, wgu_ref[j + 4])
        dx, dg = _rms_bwd(dh, xh, r, gn)
        dx = dy + dx
        dx_ref[...] = dx
        dg_ref[...] += dg
        if wo is not None:
            rest[6][...] = _dot(wo_ref[...], dx.astype(BF16).T).astype(BF16)

    extra = wo is not None
    tcols = lambda nb: pl.BlockSpec((nb, GUB, tm), lambda i: (0, 0, i))
    return _pc(body, [dy, x, gu, g, wgu, wd] + [wo] * extra, name=name, grid=(t // tm,),
               in_specs=[_rows(tm, D), _rows(tm, D), _blk3(NDEV, tm, GUB), _acc(1, D), WHOLE, WHOLE] + [WHOLE] * extra,
               out_specs=[tcols(NDEV), tcols(4), _rows(tm, D), _rows(tm, D), _acc(1, D)] + [_cols(tm)] * extra,
               out_shape=[_sds((NDEV, GUB, t), BF16), _sds((4, GUB, t), BF16), _sds((t, D), BF16), _sds((t, D), F32),
                          _sds((1, D), F32)] + [_sds((D, t), BF16)] * extra, xchg=xchg)


def _wgrad(at, b, tk, name, xchg=None):
    na, ma, t = at.shape
    nb, _, mb = b.shape
    nk = t // tk

    def body(a_ref, b_ref, out_ref, acc):
        k = pl.program_id(1)

        @pl.when(k == 0)
        def _():
            acc[...] = jnp.zeros_like(acc)

        acc[...] += _dot(a_ref[0], b_ref[0])

        @pl.when(k == nk - 1)
        def _():
            out_ref[0] = acc[...].astype(BF16)

    outs, xo = _pc(body, [at, b], name=name, grid=(max(na, nb), nk),
                   in_specs=[pl.BlockSpec((1, ma, tk), lambda j, k: (j if na > 1 else 0, 0, k)),
                             pl.BlockSpec((1, tk, mb), lambda j, k: (j if nb > 1 else 0, k, 0))],
                   out_specs=[pl.BlockSpec((1, ma, mb), lambda j, k: (j, 0, 0))], out_shape=[_sds((max(na, nb), ma, mb), BF16)],
                   scratch_shapes=[pltpu.VMEM((ma, mb), F32)], xchg=xchg)
    return outs[0] if xchg is None else (outs[0], xo)


def _conv_pre(x, g, w, b, tm, xchg=None):
    t = x.shape[0]

    def body(x_ref, g_ref, w_ref, b_ref, u_ref, glu_ref):
        _, _, h = _rms(x_ref[...], g_ref[...])
        hb = h.astype(BF16)
        for j in range(4):
            a = _dot(hb, w_ref[j]) + b_ref[:, j * PWB:(j + 1) * PWB]
            gt = _dot(hb, w_ref[j + 4]) + b_ref[:, D + j * PWB:D + (j + 1) * PWB]
            u_ref[j] = a.astype(BF16)
            u_ref[j + 4] = gt.astype(BF16)
            glu_ref[:, j * PWB:(j + 1) * PWB] = a * _sigmoid(gt)

    return _pc(body, [x, g, w, b], name="conv_pre", grid=(t // tm,),
               in_specs=[_rows(tm, D), _acc(1, D), WHOLE, _acc(1, 2 * D)], out_specs=[_blk3(NDEV, tm, PWB), _rows(tm, D)],
               out_shape=[_sds((NDEV, t, PWB), BF16), _sds((t, D), F32)], xchg=xchg)


def _halo_specs(t, tm):
    per = tm // HALO
    last = t // HALO - 1
    return [
        pl.BlockSpec((HALO, D), lambda i: (jnp.maximum(i * per - 1, 0), 0)),
        _rows(tm, D),
        pl.BlockSpec((HALO, D), lambda i: (jnp.minimum((i + 1) * per, last), 0)),
    ]


SUB = 8
CCH = 32
CLN = 256


def _fill_shifted(sh, prev, cur, nxt, tm):
    i = pl.program_id(0)
    rows = jnp.concatenate([jnp.where(i == 0, 0.0, prev[...]), cur[...], jnp.where(i == pl.num_programs(0) - 1, 0.0, nxt[...])], axis=0)
    n = tm + 2 * HALO - SUB
    for b in range(SUB):
        sh[b] = rows[b:b + n]


def _shifted(sh, off, r0, c0):
    return sh[off % SUB, r0 + off - off % SUB:r0 + off - off % SUB + CCH, c0:c0 + CLN]


def _conv_mid(glu, wdw, bdw, tm, xchg=None):
    t = glu.shape[0]

    def body(prev, cur, nxt, w_ref, b_ref, out_ref, sh):
        _fill_shifted(sh, prev, cur, nxt, tm)
        for c0 in range(0, D, CLN):
            for r0 in range(0, tm, CCH):
                acc = jnp.broadcast_to(b_ref[:, c0:c0 + CLN], (CCH, CLN))
                for k in range(CW):
                    acc = acc + w_ref[k:k + 1, c0:c0 + CLN] * _shifted(sh, k + HALO - CPAD, r0, c0)
                out_ref[r0:r0 + CCH, c0:c0 + CLN] = acc

    outs, xo = _pc(body, [glu, glu, glu, wdw, bdw], name="conv_mid", grid=(t // tm,),
                   in_specs=_halo_specs(t, tm) + [_acc(32, D), _acc(1, D)], out_specs=[_rows(tm, D)],
                   out_shape=[_sds((t, D), F32)], scratch_shapes=[pltpu.VMEM((SUB, tm + 2 * HALO - SUB, D), F32)], xchg=xchg)
    return outs[0], xo


def _conv_mid_bwd(dcv, glu, wdw, tm, xchg=None):
    t = glu.shape[0]

    def body(dp, dc, dn, gp, gc, gn, w_ref, dglu_ref, dw_ref, dsh, gsh):
        @pl.when(pl.program_id(0) == 0)
        def _():
            dw_ref[...] = jnp.zeros_like(dw_ref)

        _fill_shifted(dsh, dp, dc, dn, tm)
        _fill_shifted(gsh, gp, gc, gn, tm)
        for c0 in range(0, D, CLN):
            for r0 in range(0, tm, CCH):
                acc = jnp.zeros((CCH, CLN), F32)
                for k in range(CW):
                    acc = acc + w_ref[k:k + 1, c0:c0 + CLN] * _shifted(dsh, HALO + CPAD - k, r0, c0)
                dglu_ref[r0:r0 + CCH, c0:c0 + CLN] = acc
            for k in range(CW):
                dwk = jnp.zeros((SUB, CLN), F32)
                for r0 in range(0, tm, CCH):
                    prod = _shifted(dsh, HALO, r0, c0) * _shifted(gsh, k + HALO - CPAD, r0, c0)
                    for r in range(0, CCH, SUB):
                        dwk = dwk + prod[r:r + SUB]
                dw_ref[k:k + 1, c0:c0 + CLN] += jnp.sum(dwk, axis=0, keepdims=True)

    n = tm + 2 * HALO - SUB
    return _pc(body, [dcv, dcv, dcv, glu, glu, glu, wdw], name="conv_mid_bwd", grid=(t // tm,),
               in_specs=_halo_specs(t, tm) + _halo_specs(t, tm) + [_acc(32, D)], out_specs=[_rows(tm, D), _acc(32, D)],
               out_shape=[_sds((t, D), F32), _sds((32, D), F32)],
               scratch_shapes=[pltpu.VMEM((SUB, n, D), F32), pltpu.VMEM((SUB, n, D), F32)], xchg=xchg)


def _ln(cv, lg, lb):
    mu = jnp.mean(cv, axis=-1, keepdims=True)
    cc = cv - mu
    rs = lax.rsqrt(jnp.mean(cc * cc, axis=-1, keepdims=True) + EPS)
    lh = cc * rs
    return lh, rs, lh * lg + lb


def _conv_post(cv, x, lg, lb, w2, b2, tm):
    t = x.shape[0]

    def body(cv_ref, x_ref, lg_ref, lb_ref, w_ref, b_ref, st_ref, out_ref):
        _, _, ln = _ln(cv_ref[...], lg_ref[...], lb_ref[...])
        s = (ln * _sigmoid(ln)).astype(BF16)
        st_ref[...] = s.T
        out_ref[...] = x_ref[...] + _dot(s, w_ref[...]) + b_ref[...]

    return _pc(body, [cv, x, lg, lb, w2, b2], name="conv_post", grid=(t // tm,),
               in_specs=[_rows(tm, D), _rows(tm, D), _acc(1, D), _acc(1, D), WHOLE, _acc(1, D)],
               out_specs=[_cols(tm), _rows(tm, D)], out_shape=[_sds((D, t), BF16), _sds((t, D), F32)])[0]


def _conv_post_bwd(dx, cv, lg, lb, w2, tm, xchg=None):
    t = dx.shape[0]

    def body(dx_ref, cv_ref, lg_ref, lb_ref, w_ref, dcv_ref, part_ref):
        @pl.when(pl.program_id(0) == 0)
        def _():
            part_ref[...] = jnp.zeros_like(part_ref)

        dx = dx_ref[...]
        lg = lg_ref[...]
        lh, rs, ln = _ln(cv_ref[...], lg, lb_ref[...])
        sg = _sigmoid(ln)
        dln = _dot_nt(dx, w_ref[...]) * (sg * (1.0 + ln * (1.0 - sg)))
        dlh = dln * lg
        dcv = rs * (dlh - jnp.mean(dlh, axis=-1, keepdims=True) - lh * jnp.mean(dlh * lh, axis=-1, keepdims=True))
        dcv_ref[...] = dcv
        part_ref[0:1, :] += jnp.sum(dln * lh, axis=0, keepdims=True)
        part_ref[1:2, :] += jnp.sum(dln, axis=0, keepdims=True)
        part_ref[2:3, :] += jnp.sum(dcv, axis=0, keepdims=True)
        part_ref[3:4, :] += jnp.sum(dx, axis=0, keepdims=True)

    return _pc(body, [dx, cv, lg, lb, w2], name="conv_post_bwd", grid=(t // tm,),
               in_specs=[_rows(tm, D), _rows(tm, D), _acc(1, D), _acc(1, D), WHOLE], out_specs=[_rows(tm, D), _acc(8, D)],
               out_shape=[_sds((t, D), F32), _sds((8, D), F32)], xchg=xchg)


def _conv_pre_bwd(dglu, u, x, g, w, dx_out, tm):
    t = x.shape[0]

    def body(dglu_ref, u_ref, x_ref, g_ref, w_ref, dxo_ref, du_ref, ht_ref, dx_ref, dg_ref, db_ref):
        @pl.when(pl.program_id(0) == 0)
        def _():
            dg_ref[...] = jnp.zeros_like(dg_ref)
            db_ref[...] = jnp.zeros_like(db_ref)

        gn = g_ref[...]
        xh, r, h = _rms(x_ref[...], gn)
        ht_ref[...] = h.astype(BF16).T
        dh = jnp.zeros_like(xh)
        for j in range(4):
            a = u_ref[j].astype(F32)
            sg = _sigmoid(u_ref[j + 4].astype(F32))
            dgl = dglu_ref[:, j * PWB:(j + 1) * PWB]
            da = dgl * sg
            dgt = dgl * a * sg * (1.0 - sg)
            db_ref[:, j * PWB:(j + 1) * PWB] += jnp.sum(da, axis=0, keepdims=True)
            db_ref[:, D + j * PWB:D + (j + 1) * PWB] += jnp.sum(dgt, axis=0, keepdims=True)
            da, dgt = da.astype(BF16), dgt.astype(BF16)
            du_ref[j] = da
            du_ref[j + 4] = dgt
            dh = dh + _dot_nt(da, w_ref[j]) + _dot_nt(dgt, w_ref[j + 4])
        dx, dg = _rms_bwd(dh, xh, r, gn)
        dx_ref[...] = dxo_ref[...] + dx
        dg_ref[...] += dg

    return _pc(body, [dglu, u, x, g, w, dx_out], name="conv_pre_bwd", grid=(t // tm,),
               in_specs=[_rows(tm, D), _blk3(NDEV, tm, PWB), _rows(tm, D), _acc(1, D), WHOLE, _rows(tm, D)],
               out_specs=[_blk3(NDEV, tm, PWB), _cols(tm), _rows(tm, D), _acc(1, D), _acc(1, 2 * D)],
               out_shape=[_sds((NDEV, t, PWB), BF16), _sds((D, t), BF16), _sds((t, D), F32), _sds((1, D), F32),
                          _sds((1, 2 * D), F32)])[0]


def _adamw(w, g, m, v):
    m = B1 * m + (1.0 - B1) * g
    v = B2 * v + (1.0 - B2) * (g * g)
    m_hat = m / (1.0 - B1 ** STEP)
    v_hat = v / (1.0 - B2 ** STEP)
    return -LR * (m_hat / (jnp.sqrt(v_hat) + AEPS) + WD * w), m, v


def _reduce_adamw(lands, w, m, v, tr, name, xchg=None):
    nl, r, c = w.shape

    def body(*refs):
        l_refs, (w_ref, m_ref, v_ref, g_ref, d_ref, nm_ref, nv_ref) = refs[:nl], refs[nl:]

        def total(ref):
            g = ref[0].astype(F32)
            for j in range(1, NDEV):
                g = g + ref[j].astype(F32)
            return g

        g = total(l_refs[0])
        for n in range(1, nl):
            g = jnp.where(pl.program_id(0) == n, total(l_refs[n]), g)
        g_ref[0] = g
        d_ref[0], nm_ref[0], nv_ref[0] = _adamw(w_ref[0], g, m_ref[0], v_ref[0])

    layer = pl.BlockSpec((1, tr, c), lambda l, i: (l, i, 0))
    outs, xo = _pc(body, [*lands, w, m, v], name=name, grid=(nl, r // tr),
                   in_specs=[pl.BlockSpec((NDEV, tr, c), lambda l, i: (0, i, 0))] * nl + [layer] * 3, out_specs=[layer] * 4,
                   out_shape=[_sds((nl, r, c), F32)] * 4, xchg=xchg)
    return outs if xchg is None else (outs, xo)


def _sum_parts(parts):
    _, r, c = parts.shape

    def body(p_ref, out_ref):
        s = p_ref[0]
        for j in range(1, NDEV):
            s = s + p_ref[j]
        out_ref[...] = s

    return _pc(body, [parts], name="sum_parts", in_specs=[WHOLE], out_specs=[WHOLE], out_shape=[_sds((r, c), F32)])[0][0]


def _adamw_small(w, g, m, v):
    def body(w_ref, g_ref, m_ref, v_ref, d_ref, nm_ref, nv_ref):
        d_ref[...], nm_ref[...], nv_ref[...] = _adamw(w_ref[...], g_ref[...], m_ref[...], v_ref[...])

    return _pc(body, [w, g, m, v], name="adamw_small", in_specs=[WHOLE] * 4, out_specs=[WHOLE] * 3,
               out_shape=[_sds(w.shape, F32)] * 3)[0]


def _rows128(a):
    a = a.astype(F32)
    if a.shape[-1] % 128:
        a = jnp.pad(a, [(0, 0)] * (a.ndim - 1) + [(0, 128 - a.shape[-1] % 128)])
    return a.reshape(-1, 128)


def _pack(arrs, rows):
    p = jnp.concatenate([_rows128(a) for a in arrs], axis=0)
    return jnp.pad(p, ((0, rows - p.shape[0]), (0, 0)))


SMALL = ("attn_norm", "ffn_norm", "final_norm", "attn_sink", "conv_norm", "conv_b_dw", "conv_ln_g", "conv_ln_b", "conv_b_pw2",
         "conv_b_pw1", "conv_w_dw")
SMALL_ROWS = 72


def _pack_small(d):
    return _pack([d[k] for k in SMALL], SMALL_ROWS)


def _unpack_small(p, like):
    out, r = {}, 0
    for k in SMALL:
        shp = like[k].shape
        n = -(-shp[-1] // 128) * (math.prod(shp[:-1]))
        blk = p[r:r + n]
        if shp[-1] % 128:
            blk = blk[:, :shp[-1]]
        out[k] = blk.reshape(shp)
        r += n
    return out


NAMES = ("attn_norm", "attn_w_qkv", "attn_w_o", "attn_sink", "conv_norm", "conv_w_pw1", "conv_b_pw1", "conv_w_dw", "conv_b_dw",
         "conv_ln_g", "conv_ln_b", "conv_w_pw2", "conv_b_pw2", "ffn_norm", "ffn_w_gu", "ffn_w_down", "final_norm")
TM = 256
TL = 512
TK = 2048


def _gu_t(a):
    return jnp.swapaxes(a, 1, 2)


def kernel(x, attn_norm, attn_w_qkv, attn_w_o, attn_sink, conv_norm, conv_w_pw1, conv_b_pw1, conv_w_dw, conv_b_dw, conv_ln_g, conv_ln_b, conv_w_pw2, conv_b_pw2, ffn_norm, ffn_w_gu, ffn_w_down, final_norm, loss_target, m_attn_norm, m_attn_w_qkv, m_attn_w_o, m_attn_sink, m_conv_norm, m_conv_w_pw1, m_conv_b_pw1, m_conv_w_dw, m_conv_b_dw, m_conv_ln_g, m_conv_ln_b, m_conv_w_pw2, m_conv_b_pw2, m_ffn_norm, m_ffn_w_gu, m_ffn_w_down, m_final_norm, v_attn_norm, v_attn_w_qkv, v_attn_w_o, v_attn_sink, v_conv_norm, v_conv_w_pw1, v_conv_b_pw1, v_conv_w_dw, v_conv_b_dw, v_conv_ln_g, v_conv_ln_b, v_conv_w_pw2, v_conv_b_pw2, v_ffn_norm, v_ffn_w_gu, v_ffn_w_down, v_final_norm):
    w = dict(zip(NAMES, (attn_norm, attn_w_qkv, attn_w_o, attn_sink, conv_norm, conv_w_pw1, conv_b_pw1, conv_w_dw, conv_b_dw, conv_ln_g,
                         conv_ln_b, conv_w_pw2, conv_b_pw2, ffn_norm, ffn_w_gu, ffn_w_down, final_norm)))
    m = dict(zip(NAMES, (m_attn_norm, m_attn_w_qkv, m_attn_w_o, m_attn_sink, m_conv_norm, m_conv_w_pw1, m_conv_b_pw1, m_conv_w_dw,
                         m_conv_b_dw, m_conv_ln_g, m_conv_ln_b, m_conv_w_pw2, m_conv_b_pw2, m_ffn_norm, m_ffn_w_gu, m_ffn_w_down,
                         m_final_norm)))
    v = dict(zip(NAMES, (v_attn_norm, v_attn_w_qkv, v_attn_w_o, v_attn_sink, v_conv_norm, v_conv_w_pw1, v_conv_b_pw1, v_conv_w_dw,
                         v_conv_b_dw, v_conv_ln_g, v_conv_ln_b, v_conv_w_pw2, v_conv_b_pw2, v_ffn_norm, v_ffn_w_gu, v_ffn_w_down,
                         v_final_norm)))
    me = 4 * lax.axis_index("x") + 2 * lax.axis_index("y") + lax.axis_index("c")
    for d in (w, m, v):
        d["ffn_w_gu"] = _gu_t(d["ffn_w_gu"])
    sh = {k: w[k][0].astype(BF16) for k in ("attn_w_qkv", "attn_w_o", "conv_w_pw1", "conv_w_pw2")}
    gu_b, down_b = w["ffn_w_gu"].astype(BF16), w["ffn_w_down"].astype(BF16)
    sh.update(ffn_w_gu0=gu_b[0], ffn_w_gu1=gu_b[1], ffn_w_down0=down_b[0], ffn_w_down1=down_b[1])
    x0, tgt = x[0], loss_target[0]
    t = x0.shape[0]
    tabs = _rope_tables(t)
    g_a, sink, g_f0, g_f1, g_fin = w["attn_norm"], w["attn_sink"][0], w["ffn_norm"][0:1], w["ffn_norm"][1:2], w["final_norm"][None]
    gather, scatter = False, True

    shard_rows = _pack([w["conv_w_dw"][0], jnp.zeros((1, 128), F32), w["conv_norm"], w["conv_b_dw"], w["conv_ln_g"], w["conv_ln_b"],
                        w["conv_b_pw2"], w["conv_b_pw1"]], 40)
    wqkv_g, sm = _exchange([sh["attn_w_qkv"], shard_rows], [gather] * 2, "gather_attn")
    wqkv = wqkv_g.transpose(1, 0, 2).reshape(D, QKV)

    def full_vec(row, n=1):
        return sm[:, row:row + n, :].reshape(1, NDEV * n * 128)

    w_dw, g_c, b_dw, ln_g, ln_b = sm[:, 0:32, :].transpose(1, 0, 2).reshape(32, D), full_vec(32), full_vec(33), full_vec(34), full_vec(35)
    b_pw2, b_pw1 = full_vec(36), full_vec(37, 2)

    (q_t, kv), (wo_g, wd0_g) = _attn_pre(x0, g_a, wqkv, tabs, TM, xchg=([sh["attn_w_o"], sh["ffn_w_down0"]], [gather] * 2))
    o_t, (wgu0,) = _attn_fwd(q_t, kv, sink, xchg=([sh["ffn_w_gu0"]], [gather]))
    wo, wd0 = wo_g.reshape(D, D), wd0_g.reshape(4, GUB, D)
    (x1, gu0, x2), (wpw1, wgu1) = _ffn_fwd_attn(
        o_t, x0, wo, g_f0, wgu0, wd0, TM, xchg=([sh["conv_w_pw1"], sh["ffn_w_gu1"]], [gather] * 2))
    tl = min(TL, t)
    (u, glu), (wpw2_g,) = _conv_pre(x2, g_c, wpw1, b_pw1, tl, xchg=([sh["conv_w_pw2"]], [gather]))
    wpw2 = wpw2_g.reshape(D, D)
    cv, (wd1_g,) = _conv_mid(glu, w_dw, b_dw, TM, xchg=([sh["ffn_w_down1"]], [gather]))
    wd1 = wd1_g.reshape(4, GUB, D)
    s_t, x3 = _conv_post(cv, x2, ln_g, ln_b, wpw2, b_pw2, tl)
    (gu1, dx4, fin), _ = _ffn_fwd_final(x3, g_f1, wgu1, wd1, g_fin, tgt, TM)

    land = {}
    tk = min(TK, t)
    (dgu1_t, act1_t, h3, dx3, dg_f1), _ = _ffn_bwd(dx4, x3, gu1, g_f1, wgu1, wd1, TM, "ffn_bwd1")
    dwgu1 = _wgrad(dgu1_t, h3[None], tk, "dwgu1")
    dwd1 = _wgrad(act1_t, dx4[None], tk, "dwd1").reshape(NDEV, DFF // NDEV, D)
    (dcv, cpart), (land["ffn_w_down1"],) = _conv_post_bwd(dx3, cv, ln_g, ln_b, wpw2, tl, xchg=([dwd1], [scatter]))
    dwpw2 = _wgrad(s_t[None], dx3[None], tk, "dwpw2").reshape(NDEV, D // NDEV, D)
    (dglu, dw_dw), (land["ffn_w_gu1"], land["conv_w_pw2"]) = _conv_mid_bwd(dcv, glu, w_dw, TM, xchg=([dwgu1, dwpw2], [scatter] * 2))
    du, h2_t, dx2, dg_c, db_pw1 = _conv_pre_bwd(dglu, u, x2, g_c, wpw1, dx3, tl)
    dwpw1 = _wgrad(h2_t[None], du, tk, "dwpw1")
    (dgu0_t, act0_t, h1, dx1, dg_f0, do_t), (land["conv_w_pw1"],) = _ffn_bwd(
        dx2, x1, gu0, g_f0, wgu0, wd0, TM, "ffn_bwd0", wo=wo, xchg=([dwpw1], [scatter]))
    dwgu0 = _wgrad(dgu0_t, h1[None], tk, "dwgu0")
    dwd0 = _wgrad(act0_t, dx2[None], tk, "dwd0").reshape(NDEV, DFF // NDEV, D)
    dwo = _wgrad(o_t[None], dx1[None], tk, "dwo").reshape(NDEV, D // NDEV, D)
    (dq_t, dk, dv, dsink), (land["ffn_w_gu0"], land["ffn_w_down0"]) = _attn_bwd(
        q_t, kv, do_t, sink, TM, xchg=([dwgu0, dwd0], [scatter] * 2))
    dqkv, h0_t, dx0, dg_a = _attn_pre_bwd(dq_t, dk, dv, x0, g_a, wqkv, tabs, dx1, TM)
    dwqkv, (land["attn_w_o"],) = _wgrad(h0_t[None], dqkv[None], tk, "dwqkv", xchg=([dwo], [scatter]))
    dwqkv = dwqkv[0].reshape(D, NDEV, QKV // NDEV).transpose(1, 0, 2)

    lane0 = (lax.broadcasted_iota(jnp.int32, (1, 128), 1) == 0).astype(F32)
    parts = _pack([dg_a, dg_f0, dg_f1, fin[0:1], dsink[0:1, :NH], fin[1, 0] * lane0, jnp.zeros((6, 128), F32), dg_c, cpart[2:3],
                   cpart[0:1], cpart[1:2], cpart[3:4], db_pw1, dw_dw.reshape(32, NDEV, 128)], 352)
    land["attn_w_qkv"], parts_g = _exchange([dwqkv, parts], [scatter, gather], "scatter_attn")
    red = _sum_parts(parts_g)

    def shard_rows_of(row, n=1):
        return lax.dynamic_slice_in_dim(red, row + n * me, n, axis=0)

    gs = {
        "attn_norm": red[0:8].reshape(1, D), "ffn_norm": red[8:24].reshape(2, D), "final_norm": red[24:32].reshape(D),
        "attn_sink": red[32:33, :NH], "conv_norm": shard_rows_of(40), "conv_b_dw": shard_rows_of(48), "conv_ln_g": shard_rows_of(56),
        "conv_ln_b": shard_rows_of(64), "conv_b_pw2": shard_rows_of(72), "conv_b_pw1": shard_rows_of(80, 2).reshape(1, PWB),
        "conv_w_dw": lax.dynamic_index_in_dim(red[96:352].reshape(32, NDEV, 128), me, axis=1, keepdims=False)[None, :CW],
    }
    loss = red[33, 0]

    grads, deltas, new_m, new_v = dict(gs), {}, {}, {}
    ds, ms, vs = _adamw_small(_pack_small(w), _pack_small(gs), _pack_small(m), _pack_small(v))
    deltas.update(_unpack_small(ds, gs))
    new_m.update(_unpack_small(ms, gs))
    new_v.update(_unpack_small(vs, gs))
    for k in ("attn_w_qkv", "attn_w_o", "conv_w_pw1", "conv_w_pw2", "ffn_w_gu", "ffn_w_down"):
        lands = [land[k + "0"], land[k + "1"]] if k.startswith("ffn") else [land[k]]
        tr = {1024: 256, 128: 128, 352: 176, GUB: 176}[w[k].shape[1]]
        grads[k], deltas[k], new_m[k], new_v[k] = _reduce_adamw(lands, w[k], m[k], v[k], tr, "adamw_" + k)
    for d in (grads, deltas, new_m, new_v):
        d["ffn_w_gu"] = _gu_t(d["ffn_w_gu"])
    return (loss, dx0[None], *[grads[k] for k in NAMES], *[deltas[k] for k in NAMES], *[new_m[k] for k in NAMES],
            *[new_v[k] for k in NAMES])
```

```python
import math

import jax
import jax.numpy as jnp
from jax import lax
from jax.experimental import pallas as pl
from jax.experimental.pallas import tpu as pltpu

F32 = jnp.float32
BF16 = jnp.bfloat16

D = 1024
NH = 16
NKV = 4
HD = 64
GROUP = NH // NKV
ROT = 16
THETA = 500000.0
BLK = 128
QKV = (NH + 2 * NKV) * HD
KOFF = NH * HD
VOFF = KOFF + NKV * HD
DFF = 2816
NDEV = 8
GUB = 2 * DFF // NDEV
PWB = 2 * D // NDEV
CW = 31
CPAD = 15
HALO = 16
EPS = 1e-6
NEG = -1e30
LR, B1, B2, AEPS, WD, STEP = 0.001, 0.9, 0.999, 1e-08, 0.01, 10

VMEM_LIMIT = 56 * 1024 * 1024
MESH = pl.DeviceIdType.MESH
WHOLE = pl.BlockSpec(memory_space=pltpu.VMEM)
ANY = pl.BlockSpec(memory_space=pl.ANY)


def _place():
    x, y, c = lax.axis_index("x"), lax.axis_index("y"), lax.axis_index("c")
    return x, y, c, 4 * x + 2 * y + c


def _peer(x, y, c, j):
    px = 1 - x if j & 4 else x
    py = 1 - y if j & 2 else y
    pc = 1 - c if j & 1 else c
    return (px, py, pc), 4 * px + 2 * py + pc


SIBLING = 1
OTHER_CHIPS = (2, 4, 6)


class _Exchange:
    def __init__(self, src, dst, scatter, send, recv, loc):
        self.src, self.dst, self.scatter, self.send, self.recv, self.loc = src, dst, scatter, send, recv, loc
        self.x, self.y, self.c, self.me = _place()

    def _remote(self, k, j, s, d, to):
        peer, _ = _peer(self.x, self.y, self.c, to)
        return pltpu.make_async_remote_copy(src_ref=s, dst_ref=d, send_sem=self.send.at[k, j - 1], recv_sem=self.recv.at[k, j - 1],
                                            device_id=peer, device_id_type=MESH)

    def _slot(self, j):
        return _peer(self.x, self.y, self.c, j)[1]

    def local(self, k):
        if self.scatter[k]:
            return pltpu.make_async_copy(self.src[k].at[self.me], self.dst[k].at[0], self.loc.at[k])
        return pltpu.make_async_copy(self.src[k], self.dst[k].at[self.me], self.loc.at[k])

    def direct(self, k, j):
        if self.scatter[k]:
            return self._remote(k, j, self.src[k].at[self._slot(j)], self.dst[k].at[j], j)
        return self._remote(k, j, self.src[k], self.dst[k].at[self.me], j)

    def passed_on(self, k, j):
        rows = self.dst[k].at[self._slot(j)]
        return self._remote(k, j + 1, rows, rows, SIBLING)

    def arrival(self, k, j):
        rows = self.dst[k].at[j if self.scatter[k] else self._slot(j)]
        return self._remote(k, j, rows, rows, j)

    def sent(self, k):
        return tuple(range(1, NDEV)) if self.scatter[k] else (SIBLING,) + OTHER_CHIPS

    def start(self):
        for k in range(len(self.src)):
            self.local(k).start()
            for j in self.sent(k):
                self.direct(k, j).start()

    def finish(self):
        gathers = [k for k in range(len(self.src)) if not self.scatter[k]]
        for k in gathers:
            for j in OTHER_CHIPS:
                self.arrival(k, j).wait_recv()
                self.passed_on(k, j).start()
        for k in range(len(self.src)):
            for j in range(1, NDEV):
                if self.scatter[k] or j not in OTHER_CHIPS:
                    self.arrival(k, j).wait_recv()
            for j in self.sent(k):
                self.direct(k, j).wait_send()
            if not self.scatter[k]:
                for j in OTHER_CHIPS:
                    self.passed_on(k, j).wait_send()
            self.local(k).wait()


def _call(body, **kw):
    return pl.pallas_call(body, **kw)


def _pc(body, operands, *, name, in_specs, out_specs, out_shape, grid=None, scratch_shapes=(), xchg=None):
    kw = dict(name=name, compiler_params=pltpu.CompilerParams(
        dimension_semantics=None if grid is None else ("arbitrary",) * len(grid), vmem_limit_bytes=VMEM_LIMIT,
        has_side_effects=xchg is not None))
    if grid is not None:
        kw["grid"] = grid
    if xchg is None:
        outs = _call(body, in_specs=list(in_specs), out_specs=list(out_specs), out_shape=list(out_shape),
                     scratch_shapes=list(scratch_shapes), **kw)(*operands)
        return list(outs), []
    arrs, scatter = xchg
    nx, n_in, n_out, n_s = len(arrs), len(in_specs), len(out_specs), len(scratch_shapes)

    def wrapped(*refs):
        ins, refs = refs[:n_in], refs[n_in:]
        src, refs = refs[:nx], refs[nx:]
        outs, refs = refs[:n_out], refs[n_out:]
        dst, refs = refs[:nx], refs[nx:]
        scr, (send, recv, loc) = refs[:n_s], refs[n_s:]
        if grid is None:
            _Exchange(src, dst, scatter, send, recv, loc).start()
            body(*ins, *outs, *scr)
            _Exchange(src, dst, scatter, send, recv, loc).finish()
            return

        first = last = True
        for axis, n in enumerate(grid):
            first &= pl.program_id(axis) == 0
            last &= pl.program_id(axis) == n - 1

        @pl.when(first)
        def _():
            _Exchange(src, dst, scatter, send, recv, loc).start()

        body(*ins, *outs, *scr)

        @pl.when(last)
        def _():
            _Exchange(src, dst, scatter, send, recv, loc).finish()

    xshape = [jax.ShapeDtypeStruct(a.shape if sc else (NDEV,) + a.shape, a.dtype) for a, sc in zip(arrs, scatter)]
    sems = [pltpu.SemaphoreType.DMA((nx, NDEV - 1)), pltpu.SemaphoreType.DMA((nx, NDEV - 1)), pltpu.SemaphoreType.DMA((nx,))]
    res = _call(wrapped, in_specs=list(in_specs) + [ANY] * nx, out_specs=list(out_specs) + [ANY] * nx,
                out_shape=list(out_shape) + xshape, scratch_shapes=list(scratch_shapes) + sems, **kw)(*operands, *arrs)
    return list(res[:n_out]), list(res[n_out:])


def _exchange(arrs, scatter, name):
    return _pc(lambda: None, [], name=name, in_specs=[], out_specs=[], out_shape=[], xchg=(arrs, scatter))[1]


def _rows(tm, width):
    return pl.BlockSpec((tm, width), lambda i: (i, 0))


def _cols(tm):
    return pl.BlockSpec((D, tm), lambda i: (0, i))


def _qblocks(tm):
    return pl.BlockSpec((tm // BLK, D, BLK), lambda i: (i, 0, 0))


def _put_blocks(ref, val):
    for b in range(ref.shape[0]):
        ref[b] = val[:, b * BLK:(b + 1) * BLK]


def _get_blocks(ref):
    return jnp.concatenate([ref[b] for b in range(ref.shape[0])], axis=1)


def _blk3(nb, tm, width):
    return pl.BlockSpec((nb, tm, width), lambda i: (0, i, 0))


def _acc(rows, width):
    return pl.BlockSpec((rows, width), lambda i: (0, 0))


def _sds(shape, dtype):
    return jax.ShapeDtypeStruct(shape, dtype)


def _dot(a, b):
    return jnp.dot(a.astype(BF16), b.astype(BF16), preferred_element_type=F32)


def _dot_nt(a, b):
    return lax.dot_general(a.astype(BF16), b.astype(BF16), (((1,), (1,)), ((), ())), preferred_element_type=F32)


def _dot_tn(a, b):
    return lax.dot_general(a.astype(BF16), b.astype(BF16), (((0,), (0,)), ((), ())), preferred_element_type=F32)


def _sigmoid(x):
    return 1.0 / (1.0 + jnp.exp(-x))


def _rms(x, g):
    r = lax.rsqrt(jnp.mean(x * x, axis=-1, keepdims=True) + EPS)
    xh = x * r
    return xh, r, xh * g


def _rms_bwd(dh, xh, r, g):
    dxh = dh * g
    dg = jnp.sum(dh * xh, axis=0, keepdims=True)
    dx = r * (dxh - xh * jnp.mean(dxh * xh, axis=-1, keepdims=True))
    return dx, dg


def _lanes(t, width):
    return jnp.tile(t, (1, width // t.shape[1]))


def _rope(z, c, sa, sb):
    w = z.shape[1]
    return z * _lanes(c, w) + pltpu.roll(z, w - 8, 1) * _lanes(sa, w) + pltpu.roll(z, 8, 1) * _lanes(sb, w)


def _rope_t(dz, c, sa, sb):
    w = dz.shape[1]
    return dz * _lanes(c, w) + pltpu.roll(dz * _lanes(sa, w), 8, 1) + pltpu.roll(dz * _lanes(sb, w), w - 8, 1)


def _rope_tables(t):
    pos = jnp.arange(t, dtype=F32)
    inv_freq = THETA ** (-jnp.arange(0, ROT, 2, dtype=F32) / ROT)
    ang = pos[:, None] * inv_freq[None, :]
    cos, sin = jnp.cos(ang), jnp.sin(ang)
    one = jnp.ones((t, HD - ROT), F32)
    zero = jnp.zeros((t, HD - ROT), F32)
    z8 = jnp.zeros((t, 8), F32)
    c = jnp.concatenate([cos, cos, one], axis=1)
    sa = jnp.concatenate([-sin, z8, zero], axis=1)
    sb = jnp.concatenate([z8, sin, zero], axis=1)
    return tuple(jnp.tile(a, (1, 2)) for a in (c, sa, sb))


KVW = NKV * HD
GW = GROUP * BLK


def _attn_pre(x, g, wqkv, tabs, tm, xchg=None):
    t = x.shape[0]

    def body(x_ref, g_ref, w_ref, c_ref, sa_ref, sb_ref, qt_ref, kv_ref):
        _, _, h = _rms(x_ref[...], g_ref[...])
        z = _dot(h, w_ref[...])
        c, sa, sb = c_ref[...], sa_ref[...], sb_ref[...]
        _put_blocks(qt_ref, (_rope(z[:, :KOFF], c, sa, sb) * 0.125).T.astype(BF16))
        kv_ref[:, :KVW] = _rope(z[:, KOFF:VOFF], c, sa, sb).astype(BF16)
        kv_ref[:, KVW:] = z[:, VOFF:].astype(BF16)

    return _pc(body, [x, g, wqkv, *tabs], name="attn_pre", grid=(t // tm,),
               in_specs=[_rows(tm, D), _acc(1, D), WHOLE, _rows(tm, 128), _rows(tm, 128), _rows(tm, 128)],
               out_specs=[_qblocks(tm), _rows(tm, 2 * KVW)],
               out_shape=[_sds((t // BLK, D, BLK), BF16), _sds((t, 2 * KVW), BF16)], xchg=xchg)


def _attn_specs(nblk):
    prev = lambda i: jnp.maximum(i - 1, 0)
    nxt = lambda i: jnp.minimum(i + 1, nblk - 1)
    return [
        _qblocks(BLK),
        pl.BlockSpec((BLK, KVW), lambda i: (prev(i), 0)),
        pl.BlockSpec((BLK, KVW), lambda i: (i, 0)),
        pl.BlockSpec((BLK, KVW), lambda i: (nxt(i), 0)),
        pl.BlockSpec((BLK, KVW), lambda i: (prev(i), 1)),
        pl.BlockSpec((BLK, KVW), lambda i: (i, 1)),
        pl.BlockSpec((BLK, KVW), lambda i: (nxt(i), 1)),
    ]


def _attn_bias():
    c = lax.broadcasted_iota(jnp.int32, (3, 3 * BLK, GW), 1)
    r = lax.broadcasted_iota(jnp.int32, (3, 3 * BLK, GW), 2) & (BLK - 1)
    slab = lax.broadcasted_iota(jnp.int32, (3, 3 * BLK, GW), 0)
    valid = (c >= r) & (c - 2 * BLK <= r) & ((slab != 0) | (c >= BLK)) & ((slab != 2) | (c < 2 * BLK))
    return jnp.where(valid, 0.0, NEG).astype(F32)


def _bias_spec(nblk):
    assert nblk >= 2
    return pl.BlockSpec((1, 3 * BLK, GW), lambda i: (jnp.where(i == 0, 0, jnp.where(i == nblk - 1, 2, 1)), 0, 0))


def _group(ref, kv):
    return jnp.concatenate([ref[0, (kv * GROUP + g) * HD:(kv * GROUP + g + 1) * HD, :] for g in range(GROUP)], axis=1)


def _group_sink(sink_ref, kv):
    return jnp.concatenate([jnp.full((1, BLK), sink_ref[kv * GROUP + g], F32) for g in range(GROUP)], axis=1)


def _attn_exp(k_h, qt_g, bias, sink_g):
    s = _dot(k_h, qt_g) + bias
    m = jnp.maximum(jnp.max(s, axis=0, keepdims=True), sink_g)
    e = jnp.exp(s - m)
    es = jnp.exp(sink_g - m)
    return e, 1.0 / (jnp.sum(e, axis=0, keepdims=True) + es), es


def _attn_fwd(qt, kv, sink, xchg=None):
    t = kv.shape[0]
    nblk = t // BLK

    def body(sink_ref, bias_ref, qt_ref, kp, kc, kn, vp, vc, vn, ot_ref):
        bias = bias_ref[0]
        k = jnp.concatenate([kp[...], kc[...], kn[...]], axis=0)
        v = jnp.concatenate([vp[...], vc[...], vn[...]], axis=0)
        for h in range(NKV):
            e, inv, _ = _attn_exp(k[:, h * HD:(h + 1) * HD], _group(qt_ref, h), bias, _group_sink(sink_ref, h))
            ot_g = (_dot_tn(v[:, h * HD:(h + 1) * HD], e) * inv).astype(BF16)
            for g in range(GROUP):
                ot_ref[0, (h * GROUP + g) * HD:(h * GROUP + g + 1) * HD, :] = ot_g[:, g * BLK:(g + 1) * BLK]

    outs, xo = _pc(body, [sink, _attn_bias(), qt] + [kv] * 6, name="attn_fwd", grid=(nblk,),
                   in_specs=[pl.BlockSpec(memory_space=pltpu.SMEM), _bias_spec(nblk)] + _attn_specs(nblk),
                   out_specs=[_qblocks(BLK)], out_shape=[_sds((nblk, D, BLK), BF16)], xchg=xchg)
    return outs[0], xo


def _attn_bwd(qt, kv, dot, sink, pad, xchg=None):
    t = kv.shape[0]
    nblk = t // BLK

    def body(sink_ref, bias_ref, qt_ref, kp, kc, kn, vp, vc, vn, dot_ref, dqt_ref, dk_ref, dv_ref, ds_ref):
        i = pl.program_id(0)

        @pl.when(i == 0)
        def _():
            dk_ref[...] = jnp.zeros_like(dk_ref)
            dv_ref[...] = jnp.zeros_like(dv_ref)
            ds_ref[...] = jnp.zeros_like(ds_ref)

        bias = bias_ref[0]
        k = jnp.concatenate([kp[...], kc[...], kn[...]], axis=0)
        v = jnp.concatenate([vp[...], vc[...], vn[...]], axis=0)
        lane = lax.broadcasted_iota(jnp.int32, (1, 128), 1)
        dsink = jnp.zeros((1, 128), F32)
        rows = pl.ds(pl.multiple_of(i * BLK + (pad - BLK), BLK), 3 * BLK)
        for h in range(NKV):
            k_h, v_h = k[:, h * HD:(h + 1) * HD], v[:, h * HD:(h + 1) * HD]
            qt_g, dot_g = _group(qt_ref, h), _group(dot_ref, h)
            e, inv, es = _attn_exp(k_h, qt_g, bias, _group_sink(sink_ref, h))
            p, ps = e * inv, es * inv
            dp = _dot(v_h, dot_g)
            delta = jnp.sum(p * dp, axis=0, keepdims=True)
            ds = (p * (dp - delta)).astype(BF16)
            dqt_g = _dot_tn(k_h, ds)
            dk_ref[rows, h * HD:(h + 1) * HD] += _dot_nt(ds, qt_g)
            dv_ref[rows, h * HD:(h + 1) * HD] += _dot_nt(p, dot_g)
            psd = ps * delta
            for g in range(GROUP):
                n = h * GROUP + g
                dqt_ref[0, n * HD:(n + 1) * HD, :] = dqt_g[:, g * BLK:(g + 1) * BLK]
                dsink = dsink - jnp.where(lane == n, jnp.sum(psd[:, g * BLK:(g + 1) * BLK], axis=1, keepdims=True), 0.0)
        ds_ref[0:1, :] += dsink

    outs, xo = _pc(body, [sink, _attn_bias(), qt] + [kv] * 6 + [dot], name="attn_bwd", grid=(nblk,),
                   in_specs=[pl.BlockSpec(memory_space=pltpu.SMEM), _bias_spec(nblk)] + _attn_specs(nblk) + [_qblocks(BLK)],
                   out_specs=[_qblocks(BLK), _acc(t + 2 * pad, KVW), _acc(t + 2 * pad, KVW), _acc(8, 128)],
                   out_shape=[_sds((nblk, D, BLK), F32), _sds((t + 2 * pad, KVW), F32), _sds((t + 2 * pad, KVW), F32),
                              _sds((8, 128), F32)], xchg=xchg)
    return outs, xo


def _attn_pre_bwd(dqt, dk, dv, x, g, wqkv, tabs, dx_out, tm):
    t = x.shape[0]

    def body(dqt_ref, dk_ref, dv_ref, x_ref, g_ref, w_ref, c_ref, sa_ref, sb_ref, dxo_ref, dqkv_ref, ht_ref, dx_ref, dg_ref):
        @pl.when(pl.program_id(0) == 0)
        def _():
            dg_ref[...] = jnp.zeros_like(dg_ref)

        c, sa, sb = c_ref[...], sa_ref[...], sb_ref[...]
        dqkv_ref[:, :KOFF] = _rope_t(_get_blocks(dqt_ref).T * 0.125, c, sa, sb).astype(BF16)
        dqkv_ref[:, KOFF:VOFF] = _rope_t(dk_ref[...], c, sa, sb).astype(BF16)
        dqkv_ref[:, VOFF:] = dv_ref[...].astype(BF16)
        g = g_ref[...]
        xh, r, h = _rms(x_ref[...], g)
        ht_ref[...] = h.astype(BF16).T
        dh = _dot_nt(dqkv_ref[...], w_ref[...])
        dx, dg = _rms_bwd(dh, xh, r, g)
        dx_ref[...] = dxo_ref[...] + dx
        dg_ref[...] += dg

    return _pc(body, [dqt, dk, dv, x, g, wqkv, *tabs, dx_out], name="attn_pre_bwd", grid=(t // tm,),
               in_specs=[_qblocks(tm), pl.BlockSpec((tm, KVW), lambda i: (i + 1, 0)), pl.BlockSpec((tm, KVW), lambda i: (i + 1, 0)),
                         _rows(tm, D), _acc(1, D), WHOLE, _rows(tm, 128), _rows(tm, 128), _rows(tm, 128), _rows(tm, D)],
               out_specs=[_rows(tm, QKV), _cols(tm), _rows(tm, D), _acc(1, D)],
               out_shape=[_sds((t, QKV), BF16), _sds((D, t), BF16), _sds((t, D), F32), _sds((1, D), F32)])[0]


def _ffn(x, g, wgu_ref, wd_ref, gu_ref):
    _, _, h = _rms(x, g)
    hb = h.astype(BF16)
    y = x
    for j in range(4):
        gj = _dot_nt(hb, wgu_ref[j])
        uj = _dot_nt(hb, wgu_ref[j + 4])
        gu_ref[j] = gj.astype(BF16)
        gu_ref[j + 4] = uj.astype(BF16)
        y = y + _dot(gj * _sigmoid(gj) * uj, wd_ref[j])
    return y


def _ffn_fwd_attn(ot, x, wo, g, wgu, wd, tm, xchg=None):
    t = x.shape[0]

    def body(ot_ref, x_ref, wo_ref, g_ref, wgu_ref, wd_ref, x1_ref, gu_ref, out_ref):
        x1 = x_ref[...] + _dot_tn(_get_blocks(ot_ref), wo_ref[...])
        x1_ref[...] = x1
        out_ref[...] = _ffn(x1, g_ref[...], wgu_ref, wd_ref, gu_ref)

    return _pc(body, [ot, x, wo, g, wgu, wd], name="ffn_fwd0", grid=(t // tm,),
               in_specs=[_qblocks(tm), _rows(tm, D), WHOLE, _acc(1, D), WHOLE, WHOLE],
               out_specs=[_rows(tm, D), _blk3(NDEV, tm, GUB), _rows(tm, D)],
               out_shape=[_sds((t, D), F32), _sds((NDEV, t, GUB), BF16), _sds((t, D), F32)], xchg=xchg)


def _ffn_fwd_final(x, g, wgu, wd, g_fin, tgt, tm, xchg=None):
    t = x.shape[0]

    def body(x_ref, g_ref, wgu_ref, wd_ref, gf_ref, t_ref, gu_ref, dx_ref, part_ref):
        @pl.when(pl.program_id(0) == 0)
        def _():
            part_ref[...] = jnp.zeros_like(part_ref)

        gf = gf_ref[...]
        xh, r, y = _rms(_ffn(x_ref[...], g_ref[...], wgu_ref, wd_ref, gu_ref), gf)
        err = y - t_ref[...]
        dx, dg = _rms_bwd(err * (1.0 / D), xh, r, gf)
        dx_ref[...] = dx
        part_ref[0:1, :] += dg
        tok = jnp.sum(err * err, axis=-1, keepdims=True) * (1.0 / D)
        lane = lax.broadcasted_iota(jnp.int32, (1, D), 1)
        part_ref[1:2, :] += jnp.where(lane == 0, 0.5 * jnp.sum(tok, axis=0, keepdims=True), 0.0)

    return _pc(body, [x, g, wgu, wd, g_fin, tgt], name="ffn_fwd1", grid=(t // tm,),
               in_specs=[_rows(tm, D), _acc(1, D), WHOLE, WHOLE, _acc(1, D), _rows(tm, D)],
               out_specs=[_blk3(NDEV, tm, GUB), _rows(tm, D), _acc(8, D)],
               out_shape=[_sds((NDEV, t, GUB), BF16), _sds((t, D), F32), _sds((8, D), F32)], xchg=xchg)


def _ffn_bwd(dy, x, gu, g, wgu, wd, tm, name, wo=None, xchg=None):
    t = x.shape[0]

    def body(dy_ref, x_ref, gu_ref, g_ref, wgu_ref, wd_ref, *rest):
        wo_ref = rest[0] if wo is not None else None
        dgut_ref, actt_ref, h_ref, dx_ref, dg_ref = rest[wo is not None:][:5]

        @pl.when(pl.program_id(0) == 0)
        def _():
            dg_ref[...] = jnp.zeros_like(dg_ref)

        dy = dy_ref[...]
        dyb = dy.astype(BF16)
        gn = g_ref[...]
        xh, r, h = _rms(x_ref[...], gn)
        h_ref[...] = h.astype(BF16)
        dh = jnp.zeros_like(dy)
        for j in range(4):
            gj = gu_ref[j].astype(F32)
            uj = gu_ref[j + 4].astype(F32)
            sg = _sigmoid(gj)
            silu = gj * sg
            actt_ref[j] = (silu * uj).astype(BF16).T
            dact = _dot_nt(dyb, wd_ref[j])
            dgj = (dact * uj * (sg * (1.0 + gj * (1.0 - sg)))).astype(BF16)
            duj = (dact * silu).astype(BF16)
            dgut_ref[j] = dgj.T
            dgut_ref[j + 4] = duj.T
            dh = dh + _dot(dgj, wgu_ref[j]) + _dot(duj, wgu_ref[j + 4])
        dx, dg = _rms_bwd(dh, xh, r, gn)
        dx = dy + dx
        dx_ref[...] = dx
        dg_ref[...] += dg
        if wo is not None:
            _put_blocks(rest[6], _dot(wo_ref[...], dx.astype(BF16).T).astype(BF16))

    extra = wo is not None
    tcols = lambda nb: pl.BlockSpec((nb, GUB, tm), lambda i: (0, 0, i))
    return _pc(body, [dy, x, gu, g, wgu, wd] + [wo] * extra, name=name, grid=(t // tm,),
               in_specs=[_rows(tm, D), _rows(tm, D), _blk3(NDEV, tm, GUB), _acc(1, D), WHOLE, WHOLE] + [WHOLE] * extra,
               out_specs=[tcols(NDEV), tcols(4), _rows(tm, D), _rows(tm, D), _acc(1, D)] + [_qblocks(tm)] * extra,
               out_shape=[_sds((NDEV, GUB, t), BF16), _sds((4, GUB, t), BF16), _sds((t, D), BF16), _sds((t, D), F32),
                          _sds((1, D), F32)] + [_sds((t // BLK, D, BLK), BF16)] * extra, xchg=xchg)


def _wgrad(at, b, tk, name, xchg=None, by_block=False):
    if by_block:
        na, (_, ma, _), t = 1, at.shape, at.shape[0] * BLK
        a_spec = pl.BlockSpec((tk // BLK, ma, BLK), lambda j, k: (k, 0, 0))
    else:
        na, ma, t = at.shape
        a_spec = pl.BlockSpec((1, ma, tk), lambda j, k: (j if na > 1 else 0, 0, k))
    nb, _, mb = b.shape
    nk = t // tk

    def body(a_ref, b_ref, out_ref, acc):
        k = pl.program_id(1)

        @pl.when(k == 0)
        def _():
            acc[...] = jnp.zeros_like(acc)

        acc[...] += _dot(_get_blocks(a_ref) if by_block else a_ref[0], b_ref[0])

        @pl.when(k == nk - 1)
        def _():
            out_ref[0] = acc[...].astype(BF16)

    outs, xo = _pc(body, [at, b], name=name, grid=(max(na, nb), nk),
                   in_specs=[a_spec, pl.BlockSpec((1, tk, mb), lambda j, k: (j if nb > 1 else 0, k, 0))],
                   out_specs=[pl.BlockSpec((1, ma, mb), lambda j, k: (j, 0, 0))], out_shape=[_sds((max(na, nb), ma, mb), BF16)],
                   scratch_shapes=[pltpu.VMEM((ma, mb), F32)], xchg=xchg)
    return outs[0] if xchg is None else (outs[0], xo)


def _conv_pre(x, g, w, b, tm, xchg=None):
    t = x.shape[0]

    def body(x_ref, g_ref, w_ref, b_ref, u_ref, glu_ref):
        _, _, h = _rms(x_ref[...], g_ref[...])
        hb = h.astype(BF16)
        for j in range(4):
            a = _dot(hb, w_ref[j]) + b_ref[:, j * PWB:(j + 1) * PWB]
            gt = _dot(hb, w_ref[j + 4]) + b_ref[:, D + j * PWB:D + (j + 1) * PWB]
            u_ref[j] = a.astype(BF16)
            u_ref[j + 4] = gt.astype(BF16)
            glu_ref[:, j * PWB:(j + 1) * PWB] = a * _sigmoid(gt)

    return _pc(body, [x, g, w, b], name="conv_pre", grid=(t // tm,),
               in_specs=[_rows(tm, D), _acc(1, D), WHOLE, _acc(1, 2 * D)], out_specs=[_blk3(NDEV, tm, PWB), _rows(tm, D)],
               out_shape=[_sds((NDEV, t, PWB), BF16), _sds((t, D), F32)], xchg=xchg)


def _halo_specs(t, tm):
    per = tm // HALO
    last = t // HALO - 1
    return [
        pl.BlockSpec((HALO, D), lambda i: (jnp.maximum(i * per - 1, 0), 0)),
        _rows(tm, D),
        pl.BlockSpec((HALO, D), lambda i: (jnp.minimum((i + 1) * per, last), 0)),
    ]


SUB = 8
CCH = 32
CLN = 256


def _fill_shifted(sh, prev, cur, nxt, tm):
    i = pl.program_id(0)
    rows = jnp.concatenate([jnp.where(i == 0, 0.0, prev[...]), cur[...], jnp.where(i == pl.num_programs(0) - 1, 0.0, nxt[...])], axis=0)
    n = tm + 2 * HALO - SUB
    for b in range(SUB):
        sh[b] = rows[b:b + n]


def _shifted(sh, off, r0, c0):
    return sh[off % SUB, r0 + off - off % SUB:r0 + off - off % SUB + CCH, c0:c0 + CLN]


def _conv_mid(glu, wdw, bdw, tm, xchg=None):
    t = glu.shape[0]

    def body(prev, cur, nxt, w_ref, b_ref, out_ref, sh):
        _fill_shifted(sh, prev, cur, nxt, tm)
        for c0 in range(0, D, CLN):
            for r0 in range(0, tm, CCH):
                acc = jnp.broadcast_to(b_ref[:, c0:c0 + CLN], (CCH, CLN))
                for k in range(CW):
                    acc = acc + w_ref[k:k + 1, c0:c0 + CLN] * _shifted(sh, k + HALO - CPAD, r0, c0)
                out_ref[r0:r0 + CCH, c0:c0 + CLN] = acc

    outs, xo = _pc(body, [glu, glu, glu, wdw, bdw], name="conv_mid", grid=(t // tm,),
                   in_specs=_halo_specs(t, tm) + [_acc(32, D), _acc(1, D)], out_specs=[_rows(tm, D)],
                   out_shape=[_sds((t, D), F32)], scratch_shapes=[pltpu.VMEM((SUB, tm + 2 * HALO - SUB, D), F32)], xchg=xchg)
    return outs[0], xo


def _conv_mid_bwd(dcv, glu, wdw, tm, xchg=None):
    t = glu.shape[0]

    def body(dp, dc, dn, gp, gc, gn, w_ref, dglu_ref, dw_ref, dsh, gsh):
        @pl.when(pl.program_id(0) == 0)
        def _():
            dw_ref[...] = jnp.zeros_like(dw_ref)

        _fill_shifted(dsh, dp, dc, dn, tm)
        _fill_shifted(gsh, gp, gc, gn, tm)
        for c0 in range(0, D, CLN):
            for r0 in range(0, tm, CCH):
                acc = jnp.zeros((CCH, CLN), F32)
                for k in range(CW):
                    acc = acc + w_ref[k:k + 1, c0:c0 + CLN] * _shifted(dsh, HALO + CPAD - k, r0, c0)
                dglu_ref[r0:r0 + CCH, c0:c0 + CLN] = acc
            for k in range(CW):
                dwk = jnp.zeros((SUB, CLN), F32)
                for r0 in range(0, tm, CCH):
                    prod = _shifted(dsh, HALO, r0, c0) * _shifted(gsh, k + HALO - CPAD, r0, c0)
                    for r in range(0, CCH, SUB):
                        dwk = dwk + prod[r:r + SUB]
                dw_ref[k:k + 1, c0:c0 + CLN] += jnp.sum(dwk, axis=0, keepdims=True)

    n = tm + 2 * HALO - SUB
    return _pc(body, [dcv, dcv, dcv, glu, glu, glu, wdw], name="conv_mid_bwd", grid=(t // tm,),
               in_specs=_halo_specs(t, tm) + _halo_specs(t, tm) + [_acc(32, D)], out_specs=[_rows(tm, D), _acc(32, D)],
               out_shape=[_sds((t, D), F32), _sds((32, D), F32)],
               scratch_shapes=[pltpu.VMEM((SUB, n, D), F32), pltpu.VMEM((SUB, n, D), F32)], xchg=xchg)


def _ln(cv, lg, lb):
    mu = jnp.mean(cv, axis=-1, keepdims=True)
    cc = cv - mu
    rs = lax.rsqrt(jnp.mean(cc * cc, axis=-1, keepdims=True) + EPS)
    lh = cc * rs
    return lh, rs, lh * lg + lb


def _conv_post(cv, x, lg, lb, w2, b2, tm):
    t = x.shape[0]

    def body(cv_ref, x_ref, lg_ref, lb_ref, w_ref, b_ref, st_ref, out_ref):
        _, _, ln = _ln(cv_ref[...], lg_ref[...], lb_ref[...])
        s = (ln * _sigmoid(ln)).astype(BF16)
        st_ref[...] = s.T
        out_ref[...] = x_ref[...] + _dot(s, w_ref[...]) + b_ref[...]

    return _pc(body, [cv, x, lg, lb, w2, b2], name="conv_post", grid=(t // tm,),
               in_specs=[_rows(tm, D), _rows(tm, D), _acc(1, D), _acc(1, D), WHOLE, _acc(1, D)],
               out_specs=[_cols(tm), _rows(tm, D)], out_shape=[_sds((D, t), BF16), _sds((t, D), F32)])[0]


def _conv_post_bwd(dx, cv, lg, lb, w2, tm, xchg=None):
    t = dx.shape[0]

    def body(dx_ref, cv_ref, lg_ref, lb_ref, w_ref, dcv_ref, part_ref):
        @pl.when(pl.program_id(0) == 0)
        def _():
            part_ref[...] = jnp.zeros_like(part_ref)

        dx = dx_ref[...]
        lg = lg_ref[...]
        lh, rs, ln = _ln(cv_ref[...], lg, lb_ref[...])
        sg = _sigmoid(ln)
        dln = _dot_nt(dx, w_ref[...]) * (sg * (1.0 + ln * (1.0 - sg)))
        dlh = dln * lg
        dcv = rs * (dlh - jnp.mean(dlh, axis=-1, keepdims=True) - lh * jnp.mean(dlh * lh, axis=-1, keepdims=True))
        dcv_ref[...] = dcv
        part_ref[0:1, :] += jnp.sum(dln * lh, axis=0, keepdims=True)
        part_ref[1:2, :] += jnp.sum(dln, axis=0, keepdims=True)
        part_ref[2:3, :] += jnp.sum(dcv, axis=0, keepdims=True)
        part_ref[3:4, :] += jnp.sum(dx, axis=0, keepdims=True)

    return _pc(body, [dx, cv, lg, lb, w2], name="conv_post_bwd", grid=(t // tm,),
               in_specs=[_rows(tm, D), _rows(tm, D), _acc(1, D), _acc(1, D), WHOLE], out_specs=[_rows(tm, D), _acc(8, D)],
               out_shape=[_sds((t, D), F32), _sds((8, D), F32)], xchg=xchg)


def _conv_pre_bwd(dglu, u, x, g, w, dx_out, tm):
    t = x.shape[0]

    def body(dglu_ref, u_ref, x_ref, g_ref, w_ref, dxo_ref, du_ref, ht_ref, dx_ref, dg_ref, db_ref):
        @pl.when(pl.program_id(0) == 0)
        def _():
            dg_ref[...] = jnp.zeros_like(dg_ref)
            db_ref[...] = jnp.zeros_like(db_ref)

        gn = g_ref[...]
        xh, r, h = _rms(x_ref[...], gn)
        ht_ref[...] = h.astype(BF16).T
        dh = jnp.zeros_like(xh)
        for j in range(4):
            a = u_ref[j].astype(F32)
            sg = _sigmoid(u_ref[j + 4].astype(F32))
            dgl = dglu_ref[:, j * PWB:(j + 1) * PWB]
            da = dgl * sg
            dgt = dgl * a * sg * (1.0 - sg)
            db_ref[:, j * PWB:(j + 1) * PWB] += jnp.sum(da, axis=0, keepdims=True)
            db_ref[:, D + j * PWB:D + (j + 1) * PWB] += jnp.sum(dgt, axis=0, keepdims=True)
            da, dgt = da.astype(BF16), dgt.astype(BF16)
            du_ref[j] = da
            du_ref[j + 4] = dgt
            dh = dh + _dot_nt(da, w_ref[j]) + _dot_nt(dgt, w_ref[j + 4])
        dx, dg = _rms_bwd(dh, xh, r, gn)
        dx_ref[...] = dxo_ref[...] + dx
        dg_ref[...] += dg

    return _pc(body, [dglu, u, x, g, w, dx_out], name="conv_pre_bwd", grid=(t // tm,),
               in_specs=[_rows(tm, D), _blk3(NDEV, tm, PWB), _rows(tm, D), _acc(1, D), WHOLE, _rows(tm, D)],
               out_specs=[_blk3(NDEV, tm, PWB), _cols(tm), _rows(tm, D), _acc(1, D), _acc(1, 2 * D)],
               out_shape=[_sds((NDEV, t, PWB), BF16), _sds((D, t), BF16), _sds((t, D), F32), _sds((1, D), F32),
                          _sds((1, 2 * D), F32)])[0]


def _adamw(w, g, m, v):
    m = B1 * m + (1.0 - B1) * g
    v = B2 * v + (1.0 - B2) * (g * g)
    m_hat = m / (1.0 - B1 ** STEP)
    v_hat = v / (1.0 - B2 ** STEP)
    return -LR * (m_hat / (jnp.sqrt(v_hat) + AEPS) + WD * w), m, v


def _reduce_adamw(lands, w, m, v, tr, name, xchg=None):
    nl, r, c = w.shape

    def body(*refs):
        l_refs, (w_ref, m_ref, v_ref, g_ref, d_ref, nm_ref, nv_ref) = refs[:nl], refs[nl:]

        def total(ref):
            g = ref[0].astype(F32)
            for j in range(1, NDEV):
                g = g + ref[j].astype(F32)
            return g

        g = total(l_refs[0])
        for n in range(1, nl):
            g = jnp.where(pl.program_id(0) == n, total(l_refs[n]), g)
        g_ref[0] = g
        d_ref[0], nm_ref[0], nv_ref[0] = _adamw(w_ref[0], g, m_ref[0], v_ref[0])

    layer = pl.BlockSpec((1, tr, c), lambda l, i: (l, i, 0))
    outs, xo = _pc(body, [*lands, w, m, v], name=name, grid=(nl, r // tr),
                   in_specs=[pl.BlockSpec((NDEV, tr, c), lambda l, i: (0, i, 0))] * nl + [layer] * 3, out_specs=[layer] * 4,
                   out_shape=[_sds((nl, r, c), F32)] * 4, xchg=xchg)
    return outs if xchg is None else (outs, xo)


def _sum_parts(parts):
    _, r, c = parts.shape

    def body(p_ref, out_ref):
        s = p_ref[0]
        for j in range(1, NDEV):
            s = s + p_ref[j]
        out_ref[...] = s

    return _pc(body, [parts], name="sum_parts", in_specs=[WHOLE], out_specs=[WHOLE], out_shape=[_sds((r, c), F32)])[0][0]


def _adamw_small(w, g, m, v):
    def body(w_ref, g_ref, m_ref, v_ref, d_ref, nm_ref, nv_ref):
        d_ref[...], nm_ref[...], nv_ref[...] = _adamw(w_ref[...], g_ref[...], m_ref[...], v_ref[...])

    return _pc(body, [w, g, m, v], name="adamw_small", in_specs=[WHOLE] * 4, out_specs=[WHOLE] * 3,
               out_shape=[_sds(w.shape, F32)] * 3)[0]


def _rows128(a):
    a = a.astype(F32)
    if a.shape[-1] % 128:
        a = jnp.pad(a, [(0, 0)] * (a.ndim - 1) + [(0, 128 - a.shape[-1] % 128)])
    return a.reshape(-1, 128)


def _pack(arrs, rows):
    p = jnp.concatenate([_rows128(a) for a in arrs], axis=0)
    return jnp.pad(p, ((0, rows - p.shape[0]), (0, 0)))


SMALL = ("attn_norm", "ffn_norm", "final_norm", "attn_sink", "conv_norm", "conv_b_dw", "conv_ln_g", "conv_ln_b", "conv_b_pw2",
         "conv_b_pw1", "conv_w_dw")
SMALL_ROWS = 72


def _pack_small(d):
    return _pack([d[k] for k in SMALL], SMALL_ROWS)


def _unpack_small(p, like):
    out, r = {}, 0
    for k in SMALL:
        shp = like[k].shape
        n = -(-shp[-1] // 128) * (math.prod(shp[:-1]))
        blk = p[r:r + n]
        if shp[-1] % 128:
            blk = blk[:, :shp[-1]]
        out[k] = blk.reshape(shp)
        r += n
    return out


NAMES = ("attn_norm", "attn_w_qkv", "attn_w_o", "attn_sink", "conv_norm", "conv_w_pw1", "conv_b_pw1", "conv_w_dw", "conv_b_dw",
         "conv_ln_g", "conv_ln_b", "conv_w_pw2", "conv_b_pw2", "ffn_norm", "ffn_w_gu", "ffn_w_down", "final_norm")
TM = 256
TL = 512
TK = 2048


def _gu_t(a):
    return jnp.swapaxes(a, 1, 2)


def kernel(x, attn_norm, attn_w_qkv, attn_w_o, attn_sink, conv_norm, conv_w_pw1, conv_b_pw1, conv_w_dw, conv_b_dw, conv_ln_g, conv_ln_b, conv_w_pw2, conv_b_pw2, ffn_norm, ffn_w_gu, ffn_w_down, final_norm, loss_target, m_attn_norm, m_attn_w_qkv, m_attn_w_o, m_attn_sink, m_conv_norm, m_conv_w_pw1, m_conv_b_pw1, m_conv_w_dw, m_conv_b_dw, m_conv_ln_g, m_conv_ln_b, m_conv_w_pw2, m_conv_b_pw2, m_ffn_norm, m_ffn_w_gu, m_ffn_w_down, m_final_norm, v_attn_norm, v_attn_w_qkv, v_attn_w_o, v_attn_sink, v_conv_norm, v_conv_w_pw1, v_conv_b_pw1, v_conv_w_dw, v_conv_b_dw, v_conv_ln_g, v_conv_ln_b, v_conv_w_pw2, v_conv_b_pw2, v_ffn_norm, v_ffn_w_gu, v_ffn_w_down, v_final_norm):
    w = dict(zip(NAMES, (attn_norm, attn_w_qkv, attn_w_o, attn_sink, conv_norm, conv_w_pw1, conv_b_pw1, conv_w_dw, conv_b_dw, conv_ln_g,
                         conv_ln_b, conv_w_pw2, conv_b_pw2, ffn_norm, ffn_w_gu, ffn_w_down, final_norm)))
    m = dict(zip(NAMES, (m_attn_norm, m_attn_w_qkv, m_attn_w_o, m_attn_sink, m_conv_norm, m_conv_w_pw1, m_conv_b_pw1, m_conv_w_dw,
                         m_conv_b_dw, m_conv_ln_g, m_conv_ln_b, m_conv_w_pw2, m_conv_b_pw2, m_ffn_norm, m_ffn_w_gu, m_ffn_w_down,
                         m_final_norm)))
    v = dict(zip(NAMES, (v_attn_norm, v_attn_w_qkv, v_attn_w_o, v_attn_sink, v_conv_norm, v_conv_w_pw1, v_conv_b_pw1, v_conv_w_dw,
                         v_conv_b_dw, v_conv_ln_g, v_conv_ln_b, v_conv_w_pw2, v_conv_b_pw2, v_ffn_norm, v_ffn_w_gu, v_ffn_w_down,
                         v_final_norm)))
    me = 4 * lax.axis_index("x") + 2 * lax.axis_index("y") + lax.axis_index("c")
    for d in (w, m, v):
        d["ffn_w_gu"] = _gu_t(d["ffn_w_gu"])
    sh = {k: w[k][0].astype(BF16) for k in ("attn_w_qkv", "attn_w_o", "conv_w_pw1", "conv_w_pw2")}
    gu_b, down_b = w["ffn_w_gu"].astype(BF16), w["ffn_w_down"].astype(BF16)
    sh.update(ffn_w_gu0=gu_b[0], ffn_w_gu1=gu_b[1], ffn_w_down0=down_b[0], ffn_w_down1=down_b[1])
    x0, tgt = x[0], loss_target[0]
    t = x0.shape[0]
    tabs = _rope_tables(t)
    g_a, sink, g_f0, g_f1, g_fin = w["attn_norm"], w["attn_sink"][0], w["ffn_norm"][0:1], w["ffn_norm"][1:2], w["final_norm"][None]
    gather, scatter = False, True

    shard_rows = _pack([w["conv_w_dw"][0], jnp.zeros((1, 128), F32), w["conv_norm"], w["conv_b_dw"], w["conv_ln_g"], w["conv_ln_b"],
                        w["conv_b_pw2"], w["conv_b_pw1"]], 40)
    wqkv_g, sm = _exchange([sh["attn_w_qkv"], shard_rows], [gather] * 2, "gather_attn")
    wqkv = wqkv_g.transpose(1, 0, 2).reshape(D, QKV)

    def full_vec(row, n=1):
        return sm[:, row:row + n, :].reshape(1, NDEV * n * 128)

    w_dw, g_c, b_dw, ln_g, ln_b = sm[:, 0:32, :].transpose(1, 0, 2).reshape(32, D), full_vec(32), full_vec(33), full_vec(34), full_vec(35)
    b_pw2, b_pw1 = full_vec(36), full_vec(37, 2)

    (q_t, kv), (wo_g, wd0_g) = _attn_pre(x0, g_a, wqkv, tabs, TM, xchg=([sh["attn_w_o"], sh["ffn_w_down0"]], [gather] * 2))
    o_t, (wgu0,) = _attn_fwd(q_t, kv, sink, xchg=([sh["ffn_w_gu0"]], [gather]))
    wo, wd0 = wo_g.reshape(D, D), wd0_g.reshape(4, GUB, D)
    (x1, gu0, x2), (wpw1, wgu1) = _ffn_fwd_attn(
        o_t, x0, wo, g_f0, wgu0, wd0, TM, xchg=([sh["conv_w_pw1"], sh["ffn_w_gu1"]], [gather] * 2))
    tl = min(TL, t)
    (u, glu), (wpw2_g,) = _conv_pre(x2, g_c, wpw1, b_pw1, tl, xchg=([sh["conv_w_pw2"]], [gather]))
    wpw2 = wpw2_g.reshape(D, D)
    cv, (wd1_g,) = _conv_mid(glu, w_dw, b_dw, TM, xchg=([sh["ffn_w_down1"]], [gather]))
    wd1 = wd1_g.reshape(4, GUB, D)
    s_t, x3 = _conv_post(cv, x2, ln_g, ln_b, wpw2, b_pw2, tl)
    (gu1, dx4, fin), _ = _ffn_fwd_final(x3, g_f1, wgu1, wd1, g_fin, tgt, TM)

    land = {}
    tk = min(TK, t)
    (dgu1_t, act1_t, h3, dx3, dg_f1), _ = _ffn_bwd(dx4, x3, gu1, g_f1, wgu1, wd1, TM, "ffn_bwd1")
    dwgu1 = _wgrad(dgu1_t, h3[None], tk, "dwgu1")
    dwd1 = _wgrad(act1_t, dx4[None], tk, "dwd1").reshape(NDEV, DFF // NDEV, D)
    (dcv, cpart), (land["ffn_w_down1"],) = _conv_post_bwd(dx3, cv, ln_g, ln_b, wpw2, tl, xchg=([dwd1], [scatter]))
    dwpw2 = _wgrad(s_t[None], dx3[None], tk, "dwpw2").reshape(NDEV, D // NDEV, D)
    (dglu, dw_dw), (land["ffn_w_gu1"], land["conv_w_pw2"]) = _conv_mid_bwd(dcv, glu, w_dw, TM, xchg=([dwgu1, dwpw2], [scatter] * 2))
    du, h2_t, dx2, dg_c, db_pw1 = _conv_pre_bwd(dglu, u, x2, g_c, wpw1, dx3, tl)
    dwpw1 = _wgrad(h2_t[None], du, tk, "dwpw1")
    (dgu0_t, act0_t, h1, dx1, dg_f0, do_t), (land["conv_w_pw1"],) = _ffn_bwd(
        dx2, x1, gu0, g_f0, wgu0, wd0, TM, "ffn_bwd0", wo=wo, xchg=([dwpw1], [scatter]))
    dwgu0 = _wgrad(dgu0_t, h1[None], tk, "dwgu0")
    dwd0 = _wgrad(act0_t, dx2[None], tk, "dwd0").reshape(NDEV, DFF // NDEV, D)
    dwo = _wgrad(o_t, dx1[None], tk, "dwo", by_block=True).reshape(NDEV, D // NDEV, D)
    (dq_t, dk, dv, dsink), (land["ffn_w_gu0"], land["ffn_w_down0"]) = _attn_bwd(
        q_t, kv, do_t, sink, TM, xchg=([dwgu0, dwd0], [scatter] * 2))
    dqkv, h0_t, dx0, dg_a = _attn_pre_bwd(dq_t, dk, dv, x0, g_a, wqkv, tabs, dx1, TM)
    dwqkv, (land["attn_w_o"],) = _wgrad(h0_t[None], dqkv[None], tk, "dwqkv", xchg=([dwo], [scatter]))
    dwqkv = dwqkv[0].reshape(D, NDEV, QKV // NDEV).transpose(1, 0, 2)

    lane0 = (lax.broadcasted_iota(jnp.int32, (1, 128), 1) == 0).astype(F32)
    parts = _pack([dg_a, dg_f0, dg_f1, fin[0:1], dsink[0:1, :NH], fin[1, 0] * lane0, jnp.zeros((6, 128), F32), dg_c, cpart[2:3],
                   cpart[0:1], cpart[1:2], cpart[3:4], db_pw1, dw_dw.reshape(32, NDEV, 128)], 352)
    land["attn_w_qkv"], parts_g = _exchange([dwqkv, parts], [scatter, gather], "scatter_attn")
    red = _sum_parts(parts_g)

    def shard_rows_of(row, n=1):
        return lax.dynamic_slice_in_dim(red, row + n * me, n, axis=0)

    gs = {
        "attn_norm": red[0:8].reshape(1, D), "ffn_norm": red[8:24].reshape(2, D), "final_norm": red[24:32].reshape(D),
        "attn_sink": red[32:33, :NH], "conv_norm": shard_rows_of(40), "conv_b_dw": shard_rows_of(48), "conv_ln_g": shard_rows_of(56),
        "conv_ln_b": shard_rows_of(64), "conv_b_pw2": shard_rows_of(72), "conv_b_pw1": shard_rows_of(80, 2).reshape(1, PWB),
        "conv_w_dw": lax.dynamic_index_in_dim(red[96:352].reshape(32, NDEV, 128), me, axis=1, keepdims=False)[None, :CW],
    }
    loss = red[33, 0]

    grads, deltas, new_m, new_v = dict(gs), {}, {}, {}
    ds, ms, vs = _adamw_small(_pack_small(w), _pack_small(gs), _pack_small(m), _pack_small(v))
    deltas.update(_unpack_small(ds, gs))
    new_m.update(_unpack_small(ms, gs))
    new_v.update(_unpack_small(vs, gs))
    for k in ("attn_w_qkv", "attn_w_o", "conv_w_pw1", "conv_w_pw2", "ffn_w_gu", "ffn_w_down"):
        lands = [land[k + "0"], land[k + "1"]] if k.startswith("ffn") else [land[k]]
        tr = {1024: 256, 128: 128, 352: 176, GUB: 176}[w[k].shape[1]]
        grads[k], deltas[k], new_m[k], new_v[k] = _reduce_adamw(lands, w[k], m[k], v[k], tr, "adamw_" + k)
    for d in (grads, deltas, new_m, new_v):
        d["ffn_w_gu"] = _gu_t(d["ffn_w_gu"])
    return (loss, dx0[None], *[grads[k] for k in NAMES], *[deltas[k] for k in NAMES], *[new_m[k] for k in NAMES],
            *[new_v[k] for k in NAMES])
```

```python
import math

import jax
import jax.numpy as jnp
from jax import lax
from jax.experimental import pallas as pl
from jax.experimental.pallas import tpu as pltpu

F32 = jnp.float32
BF16 = jnp.bfloat16

D = 1024
NH = 16
NKV = 4
HD = 64
GROUP = NH // NKV
ROT = 16
THETA = 500000.0
BLK = 128
QKV = (NH + 2 * NKV) * HD
KOFF = NH * HD
VOFF = KOFF + NKV * HD
DFF = 2816
NDEV = 8
GUB = 2 * DFF // NDEV
PWB = 2 * D // NDEV
CW = 31
CPAD = 15
HALO = 16
EPS = 1e-6
NEG = -1e30
LR, B1, B2, AEPS, WD, STEP = 0.001, 0.9, 0.999, 1e-08, 0.01, 10

VMEM_LIMIT = 56 * 1024 * 1024
MESH = pl.DeviceIdType.MESH
WHOLE = pl.BlockSpec(memory_space=pltpu.VMEM)
ANY = pl.BlockSpec(memory_space=pl.ANY)


def _place():
    x, y, c = lax.axis_index("x"), lax.axis_index("y"), lax.axis_index("c")
    return x, y, c, 4 * x + 2 * y + c


def _peer(x, y, c, j):
    px = 1 - x if j & 4 else x
    py = 1 - y if j & 2 else y
    pc = 1 - c if j & 1 else c
    return (px, py, pc), 4 * px + 2 * py + pc


SIBLING = 1
OTHER_CHIPS = (2, 4, 6)
PASS_ON_LEAD = 3


class _Exchange:
    def __init__(self, src, dst, scatter, send, recv, loc):
        self.src, self.dst, self.scatter, self.send, self.recv, self.loc = src, dst, scatter, send, recv, loc
        self.x, self.y, self.c, self.me = _place()

    def _remote(self, k, j, s, d, to):
        peer, _ = _peer(self.x, self.y, self.c, to)
        return pltpu.make_async_remote_copy(src_ref=s, dst_ref=d, send_sem=self.send.at[k, j - 1], recv_sem=self.recv.at[k, j - 1],
                                            device_id=peer, device_id_type=MESH)

    def _slot(self, j):
        return _peer(self.x, self.y, self.c, j)[1]

    def local(self, k):
        if self.scatter[k]:
            return pltpu.make_async_copy(self.src[k].at[self.me], self.dst[k].at[0], self.loc.at[k])
        return pltpu.make_async_copy(self.src[k], self.dst[k].at[self.me], self.loc.at[k])

    def direct(self, k, j):
        if self.scatter[k]:
            return self._remote(k, j, self.src[k].at[self._slot(j)], self.dst[k].at[j], j)
        return self._remote(k, j, self.src[k], self.dst[k].at[self.me], j)

    def passed_on(self, k, j):
        rows = self.dst[k].at[self._slot(j)]
        return self._remote(k, j + 1, rows, rows, SIBLING)

    def arrival(self, k, j):
        rows = self.dst[k].at[j if self.scatter[k] else self._slot(j)]
        return self._remote(k, j, rows, rows, j)

    def sent(self, k):
        return tuple(range(1, NDEV)) if self.scatter[k] else (SIBLING,) + OTHER_CHIPS

    def start(self):
        for k in range(len(self.src)):
            self.local(k).start()
            for j in self.sent(k):
                self.direct(k, j).start()

    def pass_on(self):
        for k in range(len(self.src)):
            if not self.scatter[k]:
                for j in OTHER_CHIPS:
                    self.arrival(k, j).wait_recv()
                    self.passed_on(k, j).start()

    def finish(self):
        for k in range(len(self.src)):
            for j in range(1, NDEV):
                if self.scatter[k] or j not in OTHER_CHIPS:
                    self.arrival(k, j).wait_recv()
            for j in self.sent(k):
                self.direct(k, j).wait_send()
            if not self.scatter[k]:
                for j in OTHER_CHIPS:
                    self.passed_on(k, j).wait_send()
            self.local(k).wait()


def _call(body, **kw):
    return pl.pallas_call(body, **kw)


def _pc(body, operands, *, name, in_specs, out_specs, out_shape, grid=None, scratch_shapes=(), xchg=None):
    kw = dict(name=name, compiler_params=pltpu.CompilerParams(
        dimension_semantics=None if grid is None else ("arbitrary",) * len(grid), vmem_limit_bytes=VMEM_LIMIT,
        has_side_effects=xchg is not None))
    if grid is not None:
        kw["grid"] = grid
    if xchg is None:
        outs = _call(body, in_specs=list(in_specs), out_specs=list(out_specs), out_shape=list(out_shape),
                     scratch_shapes=list(scratch_shapes), **kw)(*operands)
        return list(outs), []
    arrs, scatter = xchg
    nx, n_in, n_out, n_s = len(arrs), len(in_specs), len(out_specs), len(scratch_shapes)

    def wrapped(*refs):
        ins, refs = refs[:n_in], refs[n_in:]
        src, refs = refs[:nx], refs[nx:]
        outs, refs = refs[:n_out], refs[n_out:]
        dst, refs = refs[:nx], refs[nx:]
        scr, (send, recv, loc) = refs[:n_s], refs[n_s:]
        if grid is None:
            _Exchange(src, dst, scatter, send, recv, loc).start()
            body(*ins, *outs, *scr)
            _Exchange(src, dst, scatter, send, recv, loc).pass_on()
            _Exchange(src, dst, scatter, send, recv, loc).finish()
            return

        step, steps = 0, 1
        for axis, n in enumerate(grid):
            step, steps = step * n + pl.program_id(axis), steps * n

        @pl.when(step == 0)
        def _():
            _Exchange(src, dst, scatter, send, recv, loc).start()

        body(*ins, *outs, *scr)

        @pl.when(step == max(steps - 1 - PASS_ON_LEAD, 0))
        def _():
            _Exchange(src, dst, scatter, send, recv, loc).pass_on()

        @pl.when(step == steps - 1)
        def _():
            _Exchange(src, dst, scatter, send, recv, loc).finish()

    xshape = [jax.ShapeDtypeStruct(a.shape if sc else (NDEV,) + a.shape, a.dtype) for a, sc in zip(arrs, scatter)]
    sems = [pltpu.SemaphoreType.DMA((nx, NDEV - 1)), pltpu.SemaphoreType.DMA((nx, NDEV - 1)), pltpu.SemaphoreType.DMA((nx,))]
    res = _call(wrapped, in_specs=list(in_specs) + [ANY] * nx, out_specs=list(out_specs) + [ANY] * nx,
                out_shape=list(out_shape) + xshape, scratch_shapes=list(scratch_shapes) + sems, **kw)(*operands, *arrs)
    return list(res[:n_out]), list(res[n_out:])


def _exchange(arrs, scatter, name):
    return _pc(lambda: None, [], name=name, in_specs=[], out_specs=[], out_shape=[], xchg=(arrs, scatter))[1]


def _rows(tm, width):
    return pl.BlockSpec((tm, width), lambda i: (i, 0))


def _cols(tm):
    return pl.BlockSpec((D, tm), lambda i: (0, i))


def _qblocks(tm):
    return pl.BlockSpec((tm // BLK, D, BLK), lambda i: (i, 0, 0))


def _put_blocks(ref, val):
    for b in range(ref.shape[0]):
        ref[b] = val[:, b * BLK:(b + 1) * BLK]


def _get_blocks(ref):
    return jnp.concatenate([ref[b] for b in range(ref.shape[0])], axis=1)


def _blk3(nb, tm, width):
    return pl.BlockSpec((nb, tm, width), lambda i: (0, i, 0))


def _acc(rows, width):
    return pl.BlockSpec((rows, width), lambda i: (0, 0))


def _sds(shape, dtype):
    return jax.ShapeDtypeStruct(shape, dtype)


def _dot(a, b):
    return jnp.dot(a.astype(BF16), b.astype(BF16), preferred_element_type=F32)


def _dot_nt(a, b):
    return lax.dot_general(a.astype(BF16), b.astype(BF16), (((1,), (1,)), ((), ())), preferred_element_type=F32)


def _dot_tn(a, b):
    return lax.dot_general(a.astype(BF16), b.astype(BF16), (((0,), (0,)), ((), ())), preferred_element_type=F32)


def _sigmoid(x):
    return 1.0 / (1.0 + jnp.exp(-x))


def _rms(x, g):
    r = lax.rsqrt(jnp.mean(x * x, axis=-1, keepdims=True) + EPS)
    xh = x * r
    return xh, r, xh * g


def _rms_bwd(dh, xh, r, g):
    dxh = dh * g
    dg = jnp.sum(dh * xh, axis=0, keepdims=True)
    dx = r * (dxh - xh * jnp.mean(dxh * xh, axis=-1, keepdims=True))
    return dx, dg


def _lanes(t, width):
    return jnp.tile(t, (1, width // t.shape[1]))


def _rope(z, c, sa, sb):
    w = z.shape[1]
    return z * _lanes(c, w) + pltpu.roll(z, w - 8, 1) * _lanes(sa, w) + pltpu.roll(z, 8, 1) * _lanes(sb, w)


def _rope_t(dz, c, sa, sb):
    w = dz.shape[1]
    return dz * _lanes(c, w) + pltpu.roll(dz * _lanes(sa, w), 8, 1) + pltpu.roll(dz * _lanes(sb, w), w - 8, 1)


def _rope_tables(t):
    pos = jnp.arange(t, dtype=F32)
    inv_freq = THETA ** (-jnp.arange(0, ROT, 2, dtype=F32) / ROT)
    ang = pos[:, None] * inv_freq[None, :]
    cos, sin = jnp.cos(ang), jnp.sin(ang)
    one = jnp.ones((t, HD - ROT), F32)
    zero = jnp.zeros((t, HD - ROT), F32)
    z8 = jnp.zeros((t, 8), F32)
    c = jnp.concatenate([cos, cos, one], axis=1)
    sa = jnp.concatenate([-sin, z8, zero], axis=1)
    sb = jnp.concatenate([z8, sin, zero], axis=1)
    return tuple(jnp.tile(a, (1, 2)) for a in (c, sa, sb))


KVW = NKV * HD
GW = GROUP * BLK


def _attn_pre(x, g, wqkv, tabs, tm, xchg=None):
    t = x.shape[0]

    def body(x_ref, g_ref, w_ref, c_ref, sa_ref, sb_ref, qt_ref, kv_ref):
        _, _, h = _rms(x_ref[...], g_ref[...])
        z = _dot(h, w_ref[...])
        c, sa, sb = c_ref[...], sa_ref[...], sb_ref[...]
        _put_blocks(qt_ref, (_rope(z[:, :KOFF], c, sa, sb) * 0.125).T.astype(BF16))
        kv_ref[:, :KVW] = _rope(z[:, KOFF:VOFF], c, sa, sb).astype(BF16)
        kv_ref[:, KVW:] = z[:, VOFF:].astype(BF16)

    return _pc(body, [x, g, wqkv, *tabs], name="attn_pre", grid=(t // tm,),
               in_specs=[_rows(tm, D), _acc(1, D), WHOLE, _rows(tm, 128), _rows(tm, 128), _rows(tm, 128)],
               out_specs=[_qblocks(tm), _rows(tm, 2 * KVW)],
               out_shape=[_sds((t // BLK, D, BLK), BF16), _sds((t, 2 * KVW), BF16)], xchg=xchg)


QB = 2


def _attn_specs(nblk):
    prev = lambda i: jnp.maximum(QB * i - 1, 0)
    nxt = lambda i: jnp.minimum(QB * (i + 1), nblk - 1)
    return [
        _qblocks(QB * BLK),
        pl.BlockSpec((BLK, KVW), lambda i: (prev(i), 0)),
        pl.BlockSpec((QB * BLK, KVW), lambda i: (i, 0)),
        pl.BlockSpec((BLK, KVW), lambda i: (nxt(i), 0)),
        pl.BlockSpec((BLK, KVW), lambda i: (prev(i), 1)),
        pl.BlockSpec((QB * BLK, KVW), lambda i: (i, 1)),
        pl.BlockSpec((BLK, KVW), lambda i: (nxt(i), 1)),
    ]


def _attn_bias():
    c = lax.broadcasted_iota(jnp.int32, (3, 3 * BLK, GW), 1)
    r = lax.broadcasted_iota(jnp.int32, (3, 3 * BLK, GW), 2) & (BLK - 1)
    slab = lax.broadcasted_iota(jnp.int32, (3, 3 * BLK, GW), 0)
    valid = (c >= r) & (c - 2 * BLK <= r) & ((slab != 0) | (c >= BLK)) & ((slab != 2) | (c < 2 * BLK))
    return jnp.where(valid, 0.0, NEG).astype(F32)


def _bias_of(bias_ref, blk, nblk):
    return bias_ref[jnp.where(blk == 0, 0, jnp.where(blk == nblk - 1, 2, 1))]


def _group(ref, b, kv):
    return jnp.concatenate([ref[b, (kv * GROUP + g) * HD:(kv * GROUP + g + 1) * HD, :] for g in range(GROUP)], axis=1)


def _group_sink(sink_ref, kv):
    return jnp.concatenate([jnp.full((1, BLK), sink_ref[kv * GROUP + g], F32) for g in range(GROUP)], axis=1)


def _attn_exp(k_h, qt_g, bias, sink_g):
    s = _dot(k_h, qt_g) + bias
    m = jnp.maximum(jnp.max(s, axis=0, keepdims=True), sink_g)
    e = jnp.exp(s - m)
    es = jnp.exp(sink_g - m)
    return e, 1.0 / (jnp.sum(e, axis=0, keepdims=True) + es), es


def _attn_fwd(qt, kv, sink, xchg=None):
    t = kv.shape[0]
    nblk = t // BLK

    def body(sink_ref, bias_ref, qt_ref, kp, kc, kn, vp, vc, vn, ot_ref):
        k = jnp.concatenate([kp[...], kc[...], kn[...]], axis=0)
        v = jnp.concatenate([vp[...], vc[...], vn[...]], axis=0)
        for b in range(QB):
            bias = _bias_of(bias_ref, QB * pl.program_id(0) + b, nblk)
            keys = slice(b * BLK, (b + 3) * BLK)
            for h in range(NKV):
                e, inv, _ = _attn_exp(k[keys, h * HD:(h + 1) * HD], _group(qt_ref, b, h), bias, _group_sink(sink_ref, h))
                ot_g = (_dot_tn(v[keys, h * HD:(h + 1) * HD], e) * inv).astype(BF16)
                for g in range(GROUP):
                    ot_ref[b, (h * GROUP + g) * HD:(h * GROUP + g + 1) * HD, :] = ot_g[:, g * BLK:(g + 1) * BLK]

    outs, xo = _pc(body, [sink, _attn_bias(), qt] + [kv] * 6, name="attn_fwd", grid=(nblk // QB,),
                   in_specs=[pl.BlockSpec(memory_space=pltpu.SMEM), WHOLE] + _attn_specs(nblk),
                   out_specs=[_qblocks(QB * BLK)], out_shape=[_sds((nblk, D, BLK), BF16)], xchg=xchg)
    return outs[0], xo


def _attn_bwd(qt, kv, dot, sink, pad, xchg=None):
    t = kv.shape[0]
    nblk = t // BLK

    def body(sink_ref, bias_ref, qt_ref, kp, kc, kn, vp, vc, vn, dot_ref, dqt_ref, dk_ref, dv_ref, ds_ref):
        i = pl.program_id(0)

        @pl.when(i == 0)
        def _():
            dk_ref[...] = jnp.zeros_like(dk_ref)
            dv_ref[...] = jnp.zeros_like(dv_ref)
            ds_ref[...] = jnp.zeros_like(ds_ref)

        k = jnp.concatenate([kp[...], kc[...], kn[...]], axis=0)
        v = jnp.concatenate([vp[...], vc[...], vn[...]], axis=0)
        lane = lax.broadcasted_iota(jnp.int32, (1, 128), 1)
        dsink = jnp.zeros((1, 128), F32)
        for b in range(QB):
            blk = QB * i + b
            bias = _bias_of(bias_ref, blk, nblk)
            keys = slice(b * BLK, (b + 3) * BLK)
            rows = pl.ds(pl.multiple_of(blk * BLK + (pad - BLK), BLK), 3 * BLK)
            for h in range(NKV):
                k_h, v_h = k[keys, h * HD:(h + 1) * HD], v[keys, h * HD:(h + 1) * HD]
                qt_g, dot_g = _group(qt_ref, b, h), _group(dot_ref, b, h)
                e, inv, es = _attn_exp(k_h, qt_g, bias, _group_sink(sink_ref, h))
                p, ps = e * inv, es * inv
                dp = _dot(v_h, dot_g)
                delta = jnp.sum(p * dp, axis=0, keepdims=True)
                ds = (p * (dp - delta)).astype(BF16)
                dqt_g = _dot_tn(k_h, ds)
                dk_ref[rows, h * HD:(h + 1) * HD] += _dot_nt(ds, qt_g)
                dv_ref[rows, h * HD:(h + 1) * HD] += _dot_nt(p, dot_g)
                psd = ps * delta
                for g in range(GROUP):
                    n = h * GROUP + g
                    dqt_ref[b, n * HD:(n + 1) * HD, :] = dqt_g[:, g * BLK:(g + 1) * BLK]
                    dsink = dsink - jnp.where(lane == n, jnp.sum(psd[:, g * BLK:(g + 1) * BLK], axis=1, keepdims=True), 0.0)
        ds_ref[0:1, :] += dsink

    outs, xo = _pc(body, [sink, _attn_bias(), qt] + [kv] * 6 + [dot], name="attn_bwd", grid=(nblk // QB,),
                   in_specs=[pl.BlockSpec(memory_space=pltpu.SMEM), WHOLE] + _attn_specs(nblk) + [_qblocks(QB * BLK)],
                   out_specs=[_qblocks(QB * BLK), _acc(t + 2 * pad, KVW), _acc(t + 2 * pad, KVW), _acc(8, 128)],
                   out_shape=[_sds((nblk, D, BLK), F32), _sds((t + 2 * pad, KVW), F32), _sds((t + 2 * pad, KVW), F32),
                              _sds((8, 128), F32)], xchg=xchg)
    return outs, xo


def _attn_pre_bwd(dqt, dk, dv, x, g, wqkv, tabs, dx_out, tm):
    t = x.shape[0]

    def body(dqt_ref, dk_ref, dv_ref, x_ref, g_ref, w_ref, c_ref, sa_ref, sb_ref, dxo_ref, dqkv_ref, ht_ref, dx_ref, dg_ref):
        @pl.when(pl.program_id(0) == 0)
        def _():
            dg_ref[...] = jnp.zeros_like(dg_ref)

        c, sa, sb = c_ref[...], sa_ref[...], sb_ref[...]
        dqkv_ref[:, :KOFF] = _rope_t(_get_blocks(dqt_ref).T * 0.125, c, sa, sb).astype(BF16)
        dqkv_ref[:, KOFF:VOFF] = _rope_t(dk_ref[...], c, sa, sb).astype(BF16)
        dqkv_ref[:, VOFF:] = dv_ref[...].astype(BF16)
        g = g_ref[...]
        xh, r, h = _rms(x_ref[...], g)
        ht_ref[...] = h.astype(BF16).T
        dh = _dot_nt(dqkv_ref[...], w_ref[...])
        dx, dg = _rms_bwd(dh, xh, r, g)
        dx_ref[...] = dxo_ref[...] + dx
        dg_ref[...] += dg

    return _pc(body, [dqt, dk, dv, x, g, wqkv, *tabs, dx_out], name="attn_pre_bwd", grid=(t // tm,),
               in_specs=[_qblocks(tm), pl.BlockSpec((tm, KVW), lambda i: (i + 1, 0)), pl.BlockSpec((tm, KVW), lambda i: (i + 1, 0)),
                         _rows(tm, D), _acc(1, D), WHOLE, _rows(tm, 128), _rows(tm, 128), _rows(tm, 128), _rows(tm, D)],
               out_specs=[_rows(tm, QKV), _cols(tm), _rows(tm, D), _acc(1, D)],
               out_shape=[_sds((t, QKV), BF16), _sds((D, t), BF16), _sds((t, D), F32), _sds((1, D), F32)])[0]


def _ffn(x, g, wgu_ref, wd_ref, gu_ref):
    _, _, h = _rms(x, g)
    hb = h.astype(BF16)
    y = x
    for j in range(4):
        gj = _dot_nt(hb, wgu_ref[j])
        uj = _dot_nt(hb, wgu_ref[j + 4])
        gu_ref[j] = gj.astype(BF16)
        gu_ref[j + 4] = uj.astype(BF16)
        y = y + _dot(gj * _sigmoid(gj) * uj, wd_ref[j])
    return y


def _ffn_fwd_attn(ot, x, wo, g, wgu, wd, tm, xchg=None):
    t = x.shape[0]

    def body(ot_ref, x_ref, wo_ref, g_ref, wgu_ref, wd_ref, x1_ref, gu_ref, out_ref):
        x1 = x_ref[...] + _dot_tn(_get_blocks(ot_ref), wo_ref[...])
        x1_ref[...] = x1
        out_ref[...] = _ffn(x1, g_ref[...], wgu_ref, wd_ref, gu_ref)

    return _pc(body, [ot, x, wo, g, wgu, wd], name="ffn_fwd0", grid=(t // tm,),
               in_specs=[_qblocks(tm), _rows(tm, D), WHOLE, _acc(1, D), WHOLE, WHOLE],
               out_specs=[_rows(tm, D), _blk3(NDEV, tm, GUB), _rows(tm, D)],
               out_shape=[_sds((t, D), F32), _sds((NDEV, t, GUB), BF16), _sds((t, D), F32)], xchg=xchg)


def _ffn_fwd_final(x, g, wgu, wd, g_fin, tgt, tm, xchg=None):
    t = x.shape[0]

    def body(x_ref, g_ref, wgu_ref, wd_ref, gf_ref, t_ref, gu_ref, dx_ref, part_ref):
        @pl.when(pl.program_id(0) == 0)
        def _():
            part_ref[...] = jnp.zeros_like(part_ref)

        gf = gf_ref[...]
        xh, r, y = _rms(_ffn(x_ref[...], g_ref[...], wgu_ref, wd_ref, gu_ref), gf)
        err = y - t_ref[...]
        dx, dg = _rms_bwd(err * (1.0 / D), xh, r, gf)
        dx_ref[...] = dx
        part_ref[0:1, :] += dg
        tok = jnp.sum(err * err, axis=-1, keepdims=True) * (1.0 / D)
        lane = lax.broadcasted_iota(jnp.int32, (1, D), 1)
        part_ref[1:2, :] += jnp.where(lane == 0, 0.5 * jnp.sum(tok, axis=0, keepdims=True), 0.0)

    return _pc(body, [x, g, wgu, wd, g_fin, tgt], name="ffn_fwd1", grid=(t // tm,),
               in_specs=[_rows(tm, D), _acc(1, D), WHOLE, WHOLE, _acc(1, D), _rows(tm, D)],
               out_specs=[_blk3(NDEV, tm, GUB), _rows(tm, D), _acc(8, D)],
               out_shape=[_sds((NDEV, t, GUB), BF16), _sds((t, D), F32), _sds((8, D), F32)], xchg=xchg)


def _ffn_bwd(dy, x, gu, g, wgu, wd, tm, name, wo=None, xchg=None):
    t = x.shape[0]

    def body(dy_ref, x_ref, gu_ref, g_ref, wgu_ref, wd_ref, *rest):
        wo_ref = rest[0] if wo is not None else None
        dgut_ref, actt_ref, h_ref, dx_ref, dg_ref = rest[wo is not None:][:5]

        @pl.when(pl.program_id(0) == 0)
        def _():
            dg_ref[...] = jnp.zeros_like(dg_ref)

        dy = dy_ref[...]
        dyb = dy.astype(BF16)
        gn = g_ref[...]
        xh, r, h = _rms(x_ref[...], gn)
        h_ref[...] = h.astype(BF16)
        dh = jnp.zeros_like(dy)
        for j in range(4):
            gj = gu_ref[j].astype(F32)
            uj = gu_ref[j + 4].astype(F32)
            sg = _sigmoid(gj)
            silu = gj * sg
            actt_ref[j] = (silu * uj).astype(BF16).T
            dact = _dot_nt(dyb, wd_ref[j])
            dgj = (dact * uj * (sg * (1.0 + gj * (1.0 - sg)))).astype(BF16)
            duj = (dact * silu).astype(BF16)
            dgut_ref[j] = dgj.T
            dgut_ref[j + 4] = duj.T
            dh = dh + _dot(dgj, wgu_ref[j]) + _dot(duj, wgu_ref[j + 4])
        dx, dg = _rms_bwd(dh, xh, r, gn)
        dx = dy + dx
        dx_ref[...] = dx
        dg_ref[...] += dg
        if wo is not None:
            _put_blocks(rest[6], _dot(wo_ref[...], dx.astype(BF16).T).astype(BF16))

    extra = wo is not None
    tcols = lambda nb: pl.BlockSpec((nb, GUB, tm), lambda i: (0, 0, i))
    return _pc(body, [dy, x, gu, g, wgu, wd] + [wo] * extra, name=name, grid=(t // tm,),
               in_specs=[_rows(tm, D), _rows(tm, D), _blk3(NDEV, tm, GUB), _acc(1, D), WHOLE, WHOLE] + [WHOLE] * extra,
               out_specs=[tcols(NDEV), tcols(4), _rows(tm, D), _rows(tm, D), _acc(1, D)] + [_qblocks(tm)] * extra,
               out_shape=[_sds((NDEV, GUB, t), BF16), _sds((4, GUB, t), BF16), _sds((t, D), BF16), _sds((t, D), F32),
                          _sds((1, D), F32)] + [_sds((t // BLK, D, BLK), BF16)] * extra, xchg=xchg)


def _wgrad(at, b, tk, name, xchg=None, by_block=False):
    if by_block:
        na, (_, ma, _), t = 1, at.shape, at.shape[0] * BLK
        a_spec = pl.BlockSpec((tk // BLK, ma, BLK), lambda j, k: (k, 0, 0))
    else:
        na, ma, t = at.shape
        a_spec = pl.BlockSpec((1, ma, tk), lambda j, k: (j if na > 1 else 0, 0, k))
    nb, _, mb = b.shape
    nk = t // tk

    def body(a_ref, b_ref, out_ref, acc):
        k = pl.program_id(1)

        @pl.when(k == 0)
        def _():
            acc[...] = jnp.zeros_like(acc)

        acc[...] += _dot(_get_blocks(a_ref) if by_block else a_ref[0], b_ref[0])

        @pl.when(k == nk - 1)
        def _():
            out_ref[0] = acc[...].astype(BF16)

    outs, xo = _pc(body, [at, b], name=name, grid=(max(na, nb), nk),
                   in_specs=[a_spec, pl.BlockSpec((1, tk, mb), lambda j, k: (j if nb > 1 else 0, k, 0))],
                   out_specs=[pl.BlockSpec((1, ma, mb), lambda j, k: (j, 0, 0))], out_shape=[_sds((max(na, nb), ma, mb), BF16)],
                   scratch_shapes=[pltpu.VMEM((ma, mb), F32)], xchg=xchg)
    return outs[0] if xchg is None else (outs[0], xo)


def _conv_pre(x, g, w, b, tm, xchg=None):
    t = x.shape[0]

    def body(x_ref, g_ref, w_ref, b_ref, u_ref, glu_ref):
        _, _, h = _rms(x_ref[...], g_ref[...])
        hb = h.astype(BF16)
        for j in range(4):
            a = _dot(hb, w_ref[j]) + b_ref[:, j * PWB:(j + 1) * PWB]
            gt = _dot(hb, w_ref[j + 4]) + b_ref[:, D + j * PWB:D + (j + 1) * PWB]
            u_ref[j] = a.astype(BF16)
            u_ref[j + 4] = gt.astype(BF16)
            glu_ref[:, j * PWB:(j + 1) * PWB] = a * _sigmoid(gt)

    return _pc(body, [x, g, w, b], name="conv_pre", grid=(t // tm,),
               in_specs=[_rows(tm, D), _acc(1, D), WHOLE, _acc(1, 2 * D)], out_specs=[_blk3(NDEV, tm, PWB), _rows(tm, D)],
               out_shape=[_sds((NDEV, t, PWB), BF16), _sds((t, D), F32)], xchg=xchg)


def _halo_specs(t, tm):
    per = tm // HALO
    last = t // HALO - 1
    return [
        pl.BlockSpec((HALO, D), lambda i: (jnp.maximum(i * per - 1, 0), 0)),
        _rows(tm, D),
        pl.BlockSpec((HALO, D), lambda i: (jnp.minimum((i + 1) * per, last), 0)),
    ]


SUB = 8
CCH = 32
CLN = 256


def _fill_shifted(sh, prev, cur, nxt, tm):
    i = pl.program_id(0)
    rows = jnp.concatenate([jnp.where(i == 0, 0.0, prev[...]), cur[...], jnp.where(i == pl.num_programs(0) - 1, 0.0, nxt[...])], axis=0)
    n = tm + 2 * HALO - SUB
    for b in range(SUB):
        sh[b] = rows[b:b + n]


def _shifted(sh, off, r0, c0):
    return sh[off % SUB, r0 + off - off % SUB:r0 + off - off % SUB + CCH, c0:c0 + CLN]


def _conv_mid(glu, wdw, bdw, tm, xchg=None):
    t = glu.shape[0]

    def body(prev, cur, nxt, w_ref, b_ref, out_ref, sh):
        _fill_shifted(sh, prev, cur, nxt, tm)
        for c0 in range(0, D, CLN):
            for r0 in range(0, tm, CCH):
                acc = jnp.broadcast_to(b_ref[:, c0:c0 + CLN], (CCH, CLN))
                for k in range(CW):
                    acc = acc + w_ref[k:k + 1, c0:c0 + CLN] * _shifted(sh, k + HALO - CPAD, r0, c0)
                out_ref[r0:r0 + CCH, c0:c0 + CLN] = acc

    outs, xo = _pc(body, [glu, glu, glu, wdw, bdw], name="conv_mid", grid=(t // tm,),
                   in_specs=_halo_specs(t, tm) + [_acc(32, D), _acc(1, D)], out_specs=[_rows(tm, D)],
                   out_shape=[_sds((t, D), F32)], scratch_shapes=[pltpu.VMEM((SUB, tm + 2 * HALO - SUB, D), F32)], xchg=xchg)
    return outs[0], xo


def _conv_mid_bwd(dcv, glu, wdw, tm, xchg=None):
    t = glu.shape[0]

    def body(dp, dc, dn, gp, gc, gn, w_ref, dglu_ref, dw_ref, dsh, gsh):
        @pl.when(pl.program_id(0) == 0)
        def _():
            dw_ref[...] = jnp.zeros_like(dw_ref)

        _fill_shifted(dsh, dp, dc, dn, tm)
        _fill_shifted(gsh, gp, gc, gn, tm)
        for c0 in range(0, D, CLN):
            for r0 in range(0, tm, CCH):
                acc = jnp.zeros((CCH, CLN), F32)
                for k in range(CW):
                    acc = acc + w_ref[k:k + 1, c0:c0 + CLN] * _shifted(dsh, HALO + CPAD - k, r0, c0)
                dglu_ref[r0:r0 + CCH, c0:c0 + CLN] = acc
            for k in range(CW):
                dwk = jnp.zeros((SUB, CLN), F32)
                for r0 in range(0, tm, CCH):
                    prod = _shifted(dsh, HALO, r0, c0) * _shifted(gsh, k + HALO - CPAD, r0, c0)
                    for r in range(0, CCH, SUB):
                        dwk = dwk + prod[r:r + SUB]
                dw_ref[k:k + 1, c0:c0 + CLN] += jnp.sum(dwk, axis=0, keepdims=True)

    n = tm + 2 * HALO - SUB
    return _pc(body, [dcv, dcv, dcv, glu, glu, glu, wdw], name="conv_mid_bwd", grid=(t // tm,),
               in_specs=_halo_specs(t, tm) + _halo_specs(t, tm) + [_acc(32, D)], out_specs=[_rows(tm, D), _acc(32, D)],
               out_shape=[_sds((t, D), F32), _sds((32, D), F32)],
               scratch_shapes=[pltpu.VMEM((SUB, n, D), F32), pltpu.VMEM((SUB, n, D), F32)], xchg=xchg)


def _ln(cv, lg, lb):
    mu = jnp.mean(cv, axis=-1, keepdims=True)
    cc = cv - mu
    rs = lax.rsqrt(jnp.mean(cc * cc, axis=-1, keepdims=True) + EPS)
    lh = cc * rs
    return lh, rs, lh * lg + lb


def _conv_post(cv, x, lg, lb, w2, b2, tm):
    t = x.shape[0]

    def body(cv_ref, x_ref, lg_ref, lb_ref, w_ref, b_ref, st_ref, out_ref):
        _, _, ln = _ln(cv_ref[...], lg_ref[...], lb_ref[...])
        s = (ln * _sigmoid(ln)).astype(BF16)
        st_ref[...] = s.T
        out_ref[...] = x_ref[...] + _dot(s, w_ref[...]) + b_ref[...]

    return _pc(body, [cv, x, lg, lb, w2, b2], name="conv_post", grid=(t // tm,),
               in_specs=[_rows(tm, D), _rows(tm, D), _acc(1, D), _acc(1, D), WHOLE, _acc(1, D)],
               out_specs=[_cols(tm), _rows(tm, D)], out_shape=[_sds((D, t), BF16), _sds((t, D), F32)])[0]


def _conv_post_bwd(dx, cv, lg, lb, w2, tm, xchg=None):
    t = dx.shape[0]

    def body(dx_ref, cv_ref, lg_ref, lb_ref, w_ref, dcv_ref, part_ref):
        @pl.when(pl.program_id(0) == 0)
        def _():
            part_ref[...] = jnp.zeros_like(part_ref)

        dx = dx_ref[...]
        lg = lg_ref[...]
        lh, rs, ln = _ln(cv_ref[...], lg, lb_ref[...])
        sg = _sigmoid(ln)
        dln = _dot_nt(dx, w_ref[...]) * (sg * (1.0 + ln * (1.0 - sg)))
        dlh = dln * lg
        dcv = rs * (dlh - jnp.mean(dlh, axis=-1, keepdims=True) - lh * jnp.mean(dlh * lh, axis=-1, keepdims=True))
        dcv_ref[...] = dcv
        part_ref[0:1, :] += jnp.sum(dln * lh, axis=0, keepdims=True)
        part_ref[1:2, :] += jnp.sum(dln, axis=0, keepdims=True)
        part_ref[2:3, :] += jnp.sum(dcv, axis=0, keepdims=True)
        part_ref[3:4, :] += jnp.sum(dx, axis=0, keepdims=True)

    return _pc(body, [dx, cv, lg, lb, w2], name="conv_post_bwd", grid=(t // tm,),
               in_specs=[_rows(tm, D), _rows(tm, D), _acc(1, D), _acc(1, D), WHOLE], out_specs=[_rows(tm, D), _acc(8, D)],
               out_shape=[_sds((t, D), F32), _sds((8, D), F32)], xchg=xchg)


def _conv_pre_bwd(dglu, u, x, g, w, dx_out, tm):
    t = x.shape[0]

    def body(dglu_ref, u_ref, x_ref, g_ref, w_ref, dxo_ref, du_ref, ht_ref, dx_ref, dg_ref, db_ref):
        @pl.when(pl.program_id(0) == 0)
        def _():
            dg_ref[...] = jnp.zeros_like(dg_ref)
            db_ref[...] = jnp.zeros_like(db_ref)

        gn = g_ref[...]
        xh, r, h = _rms(x_ref[...], gn)
        ht_ref[...] = h.astype(BF16).T
        dh = jnp.zeros_like(xh)
        for j in range(4):
            a = u_ref[j].astype(F32)
            sg = _sigmoid(u_ref[j + 4].astype(F32))
            dgl = dglu_ref[:, j * PWB:(j + 1) * PWB]
            da = dgl * sg
            dgt = dgl * a * sg * (1.0 - sg)
            db_ref[:, j * PWB:(j + 1) * PWB] += jnp.sum(da, axis=0, keepdims=True)
            db_ref[:, D + j * PWB:D + (j + 1) * PWB] += jnp.sum(dgt, axis=0, keepdims=True)
            da, dgt = da.astype(BF16), dgt.astype(BF16)
            du_ref[j] = da
            du_ref[j + 4] = dgt
            dh = dh + _dot_nt(da, w_ref[j]) + _dot_nt(dgt, w_ref[j + 4])
        dx, dg = _rms_bwd(dh, xh, r, gn)
        dx_ref[...] = dxo_ref[...] + dx
        dg_ref[...] += dg

    return _pc(body, [dglu, u, x, g, w, dx_out], name="conv_pre_bwd", grid=(t // tm,),
               in_specs=[_rows(tm, D), _blk3(NDEV, tm, PWB), _rows(tm, D), _acc(1, D), WHOLE, _rows(tm, D)],
               out_specs=[_blk3(NDEV, tm, PWB), _cols(tm), _rows(tm, D), _acc(1, D), _acc(1, 2 * D)],
               out_shape=[_sds((NDEV, t, PWB), BF16), _sds((D, t), BF16), _sds((t, D), F32), _sds((1, D), F32),
                          _sds((1, 2 * D), F32)])[0]


def _adamw(w, g, m, v):
    m = B1 * m + (1.0 - B1) * g
    v = B2 * v + (1.0 - B2) * (g * g)
    m_hat = m / (1.0 - B1 ** STEP)
    v_hat = v / (1.0 - B2 ** STEP)
    return -LR * (m_hat / (jnp.sqrt(v_hat) + AEPS) + WD * w), m, v


def _reduce_adamw(lands, w, m, v, tr, name, xchg=None):
    nl, r, c = w.shape

    def body(*refs):
        l_refs, (w_ref, m_ref, v_ref, g_ref, d_ref, nm_ref, nv_ref) = refs[:nl], refs[nl:]

        def total(ref):
            g = ref[0].astype(F32)
            for j in range(1, NDEV):
                g = g + ref[j].astype(F32)
            return g

        g = total(l_refs[0])
        for n in range(1, nl):
            g = jnp.where(pl.program_id(0) == n, total(l_refs[n]), g)
        g_ref[0] = g
        d_ref[0], nm_ref[0], nv_ref[0] = _adamw(w_ref[0], g, m_ref[0], v_ref[0])

    layer = pl.BlockSpec((1, tr, c), lambda l, i: (l, i, 0))
    outs, xo = _pc(body, [*lands, w, m, v], name=name, grid=(nl, r // tr),
                   in_specs=[pl.BlockSpec((NDEV, tr, c), lambda l, i: (0, i, 0))] * nl + [layer] * 3, out_specs=[layer] * 4,
                   out_shape=[_sds((nl, r, c), F32)] * 4, xchg=xchg)
    return outs if xchg is None else (outs, xo)


def _sum_parts(parts):
    _, r, c = parts.shape

    def body(p_ref, out_ref):
        s = p_ref[0]
        for j in range(1, NDEV):
            s = s + p_ref[j]
        out_ref[...] = s

    return _pc(body, [parts], name="sum_parts", in_specs=[WHOLE], out_specs=[WHOLE], out_shape=[_sds((r, c), F32)])[0][0]


def _adamw_small(w, g, m, v):
    def body(w_ref, g_ref, m_ref, v_ref, d_ref, nm_ref, nv_ref):
        d_ref[...], nm_ref[...], nv_ref[...] = _adamw(w_ref[...], g_ref[...], m_ref[...], v_ref[...])

    return _pc(body, [w, g, m, v], name="adamw_small", in_specs=[WHOLE] * 4, out_specs=[WHOLE] * 3,
               out_shape=[_sds(w.shape, F32)] * 3)[0]


def _rows128(a):
    a = a.astype(F32)
    if a.shape[-1] % 128:
        a = jnp.pad(a, [(0, 0)] * (a.ndim - 1) + [(0, 128 - a.shape[-1] % 128)])
    return a.reshape(-1, 128)


def _pack(arrs, rows):
    p = jnp.concatenate([_rows128(a) for a in arrs], axis=0)
    return jnp.pad(p, ((0, rows - p.shape[0]), (0, 0)))


SMALL = ("attn_norm", "ffn_norm", "final_norm", "attn_sink", "conv_norm", "conv_b_dw", "conv_ln_g", "conv_ln_b", "conv_b_pw2",
         "conv_b_pw1", "conv_w_dw")
SMALL_ROWS = 72


def _pack_small(d):
    return _pack([d[k] for k in SMALL], SMALL_ROWS)


def _unpack_small(p, like):
    out, r = {}, 0
    for k in SMALL:
        shp = like[k].shape
        n = -(-shp[-1] // 128) * (math.prod(shp[:-1]))
        blk = p[r:r + n]
        if shp[-1] % 128:
            blk = blk[:, :shp[-1]]
        out[k] = blk.reshape(shp)
        r += n
    return out


NAMES = ("attn_norm", "attn_w_qkv", "attn_w_o", "attn_sink", "conv_norm", "conv_w_pw1", "conv_b_pw1", "conv_w_dw", "conv_b_dw",
         "conv_ln_g", "conv_ln_b", "conv_w_pw2", "conv_b_pw2", "ffn_norm", "ffn_w_gu", "ffn_w_down", "final_norm")
TM = 256
TL = 512
TK = 2048


def _gu_t(a):
    return jnp.swapaxes(a, 1, 2)


def kernel(x, attn_norm, attn_w_qkv, attn_w_o, attn_sink, conv_norm, conv_w_pw1, conv_b_pw1, conv_w_dw, conv_b_dw, conv_ln_g, conv_ln_b, conv_w_pw2, conv_b_pw2, ffn_norm, ffn_w_gu, ffn_w_down, final_norm, loss_target, m_attn_norm, m_attn_w_qkv, m_attn_w_o, m_attn_sink, m_conv_norm, m_conv_w_pw1, m_conv_b_pw1, m_conv_w_dw, m_conv_b_dw, m_conv_ln_g, m_conv_ln_b, m_conv_w_pw2, m_conv_b_pw2, m_ffn_norm, m_ffn_w_gu, m_ffn_w_down, m_final_norm, v_attn_norm, v_attn_w_qkv, v_attn_w_o, v_attn_sink, v_conv_norm, v_conv_w_pw1, v_conv_b_pw1, v_conv_w_dw, v_conv_b_dw, v_conv_ln_g, v_conv_ln_b, v_conv_w_pw2, v_conv_b_pw2, v_ffn_norm, v_ffn_w_gu, v_ffn_w_down, v_final_norm):
    w = dict(zip(NAMES, (attn_norm, attn_w_qkv, attn_w_o, attn_sink, conv_norm, conv_w_pw1, conv_b_pw1, conv_w_dw, conv_b_dw, conv_ln_g,
                         conv_ln_b, conv_w_pw2, conv_b_pw2, ffn_norm, ffn_w_gu, ffn_w_down, final_norm)))
    m = dict(zip(NAMES, (m_attn_norm, m_attn_w_qkv, m_attn_w_o, m_attn_sink, m_conv_norm, m_conv_w_pw1, m_conv_b_pw1, m_conv_w_dw,
                         m_conv_b_dw, m_conv_ln_g, m_conv_ln_b, m_conv_w_pw2, m_conv_b_pw2, m_ffn_norm, m_ffn_w_gu, m_ffn_w_down,
                         m_final_norm)))
    v = dict(zip(NAMES, (v_attn_norm, v_attn_w_qkv, v_attn_w_o, v_attn_sink, v_conv_norm, v_conv_w_pw1, v_conv_b_pw1, v_conv_w_dw,
                         v_conv_b_dw, v_conv_ln_g, v_conv_ln_b, v_conv_w_pw2, v_conv_b_pw2, v_ffn_norm, v_ffn_w_gu, v_ffn_w_down,
                         v_final_norm)))
    me = 4 * lax.axis_index("x") + 2 * lax.axis_index("y") + lax.axis_index("c")
    for d in (w, m, v):
        d["ffn_w_gu"] = _gu_t(d["ffn_w_gu"])
    sh = {k: w[k][0].astype(BF16) for k in ("attn_w_qkv", "attn_w_o", "conv_w_pw1", "conv_w_pw2")}
    gu_b, down_b = w["ffn_w_gu"].astype(BF16), w["ffn_w_down"].astype(BF16)
    sh.update(ffn_w_gu0=gu_b[0], ffn_w_gu1=gu_b[1], ffn_w_down0=down_b[0], ffn_w_down1=down_b[1])
    x0, tgt = x[0], loss_target[0]
    t = x0.shape[0]
    tabs = _rope_tables(t)
    g_a, sink, g_f0, g_f1, g_fin = w["attn_norm"], w["attn_sink"][0], w["ffn_norm"][0:1], w["ffn_norm"][1:2], w["final_norm"][None]
    gather, scatter = False, True

    shard_rows = _pack([w["conv_w_dw"][0], jnp.zeros((1, 128), F32), w["conv_norm"], w["conv_b_dw"], w["conv_ln_g"], w["conv_ln_b"],
                        w["conv_b_pw2"], w["conv_b_pw1"]], 40)
    wqkv_g, sm = _exchange([sh["attn_w_qkv"], shard_rows], [gather] * 2, "gather_attn")
    wqkv = wqkv_g.transpose(1, 0, 2).reshape(D, QKV)

    def full_vec(row, n=1):
        return sm[:, row:row + n, :].reshape(1, NDEV * n * 128)

    w_dw, g_c, b_dw, ln_g, ln_b = sm[:, 0:32, :].transpose(1, 0, 2).reshape(32, D), full_vec(32), full_vec(33), full_vec(34), full_vec(35)
    b_pw2, b_pw1 = full_vec(36), full_vec(37, 2)

    (q_t, kv), (wo_g, wd0_g) = _attn_pre(x0, g_a, wqkv, tabs, TM, xchg=([sh["attn_w_o"], sh["ffn_w_down0"]], [gather] * 2))
    o_t, (wgu0,) = _attn_fwd(q_t, kv, sink, xchg=([sh["ffn_w_gu0"]], [gather]))
    wo, wd0 = wo_g.reshape(D, D), wd0_g.reshape(4, GUB, D)
    tl = min(TL, t)
    (x1, gu0, x2), (wpw1, wgu1) = _ffn_fwd_attn(
        o_t, x0, wo, g_f0, wgu0, wd0, tl, xchg=([sh["conv_w_pw1"], sh["ffn_w_gu1"]], [gather] * 2))
    (u, glu), (wpw2_g,) = _conv_pre(x2, g_c, wpw1, b_pw1, tl, xchg=([sh["conv_w_pw2"]], [gather]))
    wpw2 = wpw2_g.reshape(D, D)
    cv, (wd1_g,) = _conv_mid(glu, w_dw, b_dw, TM, xchg=([sh["ffn_w_down1"]], [gather]))
    wd1 = wd1_g.reshape(4, GUB, D)
    s_t, x3 = _conv_post(cv, x2, ln_g, ln_b, wpw2, b_pw2, tl)
    (gu1, dx4, fin), _ = _ffn_fwd_final(x3, g_f1, wgu1, wd1, g_fin, tgt, tl)

    land = {}
    tk = min(TK, t)
    (dgu1_t, act1_t, h3, dx3, dg_f1), _ = _ffn_bwd(dx4, x3, gu1, g_f1, wgu1, wd1, TM, "ffn_bwd1")
    dwgu1 = _wgrad(dgu1_t, h3[None], tk, "dwgu1")
    dwd1 = _wgrad(act1_t, dx4[None], tk, "dwd1").reshape(NDEV, DFF // NDEV, D)
    (dcv, cpart), (land["ffn_w_down1"],) = _conv_post_bwd(dx3, cv, ln_g, ln_b, wpw2, tl, xchg=([dwd1], [scatter]))
    dwpw2 = _wgrad(s_t[None], dx3[None], tk, "dwpw2").reshape(NDEV, D // NDEV, D)
    (dglu, dw_dw), (land["ffn_w_gu1"], land["conv_w_pw2"]) = _conv_mid_bwd(dcv, glu, w_dw, TM, xchg=([dwgu1, dwpw2], [scatter] * 2))
    du, h2_t, dx2, dg_c, db_pw1 = _conv_pre_bwd(dglu, u, x2, g_c, wpw1, dx3, tl)
    dwpw1 = _wgrad(h2_t[None], du, tk, "dwpw1")
    (dgu0_t, act0_t, h1, dx1, dg_f0, do_t), (land["conv_w_pw1"],) = _ffn_bwd(
        dx2, x1, gu0, g_f0, wgu0, wd0, TM, "ffn_bwd0", wo=wo, xchg=([dwpw1], [scatter]))
    dwgu0 = _wgrad(dgu0_t, h1[None], tk, "dwgu0")
    dwd0 = _wgrad(act0_t, dx2[None], tk, "dwd0").reshape(NDEV, DFF // NDEV, D)
    dwo = _wgrad(o_t, dx1[None], tk, "dwo", by_block=True).reshape(NDEV, D // NDEV, D)
    (dq_t, dk, dv, dsink), (land["ffn_w_gu0"], land["ffn_w_down0"]) = _attn_bwd(
        q_t, kv, do_t, sink, TM, xchg=([dwgu0, dwd0], [scatter] * 2))
    dqkv, h0_t, dx0, dg_a = _attn_pre_bwd(dq_t, dk, dv, x0, g_a, wqkv, tabs, dx1, TM)
    dwqkv, (land["attn_w_o"],) = _wgrad(h0_t[None], dqkv[None], tk, "dwqkv", xchg=([dwo], [scatter]))
    dwqkv = dwqkv[0].reshape(D, NDEV, QKV // NDEV).transpose(1, 0, 2)

    lane0 = (lax.broadcasted_iota(jnp.int32, (1, 128), 1) == 0).astype(F32)
    parts = _pack([dg_a, dg_f0, dg_f1, fin[0:1], dsink[0:1, :NH], fin[1, 0] * lane0, jnp.zeros((6, 128), F32), dg_c, cpart[2:3],
                   cpart[0:1], cpart[1:2], cpart[3:4], db_pw1, dw_dw.reshape(32, NDEV, 128)], 352)
    land["attn_w_qkv"], parts_g = _exchange([dwqkv, parts], [scatter, gather], "scatter_attn")
    red = _sum_parts(parts_g)

    def shard_rows_of(row, n=1):
        return lax.dynamic_slice_in_dim(red, row + n * me, n, axis=0)

    gs = {
        "attn_norm": red[0:8].reshape(1, D), "ffn_norm": red[8:24].reshape(2, D), "final_norm": red[24:32].reshape(D),
        "attn_sink": red[32:33, :NH], "conv_norm": shard_rows_of(40), "conv_b_dw": shard_rows_of(48), "conv_ln_g": shard_rows_of(56),
        "conv_ln_b": shard_rows_of(64), "conv_b_pw2": shard_rows_of(72), "conv_b_pw1": shard_rows_of(80, 2).reshape(1, PWB),
        "conv_w_dw": lax.dynamic_index_in_dim(red[96:352].reshape(32, NDEV, 128), me, axis=1, keepdims=False)[None, :CW],
    }
    loss = red[33, 0]

    grads, deltas, new_m, new_v = dict(gs), {}, {}, {}
    ds, ms, vs = _adamw_small(_pack_small(w), _pack_small(gs), _pack_small(m), _pack_small(v))
    deltas.update(_unpack_small(ds, gs))
    new_m.update(_unpack_small(ms, gs))
    new_v.update(_unpack_small(vs, gs))
    for k in ("attn_w_qkv", "attn_w_o", "conv_w_pw1", "conv_w_pw2", "ffn_w_gu", "ffn_w_down"):
        lands = [land[k + "0"], land[k + "1"]] if k.startswith("ffn") else [land[k]]
        tr = {1024: 256, 128: 128, 352: 176, GUB: 176}[w[k].shape[1]]
        grads[k], deltas[k], new_m[k], new_v[k] = _reduce_adamw(lands, w[k], m[k], v[k], tr, "adamw_" + k)
    for d in (grads, deltas, new_m, new_v):
        d["ffn_w_gu"] = _gu_t(d["ffn_w_gu"])
    return (loss, dx0[None], *[grads[k] for k in NAMES], *[deltas[k] for k in NAMES], *[new_m[k] for k in NAMES],
            *[new_v[k] for k in NAMES])
```

```python
import math

import jax
import jax.numpy as jnp
from jax import lax
from jax.experimental import pallas as pl
from jax.experimental.pallas import tpu as pltpu

F32 = jnp.float32
BF16 = jnp.bfloat16

D = 1024
NH = 16
NKV = 4
HD = 64
GROUP = NH // NKV
ROT = 16
THETA = 500000.0
BLK = 128
QKV = (NH + 2 * NKV) * HD
KOFF = NH * HD
VOFF = KOFF + NKV * HD
DFF = 2816
NDEV = 8
GUB = 2 * DFF // NDEV
PWB = 2 * D // NDEV
CW = 31
CPAD = 15
HALO = 16
EPS = 1e-6
NEG = -1e30
LR, B1, B2, AEPS, WD, STEP = 0.001, 0.9, 0.999, 1e-08, 0.01, 10

VMEM_LIMIT = 56 * 1024 * 1024
MESH = pl.DeviceIdType.MESH
WHOLE = pl.BlockSpec(memory_space=pltpu.VMEM)
ANY = pl.BlockSpec(memory_space=pl.ANY)


def _place():
    x, y, c = lax.axis_index("x"), lax.axis_index("y"), lax.axis_index("c")
    return x, y, c, 4 * x + 2 * y + c


def _peer(x, y, c, j):
    px = 1 - x if j & 4 else x
    py = 1 - y if j & 2 else y
    pc = 1 - c if j & 1 else c
    return (px, py, pc), 4 * px + 2 * py + pc


SIBLING = 1
OTHER_CHIPS = (2, 4, 6)
PASS_ON_LEAD = 3


class _Exchange:
    def __init__(self, src, dst, scatter, send, recv, loc):
        self.src, self.dst, self.scatter, self.send, self.recv, self.loc = src, dst, scatter, send, recv, loc
        self.x, self.y, self.c, self.me = _place()

    def _remote(self, k, j, s, d, to):
        peer, _ = _peer(self.x, self.y, self.c, to)
        return pltpu.make_async_remote_copy(src_ref=s, dst_ref=d, send_sem=self.send.at[k, j - 1], recv_sem=self.recv.at[k, j - 1],
                                            device_id=peer, device_id_type=MESH)

    def _slot(self, j):
        return _peer(self.x, self.y, self.c, j)[1]

    def local(self, k):
        if self.scatter[k]:
            return pltpu.make_async_copy(self.src[k].at[self.me], self.dst[k].at[0], self.loc.at[k])
        return pltpu.make_async_copy(self.src[k], self.dst[k].at[self.me], self.loc.at[k])

    def direct(self, k, j):
        if self.scatter[k]:
            return self._remote(k, j, self.src[k].at[self._slot(j)], self.dst[k].at[j], j)
        return self._remote(k, j, self.src[k], self.dst[k].at[self.me], j)

    def passed_on(self, k, j):
        rows = self.dst[k].at[self._slot(j)]
        return self._remote(k, j + 1, rows, rows, SIBLING)

    def arrival(self, k, j):
        rows = self.dst[k].at[j if self.scatter[k] else self._slot(j)]
        return self._remote(k, j, rows, rows, j)

    def sent(self, k):
        return tuple(range(1, NDEV)) if self.scatter[k] else (SIBLING,) + OTHER_CHIPS

    def start(self):
        for k in range(len(self.src)):
            self.local(k).start()
            for j in self.sent(k):
                self.direct(k, j).start()

    def pass_on(self):
        for k in range(len(self.src)):
            if not self.scatter[k]:
                for j in OTHER_CHIPS:
                    self.arrival(k, j).wait_recv()
                    self.passed_on(k, j).start()

    def finish(self):
        for k in range(len(self.src)):
            for j in range(1, NDEV):
                if self.scatter[k] or j not in OTHER_CHIPS:
                    self.arrival(k, j).wait_recv()
            for j in self.sent(k):
                self.direct(k, j).wait_send()
            if not self.scatter[k]:
                for j in OTHER_CHIPS:
                    self.passed_on(k, j).wait_send()
            self.local(k).wait()


def _call(body, **kw):
    return pl.pallas_call(body, **kw)


def _pc(body, operands, *, name, in_specs, out_specs, out_shape, grid=None, scratch_shapes=(), xchg=None):
    kw = dict(name=name, compiler_params=pltpu.CompilerParams(
        dimension_semantics=None if grid is None else ("arbitrary",) * len(grid), vmem_limit_bytes=VMEM_LIMIT,
        has_side_effects=xchg is not None))
    if grid is not None:
        kw["grid"] = grid
    if xchg is None:
        outs = _call(body, in_specs=list(in_specs), out_specs=list(out_specs), out_shape=list(out_shape),
                     scratch_shapes=list(scratch_shapes), **kw)(*operands)
        return list(outs), []
    arrs, scatter = xchg
    nx, n_in, n_out, n_s = len(arrs), len(in_specs), len(out_specs), len(scratch_shapes)

    def wrapped(*refs):
        ins, refs = refs[:n_in], refs[n_in:]
        src, refs = refs[:nx], refs[nx:]
        outs, refs = refs[:n_out], refs[n_out:]
        dst, refs = refs[:nx], refs[nx:]
        scr, (send, recv, loc) = refs[:n_s], refs[n_s:]
        if grid is None:
            _Exchange(src, dst, scatter, send, recv, loc).start()
            body(*ins, *outs, *scr)
            _Exchange(src, dst, scatter, send, recv, loc).pass_on()
            _Exchange(src, dst, scatter, send, recv, loc).finish()
            return

        step, steps = 0, 1
        for axis, n in enumerate(grid):
            step, steps = step * n + pl.program_id(axis), steps * n

        @pl.when(step == 0)
        def _():
            _Exchange(src, dst, scatter, send, recv, loc).start()

        body(*ins, *outs, *scr)

        @pl.when(step == max(steps - 1 - PASS_ON_LEAD, 0))
        def _():
            _Exchange(src, dst, scatter, send, recv, loc).pass_on()

        @pl.when(step == steps - 1)
        def _():
            _Exchange(src, dst, scatter, send, recv, loc).finish()

    xshape = [jax.ShapeDtypeStruct(a.shape if sc else (NDEV,) + a.shape, a.dtype) for a, sc in zip(arrs, scatter)]
    sems = [pltpu.SemaphoreType.DMA((nx, NDEV - 1)), pltpu.SemaphoreType.DMA((nx, NDEV - 1)), pltpu.SemaphoreType.DMA((nx,))]
    res = _call(wrapped, in_specs=list(in_specs) + [ANY] * nx, out_specs=list(out_specs) + [ANY] * nx,
                out_shape=list(out_shape) + xshape, scratch_shapes=list(scratch_shapes) + sems, **kw)(*operands, *arrs)
    return list(res[:n_out]), list(res[n_out:])


def _exchange(arrs, scatter, name):
    return _pc(lambda: None, [], name=name, in_specs=[], out_specs=[], out_shape=[], xchg=(arrs, scatter))[1]


def _rows(tm, width):
    return pl.BlockSpec((tm, width), lambda i: (i, 0))


def _cols(tm):
    return pl.BlockSpec((D, tm), lambda i: (0, i))


def _qblocks(tm):
    return pl.BlockSpec((tm // BLK, D, BLK), lambda i: (i, 0, 0))


def _put_blocks(ref, val):
    for b in range(ref.shape[0]):
        ref[b] = val[:, b * BLK:(b + 1) * BLK]


def _get_blocks(ref):
    return jnp.concatenate([ref[b] for b in range(ref.shape[0])], axis=1)


def _blk3(nb, tm, width):
    return pl.BlockSpec((nb, tm, width), lambda i: (0, i, 0))


def _acc(rows, width):
    return pl.BlockSpec((rows, width), lambda i: (0, 0))


def _sds(shape, dtype):
    return jax.ShapeDtypeStruct(shape, dtype)


def _dot(a, b):
    return jnp.dot(a.astype(BF16), b.astype(BF16), preferred_element_type=F32)


def _dot_nt(a, b):
    return lax.dot_general(a.astype(BF16), b.astype(BF16), (((1,), (1,)), ((), ())), preferred_element_type=F32)


def _dot_tn(a, b):
    return lax.dot_general(a.astype(BF16), b.astype(BF16), (((0,), (0,)), ((), ())), preferred_element_type=F32)


def _sigmoid(x):
    return 1.0 / (1.0 + jnp.exp(-x))


def _rms(x, g):
    r = lax.rsqrt(jnp.mean(x * x, axis=-1, keepdims=True) + EPS)
    xh = x * r
    return xh, r, xh * g


def _rms_bwd(dh, xh, r, g):
    dxh = dh * g
    dg = jnp.sum(dh * xh, axis=0, keepdims=True)
    dx = r * (dxh - xh * jnp.mean(dxh * xh, axis=-1, keepdims=True))
    return dx, dg


def _lanes(t, width):
    return jnp.tile(t, (1, width // t.shape[1]))


def _rope(z, c, sa, sb):
    w = z.shape[1]
    return z * _lanes(c, w) + pltpu.roll(z, w - 8, 1) * _lanes(sa, w) + pltpu.roll(z, 8, 1) * _lanes(sb, w)


def _rope_t(dz, c, sa, sb):
    w = dz.shape[1]
    return dz * _lanes(c, w) + pltpu.roll(dz * _lanes(sa, w), 8, 1) + pltpu.roll(dz * _lanes(sb, w), w - 8, 1)


def _rope_tables(t):
    pos = jnp.arange(t, dtype=F32)
    inv_freq = THETA ** (-jnp.arange(0, ROT, 2, dtype=F32) / ROT)
    ang = pos[:, None] * inv_freq[None, :]
    cos, sin = jnp.cos(ang), jnp.sin(ang)
    one = jnp.ones((t, HD - ROT), F32)
    zero = jnp.zeros((t, HD - ROT), F32)
    z8 = jnp.zeros((t, 8), F32)
    c = jnp.concatenate([cos, cos, one], axis=1)
    sa = jnp.concatenate([-sin, z8, zero], axis=1)
    sb = jnp.concatenate([z8, sin, zero], axis=1)
    return tuple(jnp.tile(a, (1, 2)) for a in (c, sa, sb))


KVW = NKV * HD
GW = GROUP * BLK


def _attn_pre(x, g, wqkv, tabs, tm, xchg=None):
    t = x.shape[0]

    def body(x_ref, g_ref, w_ref, c_ref, sa_ref, sb_ref, qt_ref, kv_ref):
        _, _, h = _rms(x_ref[...], g_ref[...])
        z = _dot(h, w_ref[...])
        c, sa, sb = c_ref[...], sa_ref[...], sb_ref[...]
        _put_blocks(qt_ref, (_rope(z[:, :KOFF], c, sa, sb) * 0.125).T.astype(BF16))
        kv_ref[:, :KVW] = _rope(z[:, KOFF:VOFF], c, sa, sb).astype(BF16)
        kv_ref[:, KVW:] = z[:, VOFF:].astype(BF16)

    return _pc(body, [x, g, wqkv, *tabs], name="attn_pre", grid=(t // tm,),
               in_specs=[_rows(tm, D), _acc(1, D), WHOLE, _rows(tm, 128), _rows(tm, 128), _rows(tm, 128)],
               out_specs=[_qblocks(tm), _rows(tm, 2 * KVW)],
               out_shape=[_sds((t // BLK, D, BLK), BF16), _sds((t, 2 * KVW), BF16)], xchg=xchg)


QB = 4


def _attn_specs(nblk):
    prev = lambda i: jnp.maximum(QB * i - 1, 0)
    nxt = lambda i: jnp.minimum(QB * (i + 1), nblk - 1)
    return [
        _qblocks(QB * BLK),
        pl.BlockSpec((BLK, KVW), lambda i: (prev(i), 0)),
        pl.BlockSpec((QB * BLK, KVW), lambda i: (i, 0)),
        pl.BlockSpec((BLK, KVW), lambda i: (nxt(i), 0)),
        pl.BlockSpec((BLK, KVW), lambda i: (prev(i), 1)),
        pl.BlockSpec((QB * BLK, KVW), lambda i: (i, 1)),
        pl.BlockSpec((BLK, KVW), lambda i: (nxt(i), 1)),
    ]


def _attn_bias():
    c = lax.broadcasted_iota(jnp.int32, (3, 3 * BLK, GW), 1)
    r = lax.broadcasted_iota(jnp.int32, (3, 3 * BLK, GW), 2) & (BLK - 1)
    slab = lax.broadcasted_iota(jnp.int32, (3, 3 * BLK, GW), 0)
    valid = (c >= r) & (c - 2 * BLK <= r) & ((slab != 0) | (c >= BLK)) & ((slab != 2) | (c < 2 * BLK))
    return jnp.where(valid, 0.0, NEG).astype(F32)


def _bias_of(bias_ref, blk, nblk):
    return bias_ref[jnp.where(blk == 0, 0, jnp.where(blk == nblk - 1, 2, 1))]


def _group(ref, b, kv):
    return jnp.concatenate([ref[b, (kv * GROUP + g) * HD:(kv * GROUP + g + 1) * HD, :] for g in range(GROUP)], axis=1)


def _group_sink(sink_ref, kv):
    return jnp.concatenate([jnp.full((1, BLK), sink_ref[kv * GROUP + g], F32) for g in range(GROUP)], axis=1)


def _attn_exp(k_h, qt_g, bias, sink_g):
    s = _dot(k_h, qt_g) + bias
    m = jnp.maximum(jnp.max(s, axis=0, keepdims=True), sink_g)
    e = jnp.exp(s - m)
    es = jnp.exp(sink_g - m)
    return e, 1.0 / (jnp.sum(e, axis=0, keepdims=True) + es), es


def _attn_fwd(qt, kv, sink, xchg=None):
    t = kv.shape[0]
    nblk = t // BLK

    def body(sink_ref, bias_ref, qt_ref, kp, kc, kn, vp, vc, vn, ot_ref):
        k = jnp.concatenate([kp[...], kc[...], kn[...]], axis=0)
        v = jnp.concatenate([vp[...], vc[...], vn[...]], axis=0)
        for b in range(QB):
            bias = _bias_of(bias_ref, QB * pl.program_id(0) + b, nblk)
            keys = slice(b * BLK, (b + 3) * BLK)
            for h in range(NKV):
                e, inv, _ = _attn_exp(k[keys, h * HD:(h + 1) * HD], _group(qt_ref, b, h), bias, _group_sink(sink_ref, h))
                ot_g = (_dot_tn(v[keys, h * HD:(h + 1) * HD], e) * inv).astype(BF16)
                for g in range(GROUP):
                    ot_ref[b, (h * GROUP + g) * HD:(h * GROUP + g + 1) * HD, :] = ot_g[:, g * BLK:(g + 1) * BLK]

    outs, xo = _pc(body, [sink, _attn_bias(), qt] + [kv] * 6, name="attn_fwd", grid=(nblk // QB,),
                   in_specs=[pl.BlockSpec(memory_space=pltpu.SMEM), WHOLE] + _attn_specs(nblk),
                   out_specs=[_qblocks(QB * BLK)], out_shape=[_sds((nblk, D, BLK), BF16)], xchg=xchg)
    return outs[0], xo


def _attn_bwd(qt, kv, dot, sink, pad, xchg=None):
    t = kv.shape[0]
    nblk = t // BLK

    def body(sink_ref, bias_ref, qt_ref, kp, kc, kn, vp, vc, vn, dot_ref, dqt_ref, dk_ref, dv_ref, ds_ref):
        i = pl.program_id(0)

        @pl.when(i == 0)
        def _():
            dk_ref[...] = jnp.zeros_like(dk_ref)
            dv_ref[...] = jnp.zeros_like(dv_ref)
            ds_ref[...] = jnp.zeros_like(ds_ref)

        k = jnp.concatenate([kp[...], kc[...], kn[...]], axis=0)
        v = jnp.concatenate([vp[...], vc[...], vn[...]], axis=0)
        lane = lax.broadcasted_iota(jnp.int32, (1, 128), 1)
        dsink = jnp.zeros((1, 128), F32)
        for b in range(QB):
            blk = QB * i + b
            bias = _bias_of(bias_ref, blk, nblk)
            keys = slice(b * BLK, (b + 3) * BLK)
            rows = pl.ds(pl.multiple_of(blk * BLK + (pad - BLK), BLK), 3 * BLK)
            for h in range(NKV):
                k_h, v_h = k[keys, h * HD:(h + 1) * HD], v[keys, h * HD:(h + 1) * HD]
                qt_g, dot_g = _group(qt_ref, b, h), _group(dot_ref, b, h)
                e, inv, es = _attn_exp(k_h, qt_g, bias, _group_sink(sink_ref, h))
                p, ps = e * inv, es * inv
                dp = _dot(v_h, dot_g)
                delta = jnp.sum(p * dp, axis=0, keepdims=True)
                ds = (p * (dp - delta)).astype(BF16)
                dqt_g = _dot_tn(k_h, ds)
                dk_ref[rows, h * HD:(h + 1) * HD] += _dot_nt(ds, qt_g)
                dv_ref[rows, h * HD:(h + 1) * HD] += _dot_nt(p, dot_g)
                psd = ps * delta
                for g in range(GROUP):
                    n = h * GROUP + g
                    dqt_ref[b, n * HD:(n + 1) * HD, :] = dqt_g[:, g * BLK:(g + 1) * BLK]
                    dsink = dsink - jnp.where(lane == n, jnp.sum(psd[:, g * BLK:(g + 1) * BLK], axis=1, keepdims=True), 0.0)
        ds_ref[0:1, :] += dsink

    outs, xo = _pc(body, [sink, _attn_bias(), qt] + [kv] * 6 + [dot], name="attn_bwd", grid=(nblk // QB,),
                   in_specs=[pl.BlockSpec(memory_space=pltpu.SMEM), WHOLE] + _attn_specs(nblk) + [_qblocks(QB * BLK)],
                   out_specs=[_qblocks(QB * BLK), _acc(t + 2 * pad, KVW), _acc(t + 2 * pad, KVW), _acc(8, 128)],
                   out_shape=[_sds((nblk, D, BLK), F32), _sds((t + 2 * pad, KVW), F32), _sds((t + 2 * pad, KVW), F32),
                              _sds((8, 128), F32)], xchg=xchg)
    return outs, xo


def _attn_pre_bwd(dqt, dk, dv, x, g, wqkv, tabs, dx_out, tm):
    t = x.shape[0]

    def body(dqt_ref, dk_ref, dv_ref, x_ref, g_ref, w_ref, c_ref, sa_ref, sb_ref, dxo_ref, dqkv_ref, ht_ref, dx_ref, dg_ref):
        @pl.when(pl.program_id(0) == 0)
        def _():
            dg_ref[...] = jnp.zeros_like(dg_ref)

        c, sa, sb = c_ref[...], sa_ref[...], sb_ref[...]
        dqkv_ref[:, :KOFF] = _rope_t(_get_blocks(dqt_ref).T * 0.125, c, sa, sb).astype(BF16)
        dqkv_ref[:, KOFF:VOFF] = _rope_t(dk_ref[...], c, sa, sb).astype(BF16)
        dqkv_ref[:, VOFF:] = dv_ref[...].astype(BF16)
        g = g_ref[...]
        xh, r, h = _rms(x_ref[...], g)
        ht_ref[...] = h.astype(BF16).T
        dh = _dot_nt(dqkv_ref[...], w_ref[...])
        dx, dg = _rms_bwd(dh, xh, r, g)
        dx_ref[...] = dxo_ref[...] + dx
        dg_ref[...] += dg

    return _pc(body, [dqt, dk, dv, x, g, wqkv, *tabs, dx_out], name="attn_pre_bwd", grid=(t // tm,),
               in_specs=[_qblocks(tm), pl.BlockSpec((tm, KVW), lambda i: (i + 1, 0)), pl.BlockSpec((tm, KVW), lambda i: (i + 1, 0)),
                         _rows(tm, D), _acc(1, D), WHOLE, _rows(tm, 128), _rows(tm, 128), _rows(tm, 128), _rows(tm, D)],
               out_specs=[_rows(tm, QKV), _cols(tm), _rows(tm, D), _acc(1, D)],
               out_shape=[_sds((t, QKV), BF16), _sds((D, t), BF16), _sds((t, D), F32), _sds((1, D), F32)])[0]


def _ffn(x, g, wgu_ref, wd_ref, gu_ref):
    _, _, h = _rms(x, g)
    hb = h.astype(BF16)
    y = x
    for j in range(4):
        gj = _dot_nt(hb, wgu_ref[j])
        uj = _dot_nt(hb, wgu_ref[j + 4])
        gu_ref[j] = gj.astype(BF16)
        gu_ref[j + 4] = uj.astype(BF16)
        y = y + _dot(gj * _sigmoid(gj) * uj, wd_ref[j])
    return y


def _ffn_fwd_attn(ot, x, wo, g, wgu, wd, tm, xchg=None):
    t = x.shape[0]

    def body(ot_ref, x_ref, wo_ref, g_ref, wgu_ref, wd_ref, x1_ref, gu_ref, out_ref):
        x1 = x_ref[...] + _dot_tn(_get_blocks(ot_ref), wo_ref[...])
        x1_ref[...] = x1
        out_ref[...] = _ffn(x1, g_ref[...], wgu_ref, wd_ref, gu_ref)

    return _pc(body, [ot, x, wo, g, wgu, wd], name="ffn_fwd0", grid=(t // tm,),
               in_specs=[_qblocks(tm), _rows(tm, D), WHOLE, _acc(1, D), WHOLE, WHOLE],
               out_specs=[_rows(tm, D), _blk3(NDEV, tm, GUB), _rows(tm, D)],
               out_shape=[_sds((t, D), F32), _sds((NDEV, t, GUB), BF16), _sds((t, D), F32)], xchg=xchg)


def _ffn_fwd_final(x, g, wgu, wd, g_fin, tgt, tm, xchg=None):
    t = x.shape[0]

    def body(x_ref, g_ref, wgu_ref, wd_ref, gf_ref, t_ref, gu_ref, dx_ref, part_ref):
        @pl.when(pl.program_id(0) == 0)
        def _():
            part_ref[...] = jnp.zeros_like(part_ref)

        gf = gf_ref[...]
        xh, r, y = _rms(_ffn(x_ref[...], g_ref[...], wgu_ref, wd_ref, gu_ref), gf)
        err = y - t_ref[...]
        dx, dg = _rms_bwd(err * (1.0 / D), xh, r, gf)
        dx_ref[...] = dx
        part_ref[0:1, :] += dg
        tok = jnp.sum(err * err, axis=-1, keepdims=True) * (1.0 / D)
        lane = lax.broadcasted_iota(jnp.int32, (1, D), 1)
        part_ref[1:2, :] += jnp.where(lane == 0, 0.5 * jnp.sum(tok, axis=0, keepdims=True), 0.0)

    return _pc(body, [x, g, wgu, wd, g_fin, tgt], name="ffn_fwd1", grid=(t // tm,),
               in_specs=[_rows(tm, D), _acc(1, D), WHOLE, WHOLE, _acc(1, D), _rows(tm, D)],
               out_specs=[_blk3(NDEV, tm, GUB), _rows(tm, D), _acc(8, D)],
               out_shape=[_sds((NDEV, t, GUB), BF16), _sds((t, D), F32), _sds((8, D), F32)], xchg=xchg)


def _ffn_bwd(dy, x, gu, g, wgu, wd, tm, name, wo=None, xchg=None):
    t = x.shape[0]

    def body(dy_ref, x_ref, gu_ref, g_ref, wgu_ref, wd_ref, *rest):
        wo_ref = rest[0] if wo is not None else None
        dgut_ref, actt_ref, h_ref, dx_ref, dg_ref = rest[wo is not None:][:5]

        @pl.when(pl.program_id(0) == 0)
        def _():
            dg_ref[...] = jnp.zeros_like(dg_ref)

        dy = dy_ref[...]
        dyb = dy.astype(BF16)
        gn = g_ref[...]
        xh, r, h = _rms(x_ref[...], gn)
        h_ref[...] = h.astype(BF16)
        dh = jnp.zeros_like(dy)
        for j in range(4):
            gj = gu_ref[j].astype(F32)
            uj = gu_ref[j + 4].astype(F32)
            sg = _sigmoid(gj)
            silu = gj * sg
            actt_ref[j] = (silu * uj).astype(BF16).T
            dact = _dot_nt(dyb, wd_ref[j])
            dgj = (dact * uj * (sg * (1.0 + gj * (1.0 - sg)))).astype(BF16)
            duj = (dact * silu).astype(BF16)
            dgut_ref[j] = dgj.T
            dgut_ref[j + 4] = duj.T
            dh = dh + _dot(dgj, wgu_ref[j]) + _dot(duj, wgu_ref[j + 4])
        dx, dg = _rms_bwd(dh, xh, r, gn)
        dx = dy + dx
        dx_ref[...] = dx
        dg_ref[...] += dg
        if wo is not None:
            _put_blocks(rest[6], _dot(wo_ref[...], dx.astype(BF16).T).astype(BF16))

    extra = wo is not None
    tcols = lambda nb: pl.BlockSpec((nb, GUB, tm), lambda i: (0, 0, i))
    return _pc(body, [dy, x, gu, g, wgu, wd] + [wo] * extra, name=name, grid=(t // tm,),
               in_specs=[_rows(tm, D), _rows(tm, D), _blk3(NDEV, tm, GUB), _acc(1, D), WHOLE, WHOLE] + [WHOLE] * extra,
               out_specs=[tcols(NDEV), tcols(4), _rows(tm, D), _rows(tm, D), _acc(1, D)] + [_qblocks(tm)] * extra,
               out_shape=[_sds((NDEV, GUB, t), BF16), _sds((4, GUB, t), BF16), _sds((t, D), BF16), _sds((t, D), F32),
                          _sds((1, D), F32)] + [_sds((t // BLK, D, BLK), BF16)] * extra, xchg=xchg)


def _wgrad(at, b, tk, name, xchg=None, by_block=False):
    if by_block:
        na, (_, ma, _), t = 1, at.shape, at.shape[0] * BLK
        a_spec = pl.BlockSpec((tk // BLK, ma, BLK), lambda j, k: (k, 0, 0))
    else:
        na, ma, t = at.shape
        a_spec = pl.BlockSpec((1, ma, tk), lambda j, k: (j if na > 1 else 0, 0, k))
    nb, _, mb = b.shape
    nk = t // tk

    def body(a_ref, b_ref, out_ref, acc):
        k = pl.program_id(1)

        @pl.when(k == 0)
        def _():
            acc[...] = jnp.zeros_like(acc)

        acc[...] += _dot(_get_blocks(a_ref) if by_block else a_ref[0], b_ref[0])

        @pl.when(k == nk - 1)
        def _():
            out_ref[0] = acc[...].astype(BF16)

    outs, xo = _pc(body, [at, b], name=name, grid=(max(na, nb), nk),
                   in_specs=[a_spec, pl.BlockSpec((1, tk, mb), lambda j, k: (j if nb > 1 else 0, k, 0))],
                   out_specs=[pl.BlockSpec((1, ma, mb), lambda j, k: (j, 0, 0))], out_shape=[_sds((max(na, nb), ma, mb), BF16)],
                   scratch_shapes=[pltpu.VMEM((ma, mb), F32)], xchg=xchg)
    return outs[0] if xchg is None else (outs[0], xo)


def _conv_pre(x, g, w, b, tm, xchg=None):
    t = x.shape[0]

    def body(x_ref, g_ref, w_ref, b_ref, u_ref, glu_ref):
        _, _, h = _rms(x_ref[...], g_ref[...])
        hb = h.astype(BF16)
        for j in range(4):
            a = _dot(hb, w_ref[j]) + b_ref[:, j * PWB:(j + 1) * PWB]
            gt = _dot(hb, w_ref[j + 4]) + b_ref[:, D + j * PWB:D + (j + 1) * PWB]
            u_ref[j] = a.astype(BF16)
            u_ref[j + 4] = gt.astype(BF16)
            glu_ref[:, j * PWB:(j + 1) * PWB] = a * _sigmoid(gt)

    return _pc(body, [x, g, w, b], name="conv_pre", grid=(t // tm,),
               in_specs=[_rows(tm, D), _acc(1, D), WHOLE, _acc(1, 2 * D)], out_specs=[_blk3(NDEV, tm, PWB), _rows(tm, D)],
               out_shape=[_sds((NDEV, t, PWB), BF16), _sds((t, D), F32)], xchg=xchg)


def _halo_specs(t, tm):
    per = tm // HALO
    last = t // HALO - 1
    return [
        pl.BlockSpec((HALO, D), lambda i: (jnp.maximum(i * per - 1, 0), 0)),
        _rows(tm, D),
        pl.BlockSpec((HALO, D), lambda i: (jnp.minimum((i + 1) * per, last), 0)),
    ]


SUB = 8
CCH = 32
CLN = 256


def _fill_shifted(sh, prev, cur, nxt, tm):
    i = pl.program_id(0)
    rows = jnp.concatenate([jnp.where(i == 0, 0.0, prev[...]), cur[...], jnp.where(i == pl.num_programs(0) - 1, 0.0, nxt[...])], axis=0)
    n = tm + 2 * HALO - SUB
    for b in range(SUB):
        sh[b] = rows[b:b + n]


def _shifted(sh, off, r0, c0):
    return sh[off % SUB, r0 + off - off % SUB:r0 + off - off % SUB + CCH, c0:c0 + CLN]


def _conv_mid(glu, wdw, bdw, tm, xchg=None):
    t = glu.shape[0]

    def body(prev, cur, nxt, w_ref, b_ref, out_ref, sh):
        _fill_shifted(sh, prev, cur, nxt, tm)
        for c0 in range(0, D, CLN):
            for r0 in range(0, tm, CCH):
                acc = jnp.broadcast_to(b_ref[:, c0:c0 + CLN], (CCH, CLN))
                for k in range(CW):
                    acc = acc + w_ref[k:k + 1, c0:c0 + CLN] * _shifted(sh, k + HALO - CPAD, r0, c0)
                out_ref[r0:r0 + CCH, c0:c0 + CLN] = acc

    outs, xo = _pc(body, [glu, glu, glu, wdw, bdw], name="conv_mid", grid=(t // tm,),
                   in_specs=_halo_specs(t, tm) + [_acc(32, D), _acc(1, D)], out_specs=[_rows(tm, D)],
                   out_shape=[_sds((t, D), F32)], scratch_shapes=[pltpu.VMEM((SUB, tm + 2 * HALO - SUB, D), F32)], xchg=xchg)
    return outs[0], xo


def _conv_mid_bwd(dcv, glu, wdw, tm, xchg=None):
    t = glu.shape[0]

    def body(dp, dc, dn, gp, gc, gn, w_ref, dglu_ref, dw_ref, dsh, gsh):
        @pl.when(pl.program_id(0) == 0)
        def _():
            dw_ref[...] = jnp.zeros_like(dw_ref)

        _fill_shifted(dsh, dp, dc, dn, tm)
        _fill_shifted(gsh, gp, gc, gn, tm)
        for c0 in range(0, D, CLN):
            for r0 in range(0, tm, CCH):
                acc = jnp.zeros((CCH, CLN), F32)
                for k in range(CW):
                    acc = acc + w_ref[k:k + 1, c0:c0 + CLN] * _shifted(dsh, HALO + CPAD - k, r0, c0)
                dglu_ref[r0:r0 + CCH, c0:c0 + CLN] = acc
            for k in range(CW):
                dwk = jnp.zeros((SUB, CLN), F32)
                for r0 in range(0, tm, CCH):
                    prod = _shifted(dsh, HALO, r0, c0) * _shifted(gsh, k + HALO - CPAD, r0, c0)
                    for r in range(0, CCH, SUB):
                        dwk = dwk + prod[r:r + SUB]
                dw_ref[k:k + 1, c0:c0 + CLN] += jnp.sum(dwk, axis=0, keepdims=True)

    n = tm + 2 * HALO - SUB
    return _pc(body, [dcv, dcv, dcv, glu, glu, glu, wdw], name="conv_mid_bwd", grid=(t // tm,),
               in_specs=_halo_specs(t, tm) + _halo_specs(t, tm) + [_acc(32, D)], out_specs=[_rows(tm, D), _acc(32, D)],
               out_shape=[_sds((t, D), F32), _sds((32, D), F32)],
               scratch_shapes=[pltpu.VMEM((SUB, n, D), F32), pltpu.VMEM((SUB, n, D), F32)], xchg=xchg)


def _ln(cv, lg, lb):
    mu = jnp.mean(cv, axis=-1, keepdims=True)
    cc = cv - mu
    rs = lax.rsqrt(jnp.mean(cc * cc, axis=-1, keepdims=True) + EPS)
    lh = cc * rs
    return lh, rs, lh * lg + lb


def _conv_post(cv, x, lg, lb, w2, b2, tm):
    t = x.shape[0]

    def body(cv_ref, x_ref, lg_ref, lb_ref, w_ref, b_ref, st_ref, out_ref):
        _, _, ln = _ln(cv_ref[...], lg_ref[...], lb_ref[...])
        s = (ln * _sigmoid(ln)).astype(BF16)
        st_ref[...] = s.T
        out_ref[...] = x_ref[...] + _dot(s, w_ref[...]) + b_ref[...]

    return _pc(body, [cv, x, lg, lb, w2, b2], name="conv_post", grid=(t // tm,),
               in_specs=[_rows(tm, D), _rows(tm, D), _acc(1, D), _acc(1, D), WHOLE, _acc(1, D)],
               out_specs=[_cols(tm), _rows(tm, D)], out_shape=[_sds((D, t), BF16), _sds((t, D), F32)])[0]


def _conv_post_bwd(dx, cv, lg, lb, w2, tm, xchg=None):
    t = dx.shape[0]

    def body(dx_ref, cv_ref, lg_ref, lb_ref, w_ref, dcv_ref, part_ref):
        @pl.when(pl.program_id(0) == 0)
        def _():
            part_ref[...] = jnp.zeros_like(part_ref)

        dx = dx_ref[...]
        lg = lg_ref[...]
        lh, rs, ln = _ln(cv_ref[...], lg, lb_ref[...])
        sg = _sigmoid(ln)
        dln = _dot_nt(dx, w_ref[...]) * (sg * (1.0 + ln * (1.0 - sg)))
        dlh = dln * lg
        dcv = rs * (dlh - jnp.mean(dlh, axis=-1, keepdims=True) - lh * jnp.mean(dlh * lh, axis=-1, keepdims=True))
        dcv_ref[...] = dcv
        part_ref[0:1, :] += jnp.sum(dln * lh, axis=0, keepdims=True)
        part_ref[1:2, :] += jnp.sum(dln, axis=0, keepdims=True)
        part_ref[2:3, :] += jnp.sum(dcv, axis=0, keepdims=True)
        part_ref[3:4, :] += jnp.sum(dx, axis=0, keepdims=True)

    return _pc(body, [dx, cv, lg, lb, w2], name="conv_post_bwd", grid=(t // tm,),
               in_specs=[_rows(tm, D), _rows(tm, D), _acc(1, D), _acc(1, D), WHOLE], out_specs=[_rows(tm, D), _acc(8, D)],
               out_shape=[_sds((t, D), F32), _sds((8, D), F32)], xchg=xchg)


def _conv_pre_bwd(dglu, u, x, g, w, dx_out, tm):
    t = x.shape[0]

    def body(dglu_ref, u_ref, x_ref, g_ref, w_ref, dxo_ref, du_ref, ht_ref, dx_ref, dg_ref, db_ref):
        @pl.when(pl.program_id(0) == 0)
        def _():
            dg_ref[...] = jnp.zeros_like(dg_ref)
            db_ref[...] = jnp.zeros_like(db_ref)

        gn = g_ref[...]
        xh, r, h = _rms(x_ref[...], gn)
        ht_ref[...] = h.astype(BF16).T
        dh = jnp.zeros_like(xh)
        for j in range(4):
            a = u_ref[j].astype(F32)
            sg = _sigmoid(u_ref[j + 4].astype(F32))
            dgl = dglu_ref[:, j * PWB:(j + 1) * PWB]
            da = dgl * sg
            dgt = dgl * a * sg * (1.0 - sg)
            db_ref[:, j * PWB:(j + 1) * PWB] += jnp.sum(da, axis=0, keepdims=True)
            db_ref[:, D + j * PWB:D + (j + 1) * PWB] += jnp.sum(dgt, axis=0, keepdims=True)
            da, dgt = da.astype(BF16), dgt.astype(BF16)
            du_ref[j] = da
            du_ref[j + 4] = dgt
            dh = dh + _dot_nt(da, w_ref[j]) + _dot_nt(dgt, w_ref[j + 4])
        dx, dg = _rms_bwd(dh, xh, r, gn)
        dx_ref[...] = dxo_ref[...] + dx
        dg_ref[...] += dg

    return _pc(body, [dglu, u, x, g, w, dx_out], name="conv_pre_bwd", grid=(t // tm,),
               in_specs=[_rows(tm, D), _blk3(NDEV, tm, PWB), _rows(tm, D), _acc(1, D), WHOLE, _rows(tm, D)],
               out_specs=[_blk3(NDEV, tm, PWB), _cols(tm), _rows(tm, D), _acc(1, D), _acc(1, 2 * D)],
               out_shape=[_sds((NDEV, t, PWB), BF16), _sds((D, t), BF16), _sds((t, D), F32), _sds((1, D), F32),
                          _sds((1, 2 * D), F32)])[0]


def _adamw(w, g, m, v):
    m = B1 * m + (1.0 - B1) * g
    v = B2 * v + (1.0 - B2) * (g * g)
    m_hat = m / (1.0 - B1 ** STEP)
    v_hat = v / (1.0 - B2 ** STEP)
    return -LR * (m_hat / (jnp.sqrt(v_hat) + AEPS) + WD * w), m, v


def _reduce_adamw(lands, w, m, v, tr, name, xchg=None):
    nl, r, c = w.shape

    def body(*refs):
        l_refs, (w_ref, m_ref, v_ref, g_ref, d_ref, nm_ref, nv_ref) = refs[:nl], refs[nl:]

        def total(ref):
            g = ref[0].astype(F32)
            for j in range(1, NDEV):
                g = g + ref[j].astype(F32)
            return g

        g = total(l_refs[0])
        for n in range(1, nl):
            g = jnp.where(pl.program_id(0) == n, total(l_refs[n]), g)
        g_ref[0] = g
        d_ref[0], nm_ref[0], nv_ref[0] = _adamw(w_ref[0], g, m_ref[0], v_ref[0])

    layer = pl.BlockSpec((1, tr, c), lambda l, i: (l, i, 0))
    outs, xo = _pc(body, [*lands, w, m, v], name=name, grid=(nl, r // tr),
                   in_specs=[pl.BlockSpec((NDEV, tr, c), lambda l, i: (0, i, 0))] * nl + [layer] * 3, out_specs=[layer] * 4,
                   out_shape=[_sds((nl, r, c), F32)] * 4, xchg=xchg)
    return outs if xchg is None else (outs, xo)


def _sum_parts(parts):
    _, r, c = parts.shape

    def body(p_ref, out_ref):
        s = p_ref[0]
        for j in range(1, NDEV):
            s = s + p_ref[j]
        out_ref[...] = s

    return _pc(body, [parts], name="sum_parts", in_specs=[WHOLE], out_specs=[WHOLE], out_shape=[_sds((r, c), F32)])[0][0]


def _adamw_small(w, g, m, v):
    def body(w_ref, g_ref, m_ref, v_ref, d_ref, nm_ref, nv_ref):
        d_ref[...], nm_ref[...], nv_ref[...] = _adamw(w_ref[...], g_ref[...], m_ref[...], v_ref[...])

    return _pc(body, [w, g, m, v], name="adamw_small", in_specs=[WHOLE] * 4, out_specs=[WHOLE] * 3,
               out_shape=[_sds(w.shape, F32)] * 3)[0]


def _rows128(a):
    a = a.astype(F32)
    if a.shape[-1] % 128:
        a = jnp.pad(a, [(0, 0)] * (a.ndim - 1) + [(0, 128 - a.shape[-1] % 128)])
    return a.reshape(-1, 128)


def _pack(arrs, rows):
    p = jnp.concatenate([_rows128(a) for a in arrs], axis=0)
    return jnp.pad(p, ((0, rows - p.shape[0]), (0, 0)))


SMALL = ("attn_norm", "ffn_norm", "final_norm", "attn_sink", "conv_norm", "conv_b_dw", "conv_ln_g", "conv_ln_b", "conv_b_pw2",
         "conv_b_pw1", "conv_w_dw")
SMALL_ROWS = 72


def _pack_small(d):
    return _pack([d[k] for k in SMALL], SMALL_ROWS)


def _unpack_small(p, like):
    out, r = {}, 0
    for k in SMALL:
        shp = like[k].shape
        n = -(-shp[-1] // 128) * (math.prod(shp[:-1]))
        blk = p[r:r + n]
        if shp[-1] % 128:
            blk = blk[:, :shp[-1]]
        out[k] = blk.reshape(shp)
        r += n
    return out


NAMES = ("attn_norm", "attn_w_qkv", "attn_w_o", "attn_sink", "conv_norm", "conv_w_pw1", "conv_b_pw1", "conv_w_dw", "conv_b_dw",
         "conv_ln_g", "conv_ln_b", "conv_w_pw2", "conv_b_pw2", "ffn_norm", "ffn_w_gu", "ffn_w_down", "final_norm")
TM = 256
TL = 512
TK = 2048


def _gu_t(a):
    return jnp.swapaxes(a, 1, 2)


def kernel(x, attn_norm, attn_w_qkv, attn_w_o, attn_sink, conv_norm, conv_w_pw1, conv_b_pw1, conv_w_dw, conv_b_dw, conv_ln_g, conv_ln_b, conv_w_pw2, conv_b_pw2, ffn_norm, ffn_w_gu, ffn_w_down, final_norm, loss_target, m_attn_norm, m_attn_w_qkv, m_attn_w_o, m_attn_sink, m_conv_norm, m_conv_w_pw1, m_conv_b_pw1, m_conv_w_dw, m_conv_b_dw, m_conv_ln_g, m_conv_ln_b, m_conv_w_pw2, m_conv_b_pw2, m_ffn_norm, m_ffn_w_gu, m_ffn_w_down, m_final_norm, v_attn_norm, v_attn_w_qkv, v_attn_w_o, v_attn_sink, v_conv_norm, v_conv_w_pw1, v_conv_b_pw1, v_conv_w_dw, v_conv_b_dw, v_conv_ln_g, v_conv_ln_b, v_conv_w_pw2, v_conv_b_pw2, v_ffn_norm, v_ffn_w_gu, v_ffn_w_down, v_final_norm):
    w = dict(zip(NAMES, (attn_norm, attn_w_qkv, attn_w_o, attn_sink, conv_norm, conv_w_pw1, conv_b_pw1, conv_w_dw, conv_b_dw, conv_ln_g,
                         conv_ln_b, conv_w_pw2, conv_b_pw2, ffn_norm, ffn_w_gu, ffn_w_down, final_norm)))
    m = dict(zip(NAMES, (m_attn_norm, m_attn_w_qkv, m_attn_w_o, m_attn_sink, m_conv_norm, m_conv_w_pw1, m_conv_b_pw1, m_conv_w_dw,
                         m_conv_b_dw, m_conv_ln_g, m_conv_ln_b, m_conv_w_pw2, m_conv_b_pw2, m_ffn_norm, m_ffn_w_gu, m_ffn_w_down,
                         m_final_norm)))
    v = dict(zip(NAMES, (v_attn_norm, v_attn_w_qkv, v_attn_w_o, v_attn_sink, v_conv_norm, v_conv_w_pw1, v_conv_b_pw1, v_conv_w_dw,
                         v_conv_b_dw, v_conv_ln_g, v_conv_ln_b, v_conv_w_pw2, v_conv_b_pw2, v_ffn_norm, v_ffn_w_gu, v_ffn_w_down,
                         v_final_norm)))
    me = 4 * lax.axis_index("x") + 2 * lax.axis_index("y") + lax.axis_index("c")
    for d in (w, m, v):
        d["ffn_w_gu"] = _gu_t(d["ffn_w_gu"])
    sh = {k: w[k][0].astype(BF16) for k in ("attn_w_qkv", "attn_w_o", "conv_w_pw1", "conv_w_pw2")}
    gu_b, down_b = w["ffn_w_gu"].astype(BF16), w["ffn_w_down"].astype(BF16)
    sh.update(ffn_w_gu0=gu_b[0], ffn_w_gu1=gu_b[1], ffn_w_down0=down_b[0], ffn_w_down1=down_b[1])
    x0, tgt = x[0], loss_target[0]
    t = x0.shape[0]
    tabs = _rope_tables(t)
    tl = min(TL, t)
    g_a, sink, g_f0, g_f1, g_fin = w["attn_norm"], w["attn_sink"][0], w["ffn_norm"][0:1], w["ffn_norm"][1:2], w["final_norm"][None]
    gather, scatter = False, True

    shard_rows = _pack([w["conv_w_dw"][0], jnp.zeros((1, 128), F32), w["conv_norm"], w["conv_b_dw"], w["conv_ln_g"], w["conv_ln_b"],
                        w["conv_b_pw2"], w["conv_b_pw1"]], 40)
    wqkv_g, sm = _exchange([sh["attn_w_qkv"], shard_rows], [gather] * 2, "gather_attn")
    wqkv = wqkv_g.transpose(1, 0, 2).reshape(D, QKV)

    def full_vec(row, n=1):
        return sm[:, row:row + n, :].reshape(1, NDEV * n * 128)

    w_dw, g_c, b_dw, ln_g, ln_b = sm[:, 0:32, :].transpose(1, 0, 2).reshape(32, D), full_vec(32), full_vec(33), full_vec(34), full_vec(35)
    b_pw2, b_pw1 = full_vec(36), full_vec(37, 2)

    (q_t, kv), (wo_g, wd0_g) = _attn_pre(x0, g_a, wqkv, tabs, tl, xchg=([sh["attn_w_o"], sh["ffn_w_down0"]], [gather] * 2))
    o_t, (wgu0,) = _attn_fwd(q_t, kv, sink, xchg=([sh["ffn_w_gu0"]], [gather]))
    wo, wd0 = wo_g.reshape(D, D), wd0_g.reshape(4, GUB, D)
    (x1, gu0, x2), (wpw1, wgu1) = _ffn_fwd_attn(
        o_t, x0, wo, g_f0, wgu0, wd0, tl, xchg=([sh["conv_w_pw1"], sh["ffn_w_gu1"]], [gather] * 2))
    (u, glu), (wpw2_g,) = _conv_pre(x2, g_c, wpw1, b_pw1, tl, xchg=([sh["conv_w_pw2"]], [gather]))
    wpw2 = wpw2_g.reshape(D, D)
    cv, (wd1_g,) = _conv_mid(glu, w_dw, b_dw, tl, xchg=([sh["ffn_w_down1"]], [gather]))
    wd1 = wd1_g.reshape(4, GUB, D)
    s_t, x3 = _conv_post(cv, x2, ln_g, ln_b, wpw2, b_pw2, tl)
    (gu1, dx4, fin), _ = _ffn_fwd_final(x3, g_f1, wgu1, wd1, g_fin, tgt, tl)

    land = {}
    tk, tk2 = min(TK, t), min(2 * TK, t)
    (dgu1_t, act1_t, h3, dx3, dg_f1), _ = _ffn_bwd(dx4, x3, gu1, g_f1, wgu1, wd1, TM, "ffn_bwd1")
    dwgu1 = _wgrad(dgu1_t, h3[None], tk2, "dwgu1")
    dwd1 = _wgrad(act1_t, dx4[None], tk, "dwd1").reshape(NDEV, DFF // NDEV, D)
    (dcv, cpart), (land["ffn_w_down1"],) = _conv_post_bwd(dx3, cv, ln_g, ln_b, wpw2, tl, xchg=([dwd1], [scatter]))
    dwpw2 = _wgrad(s_t[None], dx3[None], tk, "dwpw2").reshape(NDEV, D // NDEV, D)
    (dglu, dw_dw), (land["ffn_w_gu1"], land["conv_w_pw2"]) = _conv_mid_bwd(dcv, glu, w_dw, TM, xchg=([dwgu1, dwpw2], [scatter] * 2))
    du, h2_t, dx2, dg_c, db_pw1 = _conv_pre_bwd(dglu, u, x2, g_c, wpw1, dx3, tl)
    dwpw1 = _wgrad(h2_t[None], du, tk2, "dwpw1")
    (dgu0_t, act0_t, h1, dx1, dg_f0, do_t), (land["conv_w_pw1"],) = _ffn_bwd(
        dx2, x1, gu0, g_f0, wgu0, wd0, TM, "ffn_bwd0", wo=wo, xchg=([dwpw1], [scatter]))
    dwgu0 = _wgrad(dgu0_t, h1[None], tk2, "dwgu0")
    dwd0 = _wgrad(act0_t, dx2[None], tk, "dwd0").reshape(NDEV, DFF // NDEV, D)
    dwo = _wgrad(o_t, dx1[None], tk, "dwo", by_block=True).reshape(NDEV, D // NDEV, D)
    (dq_t, dk, dv, dsink), (land["ffn_w_gu0"], land["ffn_w_down0"]) = _attn_bwd(
        q_t, kv, do_t, sink, tl, xchg=([dwgu0, dwd0], [scatter] * 2))
    dqkv, h0_t, dx0, dg_a = _attn_pre_bwd(dq_t, dk, dv, x0, g_a, wqkv, tabs, dx1, tl)
    dwqkv, (land["attn_w_o"],) = _wgrad(h0_t[None], dqkv[None], tk, "dwqkv", xchg=([dwo], [scatter]))
    dwqkv = dwqkv[0].reshape(D, NDEV, QKV // NDEV).transpose(1, 0, 2)

    lane0 = (lax.broadcasted_iota(jnp.int32, (1, 128), 1) == 0).astype(F32)
    parts = _pack([dg_a, dg_f0, dg_f1, fin[0:1], dsink[0:1, :NH], fin[1, 0] * lane0, jnp.zeros((6, 128), F32), dg_c, cpart[2:3],
                   cpart[0:1], cpart[1:2], cpart[3:4], db_pw1, dw_dw.reshape(32, NDEV, 128)], 352)
    land["attn_w_qkv"], parts_g = _exchange([dwqkv, parts], [scatter, gather], "scatter_attn")
    red = _sum_parts(parts_g)

    def shard_rows_of(row, n=1):
        return lax.dynamic_slice_in_dim(red, row + n * me, n, axis=0)

    gs = {
        "attn_norm": red[0:8].reshape(1, D), "ffn_norm": red[8:24].reshape(2, D), "final_norm": red[24:32].reshape(D),
        "attn_sink": red[32:33, :NH], "conv_norm": shard_rows_of(40), "conv_b_dw": shard_rows_of(48), "conv_ln_g": shard_rows_of(56),
        "conv_ln_b": shard_rows_of(64), "conv_b_pw2": shard_rows_of(72), "conv_b_pw1": shard_rows_of(80, 2).reshape(1, PWB),
        "conv_w_dw": lax.dynamic_index_in_dim(red[96:352].reshape(32, NDEV, 128), me, axis=1, keepdims=False)[None, :CW],
    }
    loss = red[33, 0]

    grads, deltas, new_m, new_v = dict(gs), {}, {}, {}
    ds, ms, vs = _adamw_small(_pack_small(w), _pack_small(gs), _pack_small(m), _pack_small(v))
    deltas.update(_unpack_small(ds, gs))
    new_m.update(_unpack_small(ms, gs))
    new_v.update(_unpack_small(vs, gs))
    for k in ("attn_w_qkv", "attn_w_o", "conv_w_pw1", "conv_w_pw2", "ffn_w_gu", "ffn_w_down"):
        lands = [land[k + "0"], land[k + "1"]] if k.startswith("ffn") else [land[k]]
        tr = {1024: 256, 128: 128, 352: 176, GUB: 176}[w[k].shape[1]]
        grads[k], deltas[k], new_m[k], new_v[k] = _reduce_adamw(lands, w[k], m[k], v[k], tr, "adamw_" + k)
    for d in (grads, deltas, new_m, new_v):
        d["ffn_w_gu"] = _gu_t(d["ffn_w_gu"])
    return (loss, dx0[None], *[grads[k] for k in NAMES], *[deltas[k] for k in NAMES], *[new_m[k] for k in NAMES],
            *[new_v[k] for k in NAMES])
```

```python
import math

import jax
import jax.numpy as jnp
from jax import lax
from jax.experimental import pallas as pl
from jax.experimental.pallas import tpu as pltpu

F32 = jnp.float32
BF16 = jnp.bfloat16

D = 1024
NH = 16
NKV = 4
HD = 64
GROUP = NH // NKV
ROT = 16
THETA = 500000.0
BLK = 128
QKV = (NH + 2 * NKV) * HD
KOFF = NH * HD
VOFF = KOFF + NKV * HD
DFF = 2816
NDEV = 8
GUB = 2 * DFF // NDEV
PWB = 2 * D // NDEV
CW = 31
CPAD = 15
HALO = 16
EPS = 1e-6
NEG = -1e30
LR, B1, B2, AEPS, WD, STEP = 0.001, 0.9, 0.999, 1e-08, 0.01, 10

VMEM_LIMIT = 56 * 1024 * 1024
MESH = pl.DeviceIdType.MESH
WHOLE = pl.BlockSpec(memory_space=pltpu.VMEM)
ANY = pl.BlockSpec(memory_space=pl.ANY)


def _place():
    x, y, c = lax.axis_index("x"), lax.axis_index("y"), lax.axis_index("c")
    return x, y, c, 4 * x + 2 * y + c


def _peer(x, y, c, j):
    px = 1 - x if j & 4 else x
    py = 1 - y if j & 2 else y
    pc = 1 - c if j & 1 else c
    return (px, py, pc), 4 * px + 2 * py + pc


SIBLING = 1
OTHER_CHIPS = (2, 4, 6)
PASS_ON_LEAD = 3


class _Exchange:
    def __init__(self, src, dst, scatter, send, recv, loc):
        self.src, self.dst, self.scatter, self.send, self.recv, self.loc = src, dst, scatter, send, recv, loc
        self.x, self.y, self.c, self.me = _place()

    def _remote(self, k, j, s, d, to):
        peer, _ = _peer(self.x, self.y, self.c, to)
        return pltpu.make_async_remote_copy(src_ref=s, dst_ref=d, send_sem=self.send.at[k, j - 1], recv_sem=self.recv.at[k, j - 1],
                                            device_id=peer, device_id_type=MESH)

    def _slot(self, j):
        return _peer(self.x, self.y, self.c, j)[1]

    def local(self, k):
        if self.scatter[k]:
            return pltpu.make_async_copy(self.src[k].at[self.me], self.dst[k].at[0], self.loc.at[k])
        return pltpu.make_async_copy(self.src[k], self.dst[k].at[self.me], self.loc.at[k])

    def direct(self, k, j):
        if self.scatter[k]:
            return self._remote(k, j, self.src[k].at[self._slot(j)], self.dst[k].at[j], j)
        return self._remote(k, j, self.src[k], self.dst[k].at[self.me], j)

    def passed_on(self, k, j):
        rows = self.dst[k].at[self._slot(j)]
        return self._remote(k, j + 1, rows, rows, SIBLING)

    def arrival(self, k, j):
        rows = self.dst[k].at[j if self.scatter[k] else self._slot(j)]
        return self._remote(k, j, rows, rows, j)

    def sent(self, k):
        return tuple(range(1, NDEV)) if self.scatter[k] else (SIBLING,) + OTHER_CHIPS

    def start(self):
        for k in range(len(self.src)):
            self.local(k).start()
            for j in self.sent(k):
                self.direct(k, j).start()

    def pass_on(self):
        for k in range(len(self.src)):
            if not self.scatter[k]:
                for j in OTHER_CHIPS:
                    self.arrival(k, j).wait_recv()
                    self.passed_on(k, j).start()

    def finish(self):
        for k in range(len(self.src)):
            for j in range(1, NDEV):
                if self.scatter[k] or j not in OTHER_CHIPS:
                    self.arrival(k, j).wait_recv()
            for j in self.sent(k):
                self.direct(k, j).wait_send()
            if not self.scatter[k]:
                for j in OTHER_CHIPS:
                    self.passed_on(k, j).wait_send()
            self.local(k).wait()


def _call(body, **kw):
    return pl.pallas_call(body, **kw)


def _pc(body, operands, *, name, in_specs, out_specs, out_shape, grid=None, scratch_shapes=(), xchg=None):
    kw = dict(name=name, compiler_params=pltpu.CompilerParams(
        dimension_semantics=None if grid is None else ("arbitrary",) * len(grid), vmem_limit_bytes=VMEM_LIMIT,
        has_side_effects=xchg is not None))
    if grid is not None:
        kw["grid"] = grid
    if xchg is None:
        outs = _call(body, in_specs=list(in_specs), out_specs=list(out_specs), out_shape=list(out_shape),
                     scratch_shapes=list(scratch_shapes), **kw)(*operands)
        return list(outs), []
    arrs, scatter = xchg
    nx, n_in, n_out, n_s = len(arrs), len(in_specs), len(out_specs), len(scratch_shapes)

    def wrapped(*refs):
        ins, refs = refs[:n_in], refs[n_in:]
        src, refs = refs[:nx], refs[nx:]
        outs, refs = refs[:n_out], refs[n_out:]
        dst, refs = refs[:nx], refs[nx:]
        scr, (send, recv, loc) = refs[:n_s], refs[n_s:]
        if grid is None:
            _Exchange(src, dst, scatter, send, recv, loc).start()
            body(*ins, *outs, *scr)
            _Exchange(src, dst, scatter, send, recv, loc).pass_on()
            _Exchange(src, dst, scatter, send, recv, loc).finish()
            return

        step, steps = 0, 1
        for axis, n in enumerate(grid):
            step, steps = step * n + pl.program_id(axis), steps * n

        @pl.when(step == 0)
        def _():
            _Exchange(src, dst, scatter, send, recv, loc).start()

        body(*ins, *outs, *scr)

        @pl.when(step == max(steps - 1 - PASS_ON_LEAD, 0))
        def _():
            _Exchange(src, dst, scatter, send, recv, loc).pass_on()

        @pl.when(step == steps - 1)
        def _():
            _Exchange(src, dst, scatter, send, recv, loc).finish()

    xshape = [jax.ShapeDtypeStruct(a.shape if sc else (NDEV,) + a.shape, a.dtype) for a, sc in zip(arrs, scatter)]
    sems = [pltpu.SemaphoreType.DMA((nx, NDEV - 1)), pltpu.SemaphoreType.DMA((nx, NDEV - 1)), pltpu.SemaphoreType.DMA((nx,))]
    res = _call(wrapped, in_specs=list(in_specs) + [ANY] * nx, out_specs=list(out_specs) + [ANY] * nx,
                out_shape=list(out_shape) + xshape, scratch_shapes=list(scratch_shapes) + sems, **kw)(*operands, *arrs)
    return list(res[:n_out]), list(res[n_out:])


def _exchange(arrs, scatter, name):
    return _pc(lambda: None, [], name=name, in_specs=[], out_specs=[], out_shape=[], xchg=(arrs, scatter))[1]


def _rows(tm, width):
    return pl.BlockSpec((tm, width), lambda i: (i, 0))


def _cols(tm):
    return pl.BlockSpec((D, tm), lambda i: (0, i))


def _qblocks(tm):
    return pl.BlockSpec((tm // BLK, D, BLK), lambda i: (i, 0, 0))


def _put_blocks(ref, val):
    for b in range(ref.shape[0]):
        ref[b] = val[:, b * BLK:(b + 1) * BLK]


def _get_blocks(ref):
    return jnp.concatenate([ref[b] for b in range(ref.shape[0])], axis=1)


def _blk3(nb, tm, width):
    return pl.BlockSpec((nb, tm, width), lambda i: (0, i, 0))


def _acc(rows, width):
    return pl.BlockSpec((rows, width), lambda i: (0, 0))


def _sds(shape, dtype):
    return jax.ShapeDtypeStruct(shape, dtype)


def _dot(a, b):
    return jnp.dot(a.astype(BF16), b.astype(BF16), preferred_element_type=F32)


def _dot_nt(a, b):
    return lax.dot_general(a.astype(BF16), b.astype(BF16), (((1,), (1,)), ((), ())), preferred_element_type=F32)


def _dot_tn(a, b):
    return lax.dot_general(a.astype(BF16), b.astype(BF16), (((0,), (0,)), ((), ())), preferred_element_type=F32)


def _sigmoid(x):
    return 1.0 / (1.0 + jnp.exp(-x))


def _rms(x, g):
    r = lax.rsqrt(jnp.mean(x * x, axis=-1, keepdims=True) + EPS)
    xh = x * r
    return xh, r, xh * g


def _rms_bwd(dh, xh, r, g):
    dxh = dh * g
    dg = jnp.sum(dh * xh, axis=0, keepdims=True)
    dx = r * (dxh - xh * jnp.mean(dxh * xh, axis=-1, keepdims=True))
    return dx, dg


def _lanes(t, width):
    return jnp.tile(t, (1, width // t.shape[1]))


def _rope(z, c, sa, sb):
    w = z.shape[1]
    return z * _lanes(c, w) + pltpu.roll(z, w - 8, 1) * _lanes(sa, w) + pltpu.roll(z, 8, 1) * _lanes(sb, w)


def _rope_t(dz, c, sa, sb):
    w = dz.shape[1]
    return dz * _lanes(c, w) + pltpu.roll(dz * _lanes(sa, w), 8, 1) + pltpu.roll(dz * _lanes(sb, w), w - 8, 1)


def _rope_tables(t):
    pos = jnp.arange(t, dtype=F32)
    inv_freq = THETA ** (-jnp.arange(0, ROT, 2, dtype=F32) / ROT)
    ang = pos[:, None] * inv_freq[None, :]
    cos, sin = jnp.cos(ang), jnp.sin(ang)
    one = jnp.ones((t, HD - ROT), F32)
    zero = jnp.zeros((t, HD - ROT), F32)
    z8 = jnp.zeros((t, 8), F32)
    c = jnp.concatenate([cos, cos, one], axis=1)
    sa = jnp.concatenate([-sin, z8, zero], axis=1)
    sb = jnp.concatenate([z8, sin, zero], axis=1)
    return tuple(jnp.tile(a, (1, 2)) for a in (c, sa, sb))


KVW = NKV * HD
GW = GROUP * BLK


def _attn_pre(x, g, wqkv, tabs, tm, xchg=None):
    t = x.shape[0]

    def body(x_ref, g_ref, w_ref, c_ref, sa_ref, sb_ref, qt_ref, kv_ref):
        _, _, h = _rms(x_ref[...], g_ref[...])
        z = _dot(h, w_ref[...])
        c, sa, sb = c_ref[...], sa_ref[...], sb_ref[...]
        _put_blocks(qt_ref, (_rope(z[:, :KOFF], c, sa, sb) * 0.125).T.astype(BF16))
        kv_ref[:, :KVW] = _rope(z[:, KOFF:VOFF], c, sa, sb).astype(BF16)
        kv_ref[:, KVW:] = z[:, VOFF:].astype(BF16)

    return _pc(body, [x, g, wqkv, *tabs], name="attn_pre", grid=(t // tm,),
               in_specs=[_rows(tm, D), _acc(1, D), WHOLE, _rows(tm, 128), _rows(tm, 128), _rows(tm, 128)],
               out_specs=[_qblocks(tm), _rows(tm, 2 * KVW)],
               out_shape=[_sds((t // BLK, D, BLK), BF16), _sds((t, 2 * KVW), BF16)], xchg=xchg)


QB = 4


def _attn_specs(nblk):
    prev = lambda i: jnp.maximum(QB * i - 1, 0)
    nxt = lambda i: jnp.minimum(QB * (i + 1), nblk - 1)
    return [
        _qblocks(QB * BLK),
        pl.BlockSpec((BLK, KVW), lambda i: (prev(i), 0)),
        pl.BlockSpec((QB * BLK, KVW), lambda i: (i, 0)),
        pl.BlockSpec((BLK, KVW), lambda i: (nxt(i), 0)),
        pl.BlockSpec((BLK, KVW), lambda i: (prev(i), 1)),
        pl.BlockSpec((QB * BLK, KVW), lambda i: (i, 1)),
        pl.BlockSpec((BLK, KVW), lambda i: (nxt(i), 1)),
    ]


def _attn_bias():
    c = lax.broadcasted_iota(jnp.int32, (3, 3 * BLK, GW), 1)
    r = lax.broadcasted_iota(jnp.int32, (3, 3 * BLK, GW), 2) & (BLK - 1)
    slab = lax.broadcasted_iota(jnp.int32, (3, 3 * BLK, GW), 0)
    valid = (c >= r) & (c - 2 * BLK <= r) & ((slab != 0) | (c >= BLK)) & ((slab != 2) | (c < 2 * BLK))
    return jnp.where(valid, 0.0, NEG).astype(F32)


def _bias_of(bias_ref, blk, nblk):
    return bias_ref[jnp.where(blk == 0, 0, jnp.where(blk == nblk - 1, 2, 1))]


def _group(ref, b, kv):
    return jnp.concatenate([ref[b, (kv * GROUP + g) * HD:(kv * GROUP + g + 1) * HD, :] for g in range(GROUP)], axis=1)


def _group_sink(sink_ref, kv):
    return jnp.concatenate([jnp.full((1, BLK), sink_ref[kv * GROUP + g], F32) for g in range(GROUP)], axis=1)


def _attn_exp(k_h, qt_g, bias, sink_g):
    s = _dot(k_h, qt_g) + bias
    m = jnp.maximum(jnp.max(s, axis=0, keepdims=True), sink_g)
    e = jnp.exp(s - m)
    es = jnp.exp(sink_g - m)
    return e, 1.0 / (jnp.sum(e, axis=0, keepdims=True) + es), es


def _attn_fwd(qt, kv, sink, xchg=None):
    t = kv.shape[0]
    nblk = t // BLK

    def body(sink_ref, bias_ref, qt_ref, kp, kc, kn, vp, vc, vn, ot_ref):
        k = jnp.concatenate([kp[...], kc[...], kn[...]], axis=0)
        v = jnp.concatenate([vp[...], vc[...], vn[...]], axis=0)
        for b in range(QB):
            bias = _bias_of(bias_ref, QB * pl.program_id(0) + b, nblk)
            keys = slice(b * BLK, (b + 3) * BLK)
            for h in range(NKV):
                e, inv, _ = _attn_exp(k[keys, h * HD:(h + 1) * HD], _group(qt_ref, b, h), bias, _group_sink(sink_ref, h))
                ot_g = (_dot_tn(v[keys, h * HD:(h + 1) * HD], e) * inv).astype(BF16)
                for g in range(GROUP):
                    ot_ref[b, (h * GROUP + g) * HD:(h * GROUP + g + 1) * HD, :] = ot_g[:, g * BLK:(g + 1) * BLK]

    outs, xo = _pc(body, [sink, _attn_bias(), qt] + [kv] * 6, name="attn_fwd", grid=(nblk // QB,),
                   in_specs=[pl.BlockSpec(memory_space=pltpu.SMEM), WHOLE] + _attn_specs(nblk),
                   out_specs=[_qblocks(QB * BLK)], out_shape=[_sds((nblk, D, BLK), BF16)], xchg=xchg)
    return outs[0], xo


def _attn_bwd(qt, kv, dot, sink, pad, xchg=None):
    t = kv.shape[0]
    nblk = t // BLK

    def body(sink_ref, bias_ref, qt_ref, kp, kc, kn, vp, vc, vn, dot_ref, dqt_ref, dk_ref, dv_ref, ds_ref):
        i = pl.program_id(0)

        @pl.when(i == 0)
        def _():
            dk_ref[...] = jnp.zeros_like(dk_ref)
            dv_ref[...] = jnp.zeros_like(dv_ref)
            ds_ref[...] = jnp.zeros_like(ds_ref)

        k = jnp.concatenate([kp[...], kc[...], kn[...]], axis=0)
        v = jnp.concatenate([vp[...], vc[...], vn[...]], axis=0)
        lane = lax.broadcasted_iota(jnp.int32, (1, 128), 1)
        dsink = jnp.zeros((1, 128), F32)
        for b in range(QB):
            blk = QB * i + b
            bias = _bias_of(bias_ref, blk, nblk)
            keys = slice(b * BLK, (b + 3) * BLK)
            rows = pl.ds(pl.multiple_of(blk * BLK + (pad - BLK), BLK), 3 * BLK)
            for h in range(NKV):
                k_h, v_h = k[keys, h * HD:(h + 1) * HD], v[keys, h * HD:(h + 1) * HD]
                qt_g, dot_g = _group(qt_ref, b, h), _group(dot_ref, b, h)
                e, inv, es = _attn_exp(k_h, qt_g, bias, _group_sink(sink_ref, h))
                p, ps = e * inv, es * inv
                dp = _dot(v_h, dot_g)
                delta = jnp.sum(p * dp, axis=0, keepdims=True)
                ds = (p * (dp - delta)).astype(BF16)
                dqt_g = _dot_tn(k_h, ds)
                dk_ref[rows, h * HD:(h + 1) * HD] += _dot_nt(ds, qt_g)
                dv_ref[rows, h * HD:(h + 1) * HD] += _dot_nt(p, dot_g)
                psd = ps * delta
                for g in range(GROUP):
                    n = h * GROUP + g
                    dqt_ref[b, n * HD:(n + 1) * HD, :] = dqt_g[:, g * BLK:(g + 1) * BLK]
                    dsink = dsink - jnp.where(lane == n, jnp.sum(psd[:, g * BLK:(g + 1) * BLK], axis=1, keepdims=True), 0.0)
        ds_ref[0:1, :] += dsink

    outs, xo = _pc(body, [sink, _attn_bias(), qt] + [kv] * 6 + [dot], name="attn_bwd", grid=(nblk // QB,),
                   in_specs=[pl.BlockSpec(memory_space=pltpu.SMEM), WHOLE] + _attn_specs(nblk) + [_qblocks(QB * BLK)],
                   out_specs=[_qblocks(QB * BLK), _acc(t + 2 * pad, KVW), _acc(t + 2 * pad, KVW), _acc(8, 128)],
                   out_shape=[_sds((nblk, D, BLK), F32), _sds((t + 2 * pad, KVW), F32), _sds((t + 2 * pad, KVW), F32),
                              _sds((8, 128), F32)], xchg=xchg)
    return outs, xo


def _attn_pre_bwd(dqt, dk, dv, x, g, wqkv, tabs, dx_out, tm):
    t = x.shape[0]

    def body(dqt_ref, dk_ref, dv_ref, x_ref, g_ref, w_ref, c_ref, sa_ref, sb_ref, dxo_ref, dqkv_ref, ht_ref, dx_ref, dg_ref):
        @pl.when(pl.program_id(0) == 0)
        def _():
            dg_ref[...] = jnp.zeros_like(dg_ref)

        c, sa, sb = c_ref[...], sa_ref[...], sb_ref[...]
        dqkv_ref[:, :KOFF] = _rope_t(_get_blocks(dqt_ref).T * 0.125, c, sa, sb).astype(BF16)
        dqkv_ref[:, KOFF:VOFF] = _rope_t(dk_ref[...], c, sa, sb).astype(BF16)
        dqkv_ref[:, VOFF:] = dv_ref[...].astype(BF16)
        g = g_ref[...]
        xh, r, h = _rms(x_ref[...], g)
        ht_ref[...] = h.astype(BF16).T
        dh = _dot_nt(dqkv_ref[...], w_ref[...])
        dx, dg = _rms_bwd(dh, xh, r, g)
        dx_ref[...] = dxo_ref[...] + dx
        dg_ref[...] += dg

    return _pc(body, [dqt, dk, dv, x, g, wqkv, *tabs, dx_out], name="attn_pre_bwd", grid=(t // tm,),
               in_specs=[_qblocks(tm), pl.BlockSpec((tm, KVW), lambda i: (i + 1, 0)), pl.BlockSpec((tm, KVW), lambda i: (i + 1, 0)),
                         _rows(tm, D), _acc(1, D), WHOLE, _rows(tm, 128), _rows(tm, 128), _rows(tm, 128), _rows(tm, D)],
               out_specs=[_rows(tm, QKV), _cols(tm), _rows(tm, D), _acc(1, D)],
               out_shape=[_sds((t, QKV), BF16), _sds((D, t), BF16), _sds((t, D), F32), _sds((1, D), F32)])[0]


def _ffn(x, g, wgu_ref, wd_ref, sf_ref, actt_ref):
    _, _, h = _rms(x, g)
    hb = h.astype(BF16)
    y = x
    for j in range(4):
        gj = _dot_nt(hb, wgu_ref[j])
        uj = _dot_nt(hb, wgu_ref[j + 4])
        sg = _sigmoid(gj)
        silu = gj * sg
        act = (silu * uj).astype(BF16)
        sf_ref[j] = silu.astype(BF16)
        sf_ref[j + 4] = (sg * (1.0 + gj * (1.0 - sg)) * uj).astype(BF16)
        actt_ref[j] = act.T
        y = y + _dot(act, wd_ref[j])
    return y


def _tcols(nb, tm):
    return pl.BlockSpec((nb, GUB, tm), lambda i: (0, 0, i))


def _ffn_fwd_attn(ot, x, wo, g, wgu, wd, tm, xchg=None):
    t = x.shape[0]

    def body(ot_ref, x_ref, wo_ref, g_ref, wgu_ref, wd_ref, x1_ref, sf_ref, actt_ref, out_ref):
        x1 = x_ref[...] + _dot_tn(_get_blocks(ot_ref), wo_ref[...])
        x1_ref[...] = x1
        out_ref[...] = _ffn(x1, g_ref[...], wgu_ref, wd_ref, sf_ref, actt_ref)

    return _pc(body, [ot, x, wo, g, wgu, wd], name="ffn_fwd0", grid=(t // tm,),
               in_specs=[_qblocks(tm), _rows(tm, D), WHOLE, _acc(1, D), WHOLE, WHOLE],
               out_specs=[_rows(tm, D), _blk3(NDEV, tm, GUB), _tcols(4, tm), _rows(tm, D)],
               out_shape=[_sds((t, D), F32), _sds((NDEV, t, GUB), BF16), _sds((4, GUB, t), BF16), _sds((t, D), F32)], xchg=xchg)


def _ffn_fwd_final(x, g, wgu, wd, g_fin, tgt, tm, xchg=None):
    t = x.shape[0]

    def body(x_ref, g_ref, wgu_ref, wd_ref, gf_ref, t_ref, sf_ref, actt_ref, dx_ref, part_ref):
        @pl.when(pl.program_id(0) == 0)
        def _():
            part_ref[...] = jnp.zeros_like(part_ref)

        gf = gf_ref[...]
        xh, r, y = _rms(_ffn(x_ref[...], g_ref[...], wgu_ref, wd_ref, sf_ref, actt_ref), gf)
        err = y - t_ref[...]
        dx, dg = _rms_bwd(err * (1.0 / D), xh, r, gf)
        dx_ref[...] = dx
        part_ref[0:1, :] += dg
        tok = jnp.sum(err * err, axis=-1, keepdims=True) * (1.0 / D)
        lane = lax.broadcasted_iota(jnp.int32, (1, D), 1)
        part_ref[1:2, :] += jnp.where(lane == 0, 0.5 * jnp.sum(tok, axis=0, keepdims=True), 0.0)

    return _pc(body, [x, g, wgu, wd, g_fin, tgt], name="ffn_fwd1", grid=(t // tm,),
               in_specs=[_rows(tm, D), _acc(1, D), WHOLE, WHOLE, _acc(1, D), _rows(tm, D)],
               out_specs=[_blk3(NDEV, tm, GUB), _tcols(4, tm), _rows(tm, D), _acc(8, D)],
               out_shape=[_sds((NDEV, t, GUB), BF16), _sds((4, GUB, t), BF16), _sds((t, D), F32), _sds((8, D), F32)], xchg=xchg)


def _ffn_bwd(dy, x, sf, g, wgu, wd, tm, name, wo=None, xchg=None):
    t = x.shape[0]

    def body(dy_ref, x_ref, sf_ref, g_ref, wgu_ref, wd_ref, *rest):
        wo_ref = rest[0] if wo is not None else None
        dgut_ref, h_ref, dx_ref, dg_ref = rest[wo is not None:][:4]

        @pl.when(pl.program_id(0) == 0)
        def _():
            dg_ref[...] = jnp.zeros_like(dg_ref)

        dy = dy_ref[...]
        dyb = dy.astype(BF16)
        gn = g_ref[...]
        xh, r, h = _rms(x_ref[...], gn)
        h_ref[...] = h.astype(BF16)
        dh = jnp.zeros_like(dy)
        for j in range(4):
            dact = _dot_nt(dyb, wd_ref[j])
            dgj = (dact * sf_ref[j + 4]).astype(BF16)
            duj = (dact * sf_ref[j]).astype(BF16)
            dgut_ref[j] = dgj.T
            dgut_ref[j + 4] = duj.T
            dh = dh + _dot(dgj, wgu_ref[j]) + _dot(duj, wgu_ref[j + 4])
        dx, dg = _rms_bwd(dh, xh, r, gn)
        dx = dy + dx
        dx_ref[...] = dx
        dg_ref[...] += dg
        if wo is not None:
            _put_blocks(rest[5], _dot(wo_ref[...], dx.astype(BF16).T).astype(BF16))

    extra = wo is not None
    return _pc(body, [dy, x, sf, g, wgu, wd] + [wo] * extra, name=name, grid=(t // tm,),
               in_specs=[_rows(tm, D), _rows(tm, D), _blk3(NDEV, tm, GUB), _acc(1, D), WHOLE, WHOLE] + [WHOLE] * extra,
               out_specs=[_tcols(NDEV, tm), _rows(tm, D), _rows(tm, D), _acc(1, D)] + [_qblocks(tm)] * extra,
               out_shape=[_sds((NDEV, GUB, t), BF16), _sds((t, D), BF16), _sds((t, D), F32), _sds((1, D), F32)]
               + [_sds((t // BLK, D, BLK), BF16)] * extra, xchg=xchg)


def _wgrad(at, b, tk, name, xchg=None, by_block=False):
    if by_block:
        na, (_, ma, _), t = 1, at.shape, at.shape[0] * BLK
        a_spec = pl.BlockSpec((tk // BLK, ma, BLK), lambda j, k: (k, 0, 0))
    else:
        na, ma, t = at.shape
        a_spec = pl.BlockSpec((1, ma, tk), lambda j, k: (j if na > 1 else 0, 0, k))
    nb, _, mb = b.shape
    nk = t // tk

    def body(a_ref, b_ref, out_ref, acc):
        k = pl.program_id(1)

        @pl.when(k == 0)
        def _():
            acc[...] = jnp.zeros_like(acc)

        acc[...] += _dot(_get_blocks(a_ref) if by_block else a_ref[0], b_ref[0])

        @pl.when(k == nk - 1)
        def _():
            out_ref[0] = acc[...].astype(BF16)

    outs, xo = _pc(body, [at, b], name=name, grid=(max(na, nb), nk),
                   in_specs=[a_spec, pl.BlockSpec((1, tk, mb), lambda j, k: (j if nb > 1 else 0, k, 0))],
                   out_specs=[pl.BlockSpec((1, ma, mb), lambda j, k: (j, 0, 0))], out_shape=[_sds((max(na, nb), ma, mb), BF16)],
                   scratch_shapes=[pltpu.VMEM((ma, mb), F32)], xchg=xchg)
    return outs[0] if xchg is None else (outs[0], xo)


def _conv_pre(x, g, w, b, tm, xchg=None):
    t = x.shape[0]

    def body(x_ref, g_ref, w_ref, b_ref, u_ref, glu_ref):
        _, _, h = _rms(x_ref[...], g_ref[...])
        hb = h.astype(BF16)
        for j in range(4):
            a = _dot(hb, w_ref[j]) + b_ref[:, j * PWB:(j + 1) * PWB]
            gt = _dot(hb, w_ref[j + 4]) + b_ref[:, D + j * PWB:D + (j + 1) * PWB]
            u_ref[j] = a.astype(BF16)
            u_ref[j + 4] = gt.astype(BF16)
            glu_ref[:, j * PWB:(j + 1) * PWB] = a * _sigmoid(gt)

    return _pc(body, [x, g, w, b], name="conv_pre", grid=(t // tm,),
               in_specs=[_rows(tm, D), _acc(1, D), WHOLE, _acc(1, 2 * D)], out_specs=[_blk3(NDEV, tm, PWB), _rows(tm, D)],
               out_shape=[_sds((NDEV, t, PWB), BF16), _sds((t, D), F32)], xchg=xchg)


def _halo_specs(t, tm):
    per = tm // HALO
    last = t // HALO - 1
    return [
        pl.BlockSpec((HALO, D), lambda i: (jnp.maximum(i * per - 1, 0), 0)),
        _rows(tm, D),
        pl.BlockSpec((HALO, D), lambda i: (jnp.minimum((i + 1) * per, last), 0)),
    ]


SUB = 8
CCH = 32
CLN = 256


def _fill_shifted(sh, prev, cur, nxt, tm):
    i = pl.program_id(0)
    rows = jnp.concatenate([jnp.where(i == 0, 0.0, prev[...]), cur[...], jnp.where(i == pl.num_programs(0) - 1, 0.0, nxt[...])], axis=0)
    n = tm + 2 * HALO - SUB
    for b in range(SUB):
        sh[b] = rows[b:b + n]


def _shifted(sh, off, r0, c0):
    return sh[off % SUB, r0 + off - off % SUB:r0 + off - off % SUB + CCH, c0:c0 + CLN]


def _conv_mid(glu, wdw, bdw, tm, xchg=None):
    t = glu.shape[0]

    def body(prev, cur, nxt, w_ref, b_ref, out_ref, sh):
        _fill_shifted(sh, prev, cur, nxt, tm)
        for c0 in range(0, D, CLN):
            for r0 in range(0, tm, CCH):
                acc = jnp.broadcast_to(b_ref[:, c0:c0 + CLN], (CCH, CLN))
                for k in range(CW):
                    acc = acc + w_ref[k:k + 1, c0:c0 + CLN] * _shifted(sh, k + HALO - CPAD, r0, c0)
                out_ref[r0:r0 + CCH, c0:c0 + CLN] = acc

    outs, xo = _pc(body, [glu, glu, glu, wdw, bdw], name="conv_mid", grid=(t // tm,),
                   in_specs=_halo_specs(t, tm) + [_acc(32, D), _acc(1, D)], out_specs=[_rows(tm, D)],
                   out_shape=[_sds((t, D), F32)], scratch_shapes=[pltpu.VMEM((SUB, tm + 2 * HALO - SUB, D), F32)], xchg=xchg)
    return outs[0], xo


def _conv_mid_bwd(dcv, glu, wdw, tm, xchg=None):
    t = glu.shape[0]

    def body(dp, dc, dn, gp, gc, gn, w_ref, dglu_ref, dw_ref, dsh, gsh):
        @pl.when(pl.program_id(0) == 0)
        def _():
            dw_ref[...] = jnp.zeros_like(dw_ref)

        _fill_shifted(dsh, dp, dc, dn, tm)
        _fill_shifted(gsh, gp, gc, gn, tm)
        for c0 in range(0, D, CLN):
            for r0 in range(0, tm, CCH):
                acc = jnp.zeros((CCH, CLN), F32)
                for k in range(CW):
                    acc = acc + w_ref[k:k + 1, c0:c0 + CLN] * _shifted(dsh, HALO + CPAD - k, r0, c0)
                dglu_ref[r0:r0 + CCH, c0:c0 + CLN] = acc
            for k in range(CW):
                dwk = jnp.zeros((SUB, CLN), F32)
                for r0 in range(0, tm, CCH):
                    prod = _shifted(dsh, HALO, r0, c0) * _shifted(gsh, k + HALO - CPAD, r0, c0)
                    for r in range(0, CCH, SUB):
                        dwk = dwk + prod[r:r + SUB]
                dw_ref[k:k + 1, c0:c0 + CLN] += jnp.sum(dwk, axis=0, keepdims=True)

    n = tm + 2 * HALO - SUB
    return _pc(body, [dcv, dcv, dcv, glu, glu, glu, wdw], name="conv_mid_bwd", grid=(t // tm,),
               in_specs=_halo_specs(t, tm) + _halo_specs(t, tm) + [_acc(32, D)], out_specs=[_rows(tm, D), _acc(32, D)],
               out_shape=[_sds((t, D), F32), _sds((32, D), F32)],
               scratch_shapes=[pltpu.VMEM((SUB, n, D), F32), pltpu.VMEM((SUB, n, D), F32)], xchg=xchg)


def _ln(cv, lg, lb):
    mu = jnp.mean(cv, axis=-1, keepdims=True)
    cc = cv - mu
    rs = lax.rsqrt(jnp.mean(cc * cc, axis=-1, keepdims=True) + EPS)
    lh = cc * rs
    return lh, rs, lh * lg + lb


def _conv_post(cv, x, lg, lb, w2, b2, tm):
    t = x.shape[0]

    def body(cv_ref, x_ref, lg_ref, lb_ref, w_ref, b_ref, st_ref, out_ref):
        _, _, ln = _ln(cv_ref[...], lg_ref[...], lb_ref[...])
        s = (ln * _sigmoid(ln)).astype(BF16)
        st_ref[...] = s.T
        out_ref[...] = x_ref[...] + _dot(s, w_ref[...]) + b_ref[...]

    return _pc(body, [cv, x, lg, lb, w2, b2], name="conv_post", grid=(t // tm,),
               in_specs=[_rows(tm, D), _rows(tm, D), _acc(1, D), _acc(1, D), WHOLE, _acc(1, D)],
               out_specs=[_cols(tm), _rows(tm, D)], out_shape=[_sds((D, t), BF16), _sds((t, D), F32)])[0]


def _conv_post_bwd(dx, cv, lg, lb, w2, tm, xchg=None):
    t = dx.shape[0]

    def body(dx_ref, cv_ref, lg_ref, lb_ref, w_ref, dcv_ref, part_ref):
        @pl.when(pl.program_id(0) == 0)
        def _():
            part_ref[...] = jnp.zeros_like(part_ref)

        dx = dx_ref[...]
        lg = lg_ref[...]
        lh, rs, ln = _ln(cv_ref[...], lg, lb_ref[...])
        sg = _sigmoid(ln)
        dln = _dot_nt(dx, w_ref[...]) * (sg * (1.0 + ln * (1.0 - sg)))
        dlh = dln * lg
        dcv = rs * (dlh - jnp.mean(dlh, axis=-1, keepdims=True) - lh * jnp.mean(dlh * lh, axis=-1, keepdims=True))
        dcv_ref[...] = dcv
        part_ref[0:1, :] += jnp.sum(dln * lh, axis=0, keepdims=True)
        part_ref[1:2, :] += jnp.sum(dln, axis=0, keepdims=True)
        part_ref[2:3, :] += jnp.sum(dcv, axis=0, keepdims=True)
        part_ref[3:4, :] += jnp.sum(dx, axis=0, keepdims=True)

    return _pc(body, [dx, cv, lg, lb, w2], name="conv_post_bwd", grid=(t // tm,),
               in_specs=[_rows(tm, D), _rows(tm, D), _acc(1, D), _acc(1, D), WHOLE], out_specs=[_rows(tm, D), _acc(8, D)],
               out_shape=[_sds((t, D), F32), _sds((8, D), F32)], xchg=xchg)


def _conv_pre_bwd(dglu, u, x, g, w, dx_out, tm):
    t = x.shape[0]

    def body(dglu_ref, u_ref, x_ref, g_ref, w_ref, dxo_ref, du_ref, ht_ref, dx_ref, dg_ref, db_ref):
        @pl.when(pl.program_id(0) == 0)
        def _():
            dg_ref[...] = jnp.zeros_like(dg_ref)
            db_ref[...] = jnp.zeros_like(db_ref)

        gn = g_ref[...]
        xh, r, h = _rms(x_ref[...], gn)
        ht_ref[...] = h.astype(BF16).T
        dh = jnp.zeros_like(xh)
        for j in range(4):
            a = u_ref[j].astype(F32)
            sg = _sigmoid(u_ref[j + 4].astype(F32))
            dgl = dglu_ref[:, j * PWB:(j + 1) * PWB]
            da = dgl * sg
            dgt = dgl * a * sg * (1.0 - sg)
            db_ref[:, j * PWB:(j + 1) * PWB] += jnp.sum(da, axis=0, keepdims=True)
            db_ref[:, D + j * PWB:D + (j + 1) * PWB] += jnp.sum(dgt, axis=0, keepdims=True)
            da, dgt = da.astype(BF16), dgt.astype(BF16)
            du_ref[j] = da
            du_ref[j + 4] = dgt
            dh = dh + _dot_nt(da, w_ref[j]) + _dot_nt(dgt, w_ref[j + 4])
        dx, dg = _rms_bwd(dh, xh, r, gn)
        dx_ref[...] = dxo_ref[...] + dx
        dg_ref[...] += dg

    return _pc(body, [dglu, u, x, g, w, dx_out], name="conv_pre_bwd", grid=(t // tm,),
               in_specs=[_rows(tm, D), _blk3(NDEV, tm, PWB), _rows(tm, D), _acc(1, D), WHOLE, _rows(tm, D)],
               out_specs=[_blk3(NDEV, tm, PWB), _cols(tm), _rows(tm, D), _acc(1, D), _acc(1, 2 * D)],
               out_shape=[_sds((NDEV, t, PWB), BF16), _sds((D, t), BF16), _sds((t, D), F32), _sds((1, D), F32),
                          _sds((1, 2 * D), F32)])[0]


def _adamw(w, g, m, v):
    m = B1 * m + (1.0 - B1) * g
    v = B2 * v + (1.0 - B2) * (g * g)
    m_hat = m / (1.0 - B1 ** STEP)
    v_hat = v / (1.0 - B2 ** STEP)
    return -LR * (m_hat / (jnp.sqrt(v_hat) + AEPS) + WD * w), m, v


def _reduce_adamw(lands, w, m, v, tr, name, xchg=None):
    nl, r, c = w.shape

    def body(*refs):
        l_refs, (w_ref, m_ref, v_ref, g_ref, d_ref, nm_ref, nv_ref) = refs[:nl], refs[nl:]

        def total(ref):
            g = ref[0].astype(F32)
            for j in range(1, NDEV):
                g = g + ref[j].astype(F32)
            return g

        g = total(l_refs[0])
        for n in range(1, nl):
            g = jnp.where(pl.program_id(0) == n, total(l_refs[n]), g)
        g_ref[0] = g
        d_ref[0], nm_ref[0], nv_ref[0] = _adamw(w_ref[0], g, m_ref[0], v_ref[0])

    layer = pl.BlockSpec((1, tr, c), lambda l, i: (l, i, 0))
    outs, xo = _pc(body, [*lands, w, m, v], name=name, grid=(nl, r // tr),
                   in_specs=[pl.BlockSpec((NDEV, tr, c), lambda l, i: (0, i, 0))] * nl + [layer] * 3, out_specs=[layer] * 4,
                   out_shape=[_sds((nl, r, c), F32)] * 4, xchg=xchg)
    return outs if xchg is None else (outs, xo)


def _sum_parts(parts):
    _, r, c = parts.shape

    def body(p_ref, out_ref):
        s = p_ref[0]
        for j in range(1, NDEV):
            s = s + p_ref[j]
        out_ref[...] = s

    return _pc(body, [parts], name="sum_parts", in_specs=[WHOLE], out_specs=[WHOLE], out_shape=[_sds((r, c), F32)])[0][0]


def _adamw_small(w, g, m, v):
    def body(w_ref, g_ref, m_ref, v_ref, d_ref, nm_ref, nv_ref):
        d_ref[...], nm_ref[...], nv_ref[...] = _adamw(w_ref[...], g_ref[...], m_ref[...], v_ref[...])

    return _pc(body, [w, g, m, v], name="adamw_small", in_specs=[WHOLE] * 4, out_specs=[WHOLE] * 3,
               out_shape=[_sds(w.shape, F32)] * 3)[0]


def _rows128(a):
    a = a.astype(F32)
    if a.shape[-1] % 128:
        a = jnp.pad(a, [(0, 0)] * (a.ndim - 1) + [(0, 128 - a.shape[-1] % 128)])
    return a.reshape(-1, 128)


def _pack(arrs, rows):
    p = jnp.concatenate([_rows128(a) for a in arrs], axis=0)
    return jnp.pad(p, ((0, rows - p.shape[0]), (0, 0)))


SMALL = ("attn_norm", "ffn_norm", "final_norm", "attn_sink", "conv_norm", "conv_b_dw", "conv_ln_g", "conv_ln_b", "conv_b_pw2",
         "conv_b_pw1", "conv_w_dw")
SMALL_ROWS = 72


def _pack_small(d):
    return _pack([d[k] for k in SMALL], SMALL_ROWS)


def _unpack_small(p, like):
    out, r = {}, 0
    for k in SMALL:
        shp = like[k].shape
        n = -(-shp[-1] // 128) * (math.prod(shp[:-1]))
        blk = p[r:r + n]
        if shp[-1] % 128:
            blk = blk[:, :shp[-1]]
        out[k] = blk.reshape(shp)
        r += n
    return out


NAMES = ("attn_norm", "attn_w_qkv", "attn_w_o", "attn_sink", "conv_norm", "conv_w_pw1", "conv_b_pw1", "conv_w_dw", "conv_b_dw",
         "conv_ln_g", "conv_ln_b", "conv_w_pw2", "conv_b_pw2", "ffn_norm", "ffn_w_gu", "ffn_w_down", "final_norm")
TM = 256
TL = 512
TK = 2048


def _gu_t(a):
    return jnp.swapaxes(a, 1, 2)


def kernel(x, attn_norm, attn_w_qkv, attn_w_o, attn_sink, conv_norm, conv_w_pw1, conv_b_pw1, conv_w_dw, conv_b_dw, conv_ln_g, conv_ln_b, conv_w_pw2, conv_b_pw2, ffn_norm, ffn_w_gu, ffn_w_down, final_norm, loss_target, m_attn_norm, m_attn_w_qkv, m_attn_w_o, m_attn_sink, m_conv_norm, m_conv_w_pw1, m_conv_b_pw1, m_conv_w_dw, m_conv_b_dw, m_conv_ln_g, m_conv_ln_b, m_conv_w_pw2, m_conv_b_pw2, m_ffn_norm, m_ffn_w_gu, m_ffn_w_down, m_final_norm, v_attn_norm, v_attn_w_qkv, v_attn_w_o, v_attn_sink, v_conv_norm, v_conv_w_pw1, v_conv_b_pw1, v_conv_w_dw, v_conv_b_dw, v_conv_ln_g, v_conv_ln_b, v_conv_w_pw2, v_conv_b_pw2, v_ffn_norm, v_ffn_w_gu, v_ffn_w_down, v_final_norm):
    w = dict(zip(NAMES, (attn_norm, attn_w_qkv, attn_w_o, attn_sink, conv_norm, conv_w_pw1, conv_b_pw1, conv_w_dw, conv_b_dw, conv_ln_g,
                         conv_ln_b, conv_w_pw2, conv_b_pw2, ffn_norm, ffn_w_gu, ffn_w_down, final_norm)))
    m = dict(zip(NAMES, (m_attn_norm, m_attn_w_qkv, m_attn_w_o, m_attn_sink, m_conv_norm, m_conv_w_pw1, m_conv_b_pw1, m_conv_w_dw,
                         m_conv_b_dw, m_conv_ln_g, m_conv_ln_b, m_conv_w_pw2, m_conv_b_pw2, m_ffn_norm, m_ffn_w_gu, m_ffn_w_down,
                         m_final_norm)))
    v = dict(zip(NAMES, (v_attn_norm, v_attn_w_qkv, v_attn_w_o, v_attn_sink, v_conv_norm, v_conv_w_pw1, v_conv_b_pw1, v_conv_w_dw,
                         v_conv_b_dw, v_conv_ln_g, v_conv_ln_b, v_conv_w_pw2, v_conv_b_pw2, v_ffn_norm, v_ffn_w_gu, v_ffn_w_down,
                         v_final_norm)))
    me = 4 * lax.axis_index("x") + 2 * lax.axis_index("y") + lax.axis_index("c")
    for d in (w, m, v):
        d["ffn_w_gu"] = _gu_t(d["ffn_w_gu"])
    sh = {k: w[k][0].astype(BF16) for k in ("attn_w_qkv", "attn_w_o", "conv_w_pw1", "conv_w_pw2")}
    gu_b, down_b = w["ffn_w_gu"].astype(BF16), w["ffn_w_down"].astype(BF16)
    sh.update(ffn_w_gu0=gu_b[0], ffn_w_gu1=gu_b[1], ffn_w_down0=down_b[0], ffn_w_down1=down_b[1])
    x0, tgt = x[0], loss_target[0]
    t = x0.shape[0]
    tabs = _rope_tables(t)
    tl = min(TL, t)
    g_a, sink, g_f0, g_f1, g_fin = w["attn_norm"], w["attn_sink"][0], w["ffn_norm"][0:1], w["ffn_norm"][1:2], w["final_norm"][None]
    gather, scatter = False, True

    shard_rows = _pack([w["conv_w_dw"][0], jnp.zeros((1, 128), F32), w["conv_norm"], w["conv_b_dw"], w["conv_ln_g"], w["conv_ln_b"],
                        w["conv_b_pw2"], w["conv_b_pw1"]], 40)
    wqkv_g, sm = _exchange([sh["attn_w_qkv"], shard_rows], [gather] * 2, "gather_attn")
    wqkv = wqkv_g.transpose(1, 0, 2).reshape(D, QKV)

    def full_vec(row, n=1):
        return sm[:, row:row + n, :].reshape(1, NDEV * n * 128)

    w_dw, g_c, b_dw, ln_g, ln_b = sm[:, 0:32, :].transpose(1, 0, 2).reshape(32, D), full_vec(32), full_vec(33), full_vec(34), full_vec(35)
    b_pw2, b_pw1 = full_vec(36), full_vec(37, 2)

    (q_t, kv), (wo_g, wd0_g) = _attn_pre(x0, g_a, wqkv, tabs, tl, xchg=([sh["attn_w_o"], sh["ffn_w_down0"]], [gather] * 2))
    o_t, (wgu0,) = _attn_fwd(q_t, kv, sink, xchg=([sh["ffn_w_gu0"]], [gather]))
    wo, wd0 = wo_g.reshape(D, D), wd0_g.reshape(4, GUB, D)
    (x1, sf0, act0_t, x2), (wpw1, wgu1) = _ffn_fwd_attn(
        o_t, x0, wo, g_f0, wgu0, wd0, tl, xchg=([sh["conv_w_pw1"], sh["ffn_w_gu1"]], [gather] * 2))
    (u, glu), (wpw2_g,) = _conv_pre(x2, g_c, wpw1, b_pw1, tl, xchg=([sh["conv_w_pw2"]], [gather]))
    wpw2 = wpw2_g.reshape(D, D)
    cv, (wd1_g,) = _conv_mid(glu, w_dw, b_dw, tl, xchg=([sh["ffn_w_down1"]], [gather]))
    wd1 = wd1_g.reshape(4, GUB, D)
    s_t, x3 = _conv_post(cv, x2, ln_g, ln_b, wpw2, b_pw2, tl)
    (sf1, act1_t, dx4, fin), _ = _ffn_fwd_final(x3, g_f1, wgu1, wd1, g_fin, tgt, tl)

    land = {}
    tk, tk2 = min(TK, t), min(2 * TK, t)
    (dgu1_t, h3, dx3, dg_f1), _ = _ffn_bwd(dx4, x3, sf1, g_f1, wgu1, wd1, TM, "ffn_bwd1")
    dwgu1 = _wgrad(dgu1_t, h3[None], tk2, "dwgu1")
    dwd1 = _wgrad(act1_t, dx4[None], tk, "dwd1").reshape(NDEV, DFF // NDEV, D)
    (dcv, cpart), (land["ffn_w_down1"],) = _conv_post_bwd(dx3, cv, ln_g, ln_b, wpw2, tl, xchg=([dwd1], [scatter]))
    dwpw2 = _wgrad(s_t[None], dx3[None], tk, "dwpw2").reshape(NDEV, D // NDEV, D)
    (dglu, dw_dw), (land["ffn_w_gu1"], land["conv_w_pw2"]) = _conv_mid_bwd(dcv, glu, w_dw, TM, xchg=([dwgu1, dwpw2], [scatter] * 2))
    du, h2_t, dx2, dg_c, db_pw1 = _conv_pre_bwd(dglu, u, x2, g_c, wpw1, dx3, tl)
    dwpw1 = _wgrad(h2_t[None], du, tk2, "dwpw1")
    (dgu0_t, h1, dx1, dg_f0, do_t), (land["conv_w_pw1"],) = _ffn_bwd(
        dx2, x1, sf0, g_f0, wgu0, wd0, TM, "ffn_bwd0", wo=wo, xchg=([dwpw1], [scatter]))
    dwgu0 = _wgrad(dgu0_t, h1[None], tk2, "dwgu0")
    dwd0 = _wgrad(act0_t, dx2[None], tk, "dwd0").reshape(NDEV, DFF // NDEV, D)
    dwo = _wgrad(o_t, dx1[None], tk, "dwo", by_block=True).reshape(NDEV, D // NDEV, D)
    (dq_t, dk, dv, dsink), (land["ffn_w_gu0"], land["ffn_w_down0"]) = _attn_bwd(
        q_t, kv, do_t, sink, tl, xchg=([dwgu0, dwd0], [scatter] * 2))
    dqkv, h0_t, dx0, dg_a = _attn_pre_bwd(dq_t, dk, dv, x0, g_a, wqkv, tabs, dx1, tl)
    dwqkv, (land["attn_w_o"],) = _wgrad(h0_t[None], dqkv[None], tk, "dwqkv", xchg=([dwo], [scatter]))
    dwqkv = dwqkv[0].reshape(D, NDEV, QKV // NDEV).transpose(1, 0, 2)

    lane0 = (lax.broadcasted_iota(jnp.int32, (1, 128), 1) == 0).astype(F32)
    parts = _pack([dg_a, dg_f0, dg_f1, fin[0:1], dsink[0:1, :NH], fin[1, 0] * lane0, jnp.zeros((6, 128), F32), dg_c, cpart[2:3],
                   cpart[0:1], cpart[1:2], cpart[3:4], db_pw1, dw_dw.reshape(32, NDEV, 128)], 352)
    land["attn_w_qkv"], parts_g = _exchange([dwqkv, parts], [scatter, gather], "scatter_attn")
    red = _sum_parts(parts_g)

    def shard_rows_of(row, n=1):
        return lax.dynamic_slice_in_dim(red, row + n * me, n, axis=0)

    gs = {
        "attn_norm": red[0:8].reshape(1, D), "ffn_norm": red[8:24].reshape(2, D), "final_norm": red[24:32].reshape(D),
        "attn_sink": red[32:33, :NH], "conv_norm": shard_rows_of(40), "conv_b_dw": shard_rows_of(48), "conv_ln_g": shard_rows_of(56),
        "conv_ln_b": shard_rows_of(64), "conv_b_pw2": shard_rows_of(72), "conv_b_pw1": shard_rows_of(80, 2).reshape(1, PWB),
        "conv_w_dw": lax.dynamic_index_in_dim(red[96:352].reshape(32, NDEV, 128), me, axis=1, keepdims=False)[None, :CW],
    }
    loss = red[33, 0]

    grads, deltas, new_m, new_v = dict(gs), {}, {}, {}
    ds, ms, vs = _adamw_small(_pack_small(w), _pack_small(gs), _pack_small(m), _pack_small(v))
    deltas.update(_unpack_small(ds, gs))
    new_m.update(_unpack_small(ms, gs))
    new_v.update(_unpack_small(vs, gs))
    for k in ("attn_w_qkv", "attn_w_o", "conv_w_pw1", "conv_w_pw2", "ffn_w_gu", "ffn_w_down"):
        lands = [land[k + "0"], land[k + "1"]] if k.startswith("ffn") else [land[k]]
        tr = {1024: 256, 128: 128, 352: 176, GUB: 176}[w[k].shape[1]]
        grads[k], deltas[k], new_m[k], new_v[k] = _reduce_adamw(lands, w[k], m[k], v[k], tr, "adamw_" + k)
    for d in (grads, deltas, new_m, new_v):
        d["ffn_w_gu"] = _gu_t(d["ffn_w_gu"])
    return (loss, dx0[None], *[grads[k] for k in NAMES], *[deltas[k] for k in NAMES], *[new_m[k] for k in NAMES],
            *[new_v[k] for k in NAMES])
```

```python
import math

import jax
import jax.numpy as jnp
from jax import lax
from jax.experimental import pallas as pl
from jax.experimental.pallas import tpu as pltpu

F32 = jnp.float32
BF16 = jnp.bfloat16

D = 1024
NH = 16
NKV = 4
HD = 64
GROUP = NH // NKV
ROT = 16
THETA = 500000.0
BLK = 128
QKV = (NH + 2 * NKV) * HD
KOFF = NH * HD
VOFF = KOFF + NKV * HD
DFF = 2816
NDEV = 8
GUB = 2 * DFF // NDEV
PWB = 2 * D // NDEV
CW = 31
CPAD = 15
HALO = 16
EPS = 1e-6
NEG = -1e30
LR, B1, B2, AEPS, WD, STEP = 0.001, 0.9, 0.999, 1e-08, 0.01, 10

VMEM_LIMIT = 56 * 1024 * 1024
MESH = pl.DeviceIdType.MESH
WHOLE = pl.BlockSpec(memory_space=pltpu.VMEM)
ANY = pl.BlockSpec(memory_space=pl.ANY)


def _place():
    x, y, c = lax.axis_index("x"), lax.axis_index("y"), lax.axis_index("c")
    return x, y, c, 4 * x + 2 * y + c


def _peer(x, y, c, j):
    px = 1 - x if j & 4 else x
    py = 1 - y if j & 2 else y
    pc = 1 - c if j & 1 else c
    return (px, py, pc), 4 * px + 2 * py + pc


SIBLING = 1
OTHER_CHIPS = (2, 4, 6)
PASS_ON_LEAD = 3


class _Exchange:
    def __init__(self, src, dst, scatter, send, recv, loc):
        self.src, self.dst, self.scatter, self.send, self.recv, self.loc = src, dst, scatter, send, recv, loc
        self.x, self.y, self.c, self.me = _place()

    def _remote(self, k, j, s, d, to):
        peer, _ = _peer(self.x, self.y, self.c, to)
        return pltpu.make_async_remote_copy(src_ref=s, dst_ref=d, send_sem=self.send.at[k, j - 1], recv_sem=self.recv.at[k, j - 1],
                                            device_id=peer, device_id_type=MESH)

    def _slot(self, j):
        return _peer(self.x, self.y, self.c, j)[1]

    def local(self, k):
        if self.scatter[k]:
            return pltpu.make_async_copy(self.src[k].at[self.me], self.dst[k].at[0], self.loc.at[k])
        return pltpu.make_async_copy(self.src[k], self.dst[k].at[self.me], self.loc.at[k])

    def direct(self, k, j):
        if self.scatter[k]:
            return self._remote(k, j, self.src[k].at[self._slot(j)], self.dst[k].at[j], j)
        return self._remote(k, j, self.src[k], self.dst[k].at[self.me], j)

    def passed_on(self, k, j):
        rows = self.dst[k].at[self._slot(j)]
        return self._remote(k, j + 1, rows, rows, SIBLING)

    def arrival(self, k, j):
        rows = self.dst[k].at[j if self.scatter[k] else self._slot(j)]
        return self._remote(k, j, rows, rows, j)

    def sent(self, k):
        return tuple(range(1, NDEV)) if self.scatter[k] else (SIBLING,) + OTHER_CHIPS

    def start(self):
        for k in range(len(self.src)):
            self.local(k).start()
            for j in self.sent(k):
                self.direct(k, j).start()

    def pass_on(self):
        for k in range(len(self.src)):
            if not self.scatter[k]:
                for j in OTHER_CHIPS:
                    self.arrival(k, j).wait_recv()
                    self.passed_on(k, j).start()

    def finish(self):
        for k in range(len(self.src)):
            for j in range(1, NDEV):
                if self.scatter[k] or j not in OTHER_CHIPS:
                    self.arrival(k, j).wait_recv()
            for j in self.sent(k):
                self.direct(k, j).wait_send()
            if not self.scatter[k]:
                for j in OTHER_CHIPS:
                    self.passed_on(k, j).wait_send()
            self.local(k).wait()


def _call(body, **kw):
    return pl.pallas_call(body, **kw)


def _pc(body, operands, *, name, in_specs, out_specs, out_shape, grid=None, scratch_shapes=(), xchg=None):
    kw = dict(name=name, compiler_params=pltpu.CompilerParams(
        dimension_semantics=None if grid is None else ("arbitrary",) * len(grid), vmem_limit_bytes=VMEM_LIMIT,
        has_side_effects=xchg is not None))
    if grid is not None:
        kw["grid"] = grid
    if xchg is None:
        outs = _call(body, in_specs=list(in_specs), out_specs=list(out_specs), out_shape=list(out_shape),
                     scratch_shapes=list(scratch_shapes), **kw)(*operands)
        return list(outs), []
    arrs, scatter = xchg
    nx, n_in, n_out, n_s = len(arrs), len(in_specs), len(out_specs), len(scratch_shapes)

    def wrapped(*refs):
        ins, refs = refs[:n_in], refs[n_in:]
        src, refs = refs[:nx], refs[nx:]
        outs, refs = refs[:n_out], refs[n_out:]
        dst, refs = refs[:nx], refs[nx:]
        scr, (send, recv, loc) = refs[:n_s], refs[n_s:]
        if grid is None:
            _Exchange(src, dst, scatter, send, recv, loc).start()
            body(*ins, *outs, *scr)
            _Exchange(src, dst, scatter, send, recv, loc).pass_on()
            _Exchange(src, dst, scatter, send, recv, loc).finish()
            return

        step, steps = 0, 1
        for axis, n in enumerate(grid):
            step, steps = step * n + pl.program_id(axis), steps * n

        @pl.when(step == 0)
        def _():
            _Exchange(src, dst, scatter, send, recv, loc).start()

        body(*ins, *outs, *scr)

        @pl.when(step == max(steps - 1 - PASS_ON_LEAD, 0))
        def _():
            _Exchange(src, dst, scatter, send, recv, loc).pass_on()

        @pl.when(step == steps - 1)
        def _():
            _Exchange(src, dst, scatter, send, recv, loc).finish()

    xshape = [jax.ShapeDtypeStruct(a.shape if sc else (NDEV,) + a.shape, a.dtype) for a, sc in zip(arrs, scatter)]
    sems = [pltpu.SemaphoreType.DMA((nx, NDEV - 1)), pltpu.SemaphoreType.DMA((nx, NDEV - 1)), pltpu.SemaphoreType.DMA((nx,))]
    res = _call(wrapped, in_specs=list(in_specs) + [ANY] * nx, out_specs=list(out_specs) + [ANY] * nx,
                out_shape=list(out_shape) + xshape, scratch_shapes=list(scratch_shapes) + sems, **kw)(*operands, *arrs)
    return list(res[:n_out]), list(res[n_out:])


def _exchange(arrs, scatter, name):
    return _pc(lambda: None, [], name=name, in_specs=[], out_specs=[], out_shape=[], xchg=(arrs, scatter))[1]


def _rows(tm, width):
    return pl.BlockSpec((tm, width), lambda i: (i, 0))


def _cols(tm):
    return pl.BlockSpec((D, tm), lambda i: (0, i))


def _qblocks(tm):
    return pl.BlockSpec((tm // BLK, D, BLK), lambda i: (i, 0, 0))


def _put_blocks(ref, val):
    for b in range(ref.shape[0]):
        ref[b] = val[:, b * BLK:(b + 1) * BLK]


def _get_blocks(ref):
    return jnp.concatenate([ref[b] for b in range(ref.shape[0])], axis=1)


def _blk3(nb, tm, width):
    return pl.BlockSpec((nb, tm, width), lambda i: (0, i, 0))


def _acc(rows, width):
    return pl.BlockSpec((rows, width), lambda i: (0, 0))


def _sds(shape, dtype):
    return jax.ShapeDtypeStruct(shape, dtype)


def _dot(a, b):
    return jnp.dot(a.astype(BF16), b.astype(BF16), preferred_element_type=F32)


def _dot_nt(a, b):
    return lax.dot_general(a.astype(BF16), b.astype(BF16), (((1,), (1,)), ((), ())), preferred_element_type=F32)


def _dot_tn(a, b):
    return lax.dot_general(a.astype(BF16), b.astype(BF16), (((0,), (0,)), ((), ())), preferred_element_type=F32)


def _sigmoid(x):
    return 1.0 / (1.0 + jnp.exp(-x))


def _rms(x, g):
    r = lax.rsqrt(jnp.mean(x * x, axis=-1, keepdims=True) + EPS)
    xh = x * r
    return xh, r, xh * g


def _rms_bwd(dh, xh, r, g):
    dxh = dh * g
    dg = jnp.sum(dh * xh, axis=0, keepdims=True)
    dx = r * (dxh - xh * jnp.mean(dxh * xh, axis=-1, keepdims=True))
    return dx, dg


def _lanes(t, width):
    return jnp.tile(t, (1, width // t.shape[1]))


def _rope(z, c, sa, sb):
    w = z.shape[1]
    return z * _lanes(c, w) + pltpu.roll(z, w - 8, 1) * _lanes(sa, w) + pltpu.roll(z, 8, 1) * _lanes(sb, w)


def _rope_t(dz, c, sa, sb):
    w = dz.shape[1]
    return dz * _lanes(c, w) + pltpu.roll(dz * _lanes(sa, w), 8, 1) + pltpu.roll(dz * _lanes(sb, w), w - 8, 1)


def _rope_tables(t):
    pos = jnp.arange(t, dtype=F32)
    inv_freq = THETA ** (-jnp.arange(0, ROT, 2, dtype=F32) / ROT)
    ang = pos[:, None] * inv_freq[None, :]
    cos, sin = jnp.cos(ang), jnp.sin(ang)
    one = jnp.ones((t, HD - ROT), F32)
    zero = jnp.zeros((t, HD - ROT), F32)
    z8 = jnp.zeros((t, 8), F32)
    c = jnp.concatenate([cos, cos, one], axis=1)
    sa = jnp.concatenate([-sin, z8, zero], axis=1)
    sb = jnp.concatenate([z8, sin, zero], axis=1)
    return tuple(jnp.tile(a, (1, 2)) for a in (c, sa, sb))


KVW = NKV * HD
GW = GROUP * BLK


def _attn_pre(x, g, wqkv, tabs, tm, xchg=None):
    t = x.shape[0]

    def body(x_ref, g_ref, w_ref, c_ref, sa_ref, sb_ref, qt_ref, kv_ref):
        _, _, h = _rms(x_ref[...], g_ref[...])
        z = _dot(h, w_ref[...])
        c, sa, sb = c_ref[...], sa_ref[...], sb_ref[...]
        _put_blocks(qt_ref, (_rope(z[:, :KOFF], c, sa, sb) * 0.125).T.astype(BF16))
        kv_ref[:, :KVW] = _rope(z[:, KOFF:VOFF], c, sa, sb).astype(BF16)
        kv_ref[:, KVW:] = z[:, VOFF:].astype(BF16)

    return _pc(body, [x, g, wqkv, *tabs], name="attn_pre", grid=(t // tm,),
               in_specs=[_rows(tm, D), _acc(1, D), WHOLE, _rows(tm, 128), _rows(tm, 128), _rows(tm, 128)],
               out_specs=[_qblocks(tm), _rows(tm, 2 * KVW)],
               out_shape=[_sds((t // BLK, D, BLK), BF16), _sds((t, 2 * KVW), BF16)], xchg=xchg)


QB = 4


def _attn_specs(nblk):
    prev = lambda i: jnp.maximum(QB * i - 1, 0)
    nxt = lambda i: jnp.minimum(QB * (i + 1), nblk - 1)
    return [
        _qblocks(QB * BLK),
        pl.BlockSpec((BLK, KVW), lambda i: (prev(i), 0)),
        pl.BlockSpec((QB * BLK, KVW), lambda i: (i, 0)),
        pl.BlockSpec((BLK, KVW), lambda i: (nxt(i), 0)),
        pl.BlockSpec((BLK, KVW), lambda i: (prev(i), 1)),
        pl.BlockSpec((QB * BLK, KVW), lambda i: (i, 1)),
        pl.BlockSpec((BLK, KVW), lambda i: (nxt(i), 1)),
    ]


def _attn_bias():
    c = lax.broadcasted_iota(jnp.int32, (3, 3 * BLK, GW), 1)
    r = lax.broadcasted_iota(jnp.int32, (3, 3 * BLK, GW), 2) & (BLK - 1)
    slab = lax.broadcasted_iota(jnp.int32, (3, 3 * BLK, GW), 0)
    valid = (c >= r) & (c - 2 * BLK <= r) & ((slab != 0) | (c >= BLK)) & ((slab != 2) | (c < 2 * BLK))
    return jnp.where(valid, 0.0, NEG).astype(F32)


def _bias_of(bias_ref, blk, nblk):
    return bias_ref[jnp.where(blk == 0, 0, jnp.where(blk == nblk - 1, 2, 1))]


def _group(ref, b, kv):
    return jnp.concatenate([ref[b, (kv * GROUP + g) * HD:(kv * GROUP + g + 1) * HD, :] for g in range(GROUP)], axis=1)


def _group_sink(sink_ref, kv):
    return jnp.concatenate([jnp.full((1, BLK), sink_ref[kv * GROUP + g], F32) for g in range(GROUP)], axis=1)


def _attn_exp(k_h, qt_g, bias, sink_g):
    s = _dot(k_h, qt_g) + bias
    m = jnp.maximum(jnp.max(s, axis=0, keepdims=True), sink_g)
    e = jnp.exp(s - m)
    es = jnp.exp(sink_g - m)
    return e, 1.0 / (jnp.sum(e, axis=0, keepdims=True) + es), es


def _attn_fwd(qt, kv, sink, xchg=None):
    t = kv.shape[0]
    nblk = t // BLK

    def body(sink_ref, bias_ref, qt_ref, kp, kc, kn, vp, vc, vn, ot_ref):
        k = jnp.concatenate([kp[...], kc[...], kn[...]], axis=0)
        v = jnp.concatenate([vp[...], vc[...], vn[...]], axis=0)
        for b in range(QB):
            bias = _bias_of(bias_ref, QB * pl.program_id(0) + b, nblk)
            keys = slice(b * BLK, (b + 3) * BLK)
            for h in range(NKV):
                e, inv, _ = _attn_exp(k[keys, h * HD:(h + 1) * HD], _group(qt_ref, b, h), bias, _group_sink(sink_ref, h))
                ot_g = (_dot_tn(v[keys, h * HD:(h + 1) * HD], e) * inv).astype(BF16)
                for g in range(GROUP):
                    ot_ref[b, (h * GROUP + g) * HD:(h * GROUP + g + 1) * HD, :] = ot_g[:, g * BLK:(g + 1) * BLK]

    outs, xo = _pc(body, [sink, _attn_bias(), qt] + [kv] * 6, name="attn_fwd", grid=(nblk // QB,),
                   in_specs=[pl.BlockSpec(memory_space=pltpu.SMEM), WHOLE] + _attn_specs(nblk),
                   out_specs=[_qblocks(QB * BLK)], out_shape=[_sds((nblk, D, BLK), BF16)], xchg=xchg)
    return outs[0], xo


def _attn_bwd(qt, kv, dot, sink, pad, xchg=None):
    t = kv.shape[0]
    nblk = t // BLK

    def body(sink_ref, bias_ref, qt_ref, kp, kc, kn, vp, vc, vn, dot_ref, dqt_ref, dk_ref, dv_ref, ds_ref):
        i = pl.program_id(0)

        @pl.when(i == 0)
        def _():
            dk_ref[...] = jnp.zeros_like(dk_ref)
            dv_ref[...] = jnp.zeros_like(dv_ref)
            ds_ref[...] = jnp.zeros_like(ds_ref)

        k = jnp.concatenate([kp[...], kc[...], kn[...]], axis=0)
        v = jnp.concatenate([vp[...], vc[...], vn[...]], axis=0)
        lane = lax.broadcasted_iota(jnp.int32, (1, 128), 1)
        dsink = jnp.zeros((1, 128), F32)
        for b in range(QB):
            blk = QB * i + b
            bias = _bias_of(bias_ref, blk, nblk)
            keys = slice(b * BLK, (b + 3) * BLK)
            rows = pl.ds(pl.multiple_of(blk * BLK + (pad - BLK), BLK), 3 * BLK)
            for h in range(NKV):
                k_h, v_h = k[keys, h * HD:(h + 1) * HD], v[keys, h * HD:(h + 1) * HD]
                qt_g, dot_g = _group(qt_ref, b, h), _group(dot_ref, b, h)
                e, inv, es = _attn_exp(k_h, qt_g, bias, _group_sink(sink_ref, h))
                p, ps = e * inv, es * inv
                dp = _dot(v_h, dot_g)
                delta = jnp.sum(p * dp, axis=0, keepdims=True)
                ds = (p * (dp - delta)).astype(BF16)
                dqt_g = _dot_tn(k_h, ds)
                dk_ref[rows, h * HD:(h + 1) * HD] += _dot_nt(ds, qt_g)
                dv_ref[rows, h * HD:(h + 1) * HD] += _dot_nt(p, dot_g)
                psd = ps * delta
                for g in range(GROUP):
                    n = h * GROUP + g
                    dqt_ref[b, n * HD:(n + 1) * HD, :] = dqt_g[:, g * BLK:(g + 1) * BLK]
                    dsink = dsink - jnp.where(lane == n, jnp.sum(psd[:, g * BLK:(g + 1) * BLK], axis=1, keepdims=True), 0.0)
        ds_ref[0:1, :] += dsink

    outs, xo = _pc(body, [sink, _attn_bias(), qt] + [kv] * 6 + [dot], name="attn_bwd", grid=(nblk // QB,),
                   in_specs=[pl.BlockSpec(memory_space=pltpu.SMEM), WHOLE] + _attn_specs(nblk) + [_qblocks(QB * BLK)],
                   out_specs=[_qblocks(QB * BLK), _acc(t + 2 * pad, KVW), _acc(t + 2 * pad, KVW), _acc(8, 128)],
                   out_shape=[_sds((nblk, D, BLK), F32), _sds((t + 2 * pad, KVW), F32), _sds((t + 2 * pad, KVW), F32),
                              _sds((8, 128), F32)], xchg=xchg)
    return outs, xo


def _attn_pre_bwd(dqt, dk, dv, x, g, wqkv, tabs, dx_out, tm):
    t = x.shape[0]

    def body(dqt_ref, dk_ref, dv_ref, x_ref, g_ref, w_ref, c_ref, sa_ref, sb_ref, dxo_ref, dqkv_ref, ht_ref, dx_ref, dg_ref):
        @pl.when(pl.program_id(0) == 0)
        def _():
            dg_ref[...] = jnp.zeros_like(dg_ref)

        c, sa, sb = c_ref[...], sa_ref[...], sb_ref[...]
        dqkv_ref[:, :KOFF] = _rope_t(_get_blocks(dqt_ref).T * 0.125, c, sa, sb).astype(BF16)
        dqkv_ref[:, KOFF:VOFF] = _rope_t(dk_ref[...], c, sa, sb).astype(BF16)
        dqkv_ref[:, VOFF:] = dv_ref[...].astype(BF16)
        g = g_ref[...]
        xh, r, h = _rms(x_ref[...], g)
        ht_ref[...] = h.astype(BF16).T
        dh = _dot_nt(dqkv_ref[...], w_ref[...])
        dx, dg = _rms_bwd(dh, xh, r, g)
        dx_ref[...] = dxo_ref[...] + dx
        dg_ref[...] += dg

    return _pc(body, [dqt, dk, dv, x, g, wqkv, *tabs, dx_out], name="attn_pre_bwd", grid=(t // tm,),
               in_specs=[_qblocks(tm), pl.BlockSpec((tm, KVW), lambda i: (i + 1, 0)), pl.BlockSpec((tm, KVW), lambda i: (i + 1, 0)),
                         _rows(tm, D), _acc(1, D), WHOLE, _rows(tm, 128), _rows(tm, 128), _rows(tm, 128), _rows(tm, D)],
               out_specs=[_rows(tm, QKV), _cols(tm), _rows(tm, D), _acc(1, D)],
               out_shape=[_sds((t, QKV), BF16), _sds((D, t), BF16), _sds((t, D), F32), _sds((1, D), F32)])[0]


def _ffn(x, g, wgu_ref, wd_ref, sf_ref, actt_ref):
    _, _, h = _rms(x, g)
    hb = h.astype(BF16)
    y = x
    for j in range(4):
        gj = _dot_nt(hb, wgu_ref[j])
        uj = _dot_nt(hb, wgu_ref[j + 4])
        sg = _sigmoid(gj)
        silu = gj * sg
        act = (silu * uj).astype(BF16)
        sf_ref[j] = silu.astype(BF16)
        sf_ref[j + 4] = (sg * (1.0 + gj * (1.0 - sg)) * uj).astype(BF16)
        actt_ref[j] = act.T
        y = y + _dot(act, wd_ref[j])
    return y


def _tcols(nb, tm):
    return pl.BlockSpec((nb, GUB, tm), lambda i: (0, 0, i))


def _ffn_fwd_attn(ot, x, wo, g, wgu, wd, tm, xchg=None):
    t = x.shape[0]

    def body(ot_ref, x_ref, wo_ref, g_ref, wgu_ref, wd_ref, x1_ref, sf_ref, actt_ref, out_ref):
        x1 = x_ref[...] + _dot_tn(_get_blocks(ot_ref), wo_ref[...])
        x1_ref[...] = x1
        out_ref[...] = _ffn(x1, g_ref[...], wgu_ref, wd_ref, sf_ref, actt_ref)

    return _pc(body, [ot, x, wo, g, wgu, wd], name="ffn_fwd0", grid=(t // tm,),
               in_specs=[_qblocks(tm), _rows(tm, D), WHOLE, _acc(1, D), WHOLE, WHOLE],
               out_specs=[_rows(tm, D), _blk3(NDEV, tm, GUB), _tcols(4, tm), _rows(tm, D)],
               out_shape=[_sds((t, D), F32), _sds((NDEV, t, GUB), BF16), _sds((4, GUB, t), BF16), _sds((t, D), F32)], xchg=xchg)


def _ffn_fwd_final(x, g, wgu, wd, g_fin, tgt, tm, xchg=None):
    t = x.shape[0]

    def body(x_ref, g_ref, wgu_ref, wd_ref, gf_ref, t_ref, sf_ref, actt_ref, dx_ref, dxb_ref, part_ref):
        @pl.when(pl.program_id(0) == 0)
        def _():
            part_ref[...] = jnp.zeros_like(part_ref)

        gf = gf_ref[...]
        xh, r, y = _rms(_ffn(x_ref[...], g_ref[...], wgu_ref, wd_ref, sf_ref, actt_ref), gf)
        err = y - t_ref[...]
        dx, dg = _rms_bwd(err * (1.0 / D), xh, r, gf)
        dx_ref[...] = dx
        dxb_ref[...] = dx.astype(BF16)
        part_ref[0:1, :] += dg
        tok = jnp.sum(err * err, axis=-1, keepdims=True) * (1.0 / D)
        lane = lax.broadcasted_iota(jnp.int32, (1, D), 1)
        part_ref[1:2, :] += jnp.where(lane == 0, 0.5 * jnp.sum(tok, axis=0, keepdims=True), 0.0)

    return _pc(body, [x, g, wgu, wd, g_fin, tgt], name="ffn_fwd1", grid=(t // tm,),
               in_specs=[_rows(tm, D), _acc(1, D), WHOLE, WHOLE, _acc(1, D), _rows(tm, D)],
               out_specs=[_blk3(NDEV, tm, GUB), _tcols(4, tm), _rows(tm, D), _rows(tm, D), _acc(8, D)],
               out_shape=[_sds((NDEV, t, GUB), BF16), _sds((4, GUB, t), BF16), _sds((t, D), F32), _sds((t, D), BF16),
                          _sds((8, D), F32)], xchg=xchg)


def _ffn_bwd(dy, x, sf, g, wgu, wd, tm, name, wo=None, xchg=None):
    t = x.shape[0]

    def body(dy_ref, x_ref, sf_ref, g_ref, wgu_ref, wd_ref, *rest):
        wo_ref = rest[0] if wo is not None else None
        dgut_ref, h_ref, dx_ref, dg_ref = rest[wo is not None:][:4]

        @pl.when(pl.program_id(0) == 0)
        def _():
            dg_ref[...] = jnp.zeros_like(dg_ref)

        dy = dy_ref[...]
        dyb = dy.astype(BF16)
        gn = g_ref[...]
        xh, r, h = _rms(x_ref[...], gn)
        h_ref[...] = h.astype(BF16)
        dh = jnp.zeros_like(dy)
        for j in range(4):
            dact = _dot_nt(dyb, wd_ref[j])
            dgj = (dact * sf_ref[j + 4]).astype(BF16)
            duj = (dact * sf_ref[j]).astype(BF16)
            dgut_ref[j] = dgj.T
            dgut_ref[j + 4] = duj.T
            dh = dh + _dot(dgj, wgu_ref[j]) + _dot(duj, wgu_ref[j + 4])
        dx, dg = _rms_bwd(dh, xh, r, gn)
        dx = dy + dx
        dx_ref[...] = dx
        dg_ref[...] += dg
        if wo is not None:
            _put_blocks(rest[5], _dot(wo_ref[...], dx.astype(BF16).T).astype(BF16))

    extra = wo is not None
    return _pc(body, [dy, x, sf, g, wgu, wd] + [wo] * extra, name=name, grid=(t // tm,),
               in_specs=[_rows(tm, D), _rows(tm, D), _blk3(NDEV, tm, GUB), _acc(1, D), WHOLE, WHOLE] + [WHOLE] * extra,
               out_specs=[_tcols(NDEV, tm), _rows(tm, D), _rows(tm, D), _acc(1, D)] + [_qblocks(tm)] * extra,
               out_shape=[_sds((NDEV, GUB, t), BF16), _sds((t, D), BF16), _sds((t, D), F32), _sds((1, D), F32)]
               + [_sds((t // BLK, D, BLK), BF16)] * extra, xchg=xchg)


def _wgrad(at, b, tk, name, xchg=None, by_block=False):
    if by_block:
        na, (_, ma, _), t = 1, at.shape, at.shape[0] * BLK
        a_spec = pl.BlockSpec((tk // BLK, ma, BLK), lambda j, k: (k, 0, 0))
    else:
        na, ma, t = at.shape
        a_spec = pl.BlockSpec((1, ma, tk), lambda j, k: (j if na > 1 else 0, 0, k))
    nb, _, mb = b.shape
    nk = t // tk

    def body(a_ref, b_ref, out_ref, acc):
        k = pl.program_id(1)

        @pl.when(k == 0)
        def _():
            acc[...] = jnp.zeros_like(acc)

        acc[...] += _dot(_get_blocks(a_ref) if by_block else a_ref[0], b_ref[0])

        @pl.when(k == nk - 1)
        def _():
            out_ref[0] = acc[...].astype(BF16)

    outs, xo = _pc(body, [at, b], name=name, grid=(max(na, nb), nk),
                   in_specs=[a_spec, pl.BlockSpec((1, tk, mb), lambda j, k: (j if nb > 1 else 0, k, 0))],
                   out_specs=[pl.BlockSpec((1, ma, mb), lambda j, k: (j, 0, 0))], out_shape=[_sds((max(na, nb), ma, mb), BF16)],
                   scratch_shapes=[pltpu.VMEM((ma, mb), F32)], xchg=xchg)
    return outs[0] if xchg is None else (outs[0], xo)


def _conv_pre(x, g, w, b, tm, xchg=None):
    t = x.shape[0]

    def body(x_ref, g_ref, w_ref, b_ref, u_ref, glu_ref):
        _, _, h = _rms(x_ref[...], g_ref[...])
        hb = h.astype(BF16)
        for j in range(4):
            a = _dot(hb, w_ref[j]) + b_ref[:, j * PWB:(j + 1) * PWB]
            gt = _dot(hb, w_ref[j + 4]) + b_ref[:, D + j * PWB:D + (j + 1) * PWB]
            u_ref[j] = a.astype(BF16)
            u_ref[j + 4] = gt.astype(BF16)
            glu_ref[:, j * PWB:(j + 1) * PWB] = a * _sigmoid(gt)

    return _pc(body, [x, g, w, b], name="conv_pre", grid=(t // tm,),
               in_specs=[_rows(tm, D), _acc(1, D), WHOLE, _acc(1, 2 * D)], out_specs=[_blk3(NDEV, tm, PWB), _rows(tm, D)],
               out_shape=[_sds((NDEV, t, PWB), BF16), _sds((t, D), F32)], xchg=xchg)


def _halo_specs(t, tm):
    per = tm // HALO
    last = t // HALO - 1
    return [
        pl.BlockSpec((HALO, D), lambda i: (jnp.maximum(i * per - 1, 0), 0)),
        _rows(tm, D),
        pl.BlockSpec((HALO, D), lambda i: (jnp.minimum((i + 1) * per, last), 0)),
    ]


SUB = 8
CCH = 32
CLN = 256


def _fill_shifted(sh, prev, cur, nxt, tm):
    i = pl.program_id(0)
    rows = jnp.concatenate([jnp.where(i == 0, 0.0, prev[...]), cur[...], jnp.where(i == pl.num_programs(0) - 1, 0.0, nxt[...])], axis=0)
    n = tm + 2 * HALO - SUB
    for b in range(SUB):
        sh[b] = rows[b:b + n]


def _shifted(sh, off, r0, c0):
    return sh[off % SUB, r0 + off - off % SUB:r0 + off - off % SUB + CCH, c0:c0 + CLN]


def _conv_mid(glu, wdw, bdw, tm, xchg=None):
    t = glu.shape[0]

    def body(prev, cur, nxt, w_ref, b_ref, out_ref, sh):
        _fill_shifted(sh, prev, cur, nxt, tm)
        for c0 in range(0, D, CLN):
            for r0 in range(0, tm, CCH):
                acc = jnp.broadcast_to(b_ref[:, c0:c0 + CLN], (CCH, CLN))
                for k in range(CW):
                    acc = acc + w_ref[k:k + 1, c0:c0 + CLN] * _shifted(sh, k + HALO - CPAD, r0, c0)
                out_ref[r0:r0 + CCH, c0:c0 + CLN] = acc

    outs, xo = _pc(body, [glu, glu, glu, wdw, bdw], name="conv_mid", grid=(t // tm,),
                   in_specs=_halo_specs(t, tm) + [_acc(32, D), _acc(1, D)], out_specs=[_rows(tm, D)],
                   out_shape=[_sds((t, D), F32)], scratch_shapes=[pltpu.VMEM((SUB, tm + 2 * HALO - SUB, D), F32)], xchg=xchg)
    return outs[0], xo


def _conv_mid_bwd(dcv, glu, wdw, tm, xchg=None):
    t = glu.shape[0]

    def body(dp, dc, dn, gp, gc, gn, w_ref, dglu_ref, dw_ref, dsh, gsh):
        @pl.when(pl.program_id(0) == 0)
        def _():
            dw_ref[...] = jnp.zeros_like(dw_ref)

        _fill_shifted(dsh, dp, dc, dn, tm)
        _fill_shifted(gsh, gp, gc, gn, tm)
        for c0 in range(0, D, CLN):
            for r0 in range(0, tm, CCH):
                acc = jnp.zeros((CCH, CLN), F32)
                for k in range(CW):
                    acc = acc + w_ref[k:k + 1, c0:c0 + CLN] * _shifted(dsh, HALO + CPAD - k, r0, c0)
                dglu_ref[r0:r0 + CCH, c0:c0 + CLN] = acc
            for k in range(CW):
                dwk = jnp.zeros((SUB, CLN), F32)
                for r0 in range(0, tm, CCH):
                    prod = _shifted(dsh, HALO, r0, c0) * _shifted(gsh, k + HALO - CPAD, r0, c0)
                    for r in range(0, CCH, SUB):
                        dwk = dwk + prod[r:r + SUB]
                dw_ref[k:k + 1, c0:c0 + CLN] += jnp.sum(dwk, axis=0, keepdims=True)

    n = tm + 2 * HALO - SUB
    return _pc(body, [dcv, dcv, dcv, glu, glu, glu, wdw], name="conv_mid_bwd", grid=(t // tm,),
               in_specs=_halo_specs(t, tm) + _halo_specs(t, tm) + [_acc(32, D)], out_specs=[_rows(tm, D), _acc(32, D)],
               out_shape=[_sds((t, D), F32), _sds((32, D), F32)],
               scratch_shapes=[pltpu.VMEM((SUB, n, D), F32), pltpu.VMEM((SUB, n, D), F32)], xchg=xchg)


def _ln(cv, lg, lb):
    mu = jnp.mean(cv, axis=-1, keepdims=True)
    cc = cv - mu
    rs = lax.rsqrt(jnp.mean(cc * cc, axis=-1, keepdims=True) + EPS)
    lh = cc * rs
    return lh, rs, lh * lg + lb


def _conv_post(cv, x, lg, lb, w2, b2, tm):
    t = x.shape[0]

    def body(cv_ref, x_ref, lg_ref, lb_ref, w_ref, b_ref, st_ref, out_ref):
        _, _, ln = _ln(cv_ref[...], lg_ref[...], lb_ref[...])
        s = (ln * _sigmoid(ln)).astype(BF16)
        st_ref[...] = s.T
        out_ref[...] = x_ref[...] + _dot(s, w_ref[...]) + b_ref[...]

    return _pc(body, [cv, x, lg, lb, w2, b2], name="conv_post", grid=(t // tm,),
               in_specs=[_rows(tm, D), _rows(tm, D), _acc(1, D), _acc(1, D), WHOLE, _acc(1, D)],
               out_specs=[_cols(tm), _rows(tm, D)], out_shape=[_sds((D, t), BF16), _sds((t, D), F32)])[0]


def _conv_post_bwd(dx, cv, lg, lb, w2, tm, xchg=None):
    t = dx.shape[0]

    def body(dx_ref, cv_ref, lg_ref, lb_ref, w_ref, dcv_ref, part_ref):
        @pl.when(pl.program_id(0) == 0)
        def _():
            part_ref[...] = jnp.zeros_like(part_ref)

        dx = dx_ref[...]
        lg = lg_ref[...]
        lh, rs, ln = _ln(cv_ref[...], lg, lb_ref[...])
        sg = _sigmoid(ln)
        dln = _dot_nt(dx, w_ref[...]) * (sg * (1.0 + ln * (1.0 - sg)))
        dlh = dln * lg
        dcv = rs * (dlh - jnp.mean(dlh, axis=-1, keepdims=True) - lh * jnp.mean(dlh * lh, axis=-1, keepdims=True))
        dcv_ref[...] = dcv
        part_ref[0:1, :] += jnp.sum(dln * lh, axis=0, keepdims=True)
        part_ref[1:2, :] += jnp.sum(dln, axis=0, keepdims=True)
        part_ref[2:3, :] += jnp.sum(dcv, axis=0, keepdims=True)
        part_ref[3:4, :] += jnp.sum(dx, axis=0, keepdims=True)

    return _pc(body, [dx, cv, lg, lb, w2], name="conv_post_bwd", grid=(t // tm,),
               in_specs=[_rows(tm, D), _rows(tm, D), _acc(1, D), _acc(1, D), WHOLE], out_specs=[_rows(tm, D), _acc(8, D)],
               out_shape=[_sds((t, D), F32), _sds((8, D), F32)], xchg=xchg)


def _conv_pre_bwd(dglu, u, x, g, w, dx_out, tm):
    t = x.shape[0]

    def body(dglu_ref, u_ref, x_ref, g_ref, w_ref, dxo_ref, du_ref, ht_ref, dx_ref, dxb_ref, dg_ref, db_ref):
        @pl.when(pl.program_id(0) == 0)
        def _():
            dg_ref[...] = jnp.zeros_like(dg_ref)
            db_ref[...] = jnp.zeros_like(db_ref)

        gn = g_ref[...]
        xh, r, h = _rms(x_ref[...], gn)
        ht_ref[...] = h.astype(BF16).T
        dh = jnp.zeros_like(xh)
        for j in range(4):
            a = u_ref[j].astype(F32)
            sg = _sigmoid(u_ref[j + 4].astype(F32))
            dgl = dglu_ref[:, j * PWB:(j + 1) * PWB]
            da = dgl * sg
            dgt = dgl * a * sg * (1.0 - sg)
            db_ref[:, j * PWB:(j + 1) * PWB] += jnp.sum(da, axis=0, keepdims=True)
            db_ref[:, D + j * PWB:D + (j + 1) * PWB] += jnp.sum(dgt, axis=0, keepdims=True)
            da, dgt = da.astype(BF16), dgt.astype(BF16)
            du_ref[j] = da
            du_ref[j + 4] = dgt
            dh = dh + _dot_nt(da, w_ref[j]) + _dot_nt(dgt, w_ref[j + 4])
        dx, dg = _rms_bwd(dh, xh, r, gn)
        dx = dxo_ref[...] + dx
        dx_ref[...] = dx
        dxb_ref[...] = dx.astype(BF16)
        dg_ref[...] += dg

    return _pc(body, [dglu, u, x, g, w, dx_out], name="conv_pre_bwd", grid=(t // tm,),
               in_specs=[_rows(tm, D), _blk3(NDEV, tm, PWB), _rows(tm, D), _acc(1, D), WHOLE, _rows(tm, D)],
               out_specs=[_blk3(NDEV, tm, PWB), _cols(tm), _rows(tm, D), _rows(tm, D), _acc(1, D), _acc(1, 2 * D)],
               out_shape=[_sds((NDEV, t, PWB), BF16), _sds((D, t), BF16), _sds((t, D), F32), _sds((t, D), BF16),
                          _sds((1, D), F32), _sds((1, 2 * D), F32)])[0]


def _adamw(w, g, m, v):
    m = B1 * m + (1.0 - B1) * g
    v = B2 * v + (1.0 - B2) * (g * g)
    m_hat = m / (1.0 - B1 ** STEP)
    v_hat = v / (1.0 - B2 ** STEP)
    return -LR * (m_hat / (jnp.sqrt(v_hat) + AEPS) + WD * w), m, v


def _reduce_adamw(lands, w, m, v, tr, name, xchg=None):
    nl, r, c = w.shape

    def body(*refs):
        l_refs, (w_ref, m_ref, v_ref, g_ref, d_ref, nm_ref, nv_ref) = refs[:nl], refs[nl:]

        def total(ref):
            g = ref[0].astype(F32)
            for j in range(1, NDEV):
                g = g + ref[j].astype(F32)
            return g

        g = total(l_refs[0])
        for n in range(1, nl):
            g = jnp.where(pl.program_id(0) == n, total(l_refs[n]), g)
        g_ref[0] = g
        d_ref[0], nm_ref[0], nv_ref[0] = _adamw(w_ref[0], g, m_ref[0], v_ref[0])

    layer = pl.BlockSpec((1, tr, c), lambda l, i: (l, i, 0))
    outs, xo = _pc(body, [*lands, w, m, v], name=name, grid=(nl, r // tr),
                   in_specs=[pl.BlockSpec((NDEV, tr, c), lambda l, i: (0, i, 0))] * nl + [layer] * 3, out_specs=[layer] * 4,
                   out_shape=[_sds((nl, r, c), F32)] * 4, xchg=xchg)
    return outs if xchg is None else (outs, xo)


def _sum_parts(parts):
    _, r, c = parts.shape

    def body(p_ref, out_ref):
        s = p_ref[0]
        for j in range(1, NDEV):
            s = s + p_ref[j]
        out_ref[...] = s

    return _pc(body, [parts], name="sum_parts", in_specs=[WHOLE], out_specs=[WHOLE], out_shape=[_sds((r, c), F32)])[0][0]


def _adamw_small(w, g, m, v):
    def body(w_ref, g_ref, m_ref, v_ref, d_ref, nm_ref, nv_ref):
        d_ref[...], nm_ref[...], nv_ref[...] = _adamw(w_ref[...], g_ref[...], m_ref[...], v_ref[...])

    return _pc(body, [w, g, m, v], name="adamw_small", in_specs=[WHOLE] * 4, out_specs=[WHOLE] * 3,
               out_shape=[_sds(w.shape, F32)] * 3)[0]


def _rows128(a):
    a = a.astype(F32)
    if a.shape[-1] % 128:
        a = jnp.pad(a, [(0, 0)] * (a.ndim - 1) + [(0, 128 - a.shape[-1] % 128)])
    return a.reshape(-1, 128)


def _pack(arrs, rows):
    p = jnp.concatenate([_rows128(a) for a in arrs], axis=0)
    return jnp.pad(p, ((0, rows - p.shape[0]), (0, 0)))


SMALL = ("attn_norm", "ffn_norm", "final_norm", "attn_sink", "conv_norm", "conv_b_dw", "conv_ln_g", "conv_ln_b", "conv_b_pw2",
         "conv_b_pw1", "conv_w_dw")
SMALL_ROWS = 72


def _pack_small(d):
    return _pack([d[k] for k in SMALL], SMALL_ROWS)


def _unpack_small(p, like):
    out, r = {}, 0
    for k in SMALL:
        shp = like[k].shape
        n = -(-shp[-1] // 128) * (math.prod(shp[:-1]))
        blk = p[r:r + n]
        if shp[-1] % 128:
            blk = blk[:, :shp[-1]]
        out[k] = blk.reshape(shp)
        r += n
    return out


NAMES = ("attn_norm", "attn_w_qkv", "attn_w_o", "attn_sink", "conv_norm", "conv_w_pw1", "conv_b_pw1", "conv_w_dw", "conv_b_dw",
         "conv_ln_g", "conv_ln_b", "conv_w_pw2", "conv_b_pw2", "ffn_norm", "ffn_w_gu", "ffn_w_down", "final_norm")
TM = 256
TL = 512
TK = 2048


def _gu_t(a):
    return jnp.swapaxes(a, 1, 2)


def kernel(x, attn_norm, attn_w_qkv, attn_w_o, attn_sink, conv_norm, conv_w_pw1, conv_b_pw1, conv_w_dw, conv_b_dw, conv_ln_g, conv_ln_b, conv_w_pw2, conv_b_pw2, ffn_norm, ffn_w_gu, ffn_w_down, final_norm, loss_target, m_attn_norm, m_attn_w_qkv, m_attn_w_o, m_attn_sink, m_conv_norm, m_conv_w_pw1, m_conv_b_pw1, m_conv_w_dw, m_conv_b_dw, m_conv_ln_g, m_conv_ln_b, m_conv_w_pw2, m_conv_b_pw2, m_ffn_norm, m_ffn_w_gu, m_ffn_w_down, m_final_norm, v_attn_norm, v_attn_w_qkv, v_attn_w_o, v_attn_sink, v_conv_norm, v_conv_w_pw1, v_conv_b_pw1, v_conv_w_dw, v_conv_b_dw, v_conv_ln_g, v_conv_ln_b, v_conv_w_pw2, v_conv_b_pw2, v_ffn_norm, v_ffn_w_gu, v_ffn_w_down, v_final_norm):
    w = dict(zip(NAMES, (attn_norm, attn_w_qkv, attn_w_o, attn_sink, conv_norm, conv_w_pw1, conv_b_pw1, conv_w_dw, conv_b_dw, conv_ln_g,
                         conv_ln_b, conv_w_pw2, conv_b_pw2, ffn_norm, ffn_w_gu, ffn_w_down, final_norm)))
    m = dict(zip(NAMES, (m_attn_norm, m_attn_w_qkv, m_attn_w_o, m_attn_sink, m_conv_norm, m_conv_w_pw1, m_conv_b_pw1, m_conv_w_dw,
                         m_conv_b_dw, m_conv_ln_g, m_conv_ln_b, m_conv_w_pw2, m_conv_b_pw2, m_ffn_norm, m_ffn_w_gu, m_ffn_w_down,
                         m_final_norm)))
    v = dict(zip(NAMES, (v_attn_norm, v_attn_w_qkv, v_attn_w_o, v_attn_sink, v_conv_norm, v_conv_w_pw1, v_conv_b_pw1, v_conv_w_dw,
                         v_conv_b_dw, v_conv_ln_g, v_conv_ln_b, v_conv_w_pw2, v_conv_b_pw2, v_ffn_norm, v_ffn_w_gu, v_ffn_w_down,
                         v_final_norm)))
    me = 4 * lax.axis_index("x") + 2 * lax.axis_index("y") + lax.axis_index("c")
    for d in (w, m, v):
        d["ffn_w_gu"] = _gu_t(d["ffn_w_gu"])
    sh = {k: w[k][0].astype(BF16) for k in ("attn_w_qkv", "attn_w_o", "conv_w_pw1", "conv_w_pw2")}
    gu_b, down_b = w["ffn_w_gu"].astype(BF16), w["ffn_w_down"].astype(BF16)
    sh.update(ffn_w_gu0=gu_b[0], ffn_w_gu1=gu_b[1], ffn_w_down0=down_b[0], ffn_w_down1=down_b[1])
    x0, tgt = x[0], loss_target[0]
    t = x0.shape[0]
    tabs = _rope_tables(t)
    tl = min(TL, t)
    g_a, sink, g_f0, g_f1, g_fin = w["attn_norm"], w["attn_sink"][0], w["ffn_norm"][0:1], w["ffn_norm"][1:2], w["final_norm"][None]
    gather, scatter = False, True

    shard_rows = _pack([w["conv_w_dw"][0], jnp.zeros((1, 128), F32), w["conv_norm"], w["conv_b_dw"], w["conv_ln_g"], w["conv_ln_b"],
                        w["conv_b_pw2"], w["conv_b_pw1"]], 40)
    wqkv_g, sm = _exchange([sh["attn_w_qkv"], shard_rows], [gather] * 2, "gather_attn")
    wqkv = wqkv_g.transpose(1, 0, 2).reshape(D, QKV)

    def full_vec(row, n=1):
        return sm[:, row:row + n, :].reshape(1, NDEV * n * 128)

    w_dw, g_c, b_dw, ln_g, ln_b = sm[:, 0:32, :].transpose(1, 0, 2).reshape(32, D), full_vec(32), full_vec(33), full_vec(34), full_vec(35)
    b_pw2, b_pw1 = full_vec(36), full_vec(37, 2)

    (q_t, kv), (wo_g, wd0_g) = _attn_pre(x0, g_a, wqkv, tabs, tl, xchg=([sh["attn_w_o"], sh["ffn_w_down0"]], [gather] * 2))
    o_t, (wgu0,) = _attn_fwd(q_t, kv, sink, xchg=([sh["ffn_w_gu0"]], [gather]))
    wo, wd0 = wo_g.reshape(D, D), wd0_g.reshape(4, GUB, D)
    (x1, sf0, act0_t, x2), (wpw1, wgu1) = _ffn_fwd_attn(
        o_t, x0, wo, g_f0, wgu0, wd0, tl, xchg=([sh["conv_w_pw1"], sh["ffn_w_gu1"]], [gather] * 2))
    (u, glu), (wpw2_g,) = _conv_pre(x2, g_c, wpw1, b_pw1, tl, xchg=([sh["conv_w_pw2"]], [gather]))
    wpw2 = wpw2_g.reshape(D, D)
    cv, (wd1_g,) = _conv_mid(glu, w_dw, b_dw, tl, xchg=([sh["ffn_w_down1"]], [gather]))
    wd1 = wd1_g.reshape(4, GUB, D)
    s_t, x3 = _conv_post(cv, x2, ln_g, ln_b, wpw2, b_pw2, tl)
    (sf1, act1_t, dx4, dx4_b, fin), _ = _ffn_fwd_final(x3, g_f1, wgu1, wd1, g_fin, tgt, tl)

    land = {}
    tk, tk2 = min(TK, t), min(2 * TK, t)
    (dgu1_t, h3, dx3, dg_f1), _ = _ffn_bwd(dx4, x3, sf1, g_f1, wgu1, wd1, TM, "ffn_bwd1")
    dwgu1 = _wgrad(dgu1_t, h3[None], tk2, "dwgu1")
    dwd1 = _wgrad(act1_t, dx4_b[None], tk2, "dwd1").reshape(NDEV, DFF // NDEV, D)
    (dcv, cpart), (land["ffn_w_down1"],) = _conv_post_bwd(dx3, cv, ln_g, ln_b, wpw2, tl, xchg=([dwd1], [scatter]))
    dwpw2 = _wgrad(s_t[None], dx3[None], tk, "dwpw2").reshape(NDEV, D // NDEV, D)
    (dglu, dw_dw), (land["ffn_w_gu1"], land["conv_w_pw2"]) = _conv_mid_bwd(dcv, glu, w_dw, TM, xchg=([dwgu1, dwpw2], [scatter] * 2))
    du, h2_t, dx2, dx2_b, dg_c, db_pw1 = _conv_pre_bwd(dglu, u, x2, g_c, wpw1, dx3, tl)
    dwpw1 = _wgrad(h2_t[None], du, tk2, "dwpw1")
    (dgu0_t, h1, dx1, dg_f0, do_t), (land["conv_w_pw1"],) = _ffn_bwd(
        dx2, x1, sf0, g_f0, wgu0, wd0, TM, "ffn_bwd0", wo=wo, xchg=([dwpw1], [scatter]))
    dwgu0 = _wgrad(dgu0_t, h1[None], tk2, "dwgu0")
    dwd0 = _wgrad(act0_t, dx2_b[None], tk2, "dwd0").reshape(NDEV, DFF // NDEV, D)
    dwo = _wgrad(o_t, dx1[None], tk, "dwo", by_block=True).reshape(NDEV, D // NDEV, D)
    (dq_t, dk, dv, dsink), (land["ffn_w_gu0"], land["ffn_w_down0"]) = _attn_bwd(
        q_t, kv, do_t, sink, tl, xchg=([dwgu0, dwd0], [scatter] * 2))
    dqkv, h0_t, dx0, dg_a = _attn_pre_bwd(dq_t, dk, dv, x0, g_a, wqkv, tabs, dx1, tl)
    dwqkv, (land["attn_w_o"],) = _wgrad(h0_t[None], dqkv[None], tk, "dwqkv", xchg=([dwo], [scatter]))
    dwqkv = dwqkv[0].reshape(D, NDEV, QKV // NDEV).transpose(1, 0, 2)

    lane0 = (lax.broadcasted_iota(jnp.int32, (1, 128), 1) == 0).astype(F32)
    parts = _pack([dg_a, dg_f0, dg_f1, fin[0:1], dsink[0:1, :NH], fin[1, 0] * lane0, jnp.zeros((6, 128), F32), dg_c, cpart[2:3],
                   cpart[0:1], cpart[1:2], cpart[3:4], db_pw1, dw_dw.reshape(32, NDEV, 128)], 352)
    land["attn_w_qkv"], parts_g = _exchange([dwqkv, parts], [scatter, gather], "scatter_attn")
    red = _sum_parts(parts_g)

    def shard_rows_of(row, n=1):
        return lax.dynamic_slice_in_dim(red, row + n * me, n, axis=0)

    gs = {
        "attn_norm": red[0:8].reshape(1, D), "ffn_norm": red[8:24].reshape(2, D), "final_norm": red[24:32].reshape(D),
        "attn_sink": red[32:33, :NH], "conv_norm": shard_rows_of(40), "conv_b_dw": shard_rows_of(48), "conv_ln_g": shard_rows_of(56),
        "conv_ln_b": shard_rows_of(64), "conv_b_pw2": shard_rows_of(72), "conv_b_pw1": shard_rows_of(80, 2).reshape(1, PWB),
        "conv_w_dw": lax.dynamic_index_in_dim(red[96:352].reshape(32, NDEV, 128), me, axis=1, keepdims=False)[None, :CW],
    }
    loss = red[33, 0]

    grads, deltas, new_m, new_v = dict(gs), {}, {}, {}
    ds, ms, vs = _adamw_small(_pack_small(w), _pack_small(gs), _pack_small(m), _pack_small(v))
    deltas.update(_unpack_small(ds, gs))
    new_m.update(_unpack_small(ms, gs))
    new_v.update(_unpack_small(vs, gs))
    for k in ("attn_w_qkv", "attn_w_o", "conv_w_pw1", "conv_w_pw2", "ffn_w_gu", "ffn_w_down"):
        lands = [land[k + "0"], land[k + "1"]] if k.startswith("ffn") else [land[k]]
        tr = {1024: 256, 128: 128, 352: 176, GUB: 176}[w[k].shape[1]]
        grads[k], deltas[k], new_m[k], new_v[k] = _reduce_adamw(lands, w[k], m[k], v[k], tr, "adamw_" + k)
    for d in (grads, deltas, new_m, new_v):
        d["ffn_w_gu"] = _gu_t(d["ffn_w_gu"])
    return (loss, dx0[None], *[grads[k] for k in NAMES], *[deltas[k] for k in NAMES], *[new_m[k] for k in NAMES],
            *[new_v[k] for k in NAMES])
```

```python
import math

import jax
import jax.numpy as jnp
from jax import lax
from jax.experimental import pallas as pl
from jax.experimental.pallas import tpu as pltpu

F32 = jnp.float32
BF16 = jnp.bfloat16

D = 1024
NH = 16
NKV = 4
HD = 64
GROUP = NH // NKV
ROT = 16
THETA = 500000.0
BLK = 128
QKV = (NH + 2 * NKV) * HD
KOFF = NH * HD
VOFF = KOFF + NKV * HD
DFF = 2816
NDEV = 8
GUB = 2 * DFF // NDEV
PWB = 2 * D // NDEV
CW = 31
CPAD = 15
HALO = 16
EPS = 1e-6
NEG = -1e30
LR, B1, B2, AEPS, WD, STEP = 0.001, 0.9, 0.999, 1e-08, 0.01, 10

VMEM_LIMIT = 56 * 1024 * 1024
MESH = pl.DeviceIdType.MESH
WHOLE = pl.BlockSpec(memory_space=pltpu.VMEM)
ANY = pl.BlockSpec(memory_space=pl.ANY)


def _place():
    x, y, c = lax.axis_index("x"), lax.axis_index("y"), lax.axis_index("c")
    return x, y, c, 4 * x + 2 * y + c


def _peer(x, y, c, j):
    px = 1 - x if j & 4 else x
    py = 1 - y if j & 2 else y
    pc = 1 - c if j & 1 else c
    return (px, py, pc), 4 * px + 2 * py + pc


SIBLING = 1
OTHER_CHIPS = (2, 4, 6)
PASS_ON_LEAD = 3


class _Exchange:
    def __init__(self, src, dst, scatter, send, recv, loc):
        self.src, self.dst, self.scatter, self.send, self.recv, self.loc = src, dst, scatter, send, recv, loc
        self.x, self.y, self.c, self.me = _place()

    def _remote(self, k, j, s, d, to):
        peer, _ = _peer(self.x, self.y, self.c, to)
        return pltpu.make_async_remote_copy(src_ref=s, dst_ref=d, send_sem=self.send.at[k, j - 1], recv_sem=self.recv.at[k, j - 1],
                                            device_id=peer, device_id_type=MESH)

    def _slot(self, j):
        return _peer(self.x, self.y, self.c, j)[1]

    def local(self, k):
        if self.scatter[k]:
            return pltpu.make_async_copy(self.src[k].at[self.me], self.dst[k].at[0], self.loc.at[k])
        return pltpu.make_async_copy(self.src[k], self.dst[k].at[self.me], self.loc.at[k])

    def direct(self, k, j):
        if self.scatter[k]:
            return self._remote(k, j, self.src[k].at[self._slot(j)], self.dst[k].at[j], j)
        return self._remote(k, j, self.src[k], self.dst[k].at[self.me], j)

    def passed_on(self, k, j):
        rows = self.dst[k].at[self._slot(j)]
        return self._remote(k, j + 1, rows, rows, SIBLING)

    def arrival(self, k, j):
        rows = self.dst[k].at[j if self.scatter[k] else self._slot(j)]
        return self._remote(k, j, rows, rows, j)

    def sent(self, k):
        return tuple(range(1, NDEV)) if self.scatter[k] else (SIBLING,) + OTHER_CHIPS

    def start(self):
        for k in range(len(self.src)):
            self.local(k).start()
            for j in self.sent(k):
                self.direct(k, j).start()

    def pass_on(self):
        for k in range(len(self.src)):
            if not self.scatter[k]:
                for j in OTHER_CHIPS:
                    self.arrival(k, j).wait_recv()
                    self.passed_on(k, j).start()

    def finish(self):
        for k in range(len(self.src)):
            for j in range(1, NDEV):
                if self.scatter[k] or j not in OTHER_CHIPS:
                    self.arrival(k, j).wait_recv()
            for j in self.sent(k):
                self.direct(k, j).wait_send()
            if not self.scatter[k]:
                for j in OTHER_CHIPS:
                    self.passed_on(k, j).wait_send()
            self.local(k).wait()


def _call(body, **kw):
    return pl.pallas_call(body, **kw)


def _pc(body, operands, *, name, in_specs, out_specs, out_shape, grid=None, scratch_shapes=(), xchg=None):
    kw = dict(name=name, compiler_params=pltpu.CompilerParams(
        dimension_semantics=None if grid is None else ("arbitrary",) * len(grid), vmem_limit_bytes=VMEM_LIMIT,
        has_side_effects=xchg is not None))
    if grid is not None:
        kw["grid"] = grid
    if xchg is None:
        outs = _call(body, in_specs=list(in_specs), out_specs=list(out_specs), out_shape=list(out_shape),
                     scratch_shapes=list(scratch_shapes), **kw)(*operands)
        return list(outs), []
    arrs, scatter = xchg
    nx, n_in, n_out, n_s = len(arrs), len(in_specs), len(out_specs), len(scratch_shapes)

    def wrapped(*refs):
        ins, refs = refs[:n_in], refs[n_in:]
        src, refs = refs[:nx], refs[nx:]
        outs, refs = refs[:n_out], refs[n_out:]
        dst, refs = refs[:nx], refs[nx:]
        scr, (send, recv, loc) = refs[:n_s], refs[n_s:]
        if grid is None:
            _Exchange(src, dst, scatter, send, recv, loc).start()
            body(*ins, *outs, *scr)
            _Exchange(src, dst, scatter, send, recv, loc).pass_on()
            _Exchange(src, dst, scatter, send, recv, loc).finish()
            return

        step, steps = 0, 1
        for axis, n in enumerate(grid):
            step, steps = step * n + pl.program_id(axis), steps * n

        @pl.when(step == 0)
        def _():
            _Exchange(src, dst, scatter, send, recv, loc).start()

        body(*ins, *outs, *scr)

        @pl.when(step == max(steps - 1 - PASS_ON_LEAD, 0))
        def _():
            _Exchange(src, dst, scatter, send, recv, loc).pass_on()

        @pl.when(step == steps - 1)
        def _():
            _Exchange(src, dst, scatter, send, recv, loc).finish()

    xshape = [jax.ShapeDtypeStruct(a.shape if sc else (NDEV,) + a.shape, a.dtype) for a, sc in zip(arrs, scatter)]
    sems = [pltpu.SemaphoreType.DMA((nx, NDEV - 1)), pltpu.SemaphoreType.DMA((nx, NDEV - 1)), pltpu.SemaphoreType.DMA((nx,))]
    res = _call(wrapped, in_specs=list(in_specs) + [ANY] * nx, out_specs=list(out_specs) + [ANY] * nx,
                out_shape=list(out_shape) + xshape, scratch_shapes=list(scratch_shapes) + sems, **kw)(*operands, *arrs)
    return list(res[:n_out]), list(res[n_out:])


def _exchange(arrs, scatter, name):
    return _pc(lambda: None, [], name=name, in_specs=[], out_specs=[], out_shape=[], xchg=(arrs, scatter))[1]


def _rows(tm, width):
    return pl.BlockSpec((tm, width), lambda i: (i, 0))


def _cols(tm):
    return pl.BlockSpec((D, tm), lambda i: (0, i))


def _qblocks(tm):
    return pl.BlockSpec((tm // BLK, D, BLK), lambda i: (i, 0, 0))


def _put_blocks(ref, val):
    for b in range(ref.shape[0]):
        ref[b] = val[:, b * BLK:(b + 1) * BLK]


def _get_blocks(ref):
    return jnp.concatenate([ref[b] for b in range(ref.shape[0])], axis=1)


def _blk3(nb, tm, width):
    return pl.BlockSpec((nb, tm, width), lambda i: (0, i, 0))


def _acc(rows, width):
    return pl.BlockSpec((rows, width), lambda i: (0, 0))


def _sds(shape, dtype):
    return jax.ShapeDtypeStruct(shape, dtype)


def _dot(a, b):
    return jnp.dot(a.astype(BF16), b.astype(BF16), preferred_element_type=F32)


def _dot_nt(a, b):
    return lax.dot_general(a.astype(BF16), b.astype(BF16), (((1,), (1,)), ((), ())), preferred_element_type=F32)


def _dot_tn(a, b):
    return lax.dot_general(a.astype(BF16), b.astype(BF16), (((0,), (0,)), ((), ())), preferred_element_type=F32)


def _sigmoid(x):
    return 1.0 / (1.0 + jnp.exp(-x))


def _rms(x, g):
    r = lax.rsqrt(jnp.mean(x * x, axis=-1, keepdims=True) + EPS)
    xh = x * r
    return xh, r, xh * g


def _rms_bwd(dh, xh, r, g):
    dxh = dh * g
    dg = jnp.sum(dh * xh, axis=0, keepdims=True)
    dx = r * (dxh - xh * jnp.mean(dxh * xh, axis=-1, keepdims=True))
    return dx, dg


def _lanes(t, width):
    return jnp.tile(t, (1, width // t.shape[1]))


def _rope(z, c, sa, sb):
    w = z.shape[1]
    return z * _lanes(c, w) + pltpu.roll(z, w - 8, 1) * _lanes(sa, w) + pltpu.roll(z, 8, 1) * _lanes(sb, w)


def _rope_t(dz, c, sa, sb):
    w = dz.shape[1]
    return dz * _lanes(c, w) + pltpu.roll(dz * _lanes(sa, w), 8, 1) + pltpu.roll(dz * _lanes(sb, w), w - 8, 1)


def _rope_tables(t):
    pos = jnp.arange(t, dtype=F32)
    inv_freq = THETA ** (-jnp.arange(0, ROT, 2, dtype=F32) / ROT)
    ang = pos[:, None] * inv_freq[None, :]
    cos, sin = jnp.cos(ang), jnp.sin(ang)
    one = jnp.ones((t, HD - ROT), F32)
    zero = jnp.zeros((t, HD - ROT), F32)
    z8 = jnp.zeros((t, 8), F32)
    c = jnp.concatenate([cos, cos, one], axis=1)
    sa = jnp.concatenate([-sin, z8, zero], axis=1)
    sb = jnp.concatenate([z8, sin, zero], axis=1)
    return tuple(jnp.tile(a, (1, 2)) for a in (c, sa, sb))


KVW = NKV * HD
GW = GROUP * BLK


def _attn_pre(x, g, wqkv, tabs, tm, xchg=None):
    t = x.shape[0]

    def body(x_ref, g_ref, w_ref, c_ref, sa_ref, sb_ref, qt_ref, kv_ref):
        _, _, h = _rms(x_ref[...], g_ref[...])
        z = _dot(h, w_ref[...])
        c, sa, sb = c_ref[...], sa_ref[...], sb_ref[...]
        _put_blocks(qt_ref, (_rope(z[:, :KOFF], c, sa, sb) * 0.125).T.astype(BF16))
        kv_ref[:, :KVW] = _rope(z[:, KOFF:VOFF], c, sa, sb).astype(BF16)
        kv_ref[:, KVW:] = z[:, VOFF:].astype(BF16)

    return _pc(body, [x, g, wqkv, *tabs], name="attn_pre", grid=(t // tm,),
               in_specs=[_rows(tm, D), _acc(1, D), WHOLE, _rows(tm, 128), _rows(tm, 128), _rows(tm, 128)],
               out_specs=[_qblocks(tm), _rows(tm, 2 * KVW)],
               out_shape=[_sds((t // BLK, D, BLK), BF16), _sds((t, 2 * KVW), BF16)], xchg=xchg)


QB = 4


def _attn_specs(nblk):
    prev = lambda i: jnp.maximum(QB * i - 1, 0)
    nxt = lambda i: jnp.minimum(QB * (i + 1), nblk - 1)
    return [
        _qblocks(QB * BLK),
        pl.BlockSpec((BLK, KVW), lambda i: (prev(i), 0)),
        pl.BlockSpec((QB * BLK, KVW), lambda i: (i, 0)),
        pl.BlockSpec((BLK, KVW), lambda i: (nxt(i), 0)),
        pl.BlockSpec((BLK, KVW), lambda i: (prev(i), 1)),
        pl.BlockSpec((QB * BLK, KVW), lambda i: (i, 1)),
        pl.BlockSpec((BLK, KVW), lambda i: (nxt(i), 1)),
    ]


def _attn_bias():
    c = lax.broadcasted_iota(jnp.int32, (3, 3 * BLK, GW), 1)
    r = lax.broadcasted_iota(jnp.int32, (3, 3 * BLK, GW), 2) & (BLK - 1)
    slab = lax.broadcasted_iota(jnp.int32, (3, 3 * BLK, GW), 0)
    valid = (c >= r) & (c - 2 * BLK <= r) & ((slab != 0) | (c >= BLK)) & ((slab != 2) | (c < 2 * BLK))
    return jnp.where(valid, 0.0, NEG).astype(F32)


def _bias_of(bias_ref, blk, nblk):
    return bias_ref[jnp.where(blk == 0, 0, jnp.where(blk == nblk - 1, 2, 1))]


def _group(ref, b, kv):
    return jnp.concatenate([ref[b, (kv * GROUP + g) * HD:(kv * GROUP + g + 1) * HD, :] for g in range(GROUP)], axis=1)


def _group_sink(sink_ref, kv):
    return jnp.concatenate([jnp.full((1, BLK), sink_ref[kv * GROUP + g], F32) for g in range(GROUP)], axis=1)


def _attn_exp(k_h, qt_g, bias, sink_g):
    s = _dot(k_h, qt_g) + bias
    m = jnp.maximum(jnp.max(s, axis=0, keepdims=True), sink_g)
    e = jnp.exp(s - m)
    es = jnp.exp(sink_g - m)
    return e, 1.0 / (jnp.sum(e, axis=0, keepdims=True) + es), es


def _attn_fwd(qt, kv, sink, xchg=None):
    t = kv.shape[0]
    nblk = t // BLK

    def body(sink_ref, bias_ref, qt_ref, kp, kc, kn, vp, vc, vn, ot_ref):
        k = jnp.concatenate([kp[...], kc[...], kn[...]], axis=0)
        v = jnp.concatenate([vp[...], vc[...], vn[...]], axis=0)
        for b in range(QB):
            bias = _bias_of(bias_ref, QB * pl.program_id(0) + b, nblk)
            keys = slice(b * BLK, (b + 3) * BLK)
            for h in range(NKV):
                e, inv, _ = _attn_exp(k[keys, h * HD:(h + 1) * HD], _group(qt_ref, b, h), bias, _group_sink(sink_ref, h))
                ot_g = (_dot_tn(v[keys, h * HD:(h + 1) * HD], e) * inv).astype(BF16)
                for g in range(GROUP):
                    ot_ref[b, (h * GROUP + g) * HD:(h * GROUP + g + 1) * HD, :] = ot_g[:, g * BLK:(g + 1) * BLK]

    outs, xo = _pc(body, [sink, _attn_bias(), qt] + [kv] * 6, name="attn_fwd", grid=(nblk // QB,),
                   in_specs=[pl.BlockSpec(memory_space=pltpu.SMEM), WHOLE] + _attn_specs(nblk),
                   out_specs=[_qblocks(QB * BLK)], out_shape=[_sds((nblk, D, BLK), BF16)], xchg=xchg)
    return outs[0], xo


def _attn_bwd(qt, kv, dot, sink, pad, xchg=None):
    t = kv.shape[0]
    nblk = t // BLK

    def body(sink_ref, bias_ref, qt_ref, kp, kc, kn, vp, vc, vn, dot_ref, dqt_ref, dk_ref, dv_ref, ds_ref):
        i = pl.program_id(0)

        @pl.when(i == 0)
        def _():
            dk_ref[...] = jnp.zeros_like(dk_ref)
            dv_ref[...] = jnp.zeros_like(dv_ref)
            ds_ref[...] = jnp.zeros_like(ds_ref)

        k = jnp.concatenate([kp[...], kc[...], kn[...]], axis=0)
        v = jnp.concatenate([vp[...], vc[...], vn[...]], axis=0)
        lane = lax.broadcasted_iota(jnp.int32, (1, 128), 1)
        dsink = jnp.zeros((1, 128), F32)
        for b in range(QB):
            blk = QB * i + b
            bias = _bias_of(bias_ref, blk, nblk)
            keys = slice(b * BLK, (b + 3) * BLK)
            rows = pl.ds(pl.multiple_of(blk * BLK + (pad - BLK), BLK), 3 * BLK)
            for h in range(NKV):
                k_h, v_h = k[keys, h * HD:(h + 1) * HD], v[keys, h * HD:(h + 1) * HD]
                qt_g, dot_g = _group(qt_ref, b, h), _group(dot_ref, b, h)
                e, inv, es = _attn_exp(k_h, qt_g, bias, _group_sink(sink_ref, h))
                p, ps = e * inv, es * inv
                dp = _dot(v_h, dot_g)
                delta = jnp.sum(p * dp, axis=0, keepdims=True)
                ds = (p * (dp - delta)).astype(BF16)
                dqt_g = _dot_tn(k_h, ds)
                dk_ref[rows, h * HD:(h + 1) * HD] += _dot_nt(ds, qt_g)
                dv_ref[rows, h * HD:(h + 1) * HD] += _dot_nt(p, dot_g)
                psd = ps * delta
                for g in range(GROUP):
                    n = h * GROUP + g
                    dqt_ref[b, n * HD:(n + 1) * HD, :] = dqt_g[:, g * BLK:(g + 1) * BLK]
                    dsink = dsink - jnp.where(lane == n, jnp.sum(psd[:, g * BLK:(g + 1) * BLK], axis=1, keepdims=True), 0.0)
        ds_ref[0:1, :] += dsink

    outs, xo = _pc(body, [sink, _attn_bias(), qt] + [kv] * 6 + [dot], name="attn_bwd", grid=(nblk // QB,),
                   in_specs=[pl.BlockSpec(memory_space=pltpu.SMEM), WHOLE] + _attn_specs(nblk) + [_qblocks(QB * BLK)],
                   out_specs=[_qblocks(QB * BLK), _acc(t + 2 * pad, KVW), _acc(t + 2 * pad, KVW), _acc(8, 128)],
                   out_shape=[_sds((nblk, D, BLK), F32), _sds((t + 2 * pad, KVW), F32), _sds((t + 2 * pad, KVW), F32),
                              _sds((8, 128), F32)], xchg=xchg)
    return outs, xo


def _attn_pre_bwd(dqt, dk, dv, x, g, wqkv, tabs, dx_out, tm):
    t = x.shape[0]

    def body(dqt_ref, dk_ref, dv_ref, x_ref, g_ref, w_ref, c_ref, sa_ref, sb_ref, dxo_ref, dqkv_ref, ht_ref, dx_ref, dg_ref):
        @pl.when(pl.program_id(0) == 0)
        def _():
            dg_ref[...] = jnp.zeros_like(dg_ref)

        c, sa, sb = c_ref[...], sa_ref[...], sb_ref[...]
        dqkv_ref[:, :KOFF] = _rope_t(_get_blocks(dqt_ref).T * 0.125, c, sa, sb).astype(BF16)
        dqkv_ref[:, KOFF:VOFF] = _rope_t(dk_ref[...], c, sa, sb).astype(BF16)
        dqkv_ref[:, VOFF:] = dv_ref[...].astype(BF16)
        g = g_ref[...]
        xh, r, h = _rms(x_ref[...], g)
        ht_ref[...] = h.astype(BF16).T
        dh = _dot_nt(dqkv_ref[...], w_ref[...])
        dx, dg = _rms_bwd(dh, xh, r, g)
        dx_ref[...] = dxo_ref[...] + dx
        dg_ref[...] += dg

    return _pc(body, [dqt, dk, dv, x, g, wqkv, *tabs, dx_out], name="attn_pre_bwd", grid=(t // tm,),
               in_specs=[_qblocks(tm), pl.BlockSpec((tm, KVW), lambda i: (i + 1, 0)), pl.BlockSpec((tm, KVW), lambda i: (i + 1, 0)),
                         _rows(tm, D), _acc(1, D), WHOLE, _rows(tm, 128), _rows(tm, 128), _rows(tm, 128), _rows(tm, D)],
               out_specs=[_rows(tm, QKV), _cols(tm), _rows(tm, D), _acc(1, D)],
               out_shape=[_sds((t, QKV), BF16), _sds((D, t), BF16), _sds((t, D), F32), _sds((1, D), F32)])[0]


def _ffn(x, g, wgu_ref, wd_ref, sf_ref, actt_ref):
    _, _, h = _rms(x, g)
    hb = h.astype(BF16)
    y = x
    for j in range(4):
        gj = _dot_nt(hb, wgu_ref[j])
        uj = _dot_nt(hb, wgu_ref[j + 4])
        sg = _sigmoid(gj)
        silu = gj * sg
        act = (silu * uj).astype(BF16)
        sf_ref[j] = silu.astype(BF16)
        sf_ref[j + 4] = (sg * (1.0 + gj * (1.0 - sg)) * uj).astype(BF16)
        actt_ref[j] = act.T
        y = y + _dot(act, wd_ref[j])
    return y


def _tcols(nb, tm):
    return pl.BlockSpec((nb, GUB, tm), lambda i: (0, 0, i))


def _ffn_fwd_attn(ot, x, wo, g, wgu, wd, tm, xchg=None):
    t = x.shape[0]

    def body(ot_ref, x_ref, wo_ref, g_ref, wgu_ref, wd_ref, x1_ref, sf_ref, actt_ref, out_ref):
        x1 = x_ref[...] + _dot_tn(_get_blocks(ot_ref), wo_ref[...])
        x1_ref[...] = x1
        out_ref[...] = _ffn(x1, g_ref[...], wgu_ref, wd_ref, sf_ref, actt_ref)

    return _pc(body, [ot, x, wo, g, wgu, wd], name="ffn_fwd0", grid=(t // tm,),
               in_specs=[_qblocks(tm), _rows(tm, D), WHOLE, _acc(1, D), WHOLE, WHOLE],
               out_specs=[_rows(tm, D), _blk3(NDEV, tm, GUB), _tcols(4, tm), _rows(tm, D)],
               out_shape=[_sds((t, D), F32), _sds((NDEV, t, GUB), BF16), _sds((4, GUB, t), BF16), _sds((t, D), F32)], xchg=xchg)


def _ffn_fwd_final(x, g, wgu, wd, g_fin, tgt, tm, xchg=None):
    t = x.shape[0]

    def body(x_ref, g_ref, wgu_ref, wd_ref, gf_ref, t_ref, sf_ref, actt_ref, dx_ref, dxb_ref, part_ref):
        @pl.when(pl.program_id(0) == 0)
        def _():
            part_ref[...] = jnp.zeros_like(part_ref)

        gf = gf_ref[...]
        xh, r, y = _rms(_ffn(x_ref[...], g_ref[...], wgu_ref, wd_ref, sf_ref, actt_ref), gf)
        err = y - t_ref[...]
        dx, dg = _rms_bwd(err * (1.0 / D), xh, r, gf)
        dx_ref[...] = dx
        dxb_ref[...] = dx.astype(BF16)
        part_ref[0:1, :] += dg
        tok = jnp.sum(err * err, axis=-1, keepdims=True) * (1.0 / D)
        lane = lax.broadcasted_iota(jnp.int32, (1, D), 1)
        part_ref[1:2, :] += jnp.where(lane == 0, 0.5 * jnp.sum(tok, axis=0, keepdims=True), 0.0)

    return _pc(body, [x, g, wgu, wd, g_fin, tgt], name="ffn_fwd1", grid=(t // tm,),
               in_specs=[_rows(tm, D), _acc(1, D), WHOLE, WHOLE, _acc(1, D), _rows(tm, D)],
               out_specs=[_blk3(NDEV, tm, GUB), _tcols(4, tm), _rows(tm, D), _rows(tm, D), _acc(8, D)],
               out_shape=[_sds((NDEV, t, GUB), BF16), _sds((4, GUB, t), BF16), _sds((t, D), F32), _sds((t, D), BF16),
                          _sds((8, D), F32)], xchg=xchg)


def _ffn_bwd(dy, x, sf, g, wgu, wd, tm, name, wo=None, xchg=None):
    t = x.shape[0]

    def body(dy_ref, x_ref, sf_ref, g_ref, wgu_ref, wd_ref, *rest):
        wo_ref = rest[0] if wo is not None else None
        dgut_ref, h_ref, dx_ref, dg_ref = rest[wo is not None:][:4]

        @pl.when(pl.program_id(0) == 0)
        def _():
            dg_ref[...] = jnp.zeros_like(dg_ref)

        dy = dy_ref[...]
        dyb = dy.astype(BF16)
        gn = g_ref[...]
        xh, r, h = _rms(x_ref[...], gn)
        h_ref[...] = h.astype(BF16)
        dh = jnp.zeros_like(dy)
        for j in range(4):
            dact = _dot_nt(dyb, wd_ref[j])
            dgj = (dact * sf_ref[j + 4]).astype(BF16)
            duj = (dact * sf_ref[j]).astype(BF16)
            dgut_ref[j] = dgj.T
            dgut_ref[j + 4] = duj.T
            dh = dh + _dot(dgj, wgu_ref[j]) + _dot(duj, wgu_ref[j + 4])
        dx, dg = _rms_bwd(dh, xh, r, gn)
        dx = dy + dx
        dx_ref[...] = dx
        dg_ref[...] += dg
        if wo is not None:
            _put_blocks(rest[5], _dot(wo_ref[...], dx.astype(BF16).T).astype(BF16))

    extra = wo is not None
    return _pc(body, [dy, x, sf, g, wgu, wd] + [wo] * extra, name=name, grid=(t // tm,),
               in_specs=[_rows(tm, D), _rows(tm, D), _blk3(NDEV, tm, GUB), _acc(1, D), WHOLE, WHOLE] + [WHOLE] * extra,
               out_specs=[_tcols(NDEV, tm), _rows(tm, D), _rows(tm, D), _acc(1, D)] + [_qblocks(tm)] * extra,
               out_shape=[_sds((NDEV, GUB, t), BF16), _sds((t, D), BF16), _sds((t, D), F32), _sds((1, D), F32)]
               + [_sds((t // BLK, D, BLK), BF16)] * extra, xchg=xchg)


def _wgrad(at, b, tk, name, xchg=None, by_block=False):
    if by_block:
        na, (_, ma, _), t = 1, at.shape, at.shape[0] * BLK
        a_spec = pl.BlockSpec((tk // BLK, ma, BLK), lambda j, k: (k, 0, 0))
    else:
        na, ma, t = at.shape
        a_spec = pl.BlockSpec((1, ma, tk), lambda j, k: (j if na > 1 else 0, 0, k))
    nb, _, mb = b.shape
    nk = t // tk

    def body(a_ref, b_ref, out_ref, acc):
        k = pl.program_id(1)

        @pl.when(k == 0)
        def _():
            acc[...] = jnp.zeros_like(acc)

        acc[...] += _dot(_get_blocks(a_ref) if by_block else a_ref[0], b_ref[0])

        @pl.when(k == nk - 1)
        def _():
            out_ref[0] = acc[...].astype(BF16)

    outs, xo = _pc(body, [at, b], name=name, grid=(max(na, nb), nk),
                   in_specs=[a_spec, pl.BlockSpec((1, tk, mb), lambda j, k: (j if nb > 1 else 0, k, 0))],
                   out_specs=[pl.BlockSpec((1, ma, mb), lambda j, k: (j, 0, 0))], out_shape=[_sds((max(na, nb), ma, mb), BF16)],
                   scratch_shapes=[pltpu.VMEM((ma, mb), F32)], xchg=xchg)
    return outs[0] if xchg is None else (outs[0], xo)


def _conv_pre(x, g, w, b, tm, xchg=None):
    t = x.shape[0]

    def body(x_ref, g_ref, w_ref, b_ref, u_ref, glu_ref):
        _, _, h = _rms(x_ref[...], g_ref[...])
        hb = h.astype(BF16)
        for j in range(4):
            a = _dot(hb, w_ref[j]) + b_ref[:, j * PWB:(j + 1) * PWB]
            gt = _dot(hb, w_ref[j + 4]) + b_ref[:, D + j * PWB:D + (j + 1) * PWB]
            u_ref[j] = a.astype(BF16)
            u_ref[j + 4] = gt.astype(BF16)
            glu_ref[:, j * PWB:(j + 1) * PWB] = a * _sigmoid(gt)

    return _pc(body, [x, g, w, b], name="conv_pre", grid=(t // tm,),
               in_specs=[_rows(tm, D), _acc(1, D), WHOLE, _acc(1, 2 * D)], out_specs=[_blk3(NDEV, tm, PWB), _rows(tm, D)],
               out_shape=[_sds((NDEV, t, PWB), BF16), _sds((t, D), F32)], xchg=xchg)


def _halo_specs(t, tm):
    per = tm // HALO
    last = t // HALO - 1
    return [
        pl.BlockSpec((HALO, D), lambda i: (jnp.maximum(i * per - 1, 0), 0)),
        _rows(tm, D),
        pl.BlockSpec((HALO, D), lambda i: (jnp.minimum((i + 1) * per, last), 0)),
    ]


SUB = 8
CCH = 32
CLN = 256


def _fill_shifted(sh, prev, cur, nxt, tm):
    i = pl.program_id(0)
    rows = jnp.concatenate([jnp.where(i == 0, 0.0, prev[...]), cur[...], jnp.where(i == pl.num_programs(0) - 1, 0.0, nxt[...])], axis=0)
    n = tm + 2 * HALO - SUB
    for b in range(SUB):
        sh[b] = rows[b:b + n]


def _shifted(sh, off, r0, c0):
    return sh[off % SUB, r0 + off - off % SUB:r0 + off - off % SUB + CCH, c0:c0 + CLN]


def _conv_mid(glu, wdw, bdw, tm, xchg=None):
    t = glu.shape[0]

    def body(prev, cur, nxt, w_ref, b_ref, out_ref, sh):
        _fill_shifted(sh, prev, cur, nxt, tm)
        for c0 in range(0, D, CLN):
            for r0 in range(0, tm, CCH):
                acc = jnp.broadcast_to(b_ref[:, c0:c0 + CLN], (CCH, CLN))
                for k in range(CW):
                    acc = acc + w_ref[k:k + 1, c0:c0 + CLN] * _shifted(sh, k + HALO - CPAD, r0, c0)
                out_ref[r0:r0 + CCH, c0:c0 + CLN] = acc

    outs, xo = _pc(body, [glu, glu, glu, wdw, bdw], name="conv_mid", grid=(t // tm,),
                   in_specs=_halo_specs(t, tm) + [_acc(32, D), _acc(1, D)], out_specs=[_rows(tm, D)],
                   out_shape=[_sds((t, D), F32)], scratch_shapes=[pltpu.VMEM((SUB, tm + 2 * HALO - SUB, D), F32)], xchg=xchg)
    return outs[0], xo


def _conv_mid_bwd(dcv, glu, wdw, tm, xchg=None):
    t = glu.shape[0]

    def body(dp, dc, dn, gp, gc, gn, w_ref, dglu_ref, dw_ref, dsh, gsh):
        @pl.when(pl.program_id(0) == 0)
        def _():
            dw_ref[...] = jnp.zeros_like(dw_ref)

        _fill_shifted(dsh, dp, dc, dn, tm)
        _fill_shifted(gsh, gp, gc, gn, tm)
        for c0 in range(0, D, CLN):
            for r0 in range(0, tm, CCH):
                acc = jnp.zeros((CCH, CLN), F32)
                for k in range(CW):
                    acc = acc + w_ref[k:k + 1, c0:c0 + CLN] * _shifted(dsh, HALO + CPAD - k, r0, c0)
                dglu_ref[r0:r0 + CCH, c0:c0 + CLN] = acc
            for k in range(CW):
                dwk = jnp.zeros((SUB, CLN), F32)
                for r0 in range(0, tm, CCH):
                    prod = _shifted(dsh, HALO, r0, c0) * _shifted(gsh, k + HALO - CPAD, r0, c0)
                    for r in range(0, CCH, SUB):
                        dwk = dwk + prod[r:r + SUB]
                dw_ref[k:k + 1, c0:c0 + CLN] += jnp.sum(dwk, axis=0, keepdims=True)

    n = tm + 2 * HALO - SUB
    return _pc(body, [dcv, dcv, dcv, glu, glu, glu, wdw], name="conv_mid_bwd", grid=(t // tm,),
               in_specs=_halo_specs(t, tm) + _halo_specs(t, tm) + [_acc(32, D)], out_specs=[_rows(tm, D), _acc(32, D)],
               out_shape=[_sds((t, D), F32), _sds((32, D), F32)],
               scratch_shapes=[pltpu.VMEM((SUB, n, D), F32), pltpu.VMEM((SUB, n, D), F32)], xchg=xchg)


def _ln(cv, lg, lb):
    mu = jnp.mean(cv, axis=-1, keepdims=True)
    cc = cv - mu
    rs = lax.rsqrt(jnp.mean(cc * cc, axis=-1, keepdims=True) + EPS)
    lh = cc * rs
    return lh, rs, lh * lg + lb


def _conv_post(cv, x, lg, lb, w2, b2, tm):
    t = x.shape[0]

    def body(cv_ref, x_ref, lg_ref, lb_ref, w_ref, b_ref, st_ref, out_ref):
        _, _, ln = _ln(cv_ref[...], lg_ref[...], lb_ref[...])
        s = (ln * _sigmoid(ln)).astype(BF16)
        st_ref[...] = s.T
        out_ref[...] = x_ref[...] + _dot(s, w_ref[...]) + b_ref[...]

    return _pc(body, [cv, x, lg, lb, w2, b2], name="conv_post", grid=(t // tm,),
               in_specs=[_rows(tm, D), _rows(tm, D), _acc(1, D), _acc(1, D), WHOLE, _acc(1, D)],
               out_specs=[_cols(tm), _rows(tm, D)], out_shape=[_sds((D, t), BF16), _sds((t, D), F32)])[0]


def _conv_post_bwd(dx, cv, lg, lb, w2, tm, xchg=None):
    t = dx.shape[0]

    def body(dx_ref, cv_ref, lg_ref, lb_ref, w_ref, dcv_ref, part_ref):
        @pl.when(pl.program_id(0) == 0)
        def _():
            part_ref[...] = jnp.zeros_like(part_ref)

        dx = dx_ref[...]
        lg = lg_ref[...]
        lh, rs, ln = _ln(cv_ref[...], lg, lb_ref[...])
        sg = _sigmoid(ln)
        dln = _dot_nt(dx, w_ref[...]) * (sg * (1.0 + ln * (1.0 - sg)))
        dlh = dln * lg
        dcv = rs * (dlh - jnp.mean(dlh, axis=-1, keepdims=True) - lh * jnp.mean(dlh * lh, axis=-1, keepdims=True))
        dcv_ref[...] = dcv
        part_ref[0:1, :] += jnp.sum(dln * lh, axis=0, keepdims=True)
        part_ref[1:2, :] += jnp.sum(dln, axis=0, keepdims=True)
        part_ref[2:3, :] += jnp.sum(dcv, axis=0, keepdims=True)
        part_ref[3:4, :] += jnp.sum(dx, axis=0, keepdims=True)

    return _pc(body, [dx, cv, lg, lb, w2], name="conv_post_bwd", grid=(t // tm,),
               in_specs=[_rows(tm, D), _rows(tm, D), _acc(1, D), _acc(1, D), WHOLE], out_specs=[_rows(tm, D), _acc(8, D)],
               out_shape=[_sds((t, D), F32), _sds((8, D), F32)], xchg=xchg)


def _conv_pre_bwd(dglu, u, x, g, w, dx_out, tm):
    t = x.shape[0]

    def body(dglu_ref, u_ref, x_ref, g_ref, w_ref, dxo_ref, du_ref, ht_ref, dx_ref, dxb_ref, dg_ref, db_ref):
        @pl.when(pl.program_id(0) == 0)
        def _():
            dg_ref[...] = jnp.zeros_like(dg_ref)
            db_ref[...] = jnp.zeros_like(db_ref)

        gn = g_ref[...]
        xh, r, h = _rms(x_ref[...], gn)
        ht_ref[...] = h.astype(BF16).T
        dh = jnp.zeros_like(xh)
        for j in range(4):
            a = u_ref[j].astype(F32)
            sg = _sigmoid(u_ref[j + 4].astype(F32))
            dgl = dglu_ref[:, j * PWB:(j + 1) * PWB]
            da = dgl * sg
            dgt = dgl * a * sg * (1.0 - sg)
            db_ref[:, j * PWB:(j + 1) * PWB] += jnp.sum(da, axis=0, keepdims=True)
            db_ref[:, D + j * PWB:D + (j + 1) * PWB] += jnp.sum(dgt, axis=0, keepdims=True)
            da, dgt = da.astype(BF16), dgt.astype(BF16)
            du_ref[j] = da
            du_ref[j + 4] = dgt
            dh = dh + _dot_nt(da, w_ref[j]) + _dot_nt(dgt, w_ref[j + 4])
        dx, dg = _rms_bwd(dh, xh, r, gn)
        dx = dxo_ref[...] + dx
        dx_ref[...] = dx
        dxb_ref[...] = dx.astype(BF16)
        dg_ref[...] += dg

    return _pc(body, [dglu, u, x, g, w, dx_out], name="conv_pre_bwd", grid=(t // tm,),
               in_specs=[_rows(tm, D), _blk3(NDEV, tm, PWB), _rows(tm, D), _acc(1, D), WHOLE, _rows(tm, D)],
               out_specs=[_blk3(NDEV, tm, PWB), _cols(tm), _rows(tm, D), _rows(tm, D), _acc(1, D), _acc(1, 2 * D)],
               out_shape=[_sds((NDEV, t, PWB), BF16), _sds((D, t), BF16), _sds((t, D), F32), _sds((t, D), BF16),
                          _sds((1, D), F32), _sds((1, 2 * D), F32)])[0]


def _adamw(w, g, m, v):
    m = B1 * m + (1.0 - B1) * g
    v = B2 * v + (1.0 - B2) * (g * g)
    m_hat = m / (1.0 - B1 ** STEP)
    v_hat = v / (1.0 - B2 ** STEP)
    return -LR * (m_hat / (jnp.sqrt(v_hat) + AEPS) + WD * w), m, v


def _reduce_adamw(lands, w, m, v, tr, name, xchg=None):
    nl, r, c = w.shape

    def body(*refs):
        l_refs, (w_ref, m_ref, v_ref, g_ref, d_ref, nm_ref, nv_ref) = refs[:nl], refs[nl:]

        def total(ref):
            g = ref[0].astype(F32)
            for j in range(1, NDEV):
                g = g + ref[j].astype(F32)
            return g

        g = total(l_refs[0])
        for n in range(1, nl):
            g = jnp.where(pl.program_id(0) == n, total(l_refs[n]), g)
        g_ref[0] = g
        d_ref[0], nm_ref[0], nv_ref[0] = _adamw(w_ref[0], g, m_ref[0], v_ref[0])

    layer = pl.BlockSpec((1, tr, c), lambda l, i: (l, i, 0))
    outs, xo = _pc(body, [*lands, w, m, v], name=name, grid=(nl, r // tr),
                   in_specs=[pl.BlockSpec((NDEV, tr, c), lambda l, i: (0, i, 0))] * nl + [layer] * 3, out_specs=[layer] * 4,
                   out_shape=[_sds((nl, r, c), F32)] * 4, xchg=xchg)
    return outs if xchg is None else (outs, xo)


def _sum_parts(parts):
    _, r, c = parts.shape

    def body(p_ref, out_ref):
        s = p_ref[0]
        for j in range(1, NDEV):
            s = s + p_ref[j]
        out_ref[...] = s

    return _pc(body, [parts], name="sum_parts", in_specs=[WHOLE], out_specs=[WHOLE], out_shape=[_sds((r, c), F32)])[0][0]


def _adamw_small(w, g, m, v):
    def body(w_ref, g_ref, m_ref, v_ref, d_ref, nm_ref, nv_ref):
        d_ref[...], nm_ref[...], nv_ref[...] = _adamw(w_ref[...], g_ref[...], m_ref[...], v_ref[...])

    return _pc(body, [w, g, m, v], name="adamw_small", in_specs=[WHOLE] * 4, out_specs=[WHOLE] * 3,
               out_shape=[_sds(w.shape, F32)] * 3)[0]


def _rows128(a):
    a = a.astype(F32)
    if a.shape[-1] % 128:
        a = jnp.pad(a, [(0, 0)] * (a.ndim - 1) + [(0, 128 - a.shape[-1] % 128)])
    return a.reshape(-1, 128)


def _pack(arrs, rows):
    p = jnp.concatenate([_rows128(a) for a in arrs], axis=0)
    return jnp.pad(p, ((0, rows - p.shape[0]), (0, 0)))


SMALL = ("attn_norm", "ffn_norm", "final_norm", "attn_sink", "conv_norm", "conv_b_dw", "conv_ln_g", "conv_ln_b", "conv_b_pw2",
         "conv_b_pw1", "conv_w_dw")
SMALL_ROWS = 72


def _pack_small(d):
    return _pack([d[k] for k in SMALL], SMALL_ROWS)


def _unpack_small(p, like):
    out, r = {}, 0
    for k in SMALL:
        shp = like[k].shape
        n = -(-shp[-1] // 128) * (math.prod(shp[:-1]))
        blk = p[r:r + n]
        if shp[-1] % 128:
            blk = blk[:, :shp[-1]]
        out[k] = blk.reshape(shp)
        r += n
    return out


NAMES = ("attn_norm", "attn_w_qkv", "attn_w_o", "attn_sink", "conv_norm", "conv_w_pw1", "conv_b_pw1", "conv_w_dw", "conv_b_dw",
         "conv_ln_g", "conv_ln_b", "conv_w_pw2", "conv_b_pw2", "ffn_norm", "ffn_w_gu", "ffn_w_down", "final_norm")
TM = 256
TL = 512
TK = 2048


def _gu_t(a):
    return jnp.swapaxes(a, 1, 2)


def kernel(x, attn_norm, attn_w_qkv, attn_w_o, attn_sink, conv_norm, conv_w_pw1, conv_b_pw1, conv_w_dw, conv_b_dw, conv_ln_g, conv_ln_b, conv_w_pw2, conv_b_pw2, ffn_norm, ffn_w_gu, ffn_w_down, final_norm, loss_target, m_attn_norm, m_attn_w_qkv, m_attn_w_o, m_attn_sink, m_conv_norm, m_conv_w_pw1, m_conv_b_pw1, m_conv_w_dw, m_conv_b_dw, m_conv_ln_g, m_conv_ln_b, m_conv_w_pw2, m_conv_b_pw2, m_ffn_norm, m_ffn_w_gu, m_ffn_w_down, m_final_norm, v_attn_norm, v_attn_w_qkv, v_attn_w_o, v_attn_sink, v_conv_norm, v_conv_w_pw1, v_conv_b_pw1, v_conv_w_dw, v_conv_b_dw, v_conv_ln_g, v_conv_ln_b, v_conv_w_pw2, v_conv_b_pw2, v_ffn_norm, v_ffn_w_gu, v_ffn_w_down, v_final_norm):
    w = dict(zip(NAMES, (attn_norm, attn_w_qkv, attn_w_o, attn_sink, conv_norm, conv_w_pw1, conv_b_pw1, conv_w_dw, conv_b_dw, conv_ln_g,
                         conv_ln_b, conv_w_pw2, conv_b_pw2, ffn_norm, ffn_w_gu, ffn_w_down, final_norm)))
    m = dict(zip(NAMES, (m_attn_norm, m_attn_w_qkv, m_attn_w_o, m_attn_sink, m_conv_norm, m_conv_w_pw1, m_conv_b_pw1, m_conv_w_dw,
                         m_conv_b_dw, m_conv_ln_g, m_conv_ln_b, m_conv_w_pw2, m_conv_b_pw2, m_ffn_norm, m_ffn_w_gu, m_ffn_w_down,
                         m_final_norm)))
    v = dict(zip(NAMES, (v_attn_norm, v_attn_w_qkv, v_attn_w_o, v_attn_sink, v_conv_norm, v_conv_w_pw1, v_conv_b_pw1, v_conv_w_dw,
                         v_conv_b_dw, v_conv_ln_g, v_conv_ln_b, v_conv_w_pw2, v_conv_b_pw2, v_ffn_norm, v_ffn_w_gu, v_ffn_w_down,
                         v_final_norm)))
    me = 4 * lax.axis_index("x") + 2 * lax.axis_index("y") + lax.axis_index("c")
    for d in (w, m, v):
        d["ffn_w_gu"] = _gu_t(d["ffn_w_gu"])
    sh = {k: w[k][0].astype(BF16) for k in ("attn_w_qkv", "attn_w_o", "conv_w_pw1", "conv_w_pw2")}
    gu_b, down_b = w["ffn_w_gu"].astype(BF16), w["ffn_w_down"].astype(BF16)
    sh.update(ffn_w_gu0=gu_b[0], ffn_w_gu1=gu_b[1], ffn_w_down0=down_b[0], ffn_w_down1=down_b[1])
    x0, tgt = x[0], loss_target[0]
    t = x0.shape[0]
    tabs = _rope_tables(t)
    tl = min(TL, t)
    g_a, sink, g_f0, g_f1, g_fin = w["attn_norm"], w["attn_sink"][0], w["ffn_norm"][0:1], w["ffn_norm"][1:2], w["final_norm"][None]
    gather, scatter = False, True

    shard_rows = _pack([w["conv_w_dw"][0], jnp.zeros((1, 128), F32), w["conv_norm"], w["conv_b_dw"], w["conv_ln_g"], w["conv_ln_b"],
                        w["conv_b_pw2"], w["conv_b_pw1"]], 40)
    wqkv_g, sm = _exchange([sh["attn_w_qkv"], shard_rows], [gather] * 2, "gather_attn")
    wqkv = wqkv_g.transpose(1, 0, 2).reshape(D, QKV)

    def full_vec(row, n=1):
        return sm[:, row:row + n, :].reshape(1, NDEV * n * 128)

    w_dw, g_c, b_dw, ln_g, ln_b = sm[:, 0:32, :].transpose(1, 0, 2).reshape(32, D), full_vec(32), full_vec(33), full_vec(34), full_vec(35)
    b_pw2, b_pw1 = full_vec(36), full_vec(37, 2)

    (q_t, kv), (wo_g, wd0_g) = _attn_pre(x0, g_a, wqkv, tabs, tl, xchg=([sh["attn_w_o"], sh["ffn_w_down0"]], [gather] * 2))
    o_t, (wgu0,) = _attn_fwd(q_t, kv, sink, xchg=([sh["ffn_w_gu0"]], [gather]))
    wo, wd0 = wo_g.reshape(D, D), wd0_g.reshape(4, GUB, D)
    (x1, sf0, act0_t, x2), (wpw1, wgu1) = _ffn_fwd_attn(
        o_t, x0, wo, g_f0, wgu0, wd0, tl, xchg=([sh["conv_w_pw1"], sh["ffn_w_gu1"]], [gather] * 2))
    (u, glu), (wpw2_g,) = _conv_pre(x2, g_c, wpw1, b_pw1, tl, xchg=([sh["conv_w_pw2"]], [gather]))
    wpw2 = wpw2_g.reshape(D, D)
    cv, (wd1_g,) = _conv_mid(glu, w_dw, b_dw, tl, xchg=([sh["ffn_w_down1"]], [gather]))
    wd1 = wd1_g.reshape(4, GUB, D)
    s_t, x3 = _conv_post(cv, x2, ln_g, ln_b, wpw2, b_pw2, tl)
    (sf1, act1_t, dx4, dx4_b, fin), _ = _ffn_fwd_final(x3, g_f1, wgu1, wd1, g_fin, tgt, tl)

    land = {}
    tk, tk2 = min(TK, t), min(2 * TK, t)
    (dgu1_t, h3, dx3, dg_f1), _ = _ffn_bwd(dx4, x3, sf1, g_f1, wgu1, wd1, TM, "ffn_bwd1")
    dwgu1 = _wgrad(dgu1_t, h3[None], tk2, "dwgu1")
    dwd1 = _wgrad(act1_t, dx4_b[None], tk2, "dwd1").reshape(NDEV, DFF // NDEV, D)
    (dcv, cpart), (land["ffn_w_down1"],) = _conv_post_bwd(dx3, cv, ln_g, ln_b, wpw2, tl, xchg=([dwd1], [scatter]))
    dwpw2 = _wgrad(s_t[None], dx3[None], tk, "dwpw2").reshape(NDEV, D // NDEV, D)
    (dglu, dw_dw), (land["ffn_w_gu1"], land["conv_w_pw2"]) = _conv_mid_bwd(dcv, glu, w_dw, tl, xchg=([dwgu1, dwpw2], [scatter] * 2))
    du, h2_t, dx2, dx2_b, dg_c, db_pw1 = _conv_pre_bwd(dglu, u, x2, g_c, wpw1, dx3, tl)
    dwpw1 = _wgrad(h2_t[None], du, tk2, "dwpw1")
    (dgu0_t, h1, dx1, dg_f0, do_t), (land["conv_w_pw1"],) = _ffn_bwd(
        dx2, x1, sf0, g_f0, wgu0, wd0, TM, "ffn_bwd0", wo=wo, xchg=([dwpw1], [scatter]))
    dwgu0 = _wgrad(dgu0_t, h1[None], tk2, "dwgu0")
    dwd0 = _wgrad(act0_t, dx2_b[None], tk2, "dwd0").reshape(NDEV, DFF // NDEV, D)
    dwo = _wgrad(o_t, dx1[None], tk, "dwo", by_block=True).reshape(NDEV, D // NDEV, D)
    (dq_t, dk, dv, dsink), (land["ffn_w_gu0"], land["ffn_w_down0"]) = _attn_bwd(
        q_t, kv, do_t, sink, tl, xchg=([dwgu0, dwd0], [scatter] * 2))
    dqkv, h0_t, dx0, dg_a = _attn_pre_bwd(dq_t, dk, dv, x0, g_a, wqkv, tabs, dx1, tl)
    dwqkv, (land["attn_w_o"],) = _wgrad(h0_t[None], dqkv[None], tk, "dwqkv", xchg=([dwo], [scatter]))
    dwqkv = dwqkv[0].reshape(D, NDEV, QKV // NDEV).transpose(1, 0, 2)

    lane0 = (lax.broadcasted_iota(jnp.int32, (1, 128), 1) == 0).astype(F32)
    parts = _pack([dg_a, dg_f0, dg_f1, fin[0:1], dsink[0:1, :NH], fin[1, 0] * lane0, jnp.zeros((6, 128), F32), dg_c, cpart[2:3],
                   cpart[0:1], cpart[1:2], cpart[3:4], db_pw1, dw_dw.reshape(32, NDEV, 128)], 352)
    land["attn_w_qkv"], parts_g = _exchange([dwqkv, parts], [scatter, gather], "scatter_attn")
    red = _sum_parts(parts_g)

    def shard_rows_of(row, n=1):
        return lax.dynamic_slice_in_dim(red, row + n * me, n, axis=0)

    gs = {
        "attn_norm": red[0:8].reshape(1, D), "ffn_norm": red[8:24].reshape(2, D), "final_norm": red[24:32].reshape(D),
        "attn_sink": red[32:33, :NH], "conv_norm": shard_rows_of(40), "conv_b_dw": shard_rows_of(48), "conv_ln_g": shard_rows_of(56),
        "conv_ln_b": shard_rows_of(64), "conv_b_pw2": shard_rows_of(72), "conv_b_pw1": shard_rows_of(80, 2).reshape(1, PWB),
        "conv_w_dw": lax.dynamic_index_in_dim(red[96:352].reshape(32, NDEV, 128), me, axis=1, keepdims=False)[None, :CW],
    }
    loss = red[33, 0]

    grads, deltas, new_m, new_v = dict(gs), {}, {}, {}
    ds, ms, vs = _adamw_small(_pack_small(w), _pack_small(gs), _pack_small(m), _pack_small(v))
    deltas.update(_unpack_small(ds, gs))
    new_m.update(_unpack_small(ms, gs))
    new_v.update(_unpack_small(vs, gs))
    for k in ("attn_w_qkv", "attn_w_o", "conv_w_pw1", "conv_w_pw2", "ffn_w_gu", "ffn_w_down"):
        lands = [land[k + "0"], land[k + "1"]] if k.startswith("ffn") else [land[k]]
        tr = {1024: 256, 128: 128, 352: 176, GUB: 176}[w[k].shape[1]]
        grads[k], deltas[k], new_m[k], new_v[k] = _reduce_adamw(lands, w[k], m[k], v[k], tr, "adamw_" + k)
    for d in (grads, deltas, new_m, new_v):
        d["ffn_w_gu"] = _gu_t(d["ffn_w_gu"])
    return (loss, dx0[None], *[grads[k] for k in NAMES], *[deltas[k] for k in NAMES], *[new_m[k] for k in NAMES],
            *[new_v[k] for k in NAMES])
```

```python
import math

import jax
import jax.numpy as jnp
from jax import lax
from jax.experimental import pallas as pl
from jax.experimental.pallas import tpu as pltpu

F32 = jnp.float32
BF16 = jnp.bfloat16

D = 1024
NH = 16
NKV = 4
HD = 64
GROUP = NH // NKV
ROT = 16
THETA = 500000.0
BLK = 128
QKV = (NH + 2 * NKV) * HD
KOFF = NH * HD
VOFF = KOFF + NKV * HD
DFF = 2816
NDEV = 8
GUB = 2 * DFF // NDEV
PWB = 2 * D // NDEV
CW = 31
CPAD = 15
HALO = 16
EPS = 1e-6
NEG = -1e30
LR, B1, B2, AEPS, WD, STEP = 0.001, 0.9, 0.999, 1e-08, 0.01, 10

VMEM_LIMIT = 56 * 1024 * 1024
MESH = pl.DeviceIdType.MESH
WHOLE = pl.BlockSpec(memory_space=pltpu.VMEM)
ANY = pl.BlockSpec(memory_space=pl.ANY)


def _place():
    x, y, c = lax.axis_index("x"), lax.axis_index("y"), lax.axis_index("c")
    return x, y, c, 4 * x + 2 * y + c


def _peer(x, y, c, j):
    px = 1 - x if j & 4 else x
    py = 1 - y if j & 2 else y
    pc = 1 - c if j & 1 else c
    return (px, py, pc), 4 * px + 2 * py + pc


SIBLING = 1
OTHER_CHIPS = (2, 4, 6)
PASS_ON_LEAD = 3


class _Exchange:
    def __init__(self, src, dst, scatter, send, recv, loc):
        self.src, self.dst, self.scatter, self.send, self.recv, self.loc = src, dst, scatter, send, recv, loc
        self.x, self.y, self.c, self.me = _place()

    def _remote(self, k, j, s, d, to):
        peer, _ = _peer(self.x, self.y, self.c, to)
        return pltpu.make_async_remote_copy(src_ref=s, dst_ref=d, send_sem=self.send.at[k, j - 1], recv_sem=self.recv.at[k, j - 1],
                                            device_id=peer, device_id_type=MESH)

    def _slot(self, j):
        return _peer(self.x, self.y, self.c, j)[1]

    def local(self, k):
        if self.scatter[k]:
            return pltpu.make_async_copy(self.src[k].at[self.me], self.dst[k].at[0], self.loc.at[k])
        return pltpu.make_async_copy(self.src[k], self.dst[k].at[self.me], self.loc.at[k])

    def direct(self, k, j):
        if self.scatter[k]:
            return self._remote(k, j, self.src[k].at[self._slot(j)], self.dst[k].at[j], j)
        return self._remote(k, j, self.src[k], self.dst[k].at[self.me], j)

    def passed_on(self, k, j):
        rows = self.dst[k].at[self._slot(j)]
        return self._remote(k, j + 1, rows, rows, SIBLING)

    def arrival(self, k, j):
        rows = self.dst[k].at[j if self.scatter[k] else self._slot(j)]
        return self._remote(k, j, rows, rows, j)

    def sent(self, k):
        return tuple(range(1, NDEV)) if self.scatter[k] else (SIBLING,) + OTHER_CHIPS

    def start(self):
        for k in range(len(self.src)):
            self.local(k).start()
            for j in self.sent(k):
                self.direct(k, j).start()

    def pass_on(self):
        for k in range(len(self.src)):
            if not self.scatter[k]:
                for j in OTHER_CHIPS:
                    self.arrival(k, j).wait_recv()
                    self.passed_on(k, j).start()

    def finish(self):
        for k in range(len(self.src)):
            for j in range(1, NDEV):
                if self.scatter[k] or j not in OTHER_CHIPS:
                    self.arrival(k, j).wait_recv()
            for j in self.sent(k):
                self.direct(k, j).wait_send()
            if not self.scatter[k]:
                for j in OTHER_CHIPS:
                    self.passed_on(k, j).wait_send()
            self.local(k).wait()


def _call(body, **kw):
    return pl.pallas_call(body, **kw)


def _pc(body, operands, *, name, in_specs, out_specs, out_shape, grid=None, scratch_shapes=(), xchg=None):
    kw = dict(name=name, compiler_params=pltpu.CompilerParams(
        dimension_semantics=None if grid is None else ("arbitrary",) * len(grid), vmem_limit_bytes=VMEM_LIMIT,
        has_side_effects=xchg is not None))
    if grid is not None:
        kw["grid"] = grid
    if xchg is None:
        outs = _call(body, in_specs=list(in_specs), out_specs=list(out_specs), out_shape=list(out_shape),
                     scratch_shapes=list(scratch_shapes), **kw)(*operands)
        return list(outs), []
    arrs, scatter = xchg
    nx, n_in, n_out, n_s = len(arrs), len(in_specs), len(out_specs), len(scratch_shapes)

    def wrapped(*refs):
        ins, refs = refs[:n_in], refs[n_in:]
        src, refs = refs[:nx], refs[nx:]
        outs, refs = refs[:n_out], refs[n_out:]
        dst, refs = refs[:nx], refs[nx:]
        scr, (send, recv, loc) = refs[:n_s], refs[n_s:]
        if grid is None:
            _Exchange(src, dst, scatter, send, recv, loc).start()
            body(*ins, *outs, *scr)
            _Exchange(src, dst, scatter, send, recv, loc).pass_on()
            _Exchange(src, dst, scatter, send, recv, loc).finish()
            return

        step, steps = 0, 1
        for axis, n in enumerate(grid):
            step, steps = step * n + pl.program_id(axis), steps * n

        @pl.when(step == 0)
        def _():
            _Exchange(src, dst, scatter, send, recv, loc).start()

        body(*ins, *outs, *scr)

        @pl.when(step == max(steps - 1 - PASS_ON_LEAD, 0))
        def _():
            _Exchange(src, dst, scatter, send, recv, loc).pass_on()

        @pl.when(step == steps - 1)
        def _():
            _Exchange(src, dst, scatter, send, recv, loc).finish()

    xshape = [jax.ShapeDtypeStruct(a.shape if sc else (NDEV,) + a.shape, a.dtype) for a, sc in zip(arrs, scatter)]
    sems = [pltpu.SemaphoreType.DMA((nx, NDEV - 1)), pltpu.SemaphoreType.DMA((nx, NDEV - 1)), pltpu.SemaphoreType.DMA((nx,))]
    res = _call(wrapped, in_specs=list(in_specs) + [ANY] * nx, out_specs=list(out_specs) + [ANY] * nx,
                out_shape=list(out_shape) + xshape, scratch_shapes=list(scratch_shapes) + sems, **kw)(*operands, *arrs)
    return list(res[:n_out]), list(res[n_out:])


def _exchange(arrs, scatter, name):
    return _pc(lambda: None, [], name=name, in_specs=[], out_specs=[], out_shape=[], xchg=(arrs, scatter))[1]


def _rows(tm, width):
    return pl.BlockSpec((tm, width), lambda i: (i, 0))


def _cols(tm):
    return pl.BlockSpec((D, tm), lambda i: (0, i))


def _qblocks(tm):
    return pl.BlockSpec((tm // BLK, D, BLK), lambda i: (i, 0, 0))


def _put_blocks(ref, val):
    for b in range(ref.shape[0]):
        ref[b] = val[:, b * BLK:(b + 1) * BLK]


def _get_blocks(ref):
    return jnp.concatenate([ref[b] for b in range(ref.shape[0])], axis=1)


def _blk3(nb, tm, width):
    return pl.BlockSpec((nb, tm, width), lambda i: (0, i, 0))


def _acc(rows, width):
    return pl.BlockSpec((rows, width), lambda i: (0, 0))


def _sds(shape, dtype):
    return jax.ShapeDtypeStruct(shape, dtype)


def _dot(a, b):
    return jnp.dot(a.astype(BF16), b.astype(BF16), preferred_element_type=F32)


def _dot_nt(a, b):
    return lax.dot_general(a.astype(BF16), b.astype(BF16), (((1,), (1,)), ((), ())), preferred_element_type=F32)


def _dot_tn(a, b):
    return lax.dot_general(a.astype(BF16), b.astype(BF16), (((0,), (0,)), ((), ())), preferred_element_type=F32)


def _sigmoid(x):
    return 1.0 / (1.0 + jnp.exp(-x))


def _rms(x, g):
    r = lax.rsqrt(jnp.mean(x * x, axis=-1, keepdims=True) + EPS)
    xh = x * r
    return xh, r, xh * g


def _rms_bwd(dh, xh, r, g):
    dxh = dh * g
    dg = jnp.sum(dh * xh, axis=0, keepdims=True)
    dx = r * (dxh - xh * jnp.mean(dxh * xh, axis=-1, keepdims=True))
    return dx, dg


def _lanes(t, width):
    return jnp.tile(t, (1, width // t.shape[1]))


def _rope(z, c, sa, sb):
    w = z.shape[1]
    return z * _lanes(c, w) + pltpu.roll(z, w - 8, 1) * _lanes(sa, w) + pltpu.roll(z, 8, 1) * _lanes(sb, w)


def _rope_t(dz, c, sa, sb):
    w = dz.shape[1]
    return dz * _lanes(c, w) + pltpu.roll(dz * _lanes(sa, w), 8, 1) + pltpu.roll(dz * _lanes(sb, w), w - 8, 1)


def _rope_tables(t):
    pos = jnp.arange(t, dtype=F32)
    inv_freq = THETA ** (-jnp.arange(0, ROT, 2, dtype=F32) / ROT)
    ang = pos[:, None] * inv_freq[None, :]
    cos, sin = jnp.cos(ang), jnp.sin(ang)
    one = jnp.ones((t, HD - ROT), F32)
    zero = jnp.zeros((t, HD - ROT), F32)
    z8 = jnp.zeros((t, 8), F32)
    c = jnp.concatenate([cos, cos, one], axis=1)
    sa = jnp.concatenate([-sin, z8, zero], axis=1)
    sb = jnp.concatenate([z8, sin, zero], axis=1)
    return tuple(jnp.tile(a, (1, 2)) for a in (c, sa, sb))


KVW = NKV * HD
GW = GROUP * BLK


def _attn_pre(x, g, wqkv, tabs, tm, xchg=None):
    t = x.shape[0]

    def body(x_ref, g_ref, w_ref, c_ref, sa_ref, sb_ref, qt_ref, kv_ref):
        _, _, h = _rms(x_ref[...], g_ref[...])
        z = _dot(h, w_ref[...])
        c, sa, sb = c_ref[...], sa_ref[...], sb_ref[...]
        _put_blocks(qt_ref, (_rope(z[:, :KOFF], c, sa, sb) * 0.125).T.astype(BF16))
        kv_ref[:, :KVW] = _rope(z[:, KOFF:VOFF], c, sa, sb).astype(BF16)
        kv_ref[:, KVW:] = z[:, VOFF:].astype(BF16)

    return _pc(body, [x, g, wqkv, *tabs], name="attn_pre", grid=(t // tm,),
               in_specs=[_rows(tm, D), _acc(1, D), WHOLE, _rows(tm, 128), _rows(tm, 128), _rows(tm, 128)],
               out_specs=[_qblocks(tm), _rows(tm, 2 * KVW)],
               out_shape=[_sds((t // BLK, D, BLK), BF16), _sds((t, 2 * KVW), BF16)], xchg=xchg)


QB = 4


def _attn_specs(nblk):
    prev = lambda i: jnp.maximum(QB * i - 1, 0)
    nxt = lambda i: jnp.minimum(QB * (i + 1), nblk - 1)
    return [
        _qblocks(QB * BLK),
        pl.BlockSpec((BLK, KVW), lambda i: (prev(i), 0)),
        pl.BlockSpec((QB * BLK, KVW), lambda i: (i, 0)),
        pl.BlockSpec((BLK, KVW), lambda i: (nxt(i), 0)),
        pl.BlockSpec((BLK, KVW), lambda i: (prev(i), 1)),
        pl.BlockSpec((QB * BLK, KVW), lambda i: (i, 1)),
        pl.BlockSpec((BLK, KVW), lambda i: (nxt(i), 1)),
    ]


def _attn_bias():
    c = lax.broadcasted_iota(jnp.int32, (3, 3 * BLK, GW), 1)
    r = lax.broadcasted_iota(jnp.int32, (3, 3 * BLK, GW), 2) & (BLK - 1)
    slab = lax.broadcasted_iota(jnp.int32, (3, 3 * BLK, GW), 0)
    valid = (c >= r) & (c - 2 * BLK <= r) & ((slab != 0) | (c >= BLK)) & ((slab != 2) | (c < 2 * BLK))
    return jnp.where(valid, 0.0, NEG).astype(F32)


def _bias_of(bias_ref, blk, nblk):
    return bias_ref[jnp.where(blk == 0, 0, jnp.where(blk == nblk - 1, 2, 1))]


def _group(ref, b, kv):
    return jnp.concatenate([ref[b, (kv * GROUP + g) * HD:(kv * GROUP + g + 1) * HD, :] for g in range(GROUP)], axis=1)


def _group_sink(sink_ref, kv):
    return jnp.concatenate([jnp.full((1, BLK), sink_ref[kv * GROUP + g], F32) for g in range(GROUP)], axis=1)


def _attn_exp(k_h, qt_g, bias, sink_g):
    s = _dot(k_h, qt_g) + bias
    m = jnp.maximum(jnp.max(s, axis=0, keepdims=True), sink_g)
    e = jnp.exp(s - m)
    es = jnp.exp(sink_g - m)
    return e, 1.0 / (jnp.sum(e, axis=0, keepdims=True) + es), es


def _attn_fwd(qt, kv, sink, xchg=None):
    t = kv.shape[0]
    nblk = t // BLK

    def body(sink_ref, bias_ref, qt_ref, kp, kc, kn, vp, vc, vn, ot_ref):
        k = jnp.concatenate([kp[...], kc[...], kn[...]], axis=0)
        v = jnp.concatenate([vp[...], vc[...], vn[...]], axis=0)
        for b in range(QB):
            bias = _bias_of(bias_ref, QB * pl.program_id(0) + b, nblk)
            keys = slice(b * BLK, (b + 3) * BLK)
            for h in range(NKV):
                e, inv, _ = _attn_exp(k[keys, h * HD:(h + 1) * HD], _group(qt_ref, b, h), bias, _group_sink(sink_ref, h))
                ot_g = (_dot_tn(v[keys, h * HD:(h + 1) * HD], e) * inv).astype(BF16)
                for g in range(GROUP):
                    ot_ref[b, (h * GROUP + g) * HD:(h * GROUP + g + 1) * HD, :] = ot_g[:, g * BLK:(g + 1) * BLK]

    outs, xo = _pc(body, [sink, _attn_bias(), qt] + [kv] * 6, name="attn_fwd", grid=(nblk // QB,),
                   in_specs=[pl.BlockSpec(memory_space=pltpu.SMEM), WHOLE] + _attn_specs(nblk),
                   out_specs=[_qblocks(QB * BLK)], out_shape=[_sds((nblk, D, BLK), BF16)], xchg=xchg)
    return outs[0], xo


def _attn_bwd(qt, kv, dot, sink, pad, xchg=None):
    t = kv.shape[0]
    nblk = t // BLK

    def body(sink_ref, bias_ref, qt_ref, kp, kc, kn, vp, vc, vn, dot_ref, dqt_ref, dk_ref, dv_ref, ds_ref):
        i = pl.program_id(0)

        @pl.when(i == 0)
        def _():
            dk_ref[...] = jnp.zeros_like(dk_ref)
            dv_ref[...] = jnp.zeros_like(dv_ref)
            ds_ref[...] = jnp.zeros_like(ds_ref)

        k = jnp.concatenate([kp[...], kc[...], kn[...]], axis=0)
        v = jnp.concatenate([vp[...], vc[...], vn[...]], axis=0)
        lane = lax.broadcasted_iota(jnp.int32, (1, 128), 1)
        dsink = jnp.zeros((1, 128), F32)
        for b in range(QB):
            blk = QB * i + b
            bias = _bias_of(bias_ref, blk, nblk)
            keys = slice(b * BLK, (b + 3) * BLK)
            rows = pl.ds(pl.multiple_of(blk * BLK + (pad - BLK), BLK), 3 * BLK)
            for h in range(NKV):
                k_h, v_h = k[keys, h * HD:(h + 1) * HD], v[keys, h * HD:(h + 1) * HD]
                qt_g, dot_g = _group(qt_ref, b, h), _group(dot_ref, b, h)
                e, inv, es = _attn_exp(k_h, qt_g, bias, _group_sink(sink_ref, h))
                p, ps = e * inv, es * inv
                dp = _dot(v_h, dot_g)
                delta = jnp.sum(p * dp, axis=0, keepdims=True)
                ds = (p * (dp - delta)).astype(BF16)
                dqt_g = _dot_tn(k_h, ds)
                dk_ref[rows, h * HD:(h + 1) * HD] += _dot_nt(ds, qt_g)
                dv_ref[rows, h * HD:(h + 1) * HD] += _dot_nt(p, dot_g)
                psd = ps * delta
                for g in range(GROUP):
                    n = h * GROUP + g
                    dqt_ref[b, n * HD:(n + 1) * HD, :] = dqt_g[:, g * BLK:(g + 1) * BLK]
                    dsink = dsink - jnp.where(lane == n, jnp.sum(psd[:, g * BLK:(g + 1) * BLK], axis=1, keepdims=True), 0.0)
        ds_ref[0:1, :] += dsink

    outs, xo = _pc(body, [sink, _attn_bias(), qt] + [kv] * 6 + [dot], name="attn_bwd", grid=(nblk // QB,),
                   in_specs=[pl.BlockSpec(memory_space=pltpu.SMEM), WHOLE] + _attn_specs(nblk) + [_qblocks(QB * BLK)],
                   out_specs=[_qblocks(QB * BLK), _acc(t + 2 * pad, KVW), _acc(t + 2 * pad, KVW), _acc(8, 128)],
                   out_shape=[_sds((nblk, D, BLK), F32), _sds((t + 2 * pad, KVW), F32), _sds((t + 2 * pad, KVW), F32),
                              _sds((8, 128), F32)], xchg=xchg)
    return outs, xo


def _attn_pre_bwd(dqt, dk, dv, x, g, wqkv, tabs, dx_out, tm):
    t = x.shape[0]

    def body(dqt_ref, dk_ref, dv_ref, x_ref, g_ref, w_ref, c_ref, sa_ref, sb_ref, dxo_ref, dqkv_ref, ht_ref, dx_ref, dg_ref):
        @pl.when(pl.program_id(0) == 0)
        def _():
            dg_ref[...] = jnp.zeros_like(dg_ref)

        c, sa, sb = c_ref[...], sa_ref[...], sb_ref[...]
        dqkv_ref[:, :KOFF] = _rope_t(_get_blocks(dqt_ref).T * 0.125, c, sa, sb).astype(BF16)
        dqkv_ref[:, KOFF:VOFF] = _rope_t(dk_ref[...], c, sa, sb).astype(BF16)
        dqkv_ref[:, VOFF:] = dv_ref[...].astype(BF16)
        g = g_ref[...]
        xh, r, h = _rms(x_ref[...], g)
        ht_ref[...] = h.astype(BF16).T
        dh = _dot_nt(dqkv_ref[...], w_ref[...])
        dx, dg = _rms_bwd(dh, xh, r, g)
        dx_ref[...] = dxo_ref[...] + dx
        dg_ref[...] += dg

    return _pc(body, [dqt, dk, dv, x, g, wqkv, *tabs, dx_out], name="attn_pre_bwd", grid=(t // tm,),
               in_specs=[_qblocks(tm), pl.BlockSpec((tm, KVW), lambda i: (i + 1, 0)), pl.BlockSpec((tm, KVW), lambda i: (i + 1, 0)),
                         _rows(tm, D), _acc(1, D), WHOLE, _rows(tm, 128), _rows(tm, 128), _rows(tm, 128), _rows(tm, D)],
               out_specs=[_rows(tm, QKV), _cols(tm), _rows(tm, D), _acc(1, D)],
               out_shape=[_sds((t, QKV), BF16), _sds((D, t), BF16), _sds((t, D), F32), _sds((1, D), F32)])[0]


def _ffn(x, g, wgu_ref, wd_ref, sf_ref, actt_ref):
    _, _, h = _rms(x, g)
    hb = h.astype(BF16)
    y = x
    for j in range(4):
        gj = _dot_nt(hb, wgu_ref[j])
        uj = _dot_nt(hb, wgu_ref[j + 4])
        sg = _sigmoid(gj)
        silu = gj * sg
        act = (silu * uj).astype(BF16)
        sf_ref[j] = silu.astype(BF16)
        sf_ref[j + 4] = (sg * (1.0 + gj * (1.0 - sg)) * uj).astype(BF16)
        actt_ref[j] = act.T
        y = y + _dot(act, wd_ref[j])
    return y


def _tcols(nb, tm):
    return pl.BlockSpec((nb, GUB, tm), lambda i: (0, 0, i))


def _ffn_fwd_attn(ot, x, wo, g, wgu, wd, tm, xchg=None):
    t = x.shape[0]

    def body(ot_ref, x_ref, wo_ref, g_ref, wgu_ref, wd_ref, x1_ref, sf_ref, actt_ref, out_ref):
        x1 = x_ref[...] + _dot_tn(_get_blocks(ot_ref), wo_ref[...])
        x1_ref[...] = x1
        out_ref[...] = _ffn(x1, g_ref[...], wgu_ref, wd_ref, sf_ref, actt_ref)

    return _pc(body, [ot, x, wo, g, wgu, wd], name="ffn_fwd0", grid=(t // tm,),
               in_specs=[_qblocks(tm), _rows(tm, D), WHOLE, _acc(1, D), WHOLE, WHOLE],
               out_specs=[_rows(tm, D), _blk3(NDEV, tm, GUB), _tcols(4, tm), _rows(tm, D)],
               out_shape=[_sds((t, D), F32), _sds((NDEV, t, GUB), BF16), _sds((4, GUB, t), BF16), _sds((t, D), F32)], xchg=xchg)


def _ffn_fwd_final(x, g, wgu, wd, g_fin, tgt, tm, xchg=None):
    t = x.shape[0]

    def body(x_ref, g_ref, wgu_ref, wd_ref, gf_ref, t_ref, sf_ref, actt_ref, dx_ref, dxb_ref, part_ref):
        @pl.when(pl.program_id(0) == 0)
        def _():
            part_ref[...] = jnp.zeros_like(part_ref)

        gf = gf_ref[...]
        xh, r, y = _rms(_ffn(x_ref[...], g_ref[...], wgu_ref, wd_ref, sf_ref, actt_ref), gf)
        err = y - t_ref[...]
        dx, dg = _rms_bwd(err * (1.0 / D), xh, r, gf)
        dx_ref[...] = dx
        dxb_ref[...] = dx.astype(BF16)
        part_ref[0:1, :] += dg
        tok = jnp.sum(err * err, axis=-1, keepdims=True) * (1.0 / D)
        lane = lax.broadcasted_iota(jnp.int32, (1, D), 1)
        part_ref[1:2, :] += jnp.where(lane == 0, 0.5 * jnp.sum(tok, axis=0, keepdims=True), 0.0)

    return _pc(body, [x, g, wgu, wd, g_fin, tgt], name="ffn_fwd1", grid=(t // tm,),
               in_specs=[_rows(tm, D), _acc(1, D), WHOLE, WHOLE, _acc(1, D), _rows(tm, D)],
               out_specs=[_blk3(NDEV, tm, GUB), _tcols(4, tm), _rows(tm, D), _rows(tm, D), _acc(8, D)],
               out_shape=[_sds((NDEV, t, GUB), BF16), _sds((4, GUB, t), BF16), _sds((t, D), F32), _sds((t, D), BF16),
                          _sds((8, D), F32)], xchg=xchg)


def _ffn_bwd(dy, x, sf, g, wgu, wd, tm, name, wo=None, xchg=None):
    t = x.shape[0]

    def body(dy_ref, x_ref, sf_ref, g_ref, wgu_ref, wd_ref, *rest):
        wo_ref = rest[0] if wo is not None else None
        dgut_ref, h_ref, dx_ref, dxb_ref, dg_ref = rest[wo is not None:][:5]

        @pl.when(pl.program_id(0) == 0)
        def _():
            dg_ref[...] = jnp.zeros_like(dg_ref)

        dy = dy_ref[...]
        dyb = dy.astype(BF16)
        gn = g_ref[...]
        xh, r, h = _rms(x_ref[...], gn)
        h_ref[...] = h.astype(BF16)
        dh = jnp.zeros_like(dy)
        for j in range(4):
            dact = _dot_nt(dyb, wd_ref[j])
            dgj = (dact * sf_ref[j + 4]).astype(BF16)
            duj = (dact * sf_ref[j]).astype(BF16)
            dgut_ref[j] = dgj.T
            dgut_ref[j + 4] = duj.T
            dh = dh + _dot(dgj, wgu_ref[j]) + _dot(duj, wgu_ref[j + 4])
        dx, dg = _rms_bwd(dh, xh, r, gn)
        dx = dy + dx
        dx_ref[...] = dx
        dxb = dx.astype(BF16)
        dxb_ref[...] = dxb
        dg_ref[...] += dg
        if wo is not None:
            _put_blocks(rest[6], _dot(wo_ref[...], dxb.T).astype(BF16))

    extra = wo is not None
    return _pc(body, [dy, x, sf, g, wgu, wd] + [wo] * extra, name=name, grid=(t // tm,),
               in_specs=[_rows(tm, D), _rows(tm, D), _blk3(NDEV, tm, GUB), _acc(1, D), WHOLE, WHOLE] + [WHOLE] * extra,
               out_specs=[_tcols(NDEV, tm), _rows(tm, D), _rows(tm, D), _rows(tm, D), _acc(1, D)] + [_qblocks(tm)] * extra,
               out_shape=[_sds((NDEV, GUB, t), BF16), _sds((t, D), BF16), _sds((t, D), F32), _sds((t, D), BF16), _sds((1, D), F32)]
               + [_sds((t // BLK, D, BLK), BF16)] * extra, xchg=xchg)


def _wgrad(at, b, tk, name, xchg=None, by_block=False):
    if by_block:
        na, (_, ma, _), t = 1, at.shape, at.shape[0] * BLK
        a_spec = pl.BlockSpec((tk // BLK, ma, BLK), lambda j, k: (k, 0, 0))
    else:
        na, ma, t = at.shape
        a_spec = pl.BlockSpec((1, ma, tk), lambda j, k: (j if na > 1 else 0, 0, k))
    nb, _, mb = b.shape
    nk = t // tk

    def body(a_ref, b_ref, out_ref, acc):
        k = pl.program_id(1)

        @pl.when(k == 0)
        def _():
            acc[...] = jnp.zeros_like(acc)

        acc[...] += _dot(_get_blocks(a_ref) if by_block else a_ref[0], b_ref[0])

        @pl.when(k == nk - 1)
        def _():
            out_ref[0] = acc[...].astype(BF16)

    outs, xo = _pc(body, [at, b], name=name, grid=(max(na, nb), nk),
                   in_specs=[a_spec, pl.BlockSpec((1, tk, mb), lambda j, k: (j if nb > 1 else 0, k, 0))],
                   out_specs=[pl.BlockSpec((1, ma, mb), lambda j, k: (j, 0, 0))], out_shape=[_sds((max(na, nb), ma, mb), BF16)],
                   scratch_shapes=[pltpu.VMEM((ma, mb), F32)], xchg=xchg)
    return outs[0] if xchg is None else (outs[0], xo)


def _conv_pre(x, g, w, b, tm, xchg=None):
    t = x.shape[0]

    def body(x_ref, g_ref, w_ref, b_ref, u_ref, glu_ref):
        _, _, h = _rms(x_ref[...], g_ref[...])
        hb = h.astype(BF16)
        for j in range(4):
            a = _dot(hb, w_ref[j]) + b_ref[:, j * PWB:(j + 1) * PWB]
            gt = _dot(hb, w_ref[j + 4]) + b_ref[:, D + j * PWB:D + (j + 1) * PWB]
            u_ref[j] = a.astype(BF16)
            u_ref[j + 4] = gt.astype(BF16)
            glu_ref[:, j * PWB:(j + 1) * PWB] = a * _sigmoid(gt)

    return _pc(body, [x, g, w, b], name="conv_pre", grid=(t // tm,),
               in_specs=[_rows(tm, D), _acc(1, D), WHOLE, _acc(1, 2 * D)], out_specs=[_blk3(NDEV, tm, PWB), _rows(tm, D)],
               out_shape=[_sds((NDEV, t, PWB), BF16), _sds((t, D), F32)], xchg=xchg)


def _halo_specs(t, tm):
    per = tm // HALO
    last = t // HALO - 1
    return [
        pl.BlockSpec((HALO, D), lambda i: (jnp.maximum(i * per - 1, 0), 0)),
        _rows(tm, D),
        pl.BlockSpec((HALO, D), lambda i: (jnp.minimum((i + 1) * per, last), 0)),
    ]


SUB = 8
CCH = 32
CLN = 256


def _fill_shifted(sh, prev, cur, nxt, tm):
    i = pl.program_id(0)
    rows = jnp.concatenate([jnp.where(i == 0, 0.0, prev[...]), cur[...], jnp.where(i == pl.num_programs(0) - 1, 0.0, nxt[...])], axis=0)
    n = tm + 2 * HALO - SUB
    for b in range(SUB):
        sh[b] = rows[b:b + n]


def _shifted(sh, off, r0, c0):
    return sh[off % SUB, r0 + off - off % SUB:r0 + off - off % SUB + CCH, c0:c0 + CLN]


def _conv_mid(glu, wdw, bdw, tm, xchg=None):
    t = glu.shape[0]

    def body(prev, cur, nxt, w_ref, b_ref, out_ref, sh):
        _fill_shifted(sh, prev, cur, nxt, tm)
        for c0 in range(0, D, CLN):
            for r0 in range(0, tm, CCH):
                acc = jnp.broadcast_to(b_ref[:, c0:c0 + CLN], (CCH, CLN))
                for k in range(CW):
                    acc = acc + w_ref[k:k + 1, c0:c0 + CLN] * _shifted(sh, k + HALO - CPAD, r0, c0)
                out_ref[r0:r0 + CCH, c0:c0 + CLN] = acc

    outs, xo = _pc(body, [glu, glu, glu, wdw, bdw], name="conv_mid", grid=(t // tm,),
                   in_specs=_halo_specs(t, tm) + [_acc(32, D), _acc(1, D)], out_specs=[_rows(tm, D)],
                   out_shape=[_sds((t, D), F32)], scratch_shapes=[pltpu.VMEM((SUB, tm + 2 * HALO - SUB, D), F32)], xchg=xchg)
    return outs[0], xo


def _conv_mid_bwd(dcv, glu, wdw, tm, xchg=None):
    t = glu.shape[0]

    def body(dp, dc, dn, gp, gc, gn, w_ref, dglu_ref, dw_ref, dsh, gsh):
        @pl.when(pl.program_id(0) == 0)
        def _():
            dw_ref[...] = jnp.zeros_like(dw_ref)

        _fill_shifted(dsh, dp, dc, dn, tm)
        _fill_shifted(gsh, gp, gc, gn, tm)
        for c0 in range(0, D, CLN):
            for r0 in range(0, tm, CCH):
                acc = jnp.zeros((CCH, CLN), F32)
                for k in range(CW):
                    acc = acc + w_ref[k:k + 1, c0:c0 + CLN] * _shifted(dsh, HALO + CPAD - k, r0, c0)
                dglu_ref[r0:r0 + CCH, c0:c0 + CLN] = acc
            for k in range(CW):
                dwk = jnp.zeros((SUB, CLN), F32)
                for r0 in range(0, tm, CCH):
                    prod = _shifted(dsh, HALO, r0, c0) * _shifted(gsh, k + HALO - CPAD, r0, c0)
                    for r in range(0, CCH, SUB):
                        dwk = dwk + prod[r:r + SUB]
                dw_ref[k:k + 1, c0:c0 + CLN] += jnp.sum(dwk, axis=0, keepdims=True)

    n = tm + 2 * HALO - SUB
    return _pc(body, [dcv, dcv, dcv, glu, glu, glu, wdw], name="conv_mid_bwd", grid=(t // tm,),
               in_specs=_halo_specs(t, tm) + _halo_specs(t, tm) + [_acc(32, D)], out_specs=[_rows(tm, D), _acc(32, D)],
               out_shape=[_sds((t, D), F32), _sds((32, D), F32)],
               scratch_shapes=[pltpu.VMEM((SUB, n, D), F32), pltpu.VMEM((SUB, n, D), F32)], xchg=xchg)


def _ln(cv, lg, lb):
    mu = jnp.mean(cv, axis=-1, keepdims=True)
    cc = cv - mu
    rs = lax.rsqrt(jnp.mean(cc * cc, axis=-1, keepdims=True) + EPS)
    lh = cc * rs
    return lh, rs, lh * lg + lb


def _conv_post(cv, x, lg, lb, w2, b2, tm):
    t = x.shape[0]

    def body(cv_ref, x_ref, lg_ref, lb_ref, w_ref, b_ref, st_ref, out_ref):
        _, _, ln = _ln(cv_ref[...], lg_ref[...], lb_ref[...])
        s = (ln * _sigmoid(ln)).astype(BF16)
        st_ref[...] = s.T
        out_ref[...] = x_ref[...] + _dot(s, w_ref[...]) + b_ref[...]

    return _pc(body, [cv, x, lg, lb, w2, b2], name="conv_post", grid=(t // tm,),
               in_specs=[_rows(tm, D), _rows(tm, D), _acc(1, D), _acc(1, D), WHOLE, _acc(1, D)],
               out_specs=[_cols(tm), _rows(tm, D)], out_shape=[_sds((D, t), BF16), _sds((t, D), F32)])[0]


def _conv_post_bwd(dx, cv, lg, lb, w2, tm, xchg=None):
    t = dx.shape[0]

    def body(dx_ref, cv_ref, lg_ref, lb_ref, w_ref, dcv_ref, part_ref):
        @pl.when(pl.program_id(0) == 0)
        def _():
            part_ref[...] = jnp.zeros_like(part_ref)

        dx = dx_ref[...]
        lg = lg_ref[...]
        lh, rs, ln = _ln(cv_ref[...], lg, lb_ref[...])
        sg = _sigmoid(ln)
        dln = _dot_nt(dx, w_ref[...]) * (sg * (1.0 + ln * (1.0 - sg)))
        dlh = dln * lg
        dcv = rs * (dlh - jnp.mean(dlh, axis=-1, keepdims=True) - lh * jnp.mean(dlh * lh, axis=-1, keepdims=True))
        dcv_ref[...] = dcv
        part_ref[0:1, :] += jnp.sum(dln * lh, axis=0, keepdims=True)
        part_ref[1:2, :] += jnp.sum(dln, axis=0, keepdims=True)
        part_ref[2:3, :] += jnp.sum(dcv, axis=0, keepdims=True)
        part_ref[3:4, :] += jnp.sum(dx, axis=0, keepdims=True)

    return _pc(body, [dx, cv, lg, lb, w2], name="conv_post_bwd", grid=(t // tm,),
               in_specs=[_rows(tm, D), _rows(tm, D), _acc(1, D), _acc(1, D), WHOLE], out_specs=[_rows(tm, D), _acc(8, D)],
               out_shape=[_sds((t, D), F32), _sds((8, D), F32)], xchg=xchg)


def _conv_pre_bwd(dglu, u, x, g, w, dx_out, tm):
    t = x.shape[0]

    def body(dglu_ref, u_ref, x_ref, g_ref, w_ref, dxo_ref, du_ref, ht_ref, dx_ref, dxb_ref, dg_ref, db_ref):
        @pl.when(pl.program_id(0) == 0)
        def _():
            dg_ref[...] = jnp.zeros_like(dg_ref)
            db_ref[...] = jnp.zeros_like(db_ref)

        gn = g_ref[...]
        xh, r, h = _rms(x_ref[...], gn)
        ht_ref[...] = h.astype(BF16).T
        dh = jnp.zeros_like(xh)
        for j in range(4):
            a = u_ref[j].astype(F32)
            sg = _sigmoid(u_ref[j + 4].astype(F32))
            dgl = dglu_ref[:, j * PWB:(j + 1) * PWB]
            da = dgl * sg
            dgt = dgl * a * sg * (1.0 - sg)
            db_ref[:, j * PWB:(j + 1) * PWB] += jnp.sum(da, axis=0, keepdims=True)
            db_ref[:, D + j * PWB:D + (j + 1) * PWB] += jnp.sum(dgt, axis=0, keepdims=True)
            da, dgt = da.astype(BF16), dgt.astype(BF16)
            du_ref[j] = da
            du_ref[j + 4] = dgt
            dh = dh + _dot_nt(da, w_ref[j]) + _dot_nt(dgt, w_ref[j + 4])
        dx, dg = _rms_bwd(dh, xh, r, gn)
        dx = dxo_ref[...] + dx
        dx_ref[...] = dx
        dxb_ref[...] = dx.astype(BF16)
        dg_ref[...] += dg

    return _pc(body, [dglu, u, x, g, w, dx_out], name="conv_pre_bwd", grid=(t // tm,),
               in_specs=[_rows(tm, D), _blk3(NDEV, tm, PWB), _rows(tm, D), _acc(1, D), WHOLE, _rows(tm, D)],
               out_specs=[_blk3(NDEV, tm, PWB), _cols(tm), _rows(tm, D), _rows(tm, D), _acc(1, D), _acc(1, 2 * D)],
               out_shape=[_sds((NDEV, t, PWB), BF16), _sds((D, t), BF16), _sds((t, D), F32), _sds((t, D), BF16),
                          _sds((1, D), F32), _sds((1, 2 * D), F32)])[0]


def _adamw(w, g, m, v):
    m = B1 * m + (1.0 - B1) * g
    v = B2 * v + (1.0 - B2) * (g * g)
    m_hat = m / (1.0 - B1 ** STEP)
    v_hat = v / (1.0 - B2 ** STEP)
    return -LR * (m_hat / (jnp.sqrt(v_hat) + AEPS) + WD * w), m, v


def _reduce_adamw(lands, w, m, v, tr, name, xchg=None):
    nl, r, c = w.shape

    def body(*refs):
        l_refs, (w_ref, m_ref, v_ref, g_ref, d_ref, nm_ref, nv_ref) = refs[:nl], refs[nl:]

        def total(ref):
            g = ref[0].astype(F32)
            for j in range(1, NDEV):
                g = g + ref[j].astype(F32)
            return g

        g = total(l_refs[0])
        for n in range(1, nl):
            g = jnp.where(pl.program_id(0) == n, total(l_refs[n]), g)
        g_ref[0] = g
        d_ref[0], nm_ref[0], nv_ref[0] = _adamw(w_ref[0], g, m_ref[0], v_ref[0])

    layer = pl.BlockSpec((1, tr, c), lambda l, i: (l, i, 0))
    outs, xo = _pc(body, [*lands, w, m, v], name=name, grid=(nl, r // tr),
                   in_specs=[pl.BlockSpec((NDEV, tr, c), lambda l, i: (0, i, 0))] * nl + [layer] * 3, out_specs=[layer] * 4,
                   out_shape=[_sds((nl, r, c), F32)] * 4, xchg=xchg)
    return outs if xchg is None else (outs, xo)


def _sum_parts(parts):
    _, r, c = parts.shape

    def body(p_ref, out_ref):
        s = p_ref[0]
        for j in range(1, NDEV):
            s = s + p_ref[j]
        out_ref[...] = s

    return _pc(body, [parts], name="sum_parts", in_specs=[WHOLE], out_specs=[WHOLE], out_shape=[_sds((r, c), F32)])[0][0]


def _adamw_small(w, g, m, v):
    def body(w_ref, g_ref, m_ref, v_ref, d_ref, nm_ref, nv_ref):
        d_ref[...], nm_ref[...], nv_ref[...] = _adamw(w_ref[...], g_ref[...], m_ref[...], v_ref[...])

    return _pc(body, [w, g, m, v], name="adamw_small", in_specs=[WHOLE] * 4, out_specs=[WHOLE] * 3,
               out_shape=[_sds(w.shape, F32)] * 3)[0]


def _rows128(a):
    a = a.astype(F32)
    if a.shape[-1] % 128:
        a = jnp.pad(a, [(0, 0)] * (a.ndim - 1) + [(0, 128 - a.shape[-1] % 128)])
    return a.reshape(-1, 128)


def _pack(arrs, rows):
    p = jnp.concatenate([_rows128(a) for a in arrs], axis=0)
    return jnp.pad(p, ((0, rows - p.shape[0]), (0, 0)))


SMALL = ("attn_norm", "ffn_norm", "final_norm", "attn_sink", "conv_norm", "conv_b_dw", "conv_ln_g", "conv_ln_b", "conv_b_pw2",
         "conv_b_pw1", "conv_w_dw")
SMALL_ROWS = 72


def _pack_small(d):
    return _pack([d[k] for k in SMALL], SMALL_ROWS)


def _unpack_small(p, like):
    out, r = {}, 0
    for k in SMALL:
        shp = like[k].shape
        n = -(-shp[-1] // 128) * (math.prod(shp[:-1]))
        blk = p[r:r + n]
        if shp[-1] % 128:
            blk = blk[:, :shp[-1]]
        out[k] = blk.reshape(shp)
        r += n
    return out


NAMES = ("attn_norm", "attn_w_qkv", "attn_w_o", "attn_sink", "conv_norm", "conv_w_pw1", "conv_b_pw1", "conv_w_dw", "conv_b_dw",
         "conv_ln_g", "conv_ln_b", "conv_w_pw2", "conv_b_pw2", "ffn_norm", "ffn_w_gu", "ffn_w_down", "final_norm")
TM = 256
TL = 512
TK = 2048


def _gu_t(a):
    return jnp.swapaxes(a, 1, 2)


def kernel(x, attn_norm, attn_w_qkv, attn_w_o, attn_sink, conv_norm, conv_w_pw1, conv_b_pw1, conv_w_dw, conv_b_dw, conv_ln_g, conv_ln_b, conv_w_pw2, conv_b_pw2, ffn_norm, ffn_w_gu, ffn_w_down, final_norm, loss_target, m_attn_norm, m_attn_w_qkv, m_attn_w_o, m_attn_sink, m_conv_norm, m_conv_w_pw1, m_conv_b_pw1, m_conv_w_dw, m_conv_b_dw, m_conv_ln_g, m_conv_ln_b, m_conv_w_pw2, m_conv_b_pw2, m_ffn_norm, m_ffn_w_gu, m_ffn_w_down, m_final_norm, v_attn_norm, v_attn_w_qkv, v_attn_w_o, v_attn_sink, v_conv_norm, v_conv_w_pw1, v_conv_b_pw1, v_conv_w_dw, v_conv_b_dw, v_conv_ln_g, v_conv_ln_b, v_conv_w_pw2, v_conv_b_pw2, v_ffn_norm, v_ffn_w_gu, v_ffn_w_down, v_final_norm):
    w = dict(zip(NAMES, (attn_norm, attn_w_qkv, attn_w_o, attn_sink, conv_norm, conv_w_pw1, conv_b_pw1, conv_w_dw, conv_b_dw, conv_ln_g,
                         conv_ln_b, conv_w_pw2, conv_b_pw2, ffn_norm, ffn_w_gu, ffn_w_down, final_norm)))
    m = dict(zip(NAMES, (m_attn_norm, m_attn_w_qkv, m_attn_w_o, m_attn_sink, m_conv_norm, m_conv_w_pw1, m_conv_b_pw1, m_conv_w_dw,
                         m_conv_b_dw, m_conv_ln_g, m_conv_ln_b, m_conv_w_pw2, m_conv_b_pw2, m_ffn_norm, m_ffn_w_gu, m_ffn_w_down,
                         m_final_norm)))
    v = dict(zip(NAMES, (v_attn_norm, v_attn_w_qkv, v_attn_w_o, v_attn_sink, v_conv_norm, v_conv_w_pw1, v_conv_b_pw1, v_conv_w_dw,
                         v_conv_b_dw, v_conv_ln_g, v_conv_ln_b, v_conv_w_pw2, v_conv_b_pw2, v_ffn_norm, v_ffn_w_gu, v_ffn_w_down,
                         v_final_norm)))
    me = 4 * lax.axis_index("x") + 2 * lax.axis_index("y") + lax.axis_index("c")
    for d in (w, m, v):
        d["ffn_w_gu"] = _gu_t(d["ffn_w_gu"])
    sh = {k: w[k][0].astype(BF16) for k in ("attn_w_qkv", "attn_w_o", "conv_w_pw1", "conv_w_pw2")}
    gu_b, down_b = w["ffn_w_gu"].astype(BF16), w["ffn_w_down"].astype(BF16)
    sh.update(ffn_w_gu0=gu_b[0], ffn_w_gu1=gu_b[1], ffn_w_down0=down_b[0], ffn_w_down1=down_b[1])
    x0, tgt = x[0], loss_target[0]
    t = x0.shape[0]
    tabs = _rope_tables(t)
    tl = min(TL, t)
    g_a, sink, g_f0, g_f1, g_fin = w["attn_norm"], w["attn_sink"][0], w["ffn_norm"][0:1], w["ffn_norm"][1:2], w["final_norm"][None]
    gather, scatter = False, True

    shard_rows = _pack([w["conv_w_dw"][0], jnp.zeros((1, 128), F32), w["conv_norm"], w["conv_b_dw"], w["conv_ln_g"], w["conv_ln_b"],
                        w["conv_b_pw2"], w["conv_b_pw1"]], 40)
    wqkv_g, sm = _exchange([sh["attn_w_qkv"], shard_rows], [gather] * 2, "gather_attn")
    wqkv = wqkv_g.transpose(1, 0, 2).reshape(D, QKV)

    def full_vec(row, n=1):
        return sm[:, row:row + n, :].reshape(1, NDEV * n * 128)

    w_dw, g_c, b_dw, ln_g, ln_b = sm[:, 0:32, :].transpose(1, 0, 2).reshape(32, D), full_vec(32), full_vec(33), full_vec(34), full_vec(35)
    b_pw2, b_pw1 = full_vec(36), full_vec(37, 2)

    (q_t, kv), (wo_g, wd0_g) = _attn_pre(x0, g_a, wqkv, tabs, tl, xchg=([sh["attn_w_o"], sh["ffn_w_down0"]], [gather] * 2))
    o_t, (wgu0,) = _attn_fwd(q_t, kv, sink, xchg=([sh["ffn_w_gu0"]], [gather]))
    wo, wd0 = wo_g.reshape(D, D), wd0_g.reshape(4, GUB, D)
    (x1, sf0, act0_t, x2), (wpw1, wgu1) = _ffn_fwd_attn(
        o_t, x0, wo, g_f0, wgu0, wd0, tl, xchg=([sh["conv_w_pw1"], sh["ffn_w_gu1"]], [gather] * 2))
    (u, glu), (wpw2_g,) = _conv_pre(x2, g_c, wpw1, b_pw1, tl, xchg=([sh["conv_w_pw2"]], [gather]))
    wpw2 = wpw2_g.reshape(D, D)
    cv, (wd1_g,) = _conv_mid(glu, w_dw, b_dw, tl, xchg=([sh["ffn_w_down1"]], [gather]))
    wd1 = wd1_g.reshape(4, GUB, D)
    s_t, x3 = _conv_post(cv, x2, ln_g, ln_b, wpw2, b_pw2, tl)
    (sf1, act1_t, dx4, dx4_b, fin), _ = _ffn_fwd_final(x3, g_f1, wgu1, wd1, g_fin, tgt, tl)

    land = {}
    tk, tk2 = min(TK, t), min(2 * TK, t)
    (dgu1_t, h3, dx3, dx3_b, dg_f1), _ = _ffn_bwd(dx4, x3, sf1, g_f1, wgu1, wd1, TM, "ffn_bwd1")
    dwgu1 = _wgrad(dgu1_t, h3[None], tk2, "dwgu1")
    dwd1 = _wgrad(act1_t, dx4_b[None], tk2, "dwd1").reshape(NDEV, DFF // NDEV, D)
    (dcv, cpart), (land["ffn_w_down1"],) = _conv_post_bwd(dx3, cv, ln_g, ln_b, wpw2, tl, xchg=([dwd1], [scatter]))
    dwpw2 = _wgrad(s_t[None], dx3_b[None], tk2, "dwpw2").reshape(NDEV, D // NDEV, D)
    (dglu, dw_dw), (land["ffn_w_gu1"], land["conv_w_pw2"]) = _conv_mid_bwd(dcv, glu, w_dw, tl, xchg=([dwgu1, dwpw2], [scatter] * 2))
    du, h2_t, dx2, dx2_b, dg_c, db_pw1 = _conv_pre_bwd(dglu, u, x2, g_c, wpw1, dx3, tl)
    dwpw1 = _wgrad(h2_t[None], du, tk2, "dwpw1")
    (dgu0_t, h1, dx1, dx1_b, dg_f0, do_t), (land["conv_w_pw1"],) = _ffn_bwd(
        dx2, x1, sf0, g_f0, wgu0, wd0, TM, "ffn_bwd0", wo=wo, xchg=([dwpw1], [scatter]))
    dwgu0 = _wgrad(dgu0_t, h1[None], tk2, "dwgu0")
    dwd0 = _wgrad(act0_t, dx2_b[None], tk2, "dwd0").reshape(NDEV, DFF // NDEV, D)
    dwo = _wgrad(o_t, dx1_b[None], tk2, "dwo", by_block=True).reshape(NDEV, D // NDEV, D)
    (dq_t, dk, dv, dsink), (land["ffn_w_gu0"], land["ffn_w_down0"]) = _attn_bwd(
        q_t, kv, do_t, sink, tl, xchg=([dwgu0, dwd0], [scatter] * 2))
    dqkv, h0_t, dx0, dg_a = _attn_pre_bwd(dq_t, dk, dv, x0, g_a, wqkv, tabs, dx1, tl)
    dwqkv, (land["attn_w_o"],) = _wgrad(h0_t[None], dqkv[None], tk, "dwqkv", xchg=([dwo], [scatter]))
    dwqkv = dwqkv[0].reshape(D, NDEV, QKV // NDEV).transpose(1, 0, 2)

    lane0 = (lax.broadcasted_iota(jnp.int32, (1, 128), 1) == 0).astype(F32)
    parts = _pack([dg_a, dg_f0, dg_f1, fin[0:1], dsink[0:1, :NH], fin[1, 0] * lane0, jnp.zeros((6, 128), F32), dg_c, cpart[2:3],
                   cpart[0:1], cpart[1:2], cpart[3:4], db_pw1, dw_dw.reshape(32, NDEV, 128)], 352)
    land["attn_w_qkv"], parts_g = _exchange([dwqkv, parts], [scatter, gather], "scatter_attn")
    red = _sum_parts(parts_g)

    def shard_rows_of(row, n=1):
        return lax.dynamic_slice_in_dim(red, row + n * me, n, axis=0)

    gs = {
        "attn_norm": red[0:8].reshape(1, D), "ffn_norm": red[8:24].reshape(2, D), "final_norm": red[24:32].reshape(D),
        "attn_sink": red[32:33, :NH], "conv_norm": shard_rows_of(40), "conv_b_dw": shard_rows_of(48), "conv_ln_g": shard_rows_of(56),
        "conv_ln_b": shard_rows_of(64), "conv_b_pw2": shard_rows_of(72), "conv_b_pw1": shard_rows_of(80, 2).reshape(1, PWB),
        "conv_w_dw": lax.dynamic_index_in_dim(red[96:352].reshape(32, NDEV, 128), me, axis=1, keepdims=False)[None, :CW],
    }
    loss = red[33, 0]

    grads, deltas, new_m, new_v = dict(gs), {}, {}, {}
    ds, ms, vs = _adamw_small(_pack_small(w), _pack_small(gs), _pack_small(m), _pack_small(v))
    deltas.update(_unpack_small(ds, gs))
    new_m.update(_unpack_small(ms, gs))
    new_v.update(_unpack_small(vs, gs))
    for k in ("attn_w_qkv", "attn_w_o", "conv_w_pw1", "conv_w_pw2", "ffn_w_gu", "ffn_w_down"):
        lands = [land[k + "0"], land[k + "1"]] if k.startswith("ffn") else [land[k]]
        tr = {1024: 256, 128: 128, 352: 176, GUB: 176}[w[k].shape[1]]
        grads[k], deltas[k], new_m[k], new_v[k] = _reduce_adamw(lands, w[k], m[k], v[k], tr, "adamw_" + k)
    for d in (grads, deltas, new_m, new_v):
        d["ffn_w_gu"] = _gu_t(d["ffn_w_gu"])
    return (loss, dx0[None], *[grads[k] for k in NAMES], *[deltas[k] for k in NAMES], *[new_m[k] for k in NAMES],
            *[new_v[k] for k in NAMES])
```

```python
import math

import jax
import jax.numpy as jnp
from jax import lax
from jax.experimental import pallas as pl
from jax.experimental.pallas import tpu as pltpu

F32 = jnp.float32
BF16 = jnp.bfloat16

D = 1024
NH = 16
NKV = 4
HD = 64
GROUP = NH // NKV
ROT = 16
THETA = 500000.0
BLK = 128
QKV = (NH + 2 * NKV) * HD
KOFF = NH * HD
VOFF = KOFF + NKV * HD
DFF = 2816
NDEV = 8
GUB = 2 * DFF // NDEV
PWB = 2 * D // NDEV
CW = 31
CPAD = 15
HALO = 16
EPS = 1e-6
NEG = -1e30
LR, B1, B2, AEPS, WD, STEP = 0.001, 0.9, 0.999, 1e-08, 0.01, 10

VMEM_LIMIT = 56 * 1024 * 1024
MESH = pl.DeviceIdType.MESH
WHOLE = pl.BlockSpec(memory_space=pltpu.VMEM)
ANY = pl.BlockSpec(memory_space=pl.ANY)


def _place():
    x, y, c = lax.axis_index("x"), lax.axis_index("y"), lax.axis_index("c")
    return x, y, c, 4 * x + 2 * y + c


def _peer(x, y, c, j):
    px = 1 - x if j & 4 else x
    py = 1 - y if j & 2 else y
    pc = 1 - c if j & 1 else c
    return (px, py, pc), 4 * px + 2 * py + pc


SIBLING = 1
OTHER_CHIPS = (2, 4, 6)
PASS_ON_LEAD = 3


class _Exchange:
    def __init__(self, src, dst, scatter, send, recv, loc):
        self.src, self.dst, self.scatter, self.send, self.recv, self.loc = src, dst, scatter, send, recv, loc
        self.x, self.y, self.c, self.me = _place()

    def _remote(self, k, j, s, d, to):
        peer, _ = _peer(self.x, self.y, self.c, to)
        return pltpu.make_async_remote_copy(src_ref=s, dst_ref=d, send_sem=self.send.at[k, j - 1], recv_sem=self.recv.at[k, j - 1],
                                            device_id=peer, device_id_type=MESH)

    def _slot(self, j):
        return _peer(self.x, self.y, self.c, j)[1]

    def local(self, k):
        if self.scatter[k]:
            return pltpu.make_async_copy(self.src[k].at[self.me], self.dst[k].at[0], self.loc.at[k])
        return pltpu.make_async_copy(self.src[k], self.dst[k].at[self.me], self.loc.at[k])

    def direct(self, k, j):
        if self.scatter[k]:
            return self._remote(k, j, self.src[k].at[self._slot(j)], self.dst[k].at[j], j)
        return self._remote(k, j, self.src[k], self.dst[k].at[self.me], j)

    def passed_on(self, k, j):
        rows = self.dst[k].at[self._slot(j)]
        return self._remote(k, j + 1, rows, rows, SIBLING)

    def arrival(self, k, j):
        rows = self.dst[k].at[j if self.scatter[k] else self._slot(j)]
        return self._remote(k, j, rows, rows, j)

    def sent(self, k):
        return tuple(range(1, NDEV)) if self.scatter[k] else (SIBLING,) + OTHER_CHIPS

    def start(self):
        for k in range(len(self.src)):
            self.local(k).start()
            for j in self.sent(k):
                self.direct(k, j).start()

    def pass_on(self):
        for k in range(len(self.src)):
            if not self.scatter[k]:
                for j in OTHER_CHIPS:
                    self.arrival(k, j).wait_recv()
                    self.passed_on(k, j).start()

    def finish(self):
        for k in range(len(self.src)):
            for j in range(1, NDEV):
                if self.scatter[k] or j not in OTHER_CHIPS:
                    self.arrival(k, j).wait_recv()
            for j in self.sent(k):
                self.direct(k, j).wait_send()
            if not self.scatter[k]:
                for j in OTHER_CHIPS:
                    self.passed_on(k, j).wait_send()
            self.local(k).wait()


def _call(body, **kw):
    return pl.pallas_call(body, **kw)


def _pc(body, operands, *, name, in_specs, out_specs, out_shape, grid=None, scratch_shapes=(), xchg=None):
    kw = dict(name=name, compiler_params=pltpu.CompilerParams(
        dimension_semantics=None if grid is None else ("arbitrary",) * len(grid), vmem_limit_bytes=VMEM_LIMIT,
        has_side_effects=xchg is not None))
    if grid is not None:
        kw["grid"] = grid
    if xchg is None:
        outs = _call(body, in_specs=list(in_specs), out_specs=list(out_specs), out_shape=list(out_shape),
                     scratch_shapes=list(scratch_shapes), **kw)(*operands)
        return list(outs), []
    arrs, scatter = xchg
    nx, n_in, n_out, n_s = len(arrs), len(in_specs), len(out_specs), len(scratch_shapes)

    def wrapped(*refs):
        ins, refs = refs[:n_in], refs[n_in:]
        src, refs = refs[:nx], refs[nx:]
        outs, refs = refs[:n_out], refs[n_out:]
        dst, refs = refs[:nx], refs[nx:]
        scr, (send, recv, loc) = refs[:n_s], refs[n_s:]
        if grid is None:
            _Exchange(src, dst, scatter, send, recv, loc).start()
            body(*ins, *outs, *scr)
            _Exchange(src, dst, scatter, send, recv, loc).pass_on()
            _Exchange(src, dst, scatter, send, recv, loc).finish()
            return

        step, steps = 0, 1
        for axis, n in enumerate(grid):
            step, steps = step * n + pl.program_id(axis), steps * n

        @pl.when(step == 0)
        def _():
            _Exchange(src, dst, scatter, send, recv, loc).start()

        body(*ins, *outs, *scr)

        @pl.when(step == max(steps - 1 - PASS_ON_LEAD, 0))
        def _():
            _Exchange(src, dst, scatter, send, recv, loc).pass_on()

        @pl.when(step == steps - 1)
        def _():
            _Exchange(src, dst, scatter, send, recv, loc).finish()

    xshape = [jax.ShapeDtypeStruct(a.shape if sc else (NDEV,) + a.shape, a.dtype) for a, sc in zip(arrs, scatter)]
    sems = [pltpu.SemaphoreType.DMA((nx, NDEV - 1)), pltpu.SemaphoreType.DMA((nx, NDEV - 1)), pltpu.SemaphoreType.DMA((nx,))]
    res = _call(wrapped, in_specs=list(in_specs) + [ANY] * nx, out_specs=list(out_specs) + [ANY] * nx,
                out_shape=list(out_shape) + xshape, scratch_shapes=list(scratch_shapes) + sems, **kw)(*operands, *arrs)
    return list(res[:n_out]), list(res[n_out:])


def _exchange(arrs, scatter, name):
    return _pc(lambda: None, [], name=name, in_specs=[], out_specs=[], out_shape=[], xchg=(arrs, scatter))[1]


def _rows(tm, width):
    return pl.BlockSpec((tm, width), lambda i: (i, 0))


def _cols(tm):
    return pl.BlockSpec((D, tm), lambda i: (0, i))


def _qblocks(tm):
    return pl.BlockSpec((tm // BLK, D, BLK), lambda i: (i, 0, 0))


def _put_blocks(ref, val):
    for b in range(ref.shape[0]):
        ref[b] = val[:, b * BLK:(b + 1) * BLK]


def _get_blocks(ref):
    return jnp.concatenate([ref[b] for b in range(ref.shape[0])], axis=1)


def _blk3(nb, tm, width):
    return pl.BlockSpec((nb, tm, width), lambda i: (0, i, 0))


def _acc(rows, width):
    return pl.BlockSpec((rows, width), lambda i: (0, 0))


def _sds(shape, dtype):
    return jax.ShapeDtypeStruct(shape, dtype)


def _dot(a, b):
    return jnp.dot(a.astype(BF16), b.astype(BF16), preferred_element_type=F32)


def _dot_nt(a, b):
    return lax.dot_general(a.astype(BF16), b.astype(BF16), (((1,), (1,)), ((), ())), preferred_element_type=F32)


def _dot_tn(a, b):
    return lax.dot_general(a.astype(BF16), b.astype(BF16), (((0,), (0,)), ((), ())), preferred_element_type=F32)


def _sigmoid(x):
    return 1.0 / (1.0 + jnp.exp(-x))


def _rms(x, g):
    r = lax.rsqrt(jnp.mean(x * x, axis=-1, keepdims=True) + EPS)
    xh = x * r
    return xh, r, xh * g


def _rms_bwd(dh, xh, r, g):
    dxh = dh * g
    dg = jnp.sum(dh * xh, axis=0, keepdims=True)
    dx = r * (dxh - xh * jnp.mean(dxh * xh, axis=-1, keepdims=True))
    return dx, dg


def _lanes(t, width):
    return jnp.tile(t, (1, width // t.shape[1]))


def _rope(z, c, sa, sb):
    w = z.shape[1]
    return z * _lanes(c, w) + pltpu.roll(z, w - 8, 1) * _lanes(sa, w) + pltpu.roll(z, 8, 1) * _lanes(sb, w)


def _rope_t(dz, c, sa, sb):
    w = dz.shape[1]
    return dz * _lanes(c, w) + pltpu.roll(dz * _lanes(sa, w), 8, 1) + pltpu.roll(dz * _lanes(sb, w), w - 8, 1)


def _rope_tables(t):
    pos = jnp.arange(t, dtype=F32)
    inv_freq = THETA ** (-jnp.arange(0, ROT, 2, dtype=F32) / ROT)
    ang = pos[:, None] * inv_freq[None, :]
    cos, sin = jnp.cos(ang), jnp.sin(ang)
    one = jnp.ones((t, HD - ROT), F32)
    zero = jnp.zeros((t, HD - ROT), F32)
    z8 = jnp.zeros((t, 8), F32)
    c = jnp.concatenate([cos, cos, one], axis=1)
    sa = jnp.concatenate([-sin, z8, zero], axis=1)
    sb = jnp.concatenate([z8, sin, zero], axis=1)
    return tuple(jnp.tile(a, (1, 2)) for a in (c, sa, sb))


KVW = NKV * HD
GW = GROUP * BLK


def _attn_pre(x, g, wqkv, tabs, tm, xchg=None):
    t = x.shape[0]

    def body(x_ref, g_ref, w_ref, c_ref, sa_ref, sb_ref, qt_ref, kv_ref):
        _, _, h = _rms(x_ref[...], g_ref[...])
        z = _dot(h, w_ref[...])
        c, sa, sb = c_ref[...], sa_ref[...], sb_ref[...]
        _put_blocks(qt_ref, (_rope(z[:, :KOFF], c, sa, sb) * 0.125).T.astype(BF16))
        kv_ref[:, :KVW] = _rope(z[:, KOFF:VOFF], c, sa, sb).astype(BF16)
        kv_ref[:, KVW:] = z[:, VOFF:].astype(BF16)

    return _pc(body, [x, g, wqkv, *tabs], name="attn_pre", grid=(t // tm,),
               in_specs=[_rows(tm, D), _acc(1, D), WHOLE, _rows(tm, 128), _rows(tm, 128), _rows(tm, 128)],
               out_specs=[_qblocks(tm), _rows(tm, 2 * KVW)],
               out_shape=[_sds((t // BLK, D, BLK), BF16), _sds((t, 2 * KVW), BF16)], xchg=xchg)


QB = 4


def _attn_specs(nblk):
    prev = lambda i: jnp.maximum(QB * i - 1, 0)
    nxt = lambda i: jnp.minimum(QB * (i + 1), nblk - 1)
    return [
        _qblocks(QB * BLK),
        pl.BlockSpec((BLK, KVW), lambda i: (prev(i), 0)),
        pl.BlockSpec((QB * BLK, KVW), lambda i: (i, 0)),
        pl.BlockSpec((BLK, KVW), lambda i: (nxt(i), 0)),
        pl.BlockSpec((BLK, KVW), lambda i: (prev(i), 1)),
        pl.BlockSpec((QB * BLK, KVW), lambda i: (i, 1)),
        pl.BlockSpec((BLK, KVW), lambda i: (nxt(i), 1)),
    ]


def _attn_bias():
    c = lax.broadcasted_iota(jnp.int32, (3, 3 * BLK, GW), 1)
    r = lax.broadcasted_iota(jnp.int32, (3, 3 * BLK, GW), 2) & (BLK - 1)
    slab = lax.broadcasted_iota(jnp.int32, (3, 3 * BLK, GW), 0)
    valid = (c >= r) & (c - 2 * BLK <= r) & ((slab != 0) | (c >= BLK)) & ((slab != 2) | (c < 2 * BLK))
    return jnp.where(valid, 0.0, NEG).astype(F32)


def _bias_of(bias_ref, blk, nblk):
    return bias_ref[jnp.where(blk == 0, 0, jnp.where(blk == nblk - 1, 2, 1))]


def _group(ref, b, kv):
    return jnp.concatenate([ref[b, (kv * GROUP + g) * HD:(kv * GROUP + g + 1) * HD, :] for g in range(GROUP)], axis=1)


def _group_sink(sink_ref, kv):
    return jnp.concatenate([jnp.full((1, BLK), sink_ref[kv * GROUP + g], F32) for g in range(GROUP)], axis=1)


def _attn_exp(k_h, qt_g, bias, sink_g):
    s = _dot(k_h, qt_g) + bias
    m = jnp.maximum(jnp.max(s, axis=0, keepdims=True), sink_g)
    e = jnp.exp(s - m)
    es = jnp.exp(sink_g - m)
    return e, 1.0 / (jnp.sum(e, axis=0, keepdims=True) + es), es


def _attn_fwd(qt, kv, sink, xchg=None):
    t = kv.shape[0]
    nblk = t // BLK

    def body(sink_ref, bias_ref, qt_ref, kp, kc, kn, vp, vc, vn, ot_ref):
        k = jnp.concatenate([kp[...], kc[...], kn[...]], axis=0)
        v = jnp.concatenate([vp[...], vc[...], vn[...]], axis=0)
        for b in range(QB):
            bias = _bias_of(bias_ref, QB * pl.program_id(0) + b, nblk)
            keys = slice(b * BLK, (b + 3) * BLK)
            for h in range(NKV):
                e, inv, _ = _attn_exp(k[keys, h * HD:(h + 1) * HD], _group(qt_ref, b, h), bias, _group_sink(sink_ref, h))
                ot_g = (_dot_tn(v[keys, h * HD:(h + 1) * HD], e) * inv).astype(BF16)
                for g in range(GROUP):
                    ot_ref[b, (h * GROUP + g) * HD:(h * GROUP + g + 1) * HD, :] = ot_g[:, g * BLK:(g + 1) * BLK]

    outs, xo = _pc(body, [sink, _attn_bias(), qt] + [kv] * 6, name="attn_fwd", grid=(nblk // QB,),
                   in_specs=[pl.BlockSpec(memory_space=pltpu.SMEM), WHOLE] + _attn_specs(nblk),
                   out_specs=[_qblocks(QB * BLK)], out_shape=[_sds((nblk, D, BLK), BF16)], xchg=xchg)
    return outs[0], xo


def _attn_bwd(qt, kv, dot, sink, pad, xchg=None):
    t = kv.shape[0]
    nblk = t // BLK

    def body(sink_ref, bias_ref, qt_ref, kp, kc, kn, vp, vc, vn, dot_ref, dqt_ref, dk_ref, dv_ref, ds_ref):
        i = pl.program_id(0)

        @pl.when(i == 0)
        def _():
            dk_ref[...] = jnp.zeros_like(dk_ref)
            dv_ref[...] = jnp.zeros_like(dv_ref)
            ds_ref[...] = jnp.zeros_like(ds_ref)

        k = jnp.concatenate([kp[...], kc[...], kn[...]], axis=0)
        v = jnp.concatenate([vp[...], vc[...], vn[...]], axis=0)
        lane = lax.broadcasted_iota(jnp.int32, (1, 128), 1)
        dsink = jnp.zeros((1, 128), F32)
        for b in range(QB):
            blk = QB * i + b
            bias = _bias_of(bias_ref, blk, nblk)
            keys = slice(b * BLK, (b + 3) * BLK)
            rows = pl.ds(pl.multiple_of(blk * BLK + (pad - BLK), BLK), 3 * BLK)
            for h in range(NKV):
                k_h, v_h = k[keys, h * HD:(h + 1) * HD], v[keys, h * HD:(h + 1) * HD]
                qt_g, dot_g = _group(qt_ref, b, h), _group(dot_ref, b, h)
                e, inv, es = _attn_exp(k_h, qt_g, bias, _group_sink(sink_ref, h))
                p, ps = e * inv, es * inv
                dp = _dot(v_h, dot_g)
                delta = jnp.sum(p * dp, axis=0, keepdims=True)
                ds = (p * (dp - delta)).astype(BF16)
                dqt_g = _dot_tn(k_h, ds)
                dk_ref[rows, h * HD:(h + 1) * HD] += _dot_nt(ds, qt_g)
                dv_ref[rows, h * HD:(h + 1) * HD] += _dot_nt(p, dot_g)
                psd = ps * delta
                for g in range(GROUP):
                    n = h * GROUP + g
                    dqt_ref[b, n * HD:(n + 1) * HD, :] = dqt_g[:, g * BLK:(g + 1) * BLK]
                    dsink = dsink - jnp.where(lane == n, jnp.sum(psd[:, g * BLK:(g + 1) * BLK], axis=1, keepdims=True), 0.0)
        ds_ref[0:1, :] += dsink

    outs, xo = _pc(body, [sink, _attn_bias(), qt] + [kv] * 6 + [dot], name="attn_bwd", grid=(nblk // QB,),
                   in_specs=[pl.BlockSpec(memory_space=pltpu.SMEM), WHOLE] + _attn_specs(nblk) + [_qblocks(QB * BLK)],
                   out_specs=[_qblocks(QB * BLK), _acc(t + 2 * pad, KVW), _acc(t + 2 * pad, KVW), _acc(8, 128)],
                   out_shape=[_sds((nblk, D, BLK), F32), _sds((t + 2 * pad, KVW), F32), _sds((t + 2 * pad, KVW), F32),
                              _sds((8, 128), F32)], xchg=xchg)
    return outs, xo


def _attn_pre_bwd(dqt, dk, dv, x, g, wqkv, tabs, dx_out, tm):
    t = x.shape[0]

    def body(dqt_ref, dk_ref, dv_ref, x_ref, g_ref, w_ref, c_ref, sa_ref, sb_ref, dxo_ref, dqkv_ref, ht_ref, dx_ref, dg_ref):
        @pl.when(pl.program_id(0) == 0)
        def _():
            dg_ref[...] = jnp.zeros_like(dg_ref)

        c, sa, sb = c_ref[...], sa_ref[...], sb_ref[...]
        dqkv_ref[:, :KOFF] = _rope_t(_get_blocks(dqt_ref).T * 0.125, c, sa, sb).astype(BF16)
        dqkv_ref[:, KOFF:VOFF] = _rope_t(dk_ref[...], c, sa, sb).astype(BF16)
        dqkv_ref[:, VOFF:] = dv_ref[...].astype(BF16)
        g = g_ref[...]
        xh, r, h = _rms(x_ref[...], g)
        ht_ref[...] = h.astype(BF16).T
        dh = _dot_nt(dqkv_ref[...], w_ref[...])
        dx, dg = _rms_bwd(dh, xh, r, g)
        dx_ref[...] = dxo_ref[...] + dx
        dg_ref[...] += dg

    return _pc(body, [dqt, dk, dv, x, g, wqkv, *tabs, dx_out], name="attn_pre_bwd", grid=(t // tm,),
               in_specs=[_qblocks(tm), pl.BlockSpec((tm, KVW), lambda i: (i + 1, 0)), pl.BlockSpec((tm, KVW), lambda i: (i + 1, 0)),
                         _rows(tm, D), _acc(1, D), WHOLE, _rows(tm, 128), _rows(tm, 128), _rows(tm, 128), _rows(tm, D)],
               out_specs=[_rows(tm, QKV), _cols(tm), _rows(tm, D), _acc(1, D)],
               out_shape=[_sds((t, QKV), BF16), _sds((D, t), BF16), _sds((t, D), F32), _sds((1, D), F32)])[0]


def _ffn(x, g, wgu_ref, wd_ref, sf_ref, actt_ref):
    _, _, h = _rms(x, g)
    hb = h.astype(BF16)
    y = x
    for j in range(4):
        gj = _dot_nt(hb, wgu_ref[j])
        uj = _dot_nt(hb, wgu_ref[j + 4])
        sg = _sigmoid(gj)
        silu = gj * sg
        act = (silu * uj).astype(BF16)
        sf_ref[j] = silu.astype(BF16)
        sf_ref[j + 4] = (sg * (1.0 + gj * (1.0 - sg)) * uj).astype(BF16)
        actt_ref[j] = act.T
        y = y + _dot(act, wd_ref[j])
    return y


def _tcols(nb, tm):
    return pl.BlockSpec((nb, GUB, tm), lambda i: (0, 0, i))


def _ffn_fwd_attn(ot, x, wo, g, wgu, wd, tm, xchg=None):
    t = x.shape[0]

    def body(ot_ref, x_ref, wo_ref, g_ref, wgu_ref, wd_ref, x1_ref, sf_ref, actt_ref, out_ref):
        x1 = x_ref[...] + _dot_tn(_get_blocks(ot_ref), wo_ref[...])
        x1_ref[...] = x1
        out_ref[...] = _ffn(x1, g_ref[...], wgu_ref, wd_ref, sf_ref, actt_ref)

    return _pc(body, [ot, x, wo, g, wgu, wd], name="ffn_fwd0", grid=(t // tm,),
               in_specs=[_qblocks(tm), _rows(tm, D), WHOLE, _acc(1, D), WHOLE, WHOLE],
               out_specs=[_rows(tm, D), _blk3(NDEV, tm, GUB), _tcols(4, tm), _rows(tm, D)],
               out_shape=[_sds((t, D), F32), _sds((NDEV, t, GUB), BF16), _sds((4, GUB, t), BF16), _sds((t, D), F32)], xchg=xchg)


def _ffn_fwd_final(x, g, wgu, wd, g_fin, tgt, tm, xchg=None):
    t = x.shape[0]

    def body(x_ref, g_ref, wgu_ref, wd_ref, gf_ref, t_ref, sf_ref, actt_ref, dx_ref, dxb_ref, part_ref):
        @pl.when(pl.program_id(0) == 0)
        def _():
            part_ref[...] = jnp.zeros_like(part_ref)

        gf = gf_ref[...]
        xh, r, y = _rms(_ffn(x_ref[...], g_ref[...], wgu_ref, wd_ref, sf_ref, actt_ref), gf)
        err = y - t_ref[...]
        dx, dg = _rms_bwd(err * (1.0 / D), xh, r, gf)
        dx_ref[...] = dx
        dxb_ref[...] = dx.astype(BF16)
        part_ref[0:1, :] += dg
        tok = jnp.sum(err * err, axis=-1, keepdims=True) * (1.0 / D)
        lane = lax.broadcasted_iota(jnp.int32, (1, D), 1)
        part_ref[1:2, :] += jnp.where(lane == 0, 0.5 * jnp.sum(tok, axis=0, keepdims=True), 0.0)

    return _pc(body, [x, g, wgu, wd, g_fin, tgt], name="ffn_fwd1", grid=(t // tm,),
               in_specs=[_rows(tm, D), _acc(1, D), WHOLE, WHOLE, _acc(1, D), _rows(tm, D)],
               out_specs=[_blk3(NDEV, tm, GUB), _tcols(4, tm), _rows(tm, D), _rows(tm, D), _acc(8, D)],
               out_shape=[_sds((NDEV, t, GUB), BF16), _sds((4, GUB, t), BF16), _sds((t, D), F32), _sds((t, D), BF16),
                          _sds((8, D), F32)], xchg=xchg)


def _ffn_bwd(dy, x, sf, g, wgu, wd, tm, name, wo=None, xchg=None):
    t = x.shape[0]

    def body(dy_ref, x_ref, sf_ref, g_ref, wgu_ref, wd_ref, *rest):
        wo_ref = rest[0] if wo is not None else None
        dgut_ref, h_ref, dx_ref, dg_ref = rest[wo is not None:][:4]

        @pl.when(pl.program_id(0) == 0)
        def _():
            dg_ref[...] = jnp.zeros_like(dg_ref)

        dy = dy_ref[...]
        dyb = dy.astype(BF16)
        gn = g_ref[...]
        xh, r, h = _rms(x_ref[...], gn)
        h_ref[...] = h.astype(BF16)
        dh = jnp.zeros_like(dy)
        for j in range(4):
            dact = _dot_nt(dyb, wd_ref[j])
            dgj = (dact * sf_ref[j + 4]).astype(BF16)
            duj = (dact * sf_ref[j]).astype(BF16)
            dgut_ref[j] = dgj.T
            dgut_ref[j + 4] = duj.T
            dh = dh + _dot(dgj, wgu_ref[j]) + _dot(duj, wgu_ref[j + 4])
        dx, dg = _rms_bwd(dh, xh, r, gn)
        dx = dy + dx
        dx_ref[...] = dx
        dg_ref[...] += dg
        if wo is not None:
            _put_blocks(rest[5], _dot(wo_ref[...], dx.astype(BF16).T).astype(BF16))

    extra = wo is not None
    return _pc(body, [dy, x, sf, g, wgu, wd] + [wo] * extra, name=name, grid=(t // tm,),
               in_specs=[_rows(tm, D), _rows(tm, D), _blk3(NDEV, tm, GUB), _acc(1, D), WHOLE, WHOLE] + [WHOLE] * extra,
               out_specs=[_tcols(NDEV, tm), _rows(tm, D), _rows(tm, D), _acc(1, D)] + [_qblocks(tm)] * extra,
               out_shape=[_sds((NDEV, GUB, t), BF16), _sds((t, D), BF16), _sds((t, D), F32), _sds((1, D), F32)]
               + [_sds((t // BLK, D, BLK), BF16)] * extra, xchg=xchg)


def _wgrad(at, b, tk, name, xchg=None, by_block=False):
    if by_block:
        na, (_, ma, _), t = 1, at.shape, at.shape[0] * BLK
        a_spec = pl.BlockSpec((tk // BLK, ma, BLK), lambda j, k: (k, 0, 0))
    else:
        na, ma, t = at.shape
        a_spec = pl.BlockSpec((1, ma, tk), lambda j, k: (j if na > 1 else 0, 0, k))
    nb, _, mb = b.shape
    nk = t // tk

    def body(a_ref, b_ref, out_ref, acc):
        k = pl.program_id(1)

        @pl.when(k == 0)
        def _():
            acc[...] = jnp.zeros_like(acc)

        acc[...] += _dot(_get_blocks(a_ref) if by_block else a_ref[0], b_ref[0])

        @pl.when(k == nk - 1)
        def _():
            out_ref[0] = acc[...].astype(BF16)

    outs, xo = _pc(body, [at, b], name=name, grid=(max(na, nb), nk),
                   in_specs=[a_spec, pl.BlockSpec((1, tk, mb), lambda j, k: (j if nb > 1 else 0, k, 0))],
                   out_specs=[pl.BlockSpec((1, ma, mb), lambda j, k: (j, 0, 0))], out_shape=[_sds((max(na, nb), ma, mb), BF16)],
                   scratch_shapes=[pltpu.VMEM((ma, mb), F32)], xchg=xchg)
    return outs[0] if xchg is None else (outs[0], xo)


def _conv_pre(x, g, w, b, tm, xchg=None):
    t = x.shape[0]

    def body(x_ref, g_ref, w_ref, b_ref, u_ref, glu_ref):
        _, _, h = _rms(x_ref[...], g_ref[...])
        hb = h.astype(BF16)
        for j in range(4):
            a = _dot(hb, w_ref[j]) + b_ref[:, j * PWB:(j + 1) * PWB]
            gt = _dot(hb, w_ref[j + 4]) + b_ref[:, D + j * PWB:D + (j + 1) * PWB]
            u_ref[j] = a.astype(BF16)
            u_ref[j + 4] = gt.astype(BF16)
            glu_ref[:, j * PWB:(j + 1) * PWB] = a * _sigmoid(gt)

    return _pc(body, [x, g, w, b], name="conv_pre", grid=(t // tm,),
               in_specs=[_rows(tm, D), _acc(1, D), WHOLE, _acc(1, 2 * D)], out_specs=[_blk3(NDEV, tm, PWB), _rows(tm, D)],
               out_shape=[_sds((NDEV, t, PWB), BF16), _sds((t, D), F32)], xchg=xchg)


def _halo_specs(t, tm):
    per = tm // HALO
    last = t // HALO - 1
    return [
        pl.BlockSpec((HALO, D), lambda i: (jnp.maximum(i * per - 1, 0), 0)),
        _rows(tm, D),
        pl.BlockSpec((HALO, D), lambda i: (jnp.minimum((i + 1) * per, last), 0)),
    ]


SUB = 8
CCH = 32
CLN = 256


def _fill_shifted(sh, prev, cur, nxt, tm):
    i = pl.program_id(0)
    rows = jnp.concatenate([jnp.where(i == 0, 0.0, prev[...]), cur[...], jnp.where(i == pl.num_programs(0) - 1, 0.0, nxt[...])], axis=0)
    n = tm + 2 * HALO - SUB
    for b in range(SUB):
        sh[b] = rows[b:b + n]


def _shifted(sh, off, r0, c0):
    return sh[off % SUB, r0 + off - off % SUB:r0 + off - off % SUB + CCH, c0:c0 + CLN]


def _conv_mid(glu, wdw, bdw, tm, xchg=None):
    t = glu.shape[0]

    def body(prev, cur, nxt, w_ref, b_ref, out_ref, sh):
        _fill_shifted(sh, prev, cur, nxt, tm)
        for c0 in range(0, D, CLN):
            for r0 in range(0, tm, CCH):
                acc = jnp.broadcast_to(b_ref[:, c0:c0 + CLN], (CCH, CLN))
                for k in range(CW):
                    acc = acc + w_ref[k:k + 1, c0:c0 + CLN] * _shifted(sh, k + HALO - CPAD, r0, c0)
                out_ref[r0:r0 + CCH, c0:c0 + CLN] = acc

    outs, xo = _pc(body, [glu, glu, glu, wdw, bdw], name="conv_mid", grid=(t // tm,),
                   in_specs=_halo_specs(t, tm) + [_acc(32, D), _acc(1, D)], out_specs=[_rows(tm, D)],
                   out_shape=[_sds((t, D), F32)], scratch_shapes=[pltpu.VMEM((SUB, tm + 2 * HALO - SUB, D), F32)], xchg=xchg)
    return outs[0], xo


def _conv_mid_bwd(dcv, glu, wdw, tm, xchg=None):
    t = glu.shape[0]

    def body(dp, dc, dn, gp, gc, gn, w_ref, dglu_ref, dw_ref, dsh, gsh):
        @pl.when(pl.program_id(0) == 0)
        def _():
            dw_ref[...] = jnp.zeros_like(dw_ref)

        _fill_shifted(dsh, dp, dc, dn, tm)
        _fill_shifted(gsh, gp, gc, gn, tm)
        for c0 in range(0, D, CLN):
            for r0 in range(0, tm, CCH):
                acc = jnp.zeros((CCH, CLN), F32)
                for k in range(CW):
                    acc = acc + w_ref[k:k + 1, c0:c0 + CLN] * _shifted(dsh, HALO + CPAD - k, r0, c0)
                dglu_ref[r0:r0 + CCH, c0:c0 + CLN] = acc
            for k in range(CW):
                dwk = jnp.zeros((SUB, CLN), F32)
                for r0 in range(0, tm, CCH):
                    prod = _shifted(dsh, HALO, r0, c0) * _shifted(gsh, k + HALO - CPAD, r0, c0)
                    for r in range(0, CCH, SUB):
                        dwk = dwk + prod[r:r + SUB]
                dw_ref[k:k + 1, c0:c0 + CLN] += jnp.sum(dwk, axis=0, keepdims=True)

    n = tm + 2 * HALO - SUB
    return _pc(body, [dcv, dcv, dcv, glu, glu, glu, wdw], name="conv_mid_bwd", grid=(t // tm,),
               in_specs=_halo_specs(t, tm) + _halo_specs(t, tm) + [_acc(32, D)], out_specs=[_rows(tm, D), _acc(32, D)],
               out_shape=[_sds((t, D), F32), _sds((32, D), F32)],
               scratch_shapes=[pltpu.VMEM((SUB, n, D), F32), pltpu.VMEM((SUB, n, D), F32)], xchg=xchg)


def _ln(cv, lg, lb):
    mu = jnp.mean(cv, axis=-1, keepdims=True)
    cc = cv - mu
    rs = lax.rsqrt(jnp.mean(cc * cc, axis=-1, keepdims=True) + EPS)
    lh = cc * rs
    return lh, rs, lh * lg + lb


def _conv_post(cv, x, lg, lb, w2, b2, tm):
    t = x.shape[0]

    def body(cv_ref, x_ref, lg_ref, lb_ref, w_ref, b_ref, st_ref, out_ref):
        _, _, ln = _ln(cv_ref[...], lg_ref[...], lb_ref[...])
        s = (ln * _sigmoid(ln)).astype(BF16)
        st_ref[...] = s.T
        out_ref[...] = x_ref[...] + _dot(s, w_ref[...]) + b_ref[...]

    return _pc(body, [cv, x, lg, lb, w2, b2], name="conv_post", grid=(t // tm,),
               in_specs=[_rows(tm, D), _rows(tm, D), _acc(1, D), _acc(1, D), WHOLE, _acc(1, D)],
               out_specs=[_cols(tm), _rows(tm, D)], out_shape=[_sds((D, t), BF16), _sds((t, D), F32)])[0]


def _conv_post_bwd(dx, cv, lg, lb, w2, tm, xchg=None):
    t = dx.shape[0]

    def body(dx_ref, cv_ref, lg_ref, lb_ref, w_ref, dcv_ref, part_ref):
        @pl.when(pl.program_id(0) == 0)
        def _():
            part_ref[...] = jnp.zeros_like(part_ref)

        dx = dx_ref[...]
        lg = lg_ref[...]
        lh, rs, ln = _ln(cv_ref[...], lg, lb_ref[...])
        sg = _sigmoid(ln)
        dln = _dot_nt(dx, w_ref[...]) * (sg * (1.0 + ln * (1.0 - sg)))
        dlh = dln * lg
        dcv = rs * (dlh - jnp.mean(dlh, axis=-1, keepdims=True) - lh * jnp.mean(dlh * lh, axis=-1, keepdims=True))
        dcv_ref[...] = dcv
        part_ref[0:1, :] += jnp.sum(dln * lh, axis=0, keepdims=True)
        part_ref[1:2, :] += jnp.sum(dln, axis=0, keepdims=True)
        part_ref[2:3, :] += jnp.sum(dcv, axis=0, keepdims=True)
        part_ref[3:4, :] += jnp.sum(dx, axis=0, keepdims=True)

    return _pc(body, [dx, cv, lg, lb, w2], name="conv_post_bwd", grid=(t // tm,),
               in_specs=[_rows(tm, D), _rows(tm, D), _acc(1, D), _acc(1, D), WHOLE], out_specs=[_rows(tm, D), _acc(8, D)],
               out_shape=[_sds((t, D), F32), _sds((8, D), F32)], xchg=xchg)


def _conv_pre_bwd(dglu, u, x, g, w, dx_out, tm):
    t = x.shape[0]

    def body(dglu_ref, u_ref, x_ref, g_ref, w_ref, dxo_ref, du_ref, ht_ref, dx_ref, dxb_ref, dg_ref, db_ref):
        @pl.when(pl.program_id(0) == 0)
        def _():
            dg_ref[...] = jnp.zeros_like(dg_ref)
            db_ref[...] = jnp.zeros_like(db_ref)

        gn = g_ref[...]
        xh, r, h = _rms(x_ref[...], gn)
        ht_ref[...] = h.astype(BF16).T
        dh = jnp.zeros_like(xh)
        for j in range(4):
            a = u_ref[j].astype(F32)
            sg = _sigmoid(u_ref[j + 4].astype(F32))
            dgl = dglu_ref[:, j * PWB:(j + 1) * PWB]
            da = dgl * sg
            dgt = dgl * a * sg * (1.0 - sg)
            db_ref[:, j * PWB:(j + 1) * PWB] += jnp.sum(da, axis=0, keepdims=True)
            db_ref[:, D + j * PWB:D + (j + 1) * PWB] += jnp.sum(dgt, axis=0, keepdims=True)
            da, dgt = da.astype(BF16), dgt.astype(BF16)
            du_ref[j] = da
            du_ref[j + 4] = dgt
            dh = dh + _dot_nt(da, w_ref[j]) + _dot_nt(dgt, w_ref[j + 4])
        dx, dg = _rms_bwd(dh, xh, r, gn)
        dx = dxo_ref[...] + dx
        dx_ref[...] = dx
        dxb_ref[...] = dx.astype(BF16)
        dg_ref[...] += dg

    return _pc(body, [dglu, u, x, g, w, dx_out], name="conv_pre_bwd", grid=(t // tm,),
               in_specs=[_rows(tm, D), _blk3(NDEV, tm, PWB), _rows(tm, D), _acc(1, D), WHOLE, _rows(tm, D)],
               out_specs=[_blk3(NDEV, tm, PWB), _cols(tm), _rows(tm, D), _rows(tm, D), _acc(1, D), _acc(1, 2 * D)],
               out_shape=[_sds((NDEV, t, PWB), BF16), _sds((D, t), BF16), _sds((t, D), F32), _sds((t, D), BF16),
                          _sds((1, D), F32), _sds((1, 2 * D), F32)])[0]


def _adamw(w, g, m, v):
    m = B1 * m + (1.0 - B1) * g
    v = B2 * v + (1.0 - B2) * (g * g)
    m_hat = m / (1.0 - B1 ** STEP)
    v_hat = v / (1.0 - B2 ** STEP)
    return -LR * (m_hat / (jnp.sqrt(v_hat) + AEPS) + WD * w), m, v


def _reduce_adamw(lands, w, m, v, tr, name, xchg=None):
    nl, r, c = w.shape

    def body(*refs):
        l_refs, (w_ref, m_ref, v_ref, g_ref, d_ref, nm_ref, nv_ref) = refs[:nl], refs[nl:]

        def total(ref):
            g = ref[0].astype(F32)
            for j in range(1, NDEV):
                g = g + ref[j].astype(F32)
            return g

        g = total(l_refs[0])
        for n in range(1, nl):
            g = jnp.where(pl.program_id(0) == n, total(l_refs[n]), g)
        g_ref[0] = g
        d_ref[0], nm_ref[0], nv_ref[0] = _adamw(w_ref[0], g, m_ref[0], v_ref[0])

    layer = pl.BlockSpec((1, tr, c), lambda l, i: (l, i, 0))
    outs, xo = _pc(body, [*lands, w, m, v], name=name, grid=(nl, r // tr),
                   in_specs=[pl.BlockSpec((NDEV, tr, c), lambda l, i: (0, i, 0))] * nl + [layer] * 3, out_specs=[layer] * 4,
                   out_shape=[_sds((nl, r, c), F32)] * 4, xchg=xchg)
    return outs if xchg is None else (outs, xo)


def _sum_parts(parts):
    _, r, c = parts.shape

    def body(p_ref, out_ref):
        s = p_ref[0]
        for j in range(1, NDEV):
            s = s + p_ref[j]
        out_ref[...] = s

    return _pc(body, [parts], name="sum_parts", in_specs=[WHOLE], out_specs=[WHOLE], out_shape=[_sds((r, c), F32)])[0][0]


def _adamw_small(w, g, m, v):
    def body(w_ref, g_ref, m_ref, v_ref, d_ref, nm_ref, nv_ref):
        d_ref[...], nm_ref[...], nv_ref[...] = _adamw(w_ref[...], g_ref[...], m_ref[...], v_ref[...])

    return _pc(body, [w, g, m, v], name="adamw_small", in_specs=[WHOLE] * 4, out_specs=[WHOLE] * 3,
               out_shape=[_sds(w.shape, F32)] * 3)[0]


def _rows128(a):
    a = a.astype(F32)
    if a.shape[-1] % 128:
        a = jnp.pad(a, [(0, 0)] * (a.ndim - 1) + [(0, 128 - a.shape[-1] % 128)])
    return a.reshape(-1, 128)


def _pack(arrs, rows):
    p = jnp.concatenate([_rows128(a) for a in arrs], axis=0)
    return jnp.pad(p, ((0, rows - p.shape[0]), (0, 0)))


SMALL = ("attn_norm", "ffn_norm", "final_norm", "attn_sink", "conv_norm", "conv_b_dw", "conv_ln_g", "conv_ln_b", "conv_b_pw2",
         "conv_b_pw1", "conv_w_dw")
SMALL_ROWS = 72


def _pack_small(d):
    return _pack([d[k] for k in SMALL], SMALL_ROWS)


def _unpack_small(p, like):
    out, r = {}, 0
    for k in SMALL:
        shp = like[k].shape
        n = -(-shp[-1] // 128) * (math.prod(shp[:-1]))
        blk = p[r:r + n]
        if shp[-1] % 128:
            blk = blk[:, :shp[-1]]
        out[k] = blk.reshape(shp)
        r += n
    return out


NAMES = ("attn_norm", "attn_w_qkv", "attn_w_o", "attn_sink", "conv_norm", "conv_w_pw1", "conv_b_pw1", "conv_w_dw", "conv_b_dw",
         "conv_ln_g", "conv_ln_b", "conv_w_pw2", "conv_b_pw2", "ffn_norm", "ffn_w_gu", "ffn_w_down", "final_norm")
TM = 256
TL = 512
TK = 2048


def _gu_t(a):
    return jnp.swapaxes(a, 1, 2)


def kernel(x, attn_norm, attn_w_qkv, attn_w_o, attn_sink, conv_norm, conv_w_pw1, conv_b_pw1, conv_w_dw, conv_b_dw, conv_ln_g, conv_ln_b, conv_w_pw2, conv_b_pw2, ffn_norm, ffn_w_gu, ffn_w_down, final_norm, loss_target, m_attn_norm, m_attn_w_qkv, m_attn_w_o, m_attn_sink, m_conv_norm, m_conv_w_pw1, m_conv_b_pw1, m_conv_w_dw, m_conv_b_dw, m_conv_ln_g, m_conv_ln_b, m_conv_w_pw2, m_conv_b_pw2, m_ffn_norm, m_ffn_w_gu, m_ffn_w_down, m_final_norm, v_attn_norm, v_attn_w_qkv, v_attn_w_o, v_attn_sink, v_conv_norm, v_conv_w_pw1, v_conv_b_pw1, v_conv_w_dw, v_conv_b_dw, v_conv_ln_g, v_conv_ln_b, v_conv_w_pw2, v_conv_b_pw2, v_ffn_norm, v_ffn_w_gu, v_ffn_w_down, v_final_norm):
    w = dict(zip(NAMES, (attn_norm, attn_w_qkv, attn_w_o, attn_sink, conv_norm, conv_w_pw1, conv_b_pw1, conv_w_dw, conv_b_dw, conv_ln_g,
                         conv_ln_b, conv_w_pw2, conv_b_pw2, ffn_norm, ffn_w_gu, ffn_w_down, final_norm)))
    m = dict(zip(NAMES, (m_attn_norm, m_attn_w_qkv, m_attn_w_o, m_attn_sink, m_conv_norm, m_conv_w_pw1, m_conv_b_pw1, m_conv_w_dw,
                         m_conv_b_dw, m_conv_ln_g, m_conv_ln_b, m_conv_w_pw2, m_conv_b_pw2, m_ffn_norm, m_ffn_w_gu, m_ffn_w_down,
                         m_final_norm)))
    v = dict(zip(NAMES, (v_attn_norm, v_attn_w_qkv, v_attn_w_o, v_attn_sink, v_conv_norm, v_conv_w_pw1, v_conv_b_pw1, v_conv_w_dw,
                         v_conv_b_dw, v_conv_ln_g, v_conv_ln_b, v_conv_w_pw2, v_conv_b_pw2, v_ffn_norm, v_ffn_w_gu, v_ffn_w_down,
                         v_final_norm)))
    me = 4 * lax.axis_index("x") + 2 * lax.axis_index("y") + lax.axis_index("c")
    for d in (w, m, v):
        d["ffn_w_gu"] = _gu_t(d["ffn_w_gu"])
    sh = {k: w[k][0].astype(BF16) for k in ("attn_w_qkv", "attn_w_o", "conv_w_pw1", "conv_w_pw2")}
    gu_b, down_b = w["ffn_w_gu"].astype(BF16), w["ffn_w_down"].astype(BF16)
    sh.update(ffn_w_gu0=gu_b[0], ffn_w_gu1=gu_b[1], ffn_w_down0=down_b[0], ffn_w_down1=down_b[1])
    x0, tgt = x[0], loss_target[0]
    t = x0.shape[0]
    tabs = _rope_tables(t)
    tl = min(TL, t)
    g_a, sink, g_f0, g_f1, g_fin = w["attn_norm"], w["attn_sink"][0], w["ffn_norm"][0:1], w["ffn_norm"][1:2], w["final_norm"][None]
    gather, scatter = False, True

    shard_rows = _pack([w["conv_w_dw"][0], jnp.zeros((1, 128), F32), w["conv_norm"], w["conv_b_dw"], w["conv_ln_g"], w["conv_ln_b"],
                        w["conv_b_pw2"], w["conv_b_pw1"]], 40)
    wqkv_g, sm = _exchange([sh["attn_w_qkv"], shard_rows], [gather] * 2, "gather_attn")
    wqkv = wqkv_g.transpose(1, 0, 2).reshape(D, QKV)

    def full_vec(row, n=1):
        return sm[:, row:row + n, :].reshape(1, NDEV * n * 128)

    w_dw, g_c, b_dw, ln_g, ln_b = sm[:, 0:32, :].transpose(1, 0, 2).reshape(32, D), full_vec(32), full_vec(33), full_vec(34), full_vec(35)
    b_pw2, b_pw1 = full_vec(36), full_vec(37, 2)

    (q_t, kv), (wo_g, wd0_g) = _attn_pre(x0, g_a, wqkv, tabs, tl, xchg=([sh["attn_w_o"], sh["ffn_w_down0"]], [gather] * 2))
    o_t, (wgu0,) = _attn_fwd(q_t, kv, sink, xchg=([sh["ffn_w_gu0"]], [gather]))
    wo, wd0 = wo_g.reshape(D, D), wd0_g.reshape(4, GUB, D)
    (x1, sf0, act0_t, x2), (wpw1, wgu1) = _ffn_fwd_attn(
        o_t, x0, wo, g_f0, wgu0, wd0, tl, xchg=([sh["conv_w_pw1"], sh["ffn_w_gu1"]], [gather] * 2))
    (u, glu), (wpw2_g,) = _conv_pre(x2, g_c, wpw1, b_pw1, tl, xchg=([sh["conv_w_pw2"]], [gather]))
    wpw2 = wpw2_g.reshape(D, D)
    cv, (wd1_g,) = _conv_mid(glu, w_dw, b_dw, tl, xchg=([sh["ffn_w_down1"]], [gather]))
    wd1 = wd1_g.reshape(4, GUB, D)
    s_t, x3 = _conv_post(cv, x2, ln_g, ln_b, wpw2, b_pw2, tl)
    (sf1, act1_t, dx4, dx4_b, fin), _ = _ffn_fwd_final(x3, g_f1, wgu1, wd1, g_fin, tgt, tl)

    land = {}
    tk, tk2 = min(TK, t), min(2 * TK, t)
    (dgu1_t, h3, dx3, dg_f1), _ = _ffn_bwd(dx4, x3, sf1, g_f1, wgu1, wd1, TM, "ffn_bwd1")
    dwgu1 = _wgrad(dgu1_t, h3[None], tk2, "dwgu1")
    dwd1 = _wgrad(act1_t, dx4_b[None], tk2, "dwd1").reshape(NDEV, DFF // NDEV, D)
    (dcv, cpart), (land["ffn_w_down1"],) = _conv_post_bwd(dx3, cv, ln_g, ln_b, wpw2, tl, xchg=([dwd1], [scatter]))
    dwpw2 = _wgrad(s_t[None], dx3[None], tk, "dwpw2").reshape(NDEV, D // NDEV, D)
    (dglu, dw_dw), (land["ffn_w_gu1"], land["conv_w_pw2"]) = _conv_mid_bwd(dcv, glu, w_dw, tl, xchg=([dwgu1, dwpw2], [scatter] * 2))
    du, h2_t, dx2, dx2_b, dg_c, db_pw1 = _conv_pre_bwd(dglu, u, x2, g_c, wpw1, dx3, tl)
    dwpw1 = _wgrad(h2_t[None], du, tk2, "dwpw1")
    (dgu0_t, h1, dx1, dg_f0, do_t), (land["conv_w_pw1"],) = _ffn_bwd(
        dx2, x1, sf0, g_f0, wgu0, wd0, TM, "ffn_bwd0", wo=wo, xchg=([dwpw1], [scatter]))
    dwgu0 = _wgrad(dgu0_t, h1[None], tk2, "dwgu0")
    dwd0 = _wgrad(act0_t, dx2_b[None], tk2, "dwd0").reshape(NDEV, DFF // NDEV, D)
    dwo = _wgrad(o_t, dx1[None], tk, "dwo", by_block=True).reshape(NDEV, D // NDEV, D)
    (dq_t, dk, dv, dsink), (land["ffn_w_gu0"], land["ffn_w_down0"]) = _attn_bwd(
        q_t, kv, do_t, sink, tl, xchg=([dwgu0, dwd0], [scatter] * 2))
    dqkv, h0_t, dx0, dg_a = _attn_pre_bwd(dq_t, dk, dv, x0, g_a, wqkv, tabs, dx1, tl)
    lane0 = (lax.broadcasted_iota(jnp.int32, (1, 128), 1) == 0).astype(F32)
    parts = _pack([dg_a, dg_f0, dg_f1, fin[0:1], dsink[0:1, :NH], fin[1, 0] * lane0, jnp.zeros((6, 128), F32), dg_c, cpart[2:3],
                   cpart[0:1], cpart[1:2], cpart[3:4], db_pw1, dw_dw.reshape(32, NDEV, 128)], 352)
    dwqkv, (land["attn_w_o"], parts_g) = _wgrad(h0_t[None], dqkv[None], tk, "dwqkv", xchg=([dwo, parts], [scatter, gather]))
    dwqkv = dwqkv[0].reshape(D, NDEV, QKV // NDEV).transpose(1, 0, 2)
    (land["attn_w_qkv"],) = _exchange([dwqkv], [scatter], "scatter_attn")
    red = _sum_parts(parts_g)

    def shard_rows_of(row, n=1):
        return lax.dynamic_slice_in_dim(red, row + n * me, n, axis=0)

    gs = {
        "attn_norm": red[0:8].reshape(1, D), "ffn_norm": red[8:24].reshape(2, D), "final_norm": red[24:32].reshape(D),
        "attn_sink": red[32:33, :NH], "conv_norm": shard_rows_of(40), "conv_b_dw": shard_rows_of(48), "conv_ln_g": shard_rows_of(56),
        "conv_ln_b": shard_rows_of(64), "conv_b_pw2": shard_rows_of(72), "conv_b_pw1": shard_rows_of(80, 2).reshape(1, PWB),
        "conv_w_dw": lax.dynamic_index_in_dim(red[96:352].reshape(32, NDEV, 128), me, axis=1, keepdims=False)[None, :CW],
    }
    loss = red[33, 0]

    grads, deltas, new_m, new_v = dict(gs), {}, {}, {}
    ds, ms, vs = _adamw_small(_pack_small(w), _pack_small(gs), _pack_small(m), _pack_small(v))
    deltas.update(_unpack_small(ds, gs))
    new_m.update(_unpack_small(ms, gs))
    new_v.update(_unpack_small(vs, gs))
    for k in ("attn_w_qkv", "attn_w_o", "conv_w_pw1", "conv_w_pw2", "ffn_w_gu", "ffn_w_down"):
        lands = [land[k + "0"], land[k + "1"]] if k.startswith("ffn") else [land[k]]
        tr = {1024: 256, 128: 128, 352: 176, GUB: 176}[w[k].shape[1]]
        grads[k], deltas[k], new_m[k], new_v[k] = _reduce_adamw(lands, w[k], m[k], v[k], tr, "adamw_" + k)
    for d in (grads, deltas, new_m, new_v):
        d["ffn_w_gu"] = _gu_t(d["ffn_w_gu"])
    return (loss, dx0[None], *[grads[k] for k in NAMES], *[deltas[k] for k in NAMES], *[new_m[k] for k in NAMES],
            *[new_v[k] for k in NAMES])
```
